```python
import math
import jax
import jax.numpy as jnp
from jax import lax
import numpy as np

D_MODEL = 2048
BATCH = 8
SEQ = 2048
DEPTH = 2

GRID_W = 64
CTX_LEN = 256
EPS = 1e-6
F32 = jnp.float32
ROPE_THETA = 10000.0
HEAD_DIM = 128
MIX_W = D_MODEL // 2
Q_BLOCK = 128

A_HEADS = MIX_W // HEAD_DIM
A_KV_HEADS = A_HEADS // 4
WINDOW = 128

B_HEADS = MIX_W // HEAD_DIM
B_DIM = HEAD_DIM // 2
B_VDIM = 2 * B_DIM

C_WIDTH = MIX_W
C_BLOCKS = 8
C_BLOCK = C_WIDTH // C_BLOCKS
CONV_W = 4
CONV_LEFT = 2
LRU_C = 8.0

D_HEADS = MIX_W // HEAD_DIM
D_KDIM = HEAD_DIM
D_VDIM = HEAD_DIM
D_CHUNK = 64

N_BRANCH = 4
N_EXPERTS = 16
N_GROUPS = 4
TOP_K = 2
D_FF = D_MODEL // 4

A_Q = A_HEADS * HEAD_DIM
A_KV = A_KV_HEADS * HEAD_DIM
B_QK = B_HEADS * 2 * B_DIM
B_V = B_HEADS * B_VDIM
D_K = D_HEADS * D_KDIM
D_V = D_HEADS * D_VDIM
SPLITS = (A_Q, A_KV, A_KV, B_QK, B_QK, B_V, C_WIDTH, C_WIDTH, D_K, D_K, D_K, D_V, D_V, N_BRANCH * D_MODEL)
IN_W = A_Q + 2 * A_KV + 2 * B_QK + B_V + 2 * C_WIDTH + 3 * D_K + 2 * D_V + N_BRANCH * D_MODEL

kernel_name = 'hybrid_gated_parallel_mixer_moe_trunk'


def rms_norm(x, g):
    xf = x.astype(F32)
    y = xf * lax.rsqrt(jnp.mean(xf * xf, axis=-1, keepdims=True) + EPS)
    return (y * g.astype(F32)).astype(x.dtype)


def modulate(x, g, shift, scale):
    return rms_norm(x, g) * (1 + scale) + shift


def flip_seq(t):
    return jnp.flip(t, axis=1)


def rope_1d(x, pos):
    half = x.shape[-1] // 2
    inv = ROPE_THETA ** (-jnp.arange(half, dtype=F32) / half)
    ang = pos.astype(F32)[:, None] * inv[None, :]
    bshape = (1, x.shape[1]) + (1,) * (x.ndim - 3) + (half,)
    cos = jnp.cos(ang).reshape(bshape)
    sin = jnp.sin(ang).reshape(bshape)
    xf = x.astype(F32)
    x1, x2 = xf[..., :half], xf[..., half:]
    return jnp.concatenate([x1 * cos - x2 * sin, x2 * cos + x1 * sin], axis=-1).astype(x.dtype)


def rope_2d(x, rows, cols):
    half = x.shape[-1] // 2
    return jnp.concatenate([rope_1d(x[..., :half], rows), rope_1d(x[..., half:], cols)], axis=-1)


def window_attention(qc, kc, vc, ql, kl, vl, sink, ctx_out):
    Bn, L, H, d = ql.shape
    Hk = kl.shape[2]
    G = H // Hk
    Lc = kc.shape[1]
    nb = L // Q_BLOCK
    scale = d ** -0.5
    qb = ql.reshape(Bn, nb, Q_BLOCK, Hk, G, d)

    def band(t):
        tp = jnp.pad(t, ((0, 0), (Q_BLOCK, Q_BLOCK), (0, 0), (0, 0))).reshape(Bn, nb + 2, Q_BLOCK, Hk, d)
        return jnp.concatenate([tp[:, :-2], tp[:, 1:-1], tp[:, 2:]], axis=2)

    kw, vw = band(kl), band(vl)
    qpos = jnp.arange(nb)[:, None] * Q_BLOCK + jnp.arange(Q_BLOCK)[None, :]
    kpos = jnp.arange(nb)[:, None] * Q_BLOCK - Q_BLOCK + jnp.arange(3 * Q_BLOCK)[None, :]
    valid = ((jnp.abs(kpos[:, None, :] - qpos[:, :, None]) <= WINDOW)
             & (kpos[:, None, :] >= 0) & (kpos[:, None, :] < L))
    s_loc = jnp.einsum('bnqhgd,bnkhd->bhgnqk', qb, kw).astype(F32) * scale
    s_loc = jnp.where(valid, s_loc, -jnp.inf)
    s_ctx = jnp.einsum('bnqhgd,bchd->bhgnqc', qb, kc).astype(F32) * scale
    sk = sink.astype(F32).reshape(1, Hk, G, 1, 1, 1)
    s_sink = jnp.broadcast_to(sk, s_ctx.shape[:-1] + (1,))
    p = jax.nn.softmax(jnp.concatenate([s_loc, s_ctx, s_sink], axis=-1), axis=-1)
    k3 = 3 * Q_BLOCK
    p_loc = p[..., :k3].astype(vl.dtype)
    p_ctx = p[..., k3:k3 + Lc].astype(vl.dtype)
    o = jnp.einsum('bhgnqk,bnkhd->bnqhgd', p_loc, vw) + jnp.einsum('bhgnqc,bchd->bnqhgd', p_ctx, vc)
    o_lat = o.reshape(Bn, L, H * d)
    if not ctx_out:
        return None, o_lat
    qcg = qc.reshape(Bn, Lc, Hk, G, d)
    sc = jnp.einsum('bqhgd,bkhd->bhgqk', qcg, kc).astype(F32) * scale
    sc_sink = jnp.broadcast_to(sink.astype(F32).reshape(1, Hk, G, 1, 1), sc.shape[:-1] + (1,))
    pc = jax.nn.softmax(jnp.concatenate([sc, sc_sink], axis=-1), axis=-1)
    oc = jnp.einsum('bhgqk,bkhd->bqhgd', pc[..., :Lc].astype(vc.dtype), vc).reshape(Bn, Lc, H * d)
    return oc, o_lat


def diff_attention(qc, kc, vc, ql, kl, vl, lam, lam_init, subln_g, ctx_out):
    scale = B_DIM ** -0.5

    def attend(q, k, v):
        s = jnp.einsum('bqhcd,bkhcd->bhcqk', q, k).astype(F32) * scale
        p = jax.nn.softmax(s, axis=-1)
        a = (p[:, :, 0] - lam * p[:, :, 1]).astype(v.dtype)
        return jnp.einsum('bhqk,bkhd->bqhd', a, v)

    def post(o):
        y = rms_norm(o, subln_g) * (1.0 - lam_init)
        return y.reshape(o.shape[0], o.shape[1], B_HEADS * B_VDIM)

    k_all = jnp.concatenate([kc, kl], axis=1)
    v_all = jnp.concatenate([vc, vl], axis=1)
    Bn, L = ql.shape[:2]
    nb = L // Q_BLOCK
    qblocks = jnp.moveaxis(ql.reshape((Bn, nb, Q_BLOCK) + ql.shape[2:]), 1, 0)
    o = lax.map(lambda qb: attend(qb, k_all, v_all), qblocks)
    o_lat = jnp.moveaxis(o, 0, 1).reshape(Bn, L, B_HEADS, B_VDIM)
    o_ctx = post(attend(qc, kc, vc)) if ctx_out else None
    return o_ctx, post(o_lat)


def centred_conv(x, w, b):
    L = x.shape[1]
    xp = jnp.pad(x, ((0, 0), (CONV_LEFT, CONV_W - 1 - CONV_LEFT), (0, 0)))
    out = b
    for tap in range(CONV_W):
        out = out + xp[:, tap:tap + L] * w[tap]
    return out


def block_diag(x, w, b):
    Bn, L, _ = x.shape
    y = jnp.einsum('blnc,ncd->blnd', x.reshape(Bn, L, C_BLOCKS, C_BLOCK), w)
    return y.reshape(Bn, L, C_WIDTH) + b


def linear_scan(a, u, h0):
    u = u.at[:, 0].add(a[:, 0] * h0)

    def comb(left, right):
        return (left[0] * right[0], right[0] * left[1] + right[1])

    return lax.associative_scan(comb, (a, u), axis=1)[1]


def rglru_mixer(xc, yc, xl, yl, conv_w, conv_b, w_r, b_r, w_i, b_i, lam, ctx_out):
    Lc = xc.shape[1]
    u = jnp.concatenate([centred_conv(xc, conv_w, conv_b), centred_conv(xl, conv_w, conv_b)], axis=1)
    hc_dirs, hl_dirs = [], []
    for d in range(2):
        r = jax.nn.sigmoid(block_diag(u, w_r[d], b_r[d]).astype(F32))
        i = jax.nn.sigmoid(block_diag(u, w_i[d], b_i[d]).astype(F32))
        log_a = -LRU_C * r * jax.nn.softplus(-lam[d].astype(F32))
        a = jnp.exp(log_a)
        v = jnp.sqrt(-jnp.expm1(2.0 * log_a)) * i * u.astype(F32)
        ac, al, vc, vl = a[:, :Lc], a[:, Lc:], v[:, :Lc], v[:, Lc:]
        if d == 1:
            ac, al, vc, vl = flip_seq(ac), flip_seq(al), flip_seq(vc), flip_seq(vl)
        hc = linear_scan(ac, vc, jnp.zeros_like(vc[:, 0]))
        hl = linear_scan(al, vl, hc[:, -1])
        if d == 1:
            hc, hl = flip_seq(hc), flip_seq(hl)
        hc_dirs.append(hc)
        hl_dirs.append(hl)
    out_l = (hl_dirs[0] + hl_dirs[1]).astype(yl.dtype) * jax.nn.gelu(yl)
    out_c = (hc_dirs[0] + hc_dirs[1]).astype(yc.dtype) * jax.nn.gelu(yc) if ctx_out else None
    return out_c, out_l


def chunk_gla(q, k, v, g, s0):
    Bn, L, H, _ = q.shape
    V = v.shape[-1]
    C = D_CHUNK
    n = L // C

    def chunks(t):
        return t.astype(F32).reshape(Bn, n, C, H, t.shape[-1]).transpose(1, 0, 3, 2, 4)

    q, k, v, g = chunks(q), chunks(k), chunks(v), chunks(g)
    b = jnp.cumsum(g, axis=3)
    causal = jnp.tril(jnp.ones((C, C), dtype=bool))[:, :, None]

    def step(S, xs):
        qn, kn, vn, bn = xs
        rel = bn[:, :, :, None, :] - bn[:, :, None, :, :]
        decay = jnp.exp(jnp.where(causal, rel, -jnp.inf))
        att = jnp.einsum('bhik,bhijk,bhjk->bhij', qn, decay, kn)
        blast = bn[:, :, -1:, :]
        o = jnp.einsum('bhij,bhjv->bhiv', att, vn) + jnp.einsum('bhik,bhkv->bhiv', qn * jnp.exp(bn), S)
        S = jnp.exp(blast[:, :, 0, :, None]) * S + jnp.einsum('bhjk,bhjv->bhkv', kn * jnp.exp(blast - bn), vn)
        return S, o

    S, o = lax.scan(step, s0.astype(F32), (q, k, v, b))
    return o.transpose(1, 0, 3, 2, 4).reshape(Bn, L, H, V), S


def hgrn2_mixer(qc, fc, ic, gc, ql, fl, il, gl, lb, onorm_g, ctx_out):
    Bn = ql.shape[0]
    oc_dirs, ol_dirs = [], []
    for d in range(2):
        lbd = lb[d].astype(F32).reshape(D_HEADS, D_KDIM)

        def gates(z):
            zf = z.astype(F32)
            log_f = jnp.log(lbd + (1.0 - lbd) * jax.nn.sigmoid(zf))
            return log_f, (1.0 - lbd) * jax.nn.sigmoid(-zf)

        gfc, kc = gates(fc[d])
        gfl, kl = gates(fl[d])
        seq_c = (qc, kc, ic, gfc)
        seq_l = (ql, kl, il, gfl)
        if d == 1:
            seq_c = tuple(flip_seq(t) for t in seq_c)
            seq_l = tuple(flip_seq(t) for t in seq_l)
        s0 = jnp.zeros((Bn, D_HEADS, D_KDIM, D_VDIM), F32)
        oc, s_ctx = chunk_gla(seq_c[0], seq_c[1], seq_c[2], seq_c[3], s0)
        ol, _ = chunk_gla(seq_l[0], seq_l[1], seq_l[2], seq_l[3], s_ctx)
        if d == 1:
            oc, ol = flip_seq(oc), flip_seq(ol)
        oc_dirs.append(oc)
        ol_dirs.append(ol)

    def finish(o, g):
        y = rms_norm(o.astype(g.dtype), onorm_g) * jax.nn.silu(g)
        return y.reshape(y.shape[0], y.shape[1], D_V)

    out_l = finish(ol_dirs[0] + ol_dirs[1], gl)
    out_c = finish(oc_dirs[0] + oc_dirs[1], gc) if ctx_out else None
    return out_c, out_l


def moe_ffn(h, w_router, b_router, w_gate, w_up, w_down):
    shape = h.shape
    t = h.reshape(-1, shape[-1])
    n_tok = t.shape[0]
    per = N_EXPERTS // N_GROUPS
    aff = jax.nn.sigmoid(jnp.dot(t, w_router).astype(F32))
    sel = aff + b_router.astype(F32)
    group_score = jnp.sum(lax.top_k(sel.reshape(n_tok, N_GROUPS, per), TOP_K)[0], axis=-1)
    group = jnp.argmax(group_score, axis=-1)
    in_group = (jnp.arange(N_EXPERTS) // per)[None, :] == group[:, None]
    _, idx = lax.top_k(jnp.where(in_group, sel, -jnp.inf), TOP_K)
    w = jnp.take_along_axis(aff, idx, axis=-1)
    w = w / jnp.sum(w, axis=-1, keepdims=True)
    gates = jnp.sum(jax.nn.one_hot(idx, N_EXPERTS, dtype=F32) * w[..., None], axis=1).astype(t.dtype)
    y = jnp.zeros_like(t)
    for e in range(N_EXPERTS):
        hid = jax.nn.silu(jnp.dot(t, w_gate[e])) * jnp.dot(t, w_up[e])
        y = y + gates[:, e:e + 1] * jnp.dot(hid, w_down[e])
    return y.reshape(shape)


def parallel_mixers(hc, hl, rows, cols, lp, layer_idx, ctx_out):
    split_at = np.cumsum(SPLITS)[:-1].tolist()
    (aqc, akc, avc, bqc, bkc, bvc, cxc, cyc, dqc, dffc, dfbc, dic, dgc, gtc) = jnp.split(jnp.dot(hc, lp['w_in']), split_at, axis=-1)
    (aql, akl, avl, bql, bkl, bvl, cxl, cyl, dql, dffl, dfbl, dil, dgl, gtl) = jnp.split(jnp.dot(hl, lp['w_in']), split_at, axis=-1)

    def heads(t, *shape):
        return t.reshape(t.shape[:2] + shape)

    def pos2d(t):
        return rope_2d(t, rows, cols)

    qa_c = rms_norm(heads(aqc, A_HEADS, HEAD_DIM), lp['qn_a'])
    ka_c = rms_norm(heads(akc, A_KV_HEADS, HEAD_DIM), lp['kn_a'])
    qa_l = pos2d(rms_norm(heads(aql, A_HEADS, HEAD_DIM), lp['qn_a']))
    ka_l = pos2d(rms_norm(heads(akl, A_KV_HEADS, HEAD_DIM), lp['kn_a']))
    oa_c, oa_l = window_attention(qa_c, ka_c, heads(avc, A_KV_HEADS, HEAD_DIM),
                                  qa_l, ka_l, heads(avl, A_KV_HEADS, HEAD_DIM), lp['sink_a'], ctx_out)

    qb_c = rms_norm(heads(bqc, B_HEADS, 2, B_DIM), lp['qn_b'])
    kb_c = rms_norm(heads(bkc, B_HEADS, 2, B_DIM), lp['kn_b'])
    qb_l = pos2d(rms_norm(heads(bql, B_HEADS, 2, B_DIM), lp['qn_b']))
    kb_l = pos2d(rms_norm(heads(bkl, B_HEADS, 2, B_DIM), lp['kn_b']))
    lq1, lk1, lq2, lk2 = lp['lam_b'].astype(F32)
    lam_init = 0.8 - 0.6 * math.exp(-0.3 * layer_idx)
    lam = jnp.exp(jnp.sum(lq1 * lk1)) - jnp.exp(jnp.sum(lq2 * lk2)) + lam_init
    ob_c, ob_l = diff_attention(qb_c, kb_c, heads(bvc, B_HEADS, B_VDIM), qb_l, kb_l, heads(bvl, B_HEADS, B_VDIM),
                                lam, lam_init, lp['subln_b'], ctx_out)

    oc_c, oc_l = rglru_mixer(cxc, cyc, cxl, cyl, lp['conv_w'], lp['conv_b'], lp['w_rg'], lp['b_rg'],
                             lp['w_ig'], lp['b_ig'], lp['lru_lambda'], ctx_out)

    od_c, od_l = hgrn2_mixer(heads(dqc, D_HEADS, D_KDIM), (heads(dffc, D_HEADS, D_KDIM), heads(dfbc, D_HEADS, D_KDIM)),
                             heads(dic, D_HEADS, D_VDIM), heads(dgc, D_HEADS, D_VDIM),
                             heads(dql, D_HEADS, D_KDIM), (heads(dffl, D_HEADS, D_KDIM), heads(dfbl, D_HEADS, D_KDIM)),
                             heads(dil, D_HEADS, D_VDIM), heads(dgl, D_HEADS, D_VDIM),
                             lp['lb'], lp['onorm_d'], ctx_out)

    def merge(outs, gate_pre):
        o = jnp.stack(outs, axis=2)
        g = jax.nn.sigmoid(gate_pre.reshape(gate_pre.shape[:2] + (N_BRANCH, D_MODEL)))
        y = jnp.sum(g * jnp.einsum('blnc,ncd->blnd', o, lp['w_branch']), axis=2)
        return jnp.dot(y, lp['w_out'])

    out_l = merge((oa_l, ob_l, oc_l, od_l), gtl)
    out_c = merge((oa_c, ob_c, oc_c, od_c), gtc) if ctx_out else None
    return out_c, out_l


def trunk_layer(xc, xl, c, c_ctx, rows, cols, lp, w_router, b_router, layer_idx, ctx_out):
    mod_l = (jnp.dot(jax.nn.silu(c), lp['w_ada']) + lp['b_ada'])[:, None, :]
    mod_c = (jnp.dot(jax.nn.silu(c_ctx), lp['w_ada']) + lp['b_ada'])[None, None, :]
    sh1l, sc1l, g1l, sh2l, sc2l, g2l = jnp.split(mod_l, 6, axis=-1)
    sh1c, sc1c, g1c, sh2c, sc2c, g2c = jnp.split(mod_c, 6, axis=-1)
    mix_c, mix_l = parallel_mixers(modulate(xc, lp['norm1'], sh1c, sc1c), modulate(xl, lp['norm1'], sh1l, sc1l),
                                   rows, cols, lp, layer_idx, ctx_out)
    xl = xl + g1l * mix_l
    xl = xl + g2l * moe_ffn(modulate(xl, lp['norm2'], sh2l, sc2l), w_router, b_router, lp['w_gate'], lp['w_up'], lp['w_down'])
    if ctx_out:
        xc = xc + g1c * mix_c
        xc = xc + g2c * moe_ffn(modulate(xc, lp['norm2'], sh2c, sc2c), w_router, b_router, lp['w_gate'], lp['w_up'], lp['w_down'])
    return xc, xl


def setup_inputs(seed: int = 0) -> dict:
    key = jax.random.key(seed)
    keys = jax.random.split(key, 40)
    counter = [0]

    def nrm(shape, s):
        k = keys[counter[0]]
        counter[0] += 1
        return jax.random.normal(k, shape, F32) * s

    L = DEPTH
    x = nrm((BATCH, SEQ, D_MODEL), 1.0)
    c = nrm((BATCH, D_MODEL), 1.0)
    ctx = nrm((BATCH, CTX_LEN, D_MODEL), 1.0)
    c_ctx = nrm((D_MODEL,), 1.0)
    w_ada = nrm((L, D_MODEL, 6 * D_MODEL), 0.5 * D_MODEL ** -0.5)
    b_ada = nrm((L, 6 * D_MODEL), 0.02)
    norm1_g = 1.0 + nrm((L, D_MODEL), 0.02)
    norm2_g = 1.0 + nrm((L, D_MODEL), 0.02)
    w_in = nrm((L, D_MODEL, IN_W), D_MODEL ** -0.5)
    qn_a = 1.0 + nrm((L, HEAD_DIM), 0.02)
    kn_a = 1.0 + nrm((L, HEAD_DIM), 0.02)
    sink_a = nrm((L, A_HEADS), 0.5)
    qn_b = 1.0 + nrm((L, B_DIM), 0.02)
    kn_b = 1.0 + nrm((L, B_DIM), 0.02)
    lam_b = nrm((L, 4, B_DIM), 0.1)
    subln_b = 1.0 + nrm((L, B_VDIM), 0.02)
    conv_w = nrm((L, CONV_W, C_WIDTH), CONV_W ** -0.5)
    conv_b = nrm((L, C_WIDTH), 0.02)
    w_rg = nrm((L, 2, C_BLOCKS, C_BLOCK, C_BLOCK), C_BLOCK ** -0.5)
    b_rg = nrm((L, 2, C_WIDTH), 0.02)
    w_ig = nrm((L, 2, C_BLOCKS, C_BLOCK, C_BLOCK), C_BLOCK ** -0.5)
    b_ig = nrm((L, 2, C_WIDTH), 0.02)
    a0 = jax.random.uniform(keys[39], (L, 2, C_WIDTH), F32, 0.9, 0.999)
    lru_lambda = jnp.log(a0) - jnp.log1p(-a0)
    lb_d = nrm((L, 2, D_K), 0.5)
    onorm_d = 1.0 + nrm((L, D_VDIM), 0.02)
    w_branch = nrm((L, N_BRANCH, MIX_W, D_MODEL), MIX_W ** -0.5)
    w_out = nrm((L, D_MODEL, D_MODEL), D_MODEL ** -0.5)
    w_router = nrm((D_MODEL, N_EXPERTS), D_MODEL ** -0.5)
    b_router = nrm((N_EXPERTS,), 0.01)
    w_gate = nrm((L, N_EXPERTS, D_MODEL, D_FF), D_MODEL ** -0.5)
    w_up = nrm((L, N_EXPERTS, D_MODEL, D_FF), D_MODEL ** -0.5)
    w_down = nrm((L, N_EXPERTS, D_FF, D_MODEL), D_FF ** -0.5)
    return {'x': x, 'c': c, 'ctx': ctx, 'c_ctx': c_ctx, 'w_ada': w_ada, 'b_ada': b_ada,
            'norm1_g': norm1_g, 'norm2_g': norm2_g, 'w_in': w_in, 'qn_a': qn_a, 'kn_a': kn_a,
            'sink_a': sink_a, 'qn_b': qn_b, 'kn_b': kn_b, 'lam_b': lam_b, 'subln_b': subln_b,
            'conv_w': conv_w, 'conv_b': conv_b, 'w_rg': w_rg, 'b_rg': b_rg, 'w_ig': w_ig, 'b_ig': b_ig,
            'lru_lambda': lru_lambda, 'lb_d': lb_d, 'onorm_d': onorm_d, 'w_branch': w_branch,
            'w_out': w_out, 'w_router': w_router, 'b_router': b_router, 'w_gate': w_gate,
            'w_up': w_up, 'w_down': w_down}


def reference(x, c, ctx, c_ctx, w_ada, b_ada, norm1_g, norm2_g, w_in, qn_a, kn_a, sink_a, qn_b, kn_b,
              lam_b, subln_b, conv_w, conv_b, w_rg, b_rg, w_ig, b_ig, lru_lambda, lb_d, onorm_d,
              w_branch, w_out, w_router, b_router, w_gate, w_up, w_down):
    n_rows = x.shape[1] // GRID_W
    rows = jnp.repeat(jnp.arange(n_rows), GRID_W)
    cols = jnp.tile(jnp.arange(GRID_W), n_rows)
    lb_w = jax.nn.softmax(lb_d.astype(F32), axis=0)
    lb_all = jnp.cumsum(lb_w, axis=0) - lb_w[0:1]
    xc, xl = ctx, x
    for l in range(DEPTH):
        lp = {'w_ada': w_ada[l], 'b_ada': b_ada[l], 'norm1': norm1_g[l], 'norm2': norm2_g[l],
              'w_in': w_in[l], 'qn_a': qn_a[l], 'kn_a': kn_a[l], 'sink_a': sink_a[l],
              'qn_b': qn_b[l], 'kn_b': kn_b[l], 'lam_b': lam_b[l], 'subln_b': subln_b[l],
              'conv_w': conv_w[l], 'conv_b': conv_b[l], 'w_rg': w_rg[l], 'b_rg': b_rg[l],
              'w_ig': w_ig[l], 'b_ig': b_ig[l], 'lru_lambda': lru_lambda[l], 'lb': lb_all[l],
              'onorm_d': onorm_d[l], 'w_branch': w_branch[l], 'w_out': w_out[l],
              'w_gate': w_gate[l], 'w_up': w_up[l], 'w_down': w_down[l]}
        xc, xl = trunk_layer(xc, xl, c, c_ctx, rows, cols, lp, w_router, b_router, l, l < DEPTH - 1)
    return xl
```

```python
import functools
import math

import jax
import jax.numpy as jnp
from jax import lax
from jax.experimental import pallas as pl
from jax.experimental.pallas import tpu as pltpu

F32 = jnp.float32
BF16 = jnp.bfloat16

LANE = 128
SUBLANE = 8
VMEM_LIMIT_BYTES = 56 * 1024 * 1024

EPS = 1e-6
ROPE_THETA = 10000.0
GRID_W = 64
HEAD_DIM = 128
WINDOW = 128
Q_BLOCK = 128
N_HEADS = 8
A_KV_HEADS = 2
A_GROUP = N_HEADS // A_KV_HEADS
B_DIM = 64
LRU_C = 8.0
GLA_CHUNK = 64
N_EXPERTS = 16
N_GROUPS = 4
EXPERTS_PER_GROUP = N_EXPERTS // N_GROUPS
NEG_BIG = -1e30

COL_AQ, COL_AK, COL_AV = 0, 8, 10
COL_BQ, COL_BK, COL_BV = 12, 20, 28
COL_CX, COL_CY = 36, 44
COL_DQ, COL_DFF, COL_DFB, COL_DI, COL_DG = 52, 60, 68, 76, 84
COL_GATE = 92
N_QKV_BLOCKS = 36

NT_DIMS = (((1,), (1,)), ((), ()))
TN_DIMS = (((0,), (0,)), ((), ()))


def _cparams(*sem):
    return pltpu.CompilerParams(dimension_semantics=sem, vmem_limit_bytes=VMEM_LIMIT_BYTES)


def _tile(n, pref):
    t = min(n, pref)
    while n % t:
        t //= 2
    return t


def _sigmoid(x):
    return jax.nn.sigmoid(x)


def _modulated_norm(x, g, sh, sc):
    ms = jnp.mean(x * x, axis=-1, keepdims=True)
    return (x * lax.rsqrt(ms + EPS) * g) * (1.0 + sc) + sh


def _ada_kernel(c_ref, w_ref, b_ref, o_ref):
    c = c_ref[...]
    s = (c * _sigmoid(c)).astype(BF16)
    o_ref[0] = jnp.dot(s, w_ref[0].astype(BF16), preferred_element_type=F32) + b_ref[0]


def _ada(c_pad, w_ada, b_ada):
    nl, d, n = w_ada.shape
    rows = c_pad.shape[0]
    tn = _tile(n, 1024)
    return pl.pallas_call(
        _ada_kernel,
        grid=(nl, n // tn),
        in_specs=[
            pl.BlockSpec((rows, d), lambda l, j: (0, 0)),
            pl.BlockSpec((1, d, tn), lambda l, j: (l, 0, j)),
            pl.BlockSpec((1, 1, tn), lambda l, j: (l, 0, j)),
        ],
        out_specs=pl.BlockSpec((1, rows, tn), lambda l, j: (l, 0, j)),
        out_shape=jax.ShapeDtypeStruct((nl, rows, n), F32),
        compiler_params=_cparams("parallel", "parallel"),
        name="ada_mod",
    )(c_pad, w_ada, b_ada.reshape(nl, 1, n))


def _norm_proj_kernel(x_ref, g_ref, sh_ref, sc_ref, w_ref, o_ref, hn_ref):
    @pl.when(pl.program_id(2) == 0)
    def _():
        hn_ref[...] = _modulated_norm(x_ref[0], g_ref[...], sh_ref[0], sc_ref[0]).astype(BF16)

    o_ref[0] = jnp.dot(hn_ref[...], w_ref[...], preferred_element_type=F32)


def _norm_proj(x, g, sh, sc, w):
    nb, r, d = x.shape
    n = w.shape[1]
    tm = _tile(r, 1024)
    tn = _tile(n, 512)
    return pl.pallas_call(
        _norm_proj_kernel,
        grid=(nb, r // tm, n // tn),
        in_specs=[
            pl.BlockSpec((1, tm, d), lambda b, t, j: (b, t, 0)),
            pl.BlockSpec((1, d), lambda b, t, j: (0, 0)),
            pl.BlockSpec((1, 1, d), lambda b, t, j: (b, 0, 0)),
            pl.BlockSpec((1, 1, d), lambda b, t, j: (b, 0, 0)),
            pl.BlockSpec((d, tn), lambda b, t, j: (0, j)),
        ],
        out_specs=pl.BlockSpec((1, tm, tn), lambda b, t, j: (b, t, j)),
        out_shape=jax.ShapeDtypeStruct((nb, r, n), F32),
        scratch_shapes=[pltpu.VMEM((tm, d), BF16)],
        compiler_params=_cparams("parallel", "parallel", "arbitrary"),
        name="norm_proj",
    )(x, g, sh, sc, w)


def _rope_partner(y, lane, half):
    return jnp.where((lane % (2 * half)) < half,
                     pltpu.roll(y, LANE - half, 1), pltpu.roll(y, half, 1))


def _prep_kernel(*refs, rope):
    if rope:
        x_ref, g_ref, ca_ref, sa_ref, cb_ref, sb_ref, o_ref = refs
    else:
        x_ref, g_ref, o_ref = refs
    j = pl.program_id(2)
    lane = lax.broadcasted_iota(jnp.int32, x_ref.shape[1:], 1)

    @pl.when(j < COL_AV)
    def _():
        x = x_ref[0]
        ms = jnp.mean(x * x, axis=-1, keepdims=True)
        y = x * lax.rsqrt(ms + EPS) * g_ref[0]
        if rope:
            y = y * ca_ref[...] + _rope_partner(y, lane, 32) * sa_ref[...]
        o_ref[0] = y.astype(BF16)

    @pl.when((j >= COL_BQ) & (j < COL_BV))
    def _():
        x = x_ref[0]
        x2 = x * x
        lo = lane < B_DIM
        s_lo = jnp.sum(jnp.where(lo, x2, 0.0), axis=-1, keepdims=True)
        s_hi = jnp.sum(jnp.where(lo, 0.0, x2), axis=-1, keepdims=True)
        inv = jnp.where(lo, lax.rsqrt(s_lo / B_DIM + EPS), lax.rsqrt(s_hi / B_DIM + EPS))
        y = x * inv * g_ref[0]
        if rope:
            y = y * cb_ref[...] + _rope_partner(y, lane, 16) * sb_ref[...]
        o_ref[0] = y.astype(BF16)

    @pl.when(((j >= COL_AV) & (j < COL_BQ)) | (j >= COL_BV))
    def _():
        o_ref[0] = x_ref[0].astype(BF16)


def _prep(proj, gains, rope_tabs):
    nb, r, _ = proj.shape
    tm = _tile(r, 512)
    rope = rope_tabs is not None
    in_specs = [
        pl.BlockSpec((1, tm, LANE), lambda b, t, j: (b, t, j)),
        pl.BlockSpec((1, 1, LANE), lambda b, t, j: (j, 0, 0)),
    ]
    args = [proj, gains]
    if rope:
        in_specs += [pl.BlockSpec((tm, LANE), lambda b, t, j: (t, 0))] * 4
        args += list(rope_tabs)
    return pl.pallas_call(
        functools.partial(_prep_kernel, rope=rope),
        grid=(nb, r // tm, N_QKV_BLOCKS),
        in_specs=in_specs,
        out_specs=pl.BlockSpec((1, tm, LANE), lambda b, t, j: (b, t, j)),
        out_shape=jax.ShapeDtypeStruct((nb, r, N_QKV_BLOCKS * LANE), BF16),
        compiler_params=_cparams("parallel", "parallel", "arbitrary"),
        name="qkv_prep",
    )(*args)


def _rope_tables(seq):
    pos = jnp.arange(seq)
    rows = (pos // GRID_W).astype(F32)[:, None]
    cols = (pos % GRID_W).astype(F32)[:, None]
    lane = jnp.arange(LANE)

    def tables(half):
        inv = ROPE_THETA ** (-(lane % half).astype(F32) / half)
        use_rows = (lane % (4 * half)) < 2 * half
        ang = jnp.where(use_rows[None, :], rows, cols) * inv[None, :]
        sign = jnp.where((lane % (2 * half)) < half, -1.0, 1.0)
        return jnp.cos(ang), jnp.sin(ang) * sign[None, :]

    ca, sa = tables(32)
    cb, sb = tables(16)
    return ca, sa, cb, sb


def _stack_heads(q):
    return jnp.concatenate([q[:, g * HEAD_DIM:(g + 1) * HEAD_DIM] for g in range(A_GROUP)], axis=0)


def _unstack_heads(o, rows):
    return jnp.concatenate([o[g * rows:(g + 1) * rows] for g in range(A_GROUP)], axis=1)


def _sink_column(sink_ref, kvh, rows):
    return jnp.concatenate(
        [jnp.full((rows, 1), sink_ref[kvh * A_GROUP + g], F32) for g in range(A_GROUP)], axis=0)


def _attn_a_lat_kernel(sink_ref, q_ref, kl_ref, vl_ref, kc_ref, vc_ref, o_ref, *, seq):
    kvh = pl.program_id(1)
    i = pl.program_id(2)
    band = 3 * Q_BLOCK
    qs = _stack_heads(q_ref[0])
    start = pl.multiple_of(jnp.clip((i - 1) * Q_BLOCK, 0, seq - band), Q_BLOCK)
    kb = kl_ref[0, pl.ds(start, band), :]
    vb = vl_ref[0, pl.ds(start, band), :]
    s_loc = lax.dot_general(qs, kb, NT_DIMS, preferred_element_type=F32)
    s_ctx = lax.dot_general(qs, kc_ref[0], NT_DIMS, preferred_element_type=F32)
    row = lax.broadcasted_iota(jnp.int32, s_loc.shape, 0)
    col = lax.broadcasted_iota(jnp.int32, s_loc.shape, 1)
    qpos = i * Q_BLOCK + (row % Q_BLOCK)
    kpos = start + col
    s_loc = jnp.where(jnp.abs(kpos - qpos) <= WINDOW, s_loc, NEG_BIG)
    sk = _sink_column(sink_ref, kvh, Q_BLOCK)
    m = jnp.maximum(jnp.maximum(jnp.max(s_loc, axis=-1, keepdims=True),
                                jnp.max(s_ctx, axis=-1, keepdims=True)), sk)
    p_loc = jnp.exp(s_loc - m)
    p_ctx = jnp.exp(s_ctx - m)
    den = (jnp.sum(p_loc, axis=-1, keepdims=True) + jnp.sum(p_ctx, axis=-1, keepdims=True)
           + jnp.exp(sk - m))
    o = (jnp.dot(p_loc.astype(BF16), vb, preferred_element_type=F32)
         + jnp.dot(p_ctx.astype(BF16), vc_ref[0], preferred_element_type=F32)) / den
    o_ref[0] = _unstack_heads(o, Q_BLOCK).astype(BF16)


def _attn_a_ctx_kernel(sink_ref, q_ref, kc_ref, vc_ref, o_ref):
    kvh = pl.program_id(1)
    rows = q_ref.shape[1]
    qs = _stack_heads(q_ref[0])
    s = lax.dot_general(qs, kc_ref[0], NT_DIMS, preferred_element_type=F32)
    sk = _sink_column(sink_ref, kvh, rows)
    m = jnp.maximum(jnp.max(s, axis=-1, keepdims=True), sk)
    p = jnp.exp(s - m)
    den = jnp.sum(p, axis=-1, keepdims=True) + jnp.exp(sk - m)
    o = jnp.dot(p.astype(BF16), vc_ref[0], preferred_element_type=F32) / den
    o_ref[0] = _unstack_heads(o, rows).astype(BF16)


def _attn_a(sink, p_l, p_c, ctx_out):
    b, seq, _ = p_l.shape
    lc = p_c.shape[1]
    gw = A_GROUP * HEAD_DIM
    smem = pl.BlockSpec(memory_space=pltpu.SMEM)
    o_l = pl.pallas_call(
        functools.partial(_attn_a_lat_kernel, seq=seq),
        grid=(b, A_KV_HEADS, seq // Q_BLOCK),
        in_specs=[
            smem,
            pl.BlockSpec((1, Q_BLOCK, gw), lambda n, h, i: (n, i, h)),
            pl.BlockSpec((1, seq, LANE), lambda n, h, i: (n, 0, COL_AK + h)),
            pl.BlockSpec((1, seq, LANE), lambda n, h, i: (n, 0, COL_AV + h)),
            pl.BlockSpec((1, lc, LANE), lambda n, h, i: (n, 0, COL_AK + h)),
            pl.BlockSpec((1, lc, LANE), lambda n, h, i: (n, 0, COL_AV + h)),
        ],
        out_specs=pl.BlockSpec((1, Q_BLOCK, gw), lambda n, h, i: (n, i, h)),
        out_shape=jax.ShapeDtypeStruct((b, seq, N_HEADS * HEAD_DIM), BF16),
        compiler_params=_cparams("parallel", "parallel", "arbitrary"),
        name="attn_a_latent",
    )(sink, p_l, p_l, p_l, p_c, p_c)
    o_c = None
    if ctx_out:
        o_c = pl.pallas_call(
            _attn_a_ctx_kernel,
            grid=(b, A_KV_HEADS),
            in_specs=[
                smem,
                pl.BlockSpec((1, lc, gw), lambda n, h: (n, 0, h)),
                pl.BlockSpec((1, lc, LANE), lambda n, h: (n, 0, COL_AK + h)),
                pl.BlockSpec((1, lc, LANE), lambda n, h: (n, 0, COL_AV + h)),
            ],
            out_specs=pl.BlockSpec((1, lc, gw), lambda n, h: (n, 0, h)),
            out_shape=jax.ShapeDtypeStruct((b, lc, N_HEADS * HEAD_DIM), BF16),
            compiler_params=_cparams("parallel", "parallel"),
            name="attn_a_ctx",
        )(sink, p_c, p_c, p_c)
    return o_c, o_l


def _attn_b_kernel(lam_ref, q_ref, *refs, nseg, post_scale):
    k_refs = refs[:nseg]
    v_refs = refs[nseg:2 * nseg]
    g_ref, o_ref = refs[2 * nseg], refs[2 * nseg + 1]
    q = q_ref[0]
    tq = q.shape[0]
    lane = lax.broadcasted_iota(jnp.int32, q.shape, 1)
    zero = jnp.zeros_like(q)
    q2 = jnp.concatenate([jnp.where(lane < B_DIM, q, zero), jnp.where(lane < B_DIM, zero, q)], axis=0)
    ss = [lax.dot_general(q2, k[0], NT_DIMS, preferred_element_type=F32) for k in k_refs]
    m = functools.reduce(jnp.maximum, [jnp.max(s, axis=-1, keepdims=True) for s in ss])
    ps = [jnp.exp(s - m) for s in ss]
    den = functools.reduce(jnp.add, [jnp.sum(p, axis=-1, keepdims=True) for p in ps])
    acc = functools.reduce(
        jnp.add, [jnp.dot(p.astype(BF16), v[0], preferred_element_type=F32) for p, v in zip(ps, v_refs)])
    o2 = acc / den
    o = o2[:tq] - lam_ref[0] * o2[tq:]
    ms = jnp.mean(o * o, axis=-1, keepdims=True)
    o_ref[0] = ((o * lax.rsqrt(ms + EPS) * g_ref[...]) * post_scale).astype(BF16)


def _attn_b_call(lam, q_src, k_srcs, subln, post_scale, name):
    b, rq, _ = q_src.shape
    tq = _tile(rq, 256)
    smem = pl.BlockSpec(memory_space=pltpu.SMEM)
    in_specs = [smem, pl.BlockSpec((1, tq, LANE), lambda n, h, i: (n, i, COL_BQ + h))]
    in_specs += [pl.BlockSpec((1, s.shape[1], LANE), lambda n, h, i: (n, 0, COL_BK + h)) for s in k_srcs]
    in_specs += [pl.BlockSpec((1, s.shape[1], LANE), lambda n, h, i: (n, 0, COL_BV + h)) for s in k_srcs]
    in_specs += [pl.BlockSpec((1, LANE), lambda n, h, i: (0, 0))]
    return pl.pallas_call(
        functools.partial(_attn_b_kernel, nseg=len(k_srcs), post_scale=post_scale),
        grid=(b, N_HEADS, rq // tq),
        in_specs=in_specs,
        out_specs=pl.BlockSpec((1, tq, LANE), lambda n, h, i: (n, i, h)),
        out_shape=jax.ShapeDtypeStruct((b, rq, N_HEADS * HEAD_DIM), BF16),
        compiler_params=_cparams("parallel", "parallel", "arbitrary"),
        name=name,
    )(lam, q_src, *k_srcs, *k_srcs, subln)


def _attn_b(lam, p_l, p_c, subln, post_scale, ctx_out):
    o_l = _attn_b_call(lam, p_l, [p_c, p_l], subln, post_scale, "attn_b_latent")
    o_c = _attn_b_call(lam, p_c, [p_c], subln, post_scale, "attn_b_ctx") if ctx_out else None
    return o_c, o_l


def _shift_rows(x, s):
    n = x.shape[0]
    row = lax.broadcasted_iota(jnp.int32, x.shape, 0)
    rolled = pltpu.roll(x, s % n, 0)
    keep = (row >= s) if s > 0 else (row < n + s)
    return jnp.where(keep, rolled, 0.0)


def _centred_conv(x, w_ref, b_ref):
    out = b_ref[...] + _shift_rows(x, 2) * w_ref[0:1, :]
    out = out + _shift_rows(x, 1) * w_ref[1:2, :]
    out = out + x * w_ref[2:3, :]
    out = out + _shift_rows(x, -1) * w_ref[3:4, :]
    return out


def _tile_scan(a, v, reverse):
    n = a.shape[0]
    row = lax.broadcasted_iota(jnp.int32, a.shape, 0) % SUBLANE
    for s in (1, 2, 4):
        if reverse:
            keep = row < SUBLANE - s
            a_sh = jnp.where(keep, pltpu.roll(a, n - s, 0), 1.0)
            v_sh = jnp.where(keep, pltpu.roll(v, n - s, 0), 0.0)
        else:
            keep = row >= s
            a_sh = jnp.where(keep, pltpu.roll(a, s, 0), 1.0)
            v_sh = jnp.where(keep, pltpu.roll(v, s, 0), 0.0)
        v = v + a * v_sh
        a = a * a_sh
    return a, v


def _seg_tile(t, ntc, ntl, reverse):
    if not reverse:
        return t
    return jnp.where(t < ntc, ntc - 1 - t, 2 * ntc + ntl - 1 - t)


def _lru_kernel(xc_ref, yc_ref, xl_ref, yl_ref, cw_ref, cb_ref, wr_ref, br_ref, wi_ref, bi_ref,
                lam_ref, *refs, ctx_out):
    if ctx_out:
        oc_ref, ol_ref, ac_ref, hl_ref, hs_ref = refs
    else:
        ol_ref, ac_ref, hl_ref, hs_ref = refs
    lc, seq = xc_ref.shape[1], xl_ref.shape[1]
    ntc, ntl = lc // SUBLANE, seq // SUBLANE
    for d in range(2):
        lam = lam_ref[d:d + 1, :]
        sp = jnp.maximum(-lam, 0.0) + jnp.log1p(jnp.exp(-jnp.abs(lam)))
        for x_ref, off in ((xc_ref, 0), (xl_ref, lc)):
            n = x_ref.shape[1]
            u = _centred_conv(x_ref[0], cw_ref, cb_ref)
            ub = u.astype(BF16)
            r = _sigmoid(jnp.dot(ub, wr_ref[d, 0], preferred_element_type=F32) + br_ref[d:d + 1, :])
            gi = _sigmoid(jnp.dot(ub, wi_ref[d, 0], preferred_element_type=F32) + bi_ref[d:d + 1, :])
            log_a = -LRU_C * r * sp
            a = jnp.exp(log_a)
            v = jnp.sqrt(jnp.tanh(-log_a) * (1.0 + a * a)) * gi * u
            a_cum, h_loc = _tile_scan(a, v, reverse=(d == 1))
            ac_ref[d, off:off + n, :] = a_cum
            hl_ref[d, off:off + n, :] = h_loc

    def step(t, carry):
        new = []
        for d in range(2):
            tile = _seg_tile(t, ntc, ntl, d == 1)
            r0 = pl.multiple_of(tile * SUBLANE, SUBLANE)
            h = hl_ref[d, pl.ds(r0, SUBLANE), :] + ac_ref[d, pl.ds(r0, SUBLANE), :] * carry[d]
            if d == 0:
                hs_ref[pl.ds(r0, SUBLANE), :] = h
                new.append(h[SUBLANE - 1:SUBLANE, :])
            else:
                hl_ref[d, pl.ds(r0, SUBLANE), :] = h
                new.append(h[0:1, :])
        return tuple(new)

    zero = jnp.zeros((1, LANE), F32)
    lax.fori_loop(0, ntc + ntl, step, (zero, zero))
    if ctx_out:
        hc = hs_ref[0:lc, :] + hl_ref[1, 0:lc, :]
        oc_ref[0] = (hc * jax.nn.gelu(yc_ref[0])).astype(BF16)
    hl = hs_ref[lc:lc + seq, :] + hl_ref[1, lc:lc + seq, :]
    ol_ref[0] = (hl * jax.nn.gelu(yl_ref[0])).astype(BF16)


def _lru(proj_l, proj_c, conv_w, conv_b, w_r, b_r, w_i, b_i, lam, ctx_out):
    b, seq, _ = proj_l.shape
    lc = proj_c.shape[1]
    nblk = w_r.shape[1]
    width = nblk * LANE

    def seg(n, col):
        return pl.BlockSpec((1, n, LANE), lambda i, j: (i, 0, col + j))

    vec2 = pl.BlockSpec((2, LANE), lambda i, j: (0, j))
    wspec = pl.BlockSpec((2, 1, LANE, LANE), lambda i, j: (0, j, 0, 0))
    out_specs = [pl.BlockSpec((1, seq, LANE), lambda i, j: (i, 0, j))]
    out_shape = [jax.ShapeDtypeStruct((b, seq, width), BF16)]
    if ctx_out:
        out_specs.insert(0, pl.BlockSpec((1, lc, LANE), lambda i, j: (i, 0, j)))
        out_shape.insert(0, jax.ShapeDtypeStruct((b, lc, width), BF16))
    outs = pl.pallas_call(
        functools.partial(_lru_kernel, ctx_out=ctx_out),
        grid=(b, nblk),
        in_specs=[
            seg(lc, COL_CX), seg(lc, COL_CY), seg(seq, COL_CX), seg(seq, COL_CY),
            pl.BlockSpec((4, LANE), lambda i, j: (0, j)),
            pl.BlockSpec((1, LANE), lambda i, j: (0, j)),
            wspec, vec2, wspec, vec2, vec2,
        ],
        out_specs=out_specs,
        out_shape=out_shape,
        scratch_shapes=[
            pltpu.VMEM((2, lc + seq, LANE), F32),
            pltpu.VMEM((2, lc + seq, LANE), F32),
            pltpu.VMEM((lc + seq, LANE), F32),
        ],
        compiler_params=_cparams("parallel", "parallel"),
        name="rglru",
    )(proj_c, proj_c, proj_l, proj_l, conv_w, conv_b.reshape(1, width), w_r, b_r, w_i, b_i, lam)
    return (outs[0], outs[1]) if ctx_out else (None, outs[0])


def _gla_level_masks(reverse):
    c = GLA_CHUNK
    row = lax.broadcasted_iota(jnp.int32, (c, c), 0)
    col = lax.broadcasted_iota(jnp.int32, (c, c), 1)
    masks = {}
    for s in (32, 16, 8):
        same = (row // (2 * s)) == (col // (2 * s))
        if reverse:
            masks[s] = same & ((row % (2 * s)) < s) & ((col % (2 * s)) >= s)
        else:
            masks[s] = same & ((row % (2 * s)) >= s) & ((col % (2 * s)) < s)
    tri = (row <= col) if reverse else (row >= col)
    return masks, tri.astype(F32)


def _gla_chunk(q, k, v, g, st, masks, tri, reverse):
    c = GLA_CHUNK
    b = jnp.dot(tri, g, precision=lax.Precision.HIGHEST, preferred_element_type=F32)
    att = jnp.zeros((c, c), F32)
    for s in (32, 16, 8):
        b3 = b.reshape(c // (2 * s), 2 * s, LANE)
        rr = s if reverse else s - 1
        rho = jnp.broadcast_to(b3[:, rr:rr + 1, :], b3.shape).reshape(c, LANE)
        e = jnp.exp(-jnp.abs(b - rho))
        a = lax.dot_general((q * e).astype(BF16), (k * e).astype(BF16), NT_DIMS,
                            preferred_element_type=F32)
        att = att + jnp.where(masks[s], a, 0.0)
    vb = v.astype(BF16)
    o = jnp.dot(att.astype(BF16), vb, preferred_element_type=F32)
    nb = c // SUBLANE
    b3 = b.reshape(nb, SUBLANE, LANE)
    q3 = q.reshape(nb, SUBLANE, LANE)
    k3 = k.reshape(nb, SUBLANE, LANE)
    v3 = v.reshape(nb, SUBLANE, LANE)
    rowi = lax.broadcasted_iota(jnp.int32, b3.shape, 1)
    od = jnp.zeros(b3.shape, F32)
    for jj in range(SUBLANE):
        keep = (rowi <= jj) if reverse else (rowi >= jj)
        e = jnp.where(keep, jnp.exp(jnp.minimum(b3 - b3[:, jj:jj + 1, :], 0.0)), 0.0)
        sj = jnp.sum(q3 * e * k3[:, jj:jj + 1, :], axis=-1, keepdims=True)
        od = od + sj * v3[:, jj:jj + 1, :]
    o = o + od.reshape(c, LANE)
    o = o + lax.dot_general((q * jnp.exp(b)).astype(BF16), st.astype(BF16), NT_DIMS,
                            preferred_element_type=F32)
    b_end = b[0:1, :] if reverse else b[c - 1:c, :]
    khat = (k * jnp.exp(b_end - b)).astype(BF16)
    st_new = st * jnp.exp(b_end) + lax.dot_general(vb, khat, TN_DIMS, preferred_element_type=F32)
    return o, st_new


def _gla_kernel(lb_ref, on_ref, qc_ref, ffc_ref, fbc_ref, ic_ref, gc_ref,
                ql_ref, ffl_ref, fbl_ref, il_ref, gl_ref, *refs, ctx_out):
    if ctx_out:
        oc_ref, ol_ref, q_s, v_s, g_s, k_s, o_s, st_s = refs
    else:
        ol_ref, q_s, v_s, g_s, k_s, o_s, st_s = refs
    lc, seq = qc_ref.shape[1], ql_ref.shape[1]
    ncc, ncl = lc // GLA_CHUNK, seq // GLA_CHUNK
    for off, n, q_ref, i_ref, f_refs in ((0, lc, qc_ref, ic_ref, (ffc_ref, fbc_ref)),
                                         (lc, seq, ql_ref, il_ref, (ffl_ref, fbl_ref))):
        q_s[off:off + n, :] = q_ref[0]
        v_s[off:off + n, :] = i_ref[0]
        for d in range(2):
            z = f_refs[d][0]
            lbd = lb_ref[d:d + 1, :]
            ez = jnp.exp(-jnp.abs(z))
            r = 1.0 / (1.0 + ez)
            pos = z >= 0.0
            sig_p = jnp.where(pos, r, ez * r)
            sig_n = jnp.where(pos, ez * r, r)
            g_s[d, off:off + n, :] = jnp.log(lbd + (1.0 - lbd) * sig_p)
            k_s[d, off:off + n, :] = (1.0 - lbd) * sig_n
    st_s[...] = jnp.zeros(st_s.shape, F32)
    consts = [_gla_level_masks(False), _gla_level_masks(True)]

    def step(cidx, carry):
        for d in range(2):
            chunk = _seg_tile(cidx, ncc, ncl, d == 1)
            r0 = pl.multiple_of(chunk * GLA_CHUNK, GLA_CHUNK)
            rows = pl.ds(r0, GLA_CHUNK)
            o, st_new = _gla_chunk(q_s[rows, :], k_s[d, rows, :], v_s[rows, :], g_s[d, rows, :],
                                   st_s[d], consts[d][0], consts[d][1], d == 1)
            st_s[d] = st_new
            o_s[d, rows, :] = o
        return carry

    lax.fori_loop(0, ncc + ncl, step, 0)

    def finish(o, gate):
        ms = jnp.mean(o * o, axis=-1, keepdims=True)
        return ((o * lax.rsqrt(ms + EPS) * on_ref[...]) * (gate * _sigmoid(gate))).astype(BF16)

    if ctx_out:
        oc_ref[0] = finish(o_s[0, 0:lc, :] + o_s[1, 0:lc, :], gc_ref[0])
    ol_ref[0] = finish(o_s[0, lc:lc + seq, :] + o_s[1, lc:lc + seq, :], gl_ref[0])


def _gla(proj_l, proj_c, lb, onorm, ctx_out):
    b, seq, _ = proj_l.shape
    lc = proj_c.shape[1]
    width = N_HEADS * HEAD_DIM
    nt = lc + seq

    def seg(n, col):
        return pl.BlockSpec((1, n, LANE), lambda i, h: (i, 0, col + h))

    cols = (COL_DQ, COL_DFF, COL_DFB, COL_DI, COL_DG)
    out_specs = [pl.BlockSpec((1, seq, LANE), lambda i, h: (i, 0, h))]
    out_shape = [jax.ShapeDtypeStruct((b, seq, width), BF16)]
    if ctx_out:
        out_specs.insert(0, pl.BlockSpec((1, lc, LANE), lambda i, h: (i, 0, h)))
        out_shape.insert(0, jax.ShapeDtypeStruct((b, lc, width), BF16))
    outs = pl.pallas_call(
        functools.partial(_gla_kernel, ctx_out=ctx_out),
        grid=(b, N_HEADS),
        in_specs=[pl.BlockSpec((2, LANE), lambda i, h: (0, h)),
                  pl.BlockSpec((1, LANE), lambda i, h: (0, 0))]
        + [seg(lc, c) for c in cols] + [seg(seq, c) for c in cols],
        out_specs=out_specs,
        out_shape=out_shape,
        scratch_shapes=[
            pltpu.VMEM((nt, LANE), F32),
            pltpu.VMEM((nt, LANE), F32),
            pltpu.VMEM((2, nt, LANE), F32),
            pltpu.VMEM((2, nt, LANE), F32),
            pltpu.VMEM((2, nt, LANE), F32),
            pltpu.VMEM((2, LANE, LANE), F32),
        ],
        compiler_params=_cparams("parallel", "parallel"),
        name="hgrn2",
    )(lb, onorm, *([proj_c] * 5), *([proj_l] * 5))
    return (outs[0], outs[1]) if ctx_out else (None, outs[0])


def _merge_kernel(oa_ref, ob_ref, oc_ref, od_ref, g0_ref, g1_ref, g2_ref, g3_ref,
                  w0_ref, w1_ref, w2_ref, w3_ref, y_ref):
    acc = None
    for o_ref, g_ref, w_ref in ((oa_ref, g0_ref, w0_ref), (ob_ref, g1_ref, w1_ref),
                                (oc_ref, g2_ref, w2_ref), (od_ref, g3_ref, w3_ref)):
        t = _sigmoid(g_ref[0]) * jnp.dot(o_ref[0], w_ref[0], preferred_element_type=F32)
        acc = t if acc is None else acc + t
    y_ref[0] = acc.astype(BF16)


def _merge(outs, proj, w_branch):
    nb, r, mw = outs[0].shape
    d = w_branch.shape[2]
    tm = _tile(r, 512)
    tn = 512
    nj = d // tn
    o_spec = pl.BlockSpec((1, tm, mw), lambda b, t, j: (b, t, 0))
    g_specs = [pl.BlockSpec((1, tm, tn), functools.partial(
        lambda b, t, j, n: (b, t, (COL_GATE * LANE) // tn + n * nj + j), n=n)) for n in range(4)]
    w_specs = [pl.BlockSpec((1, mw, tn), functools.partial(lambda b, t, j, n: (n, 0, j), n=n))
               for n in range(4)]
    return pl.pallas_call(
        _merge_kernel,
        grid=(nb, r // tm, nj),
        in_specs=[o_spec] * 4 + g_specs + w_specs,
        out_specs=pl.BlockSpec((1, tm, tn), lambda b, t, j: (b, t, j)),
        out_shape=jax.ShapeDtypeStruct((nb, r, d), BF16),
        compiler_params=_cparams("parallel", "parallel", "arbitrary"),
        name="branch_merge",
    )(*outs, *([proj] * 4), *([w_branch] * 4))


def _out_proj_kernel(y_ref, w_ref, x_ref, g_ref, o_ref):
    o_ref[0] = x_ref[0] + g_ref[0] * jnp.dot(y_ref[0], w_ref[...], preferred_element_type=F32)


def _out_proj(y, w, x, gate):
    nb, r, d = x.shape
    tm = _tile(r, 1024)
    tn = 512
    return pl.pallas_call(
        _out_proj_kernel,
        grid=(nb, r // tm, d // tn),
        in_specs=[
            pl.BlockSpec((1, tm, d), lambda b, t, j: (b, t, 0)),
            pl.BlockSpec((d, tn), lambda b, t, j: (0, j)),
            pl.BlockSpec((1, tm, tn), lambda b, t, j: (b, t, j)),
            pl.BlockSpec((1, 1, tn), lambda b, t, j: (b, 0, j)),
        ],
        out_specs=pl.BlockSpec((1, tm, tn), lambda b, t, j: (b, t, j)),
        out_shape=jax.ShapeDtypeStruct((nb, r, d), F32),
        compiler_params=_cparams("parallel", "parallel", "arbitrary"),
        name="out_proj",
    )(y, w, x, gate)


def _route(logits_t, bias_col):
    aff = _sigmoid(logits_t)
    sel = aff + bias_col
    aff_r = [aff[e:e + 1, :] for e in range(N_EXPERTS)]
    sel_r = [sel[e:e + 1, :] for e in range(N_EXPERTS)]
    scores = []
    for g in range(N_GROUPS):
        v = sel_r[g * EXPERTS_PER_GROUP:(g + 1) * EXPERTS_PER_GROUP]
        m1 = functools.reduce(jnp.maximum, v)
        taken = jnp.zeros(m1.shape, jnp.bool_)
        second = jnp.full(m1.shape, -jnp.inf, F32)
        for x in v:
            first = (x == m1) & jnp.logical_not(taken)
            taken = taken | first
            second = jnp.where(first, second, jnp.maximum(second, x))
        scores.append(m1 + second)
    best, gidx = scores[0], jnp.zeros(scores[0].shape, jnp.int32)
    for g in range(1, N_GROUPS):
        better = scores[g] > best
        gidx = jnp.where(better, g, gidx)
        best = jnp.where(better, scores[g], best)
    masked = [jnp.where(gidx == e // EXPERTS_PER_GROUP, sel_r[e], -jnp.inf) for e in range(N_EXPERTS)]

    def first_argmax(vals, exclude):
        bv = jnp.full(vals[0].shape, -jnp.inf, F32)
        bi = jnp.full(vals[0].shape, -1, jnp.int32)
        for e, x in enumerate(vals):
            better = x > bv
            if exclude is not None:
                better = better & (exclude != e)
            bi = jnp.where(better, e, bi)
            bv = jnp.where(better, x, bv)
        return bi

    i1 = first_argmax(masked, None)
    i2 = first_argmax(masked, i1)
    w1 = functools.reduce(jnp.add, [jnp.where(i1 == e, aff_r[e], 0.0) for e in range(N_EXPERTS)])
    w2 = functools.reduce(jnp.add, [jnp.where(i2 == e, aff_r[e], 0.0) for e in range(N_EXPERTS)])
    tot = w1 + w2
    g1, g2 = w1 / tot, w2 / tot
    width = logits_t.shape[1]
    rowi = lax.broadcasted_iota(jnp.int32, (LANE, width), 0)
    i1b = jnp.broadcast_to(i1, (LANE, width))
    i2b = jnp.broadcast_to(i2, (LANE, width))
    return (jnp.where(rowi == i1b, jnp.broadcast_to(g1, (LANE, width)), 0.0)
            + jnp.where(rowi == i2b, jnp.broadcast_to(g2, (LANE, width)), 0.0))


def _moe_kernel(x_ref, g_ref, sh_ref, sc_ref, gate_ref, wrt_ref, br_ref, wg_ref, wu_ref, wd_ref,
                o_ref, hn_ref, gt_ref, acc_ref):
    e = pl.program_id(2)

    @pl.when(e == 0)
    def _():
        h = _modulated_norm(x_ref[0], g_ref[...], sh_ref[0], sc_ref[0])
        hn_ref[...] = h.astype(BF16)
        logits_t = lax.dot_general(wrt_ref[...], h, NT_DIMS, precision=lax.Precision.HIGHEST,
                                   preferred_element_type=F32)
        gt_ref[...] = _route(logits_t, br_ref[...]).T
        acc_ref[...] = jnp.zeros(acc_ref.shape, F32)

    hn = hn_ref[...]
    a = jnp.dot(hn, wg_ref[0], preferred_element_type=F32)
    u = jnp.dot(hn, wu_ref[0], preferred_element_type=F32)
    hid = ((a * _sigmoid(a)) * u).astype(BF16)
    y = jnp.dot(hid, wd_ref[0], preferred_element_type=F32)
    lane = lax.broadcasted_iota(jnp.int32, gt_ref.shape, 1)
    col = jnp.sum(jnp.where(lane == e, gt_ref[...], 0.0), axis=-1, keepdims=True)
    acc_ref[...] += col * y

    @pl.when(e == N_EXPERTS - 1)
    def _():
        o_ref[0] = x_ref[0] + gate_ref[0] * acc_ref[...]


def _moe(x, g, sh, sc, gate, w_router_t, b_router, w_gate, w_up, w_down):
    nb, r, d = x.shape
    dff = w_gate.shape[2]
    tm = _tile(r, 512)
    vec = pl.BlockSpec((1, 1, d), lambda b, t, e: (b, 0, 0))
    return pl.pallas_call(
        _moe_kernel,
        grid=(nb, r // tm, N_EXPERTS),
        in_specs=[
            pl.BlockSpec((1, tm, d), lambda b, t, e: (b, t, 0)),
            pl.BlockSpec((1, d), lambda b, t, e: (0, 0)),
            vec, vec, vec,
            pl.BlockSpec((N_EXPERTS, d), lambda b, t, e: (0, 0)),
            pl.BlockSpec((N_EXPERTS, 1), lambda b, t, e: (0, 0)),
            pl.BlockSpec((1, d, dff), lambda b, t, e: (e, 0, 0)),
            pl.BlockSpec((1, d, dff), lambda b, t, e: (e, 0, 0)),
            pl.BlockSpec((1, dff, d), lambda b, t, e: (e, 0, 0)),
        ],
        out_specs=pl.BlockSpec((1, tm, d), lambda b, t, e: (b, t, 0)),
        out_shape=jax.ShapeDtypeStruct((nb, r, d), F32),
        scratch_shapes=[
            pltpu.VMEM((tm, d), BF16),
            pltpu.VMEM((tm, LANE), F32),
            pltpu.VMEM((tm, d), F32),
        ],
        compiler_params=_cparams("parallel", "parallel", "arbitrary"),
        name="moe_ffn",
    )(x, g, sh, sc, gate, w_router_t, b_router.reshape(N_EXPERTS, 1), w_gate, w_up, w_down)


def kernel(x, c, ctx, c_ctx, w_ada, b_ada, norm1_g, norm2_g, w_in, qn_a, kn_a, sink_a, qn_b, kn_b,
           lam_b, subln_b, conv_w, conv_b, w_rg, b_rg, w_ig, b_ig, lru_lambda, lb_d, onorm_d,
           w_branch, w_out, w_router, b_router, w_gate, w_up, w_down):
    bsz, seq, d = x.shape
    lc = ctx.shape[1]
    depth = w_in.shape[0]

    n_rows = -(-(bsz + 1) // SUBLANE) * SUBLANE
    c_pad = jnp.zeros((n_rows, d), F32).at[:bsz].set(c).at[bsz].set(c_ctx)
    mod = _ada(c_pad, w_ada, b_ada).reshape(depth, n_rows, 6, d)

    lb_w = jax.nn.softmax(lb_d.astype(F32), axis=0)
    lb_all = jnp.cumsum(lb_w, axis=0) - lb_w[0:1]
    rope_tabs = _rope_tables(seq)
    w_router_t = w_router.T

    xl = x
    xc = ctx.reshape(1, bsz * lc, d)
    for l in range(depth):
        ctx_out = l < depth - 1
        mod_l = [mod[l, :bsz, k][:, None, :] for k in range(6)]
        mod_c = [mod[l, bsz:bsz + 1, k][:, None, :] for k in range(6)]
        w_in_l = w_in[l].astype(BF16)
        g1 = norm1_g[l].reshape(1, d)
        g2 = norm2_g[l].reshape(1, d)

        proj_l = _norm_proj(xl, g1, mod_l[0], mod_l[1], w_in_l)
        proj_c = _norm_proj(xc, g1, mod_c[0], mod_c[1], w_in_l).reshape(bsz, lc, -1)

        qa = qn_a[l] * (HEAD_DIM ** -0.5)
        qb = jnp.tile(qn_b[l], 2) * (B_DIM ** -0.5)
        kb = jnp.tile(kn_b[l], 2)
        one = jnp.ones((LANE,), F32)
        gains = jnp.stack([qa] * 8 + [kn_a[l]] * 2 + [one] * 2 + [qb] * 8 + [kb] * 8 + [one] * 8)
        gains = gains.reshape(N_QKV_BLOCKS, 1, LANE)
        p_l = _prep(proj_l, gains, rope_tabs)
        p_c = _prep(proj_c, gains, None)

        oa_c, oa_l = _attn_a(sink_a[l], p_l, p_c, ctx_out)

        lq1, lk1, lq2, lk2 = lam_b[l].astype(F32)
        lam_init = 0.8 - 0.6 * math.exp(-0.3 * l)
        lam = (jnp.exp(jnp.sum(lq1 * lk1)) - jnp.exp(jnp.sum(lq2 * lk2)) + lam_init).reshape(1)
        ob_c, ob_l = _attn_b(lam, p_l, p_c, subln_b[l].reshape(1, LANE), 1.0 - lam_init, ctx_out)

        oc_c, oc_l = _lru(proj_l, proj_c, conv_w[l], conv_b[l], w_rg[l].astype(BF16), b_rg[l],
                          w_ig[l].astype(BF16), b_ig[l], lru_lambda[l], ctx_out)
        od_c, od_l = _gla(proj_l, proj_c, lb_all[l], onorm_d[l].reshape(1, LANE), ctx_out)

        w_branch_l = w_branch[l].astype(BF16)
        w_out_l = w_out[l].astype(BF16)
        moe_w = (w_router_t, b_router, w_gate[l].astype(BF16), w_up[l].astype(BF16),
                 w_down[l].astype(BF16))

        y_l = _merge((oa_l, ob_l, oc_l, od_l), proj_l, w_branch_l)
        xl = _out_proj(y_l, w_out_l, xl, mod_l[2])
        xl = _moe(xl, g2, mod_l[3], mod_l[4], mod_l[5], *moe_w)
        if ctx_out:
            flat = lambda t: t.reshape(1, bsz * lc, -1)
            y_c = _merge(tuple(flat(t) for t in (oa_c, ob_c, oc_c, od_c)), flat(proj_c), w_branch_l)
            xc = _out_proj(y_c, w_out_l, xc, mod_c[2])
            xc = _moe(xc, g2, mod_c[3], mod_c[4], mod_c[5], *moe_w)
    return xl
```

```python
import functools
import math

import jax
import jax.numpy as jnp
from jax import lax
from jax.experimental import pallas as pl
from jax.experimental.pallas import tpu as pltpu

F32 = jnp.float32
BF16 = jnp.bfloat16

LANE = 128
SUBLANE = 8
VMEM_LIMIT_BYTES = 56 * 1024 * 1024

EPS = 1e-6
ROPE_THETA = 10000.0
GRID_W = 64
HEAD_DIM = 128
WINDOW = 128
Q_BLOCK = 128
N_HEADS = 8
A_KV_HEADS = 2
A_GROUP = N_HEADS // A_KV_HEADS
B_DIM = 64
LRU_C = 8.0
GLA_CHUNK = 64
N_EXPERTS = 16
N_GROUPS = 4
EXPERTS_PER_GROUP = N_EXPERTS // N_GROUPS
NEG_BIG = -1e30
TINY = 1e-37

COL_AQ, COL_AK, COL_AV = 0, 8, 10
COL_BQ, COL_BK, COL_BV = 12, 20, 28
COL_CX, COL_CY = 36, 44
COL_DQ, COL_DFF, COL_DFB, COL_DI, COL_DG = 52, 60, 68, 76, 84
COL_GATE = 92
N_QKV_BLOCKS = 36

NT_DIMS = (((1,), (1,)), ((), ()))
TN_DIMS = (((0,), (0,)), ((), ()))


def _cparams(*sem):
    return pltpu.CompilerParams(dimension_semantics=sem, vmem_limit_bytes=VMEM_LIMIT_BYTES)


def _tile(n, pref):
    t = min(n, pref)
    while n % t:
        t //= 2
    return t


def _sigmoid(x):
    return jax.nn.sigmoid(x)


def _modulated_norm(x, g, sh, sc):
    ms = jnp.mean(x * x, axis=-1, keepdims=True)
    return (x * lax.rsqrt(ms + EPS) * g) * (1.0 + sc) + sh


def _ada_kernel(c_ref, w_ref, b_ref, o_ref):
    c = c_ref[...]
    s = (c * _sigmoid(c)).astype(BF16)
    o_ref[0] = jnp.dot(s, w_ref[0].astype(BF16), preferred_element_type=F32) + b_ref[0]


def _ada(c_pad, w_ada, b_ada):
    nl, d, n = w_ada.shape
    rows = c_pad.shape[0]
    tn = _tile(n, 1024)
    return pl.pallas_call(
        _ada_kernel,
        grid=(nl, n // tn),
        in_specs=[
            pl.BlockSpec((rows, d), lambda l, j: (0, 0)),
            pl.BlockSpec((1, d, tn), lambda l, j: (l, 0, j)),
            pl.BlockSpec((1, 1, tn), lambda l, j: (l, 0, j)),
        ],
        out_specs=pl.BlockSpec((1, rows, tn), lambda l, j: (l, 0, j)),
        out_shape=jax.ShapeDtypeStruct((nl, rows, n), F32),
        compiler_params=_cparams("parallel", "parallel"),
        name="ada_mod",
    )(c_pad, w_ada, b_ada.reshape(nl, 1, n))


def _norm_proj_kernel(x_ref, g_ref, sh_ref, sc_ref, w_ref, o_ref, hn_ref):
    @pl.when(pl.program_id(2) == 0)
    def _():
        hn_ref[...] = _modulated_norm(x_ref[0], g_ref[...], sh_ref[0], sc_ref[0]).astype(BF16)

    o_ref[0] = jnp.dot(hn_ref[...], w_ref[...], preferred_element_type=F32)


def _norm_proj(x, g, sh, sc, w):
    nb, r, d = x.shape
    n = w.shape[1]
    tm = _tile(r, 1024)
    tn = _tile(n, 512)
    return pl.pallas_call(
        _norm_proj_kernel,
        grid=(nb, r // tm, n // tn),
        in_specs=[
            pl.BlockSpec((1, tm, d), lambda b, t, j: (b, t, 0)),
            pl.BlockSpec((1, d), lambda b, t, j: (0, 0)),
            pl.BlockSpec((1, 1, d), lambda b, t, j: (b, 0, 0)),
            pl.BlockSpec((1, 1, d), lambda b, t, j: (b, 0, 0)),
            pl.BlockSpec((d, tn), lambda b, t, j: (0, j)),
        ],
        out_specs=pl.BlockSpec((1, tm, tn), lambda b, t, j: (b, t, j)),
        out_shape=jax.ShapeDtypeStruct((nb, r, n), F32),
        scratch_shapes=[pltpu.VMEM((tm, d), BF16)],
        compiler_params=_cparams("parallel", "parallel", "arbitrary"),
        name="norm_proj",
    )(x, g, sh, sc, w)


def _rope_partner(y, lane, half):
    return jnp.where((lane % (2 * half)) < half,
                     pltpu.roll(y, LANE - half, 1), pltpu.roll(y, half, 1))


def _prep_kernel(*refs, rope):
    if rope:
        x_ref, g_ref, ca_ref, sa_ref, cb_ref, sb_ref, o_ref = refs
    else:
        x_ref, g_ref, o_ref = refs
    lane = lax.broadcasted_iota(jnp.int32, (x_ref.shape[1], LANE), 1)
    lo = lane < B_DIM
    for j in range(N_QKV_BLOCKS):
        cols = slice(j * LANE, (j + 1) * LANE)
        x = x_ref[0, :, cols]
        if j < COL_AV:
            ms = jnp.mean(x * x, axis=-1, keepdims=True)
            y = x * lax.rsqrt(ms + EPS) * g_ref[j]
            if rope:
                y = y * ca_ref[...] + _rope_partner(y, lane, 32) * sa_ref[...]
        elif COL_BQ <= j < COL_BV:
            x2 = x * x
            s_lo = jnp.sum(jnp.where(lo, x2, 0.0), axis=-1, keepdims=True)
            s_hi = jnp.sum(jnp.where(lo, 0.0, x2), axis=-1, keepdims=True)
            inv = jnp.where(lo, lax.rsqrt(s_lo / B_DIM + EPS), lax.rsqrt(s_hi / B_DIM + EPS))
            y = x * inv * g_ref[j]
            if rope:
                y = y * cb_ref[...] + _rope_partner(y, lane, 16) * sb_ref[...]
        else:
            y = x
        o_ref[0, :, cols] = y.astype(BF16)


def _prep(proj, gains, rope_tabs):
    nb, r, _ = proj.shape
    tm = _tile(r, 256)
    width = N_QKV_BLOCKS * LANE
    rope = rope_tabs is not None
    in_specs = [
        pl.BlockSpec((1, tm, width), lambda b, t: (b, t, 0)),
        pl.BlockSpec((N_QKV_BLOCKS, 1, LANE), lambda b, t: (0, 0, 0)),
    ]
    args = [proj, gains]
    if rope:
        in_specs += [pl.BlockSpec((tm, LANE), lambda b, t: (t, 0))] * 4
        args += list(rope_tabs)
    return pl.pallas_call(
        functools.partial(_prep_kernel, rope=rope),
        grid=(nb, r // tm),
        in_specs=in_specs,
        out_specs=pl.BlockSpec((1, tm, width), lambda b, t: (b, t, 0)),
        out_shape=jax.ShapeDtypeStruct((nb, r, width), BF16),
        compiler_params=_cparams("parallel", "parallel"),
        name="qkv_prep",
    )(*args)


def _rope_tables(seq):
    pos = jnp.arange(seq)
    rows = (pos // GRID_W).astype(F32)[:, None]
    cols = (pos % GRID_W).astype(F32)[:, None]
    lane = jnp.arange(LANE)

    def tables(half):
        inv = ROPE_THETA ** (-(lane % half).astype(F32) / half)
        use_rows = (lane % (4 * half)) < 2 * half
        ang = jnp.where(use_rows[None, :], rows, cols) * inv[None, :]
        sign = jnp.where((lane % (2 * half)) < half, -1.0, 1.0)
        return jnp.cos(ang), jnp.sin(ang) * sign[None, :]

    ca, sa = tables(32)
    cb, sb = tables(16)
    return ca, sa, cb, sb


def _stack_heads(q):
    return jnp.concatenate([q[:, g * HEAD_DIM:(g + 1) * HEAD_DIM] for g in range(A_GROUP)], axis=0)


def _unstack_heads(o, rows):
    return jnp.concatenate([o[g * rows:(g + 1) * rows] for g in range(A_GROUP)], axis=1)


def _sink_column(sink_ref, kvh, rows):
    return jnp.concatenate(
        [jnp.full((rows, 1), sink_ref[kvh * A_GROUP + g], F32) for g in range(A_GROUP)], axis=0)


def _attn_a_lat_kernel(sink_ref, q_ref, kl_ref, vl_ref, kc_ref, vc_ref, o_ref, *, seq):
    kvh = pl.program_id(1)
    i = pl.program_id(2)
    band = 3 * Q_BLOCK
    qs = _stack_heads(q_ref[0])
    start = pl.multiple_of(jnp.clip((i - 1) * Q_BLOCK, 0, seq - band), Q_BLOCK)
    kb = kl_ref[0, pl.ds(start, band), :]
    vb = vl_ref[0, pl.ds(start, band), :]
    s_loc = lax.dot_general(qs, kb, NT_DIMS, preferred_element_type=F32)
    s_ctx = lax.dot_general(qs, kc_ref[0], NT_DIMS, preferred_element_type=F32)
    row = lax.broadcasted_iota(jnp.int32, s_loc.shape, 0)
    col = lax.broadcasted_iota(jnp.int32, s_loc.shape, 1)
    qpos = i * Q_BLOCK + (row % Q_BLOCK)
    kpos = start + col
    s_loc = jnp.where(jnp.abs(kpos - qpos) <= WINDOW, s_loc, NEG_BIG)
    sk = _sink_column(sink_ref, kvh, Q_BLOCK)
    m = jnp.maximum(jnp.maximum(jnp.max(s_loc, axis=-1, keepdims=True),
                                jnp.max(s_ctx, axis=-1, keepdims=True)), sk)
    p_loc = jnp.exp(s_loc - m)
    p_ctx = jnp.exp(s_ctx - m)
    den = (jnp.sum(p_loc, axis=-1, keepdims=True) + jnp.sum(p_ctx, axis=-1, keepdims=True)
           + jnp.exp(sk - m))
    o = (jnp.dot(p_loc.astype(BF16), vb, preferred_element_type=F32)
         + jnp.dot(p_ctx.astype(BF16), vc_ref[0], preferred_element_type=F32)) / den
    o_ref[0] = _unstack_heads(o, Q_BLOCK).astype(BF16)


def _attn_a_ctx_kernel(sink_ref, q_ref, kc_ref, vc_ref, o_ref):
    kvh = pl.program_id(1)
    rows = q_ref.shape[1]
    qs = _stack_heads(q_ref[0])
    s = lax.dot_general(qs, kc_ref[0], NT_DIMS, preferred_element_type=F32)
    sk = _sink_column(sink_ref, kvh, rows)
    m = jnp.maximum(jnp.max(s, axis=-1, keepdims=True), sk)
    p = jnp.exp(s - m)
    den = jnp.sum(p, axis=-1, keepdims=True) + jnp.exp(sk - m)
    o = jnp.dot(p.astype(BF16), vc_ref[0], preferred_element_type=F32) / den
    o_ref[0] = _unstack_heads(o, rows).astype(BF16)


def _attn_a(sink, p_l, p_c, ctx_out):
    b, seq, _ = p_l.shape
    lc = p_c.shape[1]
    gw = A_GROUP * HEAD_DIM
    smem = pl.BlockSpec(memory_space=pltpu.SMEM)
    o_l = pl.pallas_call(
        functools.partial(_attn_a_lat_kernel, seq=seq),
        grid=(b, A_KV_HEADS, seq // Q_BLOCK),
        in_specs=[
            smem,
            pl.BlockSpec((1, Q_BLOCK, gw), lambda n, h, i: (n, i, h)),
            pl.BlockSpec((1, seq, LANE), lambda n, h, i: (n, 0, COL_AK + h)),
            pl.BlockSpec((1, seq, LANE), lambda n, h, i: (n, 0, COL_AV + h)),
            pl.BlockSpec((1, lc, LANE), lambda n, h, i: (n, 0, COL_AK + h)),
            pl.BlockSpec((1, lc, LANE), lambda n, h, i: (n, 0, COL_AV + h)),
        ],
        out_specs=pl.BlockSpec((1, Q_BLOCK, gw), lambda n, h, i: (n, i, h)),
        out_shape=jax.ShapeDtypeStruct((b, seq, N_HEADS * HEAD_DIM), BF16),
        compiler_params=_cparams("parallel", "parallel", "arbitrary"),
        name="attn_a_latent",
    )(sink, p_l, p_l, p_l, p_c, p_c)
    o_c = None
    if ctx_out:
        o_c = pl.pallas_call(
            _attn_a_ctx_kernel,
            grid=(b, A_KV_HEADS),
            in_specs=[
                smem,
                pl.BlockSpec((1, lc, gw), lambda n, h: (n, 0, h)),
                pl.BlockSpec((1, lc, LANE), lambda n, h: (n, 0, COL_AK + h)),
                pl.BlockSpec((1, lc, LANE), lambda n, h: (n, 0, COL_AV + h)),
            ],
            out_specs=pl.BlockSpec((1, lc, gw), lambda n, h: (n, 0, h)),
            out_shape=jax.ShapeDtypeStruct((b, lc, N_HEADS * HEAD_DIM), BF16),
            compiler_params=_cparams("parallel", "parallel"),
            name="attn_a_ctx",
        )(sink, p_c, p_c, p_c)
    return o_c, o_l


def _attn_b_kernel(lam_ref, q_ref, *refs, nseg, post_scale):
    k_refs = refs[:nseg]
    v_refs = refs[nseg:2 * nseg]
    g_ref, o_ref = refs[2 * nseg], refs[2 * nseg + 1]
    q = q_ref[0]
    tq = q.shape[0]
    lane = lax.broadcasted_iota(jnp.int32, q.shape, 1)
    zero = jnp.zeros_like(q)
    q2 = jnp.concatenate([jnp.where(lane < B_DIM, q, zero), jnp.where(lane < B_DIM, zero, q)], axis=0)
    ss = [lax.dot_general(q2, k[0], NT_DIMS, preferred_element_type=F32) for k in k_refs]
    m = functools.reduce(jnp.maximum, [jnp.max(s, axis=-1, keepdims=True) for s in ss])
    ps = [jnp.exp(s - m) for s in ss]
    den = functools.reduce(jnp.add, [jnp.sum(p, axis=-1, keepdims=True) for p in ps])
    acc = functools.reduce(
        jnp.add, [jnp.dot(p.astype(BF16), v[0], preferred_element_type=F32) for p, v in zip(ps, v_refs)])
    o2 = acc / den
    o = o2[:tq] - lam_ref[0] * o2[tq:]
    ms = jnp.mean(o * o, axis=-1, keepdims=True)
    o_ref[0] = ((o * lax.rsqrt(ms + EPS) * g_ref[...]) * post_scale).astype(BF16)


def _attn_b_call(lam, q_src, k_srcs, subln, post_scale, name):
    b, rq, _ = q_src.shape
    tq = _tile(rq, 256)
    smem = pl.BlockSpec(memory_space=pltpu.SMEM)
    in_specs = [smem, pl.BlockSpec((1, tq, LANE), lambda n, h, i: (n, i, COL_BQ + h))]
    in_specs += [pl.BlockSpec((1, s.shape[1], LANE), lambda n, h, i: (n, 0, COL_BK + h)) for s in k_srcs]
    in_specs += [pl.BlockSpec((1, s.shape[1], LANE), lambda n, h, i: (n, 0, COL_BV + h)) for s in k_srcs]
    in_specs += [pl.BlockSpec((1, LANE), lambda n, h, i: (0, 0))]
    return pl.pallas_call(
        functools.partial(_attn_b_kernel, nseg=len(k_srcs), post_scale=post_scale),
        grid=(b, N_HEADS, rq // tq),
        in_specs=in_specs,
        out_specs=pl.BlockSpec((1, tq, LANE), lambda n, h, i: (n, i, h)),
        out_shape=jax.ShapeDtypeStruct((b, rq, N_HEADS * HEAD_DIM), BF16),
        compiler_params=_cparams("parallel", "parallel", "arbitrary"),
        name=name,
    )(lam, q_src, *k_srcs, *k_srcs, subln)


def _attn_b(lam, p_l, p_c, subln, post_scale, ctx_out):
    o_l = _attn_b_call(lam, p_l, [p_c, p_l], subln, post_scale, "attn_b_latent")
    o_c = _attn_b_call(lam, p_c, [p_c], subln, post_scale, "attn_b_ctx") if ctx_out else None
    return o_c, o_l


def _centred_conv(x_ref, pad_ref, w_ref, b_ref):
    n = x_ref.shape[1]
    zeros = jnp.zeros((SUBLANE, LANE), F32)
    pad_ref[0:SUBLANE, :] = zeros
    pad_ref[SUBLANE:SUBLANE + n, :] = x_ref[0]
    pad_ref[SUBLANE + n:2 * SUBLANE + n, :] = zeros
    out = b_ref[...]
    for tap in range(4):
        out = out + pad_ref[pl.ds(SUBLANE - 2 + tap, n), :] * w_ref[tap:tap + 1, :]
    return out


def _tile_scan(a, v, reverse):
    n = a.shape[0]
    row = lax.broadcasted_iota(jnp.int32, a.shape, 0) % SUBLANE
    for s in (1, 2, 4):
        if reverse:
            keep = row < SUBLANE - s
            a_sh = jnp.where(keep, pltpu.roll(a, n - s, 0), 1.0)
            v_sh = jnp.where(keep, pltpu.roll(v, n - s, 0), 0.0)
        else:
            keep = row >= s
            a_sh = jnp.where(keep, pltpu.roll(a, s, 0), 1.0)
            v_sh = jnp.where(keep, pltpu.roll(v, s, 0), 0.0)
        v = v + a * v_sh
        a = a * a_sh
    return a, v


def _seg_tile(t, ntc, ntl, reverse):
    if not reverse:
        return t
    return jnp.where(t < ntc, ntc - 1 - t, 2 * ntc + ntl - 1 - t)


def _lru_kernel(xc_ref, yc_ref, xl_ref, yl_ref, cw_ref, cb_ref, wr_ref, br_ref, wi_ref, bi_ref,
                lam_ref, *refs, ctx_out):
    if ctx_out:
        oc_ref, ol_ref, ac_ref, hl_ref, hs_ref, pad_ref = refs
    else:
        ol_ref, ac_ref, hl_ref, hs_ref, pad_ref = refs
    lc, seq = xc_ref.shape[1], xl_ref.shape[1]
    ntc, ntl = lc // SUBLANE, seq // SUBLANE
    for x_ref, off in ((xc_ref, 0), (xl_ref, lc)):
        n = x_ref.shape[1]
        u = _centred_conv(x_ref, pad_ref, cw_ref, cb_ref)
        ub = u.astype(BF16)
        for d in range(2):
            lam = lam_ref[d:d + 1, :]
            sp = jnp.maximum(-lam, 0.0) + jnp.log1p(jnp.exp(-jnp.abs(lam)))
            r = _sigmoid(jnp.dot(ub, wr_ref[d, 0], preferred_element_type=F32) + br_ref[d:d + 1, :])
            gi = _sigmoid(jnp.dot(ub, wi_ref[d, 0], preferred_element_type=F32) + bi_ref[d:d + 1, :])
            log_a = -LRU_C * r * sp
            a = jnp.exp(log_a)
            y = jnp.tanh(-log_a) * (1.0 + a * a)
            v = (y * lax.rsqrt(jnp.maximum(y, TINY))) * gi * u
            a_cum, h_loc = _tile_scan(a, v, reverse=(d == 1))
            ac_ref[d, off:off + n, :] = a_cum
            hl_ref[d, off:off + n, :] = h_loc

    def step(t, carry):
        new = []
        for d in range(2):
            tile = _seg_tile(t, ntc, ntl, d == 1)
            r0 = pl.multiple_of(tile * SUBLANE, SUBLANE)
            h = hl_ref[d, pl.ds(r0, SUBLANE), :] + ac_ref[d, pl.ds(r0, SUBLANE), :] * carry[d]
            if d == 0:
                hs_ref[pl.ds(r0, SUBLANE), :] = h
                new.append(h[SUBLANE - 1:SUBLANE, :])
            else:
                hl_ref[d, pl.ds(r0, SUBLANE), :] = h
                new.append(h[0:1, :])
        return tuple(new)

    zero = jnp.zeros((1, LANE), F32)
    lax.fori_loop(0, ntc + ntl, step, (zero, zero))
    if ctx_out:
        hc = hs_ref[0:lc, :] + hl_ref[1, 0:lc, :]
        oc_ref[0] = (hc * jax.nn.gelu(yc_ref[0])).astype(BF16)
    hl = hs_ref[lc:lc + seq, :] + hl_ref[1, lc:lc + seq, :]
    ol_ref[0] = (hl * jax.nn.gelu(yl_ref[0])).astype(BF16)


def _lru(proj_l, proj_c, conv_w, conv_b, w_r, b_r, w_i, b_i, lam, ctx_out):
    b, seq, _ = proj_l.shape
    lc = proj_c.shape[1]
    nblk = w_r.shape[1]
    width = nblk * LANE

    def seg(n, col):
        return pl.BlockSpec((1, n, LANE), lambda i, j: (i, 0, col + j))

    vec2 = pl.BlockSpec((2, LANE), lambda i, j: (0, j))
    wspec = pl.BlockSpec((2, 1, LANE, LANE), lambda i, j: (0, j, 0, 0))
    out_specs = [pl.BlockSpec((1, seq, LANE), lambda i, j: (i, 0, j))]
    out_shape = [jax.ShapeDtypeStruct((b, seq, width), BF16)]
    if ctx_out:
        out_specs.insert(0, pl.BlockSpec((1, lc, LANE), lambda i, j: (i, 0, j)))
        out_shape.insert(0, jax.ShapeDtypeStruct((b, lc, width), BF16))
    outs = pl.pallas_call(
        functools.partial(_lru_kernel, ctx_out=ctx_out),
        grid=(b, nblk),
        in_specs=[
            seg(lc, COL_CX), seg(lc, COL_CY), seg(seq, COL_CX), seg(seq, COL_CY),
            pl.BlockSpec((4, LANE), lambda i, j: (0, j)),
            pl.BlockSpec((1, LANE), lambda i, j: (0, j)),
            wspec, vec2, wspec, vec2, vec2,
        ],
        out_specs=out_specs,
        out_shape=out_shape,
        scratch_shapes=[
            pltpu.VMEM((2, lc + seq, LANE), F32),
            pltpu.VMEM((2, lc + seq, LANE), F32),
            pltpu.VMEM((lc + seq, LANE), F32),
            pltpu.VMEM((max(lc, seq) + 2 * SUBLANE, LANE), F32),
        ],
        compiler_params=_cparams("parallel", "parallel"),
        name="rglru",
    )(proj_c, proj_c, proj_l, proj_l, conv_w, conv_b.reshape(1, width), w_r, b_r, w_i, b_i, lam)
    return (outs[0], outs[1]) if ctx_out else (None, outs[0])


def _gla_level_masks(reverse):
    c = GLA_CHUNK
    row = lax.broadcasted_iota(jnp.int32, (c, c), 0)
    col = lax.broadcasted_iota(jnp.int32, (c, c), 1)
    masks = {}
    for s in (32, 16, 8):
        same = (row // (2 * s)) == (col // (2 * s))
        if reverse:
            masks[s] = same & ((row % (2 * s)) < s) & ((col % (2 * s)) >= s)
        else:
            masks[s] = same & ((row % (2 * s)) >= s) & ((col % (2 * s)) < s)
    tri = (row <= col) if reverse else (row >= col)
    return masks, tri.astype(F32)


def _gla_chunk(q, k, v, g, st, masks, tri, reverse):
    c = GLA_CHUNK
    b = jnp.dot(tri, g, precision=lax.Precision.HIGHEST, preferred_element_type=F32)
    att = jnp.zeros((c, c), F32)
    for s in (32, 16, 8):
        b3 = b.reshape(c // (2 * s), 2 * s, LANE)
        rr = s if reverse else s - 1
        rho = jnp.broadcast_to(b3[:, rr:rr + 1, :], b3.shape).reshape(c, LANE)
        e = jnp.exp(-jnp.abs(b - rho))
        a = lax.dot_general((q * e).astype(BF16), (k * e).astype(BF16), NT_DIMS,
                            preferred_element_type=F32)
        att = att + jnp.where(masks[s], a, 0.0)
    vb = v.astype(BF16)
    o = jnp.dot(att.astype(BF16), vb, preferred_element_type=F32)
    nb = c // SUBLANE
    b3 = b.reshape(nb, SUBLANE, LANE)
    q3 = q.reshape(nb, SUBLANE, LANE)
    k3 = k.reshape(nb, SUBLANE, LANE)
    v3 = v.reshape(nb, SUBLANE, LANE)
    rowi = lax.broadcasted_iota(jnp.int32, b3.shape, 1)
    od = jnp.zeros(b3.shape, F32)
    for jj in range(SUBLANE):
        keep = (rowi <= jj) if reverse else (rowi >= jj)
        e = jnp.where(keep, jnp.exp(jnp.minimum(b3 - b3[:, jj:jj + 1, :], 0.0)), 0.0)
        sj = jnp.sum(q3 * e * k3[:, jj:jj + 1, :], axis=-1, keepdims=True)
        od = od + sj * v3[:, jj:jj + 1, :]
    o = o + od.reshape(c, LANE)
    o = o + lax.dot_general((q * jnp.exp(b)).astype(BF16), st.astype(BF16), NT_DIMS,
                            preferred_element_type=F32)
    b_end = b[0:1, :] if reverse else b[c - 1:c, :]
    khat = (k * jnp.exp(b_end - b)).astype(BF16)
    st_new = st * jnp.exp(b_end) + lax.dot_general(vb, khat, TN_DIMS, preferred_element_type=F32)
    return o, st_new


def _gla_kernel(lb_ref, on_ref, qc_ref, ffc_ref, fbc_ref, ic_ref, gc_ref,
                ql_ref, ffl_ref, fbl_ref, il_ref, gl_ref, *refs, ctx_out):
    if ctx_out:
        oc_ref, ol_ref, q_s, v_s, g_s, k_s, o_s, st_s = refs
    else:
        ol_ref, q_s, v_s, g_s, k_s, o_s, st_s = refs
    lc, seq = qc_ref.shape[1], ql_ref.shape[1]
    ncc, ncl = lc // GLA_CHUNK, seq // GLA_CHUNK
    for off, n, q_ref, i_ref, f_refs in ((0, lc, qc_ref, ic_ref, (ffc_ref, fbc_ref)),
                                         (lc, seq, ql_ref, il_ref, (ffl_ref, fbl_ref))):
        q_s[off:off + n, :] = q_ref[0]
        v_s[off:off + n, :] = i_ref[0]
        for d in range(2):
            z = f_refs[d][0]
            lbd = lb_ref[d:d + 1, :]
            ez = jnp.exp(-jnp.abs(z))
            r = 1.0 / (1.0 + ez)
            pos = z >= 0.0
            sig_p = jnp.where(pos, r, ez * r)
            sig_n = jnp.where(pos, ez * r, r)
            g_s[d, off:off + n, :] = jnp.log(lbd + (1.0 - lbd) * sig_p)
            k_s[d, off:off + n, :] = (1.0 - lbd) * sig_n
    st_s[...] = jnp.zeros(st_s.shape, F32)
    consts = [_gla_level_masks(False), _gla_level_masks(True)]

    def step(cidx, carry):
        for d in range(2):
            chunk = _seg_tile(cidx, ncc, ncl, d == 1)
            r0 = pl.multiple_of(chunk * GLA_CHUNK, GLA_CHUNK)
            rows = pl.ds(r0, GLA_CHUNK)
            o, st_new = _gla_chunk(q_s[rows, :], k_s[d, rows, :], v_s[rows, :], g_s[d, rows, :],
                                   st_s[d], consts[d][0], consts[d][1], d == 1)
            st_s[d] = st_new
            o_s[d, rows, :] = o
        return carry

    lax.fori_loop(0, ncc + ncl, step, 0, unroll=4)

    def finish(o, gate):
        ms = jnp.mean(o * o, axis=-1, keepdims=True)
        return ((o * lax.rsqrt(ms + EPS) * on_ref[...]) * (gate * _sigmoid(gate))).astype(BF16)

    if ctx_out:
        oc_ref[0] = finish(o_s[0, 0:lc, :] + o_s[1, 0:lc, :], gc_ref[0])
    ol_ref[0] = finish(o_s[0, lc:lc + seq, :] + o_s[1, lc:lc + seq, :], gl_ref[0])


def _gla(proj_l, proj_c, lb, onorm, ctx_out):
    b, seq, _ = proj_l.shape
    lc = proj_c.shape[1]
    width = N_HEADS * HEAD_DIM
    nt = lc + seq

    def seg(n, col):
        return pl.BlockSpec((1, n, LANE), lambda i, h: (i, 0, col + h))

    cols = (COL_DQ, COL_DFF, COL_DFB, COL_DI, COL_DG)
    out_specs = [pl.BlockSpec((1, seq, LANE), lambda i, h: (i, 0, h))]
    out_shape = [jax.ShapeDtypeStruct((b, seq, width), BF16)]
    if ctx_out:
        out_specs.insert(0, pl.BlockSpec((1, lc, LANE), lambda i, h: (i, 0, h)))
        out_shape.insert(0, jax.ShapeDtypeStruct((b, lc, width), BF16))
    outs = pl.pallas_call(
        functools.partial(_gla_kernel, ctx_out=ctx_out),
        grid=(b, N_HEADS),
        in_specs=[pl.BlockSpec((2, LANE), lambda i, h: (0, h)),
                  pl.BlockSpec((1, LANE), lambda i, h: (0, 0))]
        + [seg(lc, c) for c in cols] + [seg(seq, c) for c in cols],
        out_specs=out_specs,
        out_shape=out_shape,
        scratch_shapes=[
            pltpu.VMEM((nt, LANE), F32),
            pltpu.VMEM((nt, LANE), F32),
            pltpu.VMEM((2, nt, LANE), F32),
            pltpu.VMEM((2, nt, LANE), F32),
            pltpu.VMEM((2, nt, LANE), F32),
            pltpu.VMEM((2, LANE, LANE), F32),
        ],
        compiler_params=_cparams("parallel", "parallel"),
        name="hgrn2",
    )(lb, onorm, *([proj_c] * 5), *([proj_l] * 5))
    return (outs[0], outs[1]) if ctx_out else (None, outs[0])


def _merge_kernel(oa_ref, ob_ref, oc_ref, od_ref, g0_ref, g1_ref, g2_ref, g3_ref,
                  w0_ref, w1_ref, w2_ref, w3_ref, y_ref):
    acc = None
    for o_ref, g_ref, w_ref in ((oa_ref, g0_ref, w0_ref), (ob_ref, g1_ref, w1_ref),
                                (oc_ref, g2_ref, w2_ref), (od_ref, g3_ref, w3_ref)):
        t = _sigmoid(g_ref[0]) * jnp.dot(o_ref[0], w_ref[0], preferred_element_type=F32)
        acc = t if acc is None else acc + t
    y_ref[0] = acc.astype(BF16)


def _merge(outs, proj, w_branch):
    nb, r, mw = outs[0].shape
    d = w_branch.shape[2]
    tm = _tile(r, 512)
    tn = 512
    nj = d // tn
    o_spec = pl.BlockSpec((1, tm, mw), lambda b, t, j: (b, t, 0))
    g_specs = [pl.BlockSpec((1, tm, tn), functools.partial(
        lambda b, t, j, n: (b, t, (COL_GATE * LANE) // tn + n * nj + j), n=n)) for n in range(4)]
    w_specs = [pl.BlockSpec((1, mw, tn), functools.partial(lambda b, t, j, n: (n, 0, j), n=n))
               for n in range(4)]
    return pl.pallas_call(
        _merge_kernel,
        grid=(nb, r // tm, nj),
        in_specs=[o_spec] * 4 + g_specs + w_specs,
        out_specs=pl.BlockSpec((1, tm, tn), lambda b, t, j: (b, t, j)),
        out_shape=jax.ShapeDtypeStruct((nb, r, d), BF16),
        compiler_params=_cparams("parallel", "parallel", "arbitrary"),
        name="branch_merge",
    )(*outs, *([proj] * 4), *([w_branch] * 4))


def _out_proj_kernel(y_ref, w_ref, x_ref, g_ref, o_ref):
    o_ref[0] = x_ref[0] + g_ref[0] * jnp.dot(y_ref[0], w_ref[...], preferred_element_type=F32)


def _out_proj(y, w, x, gate):
    nb, r, d = x.shape
    tm = _tile(r, 1024)
    tn = 512
    return pl.pallas_call(
        _out_proj_kernel,
        grid=(nb, r // tm, d // tn),
        in_specs=[
            pl.BlockSpec((1, tm, d), lambda b, t, j: (b, t, 0)),
            pl.BlockSpec((d, tn), lambda b, t, j: (0, j)),
            pl.BlockSpec((1, tm, tn), lambda b, t, j: (b, t, j)),
            pl.BlockSpec((1, 1, tn), lambda b, t, j: (b, 0, j)),
        ],
        out_specs=pl.BlockSpec((1, tm, tn), lambda b, t, j: (b, t, j)),
        out_shape=jax.ShapeDtypeStruct((nb, r, d), F32),
        compiler_params=_cparams("parallel", "parallel", "arbitrary"),
        name="out_proj",
    )(y, w, x, gate)


def _route(logits_t, bias_col):
    aff = _sigmoid(logits_t)
    sel = aff + bias_col
    aff_r = [aff[e:e + 1, :] for e in range(N_EXPERTS)]
    sel_r = [sel[e:e + 1, :] for e in range(N_EXPERTS)]
    scores = []
    for g in range(N_GROUPS):
        v = sel_r[g * EXPERTS_PER_GROUP:(g + 1) * EXPERTS_PER_GROUP]
        m1 = functools.reduce(jnp.maximum, v)
        taken = jnp.zeros(m1.shape, jnp.bool_)
        second = jnp.full(m1.shape, -jnp.inf, F32)
        for x in v:
            first = (x == m1) & jnp.logical_not(taken)
            taken = taken | first
            second = jnp.where(first, second, jnp.maximum(second, x))
        scores.append(m1 + second)
    best, gidx = scores[0], jnp.zeros(scores[0].shape, jnp.int32)
    for g in range(1, N_GROUPS):
        better = scores[g] > best
        gidx = jnp.where(better, g, gidx)
        best = jnp.where(better, scores[g], best)
    masked = [jnp.where(gidx == e // EXPERTS_PER_GROUP, sel_r[e], -jnp.inf) for e in range(N_EXPERTS)]

    def first_argmax(vals, exclude):
        bv = jnp.full(vals[0].shape, -jnp.inf, F32)
        bi = jnp.full(vals[0].shape, -1, jnp.int32)
        for e, x in enumerate(vals):
            better = x > bv
            if exclude is not None:
                better = better & (exclude != e)
            bi = jnp.where(better, e, bi)
            bv = jnp.where(better, x, bv)
        return bi

    i1 = first_argmax(masked, None)
    i2 = first_argmax(masked, i1)
    w1 = functools.reduce(jnp.add, [jnp.where(i1 == e, aff_r[e], 0.0) for e in range(N_EXPERTS)])
    w2 = functools.reduce(jnp.add, [jnp.where(i2 == e, aff_r[e], 0.0) for e in range(N_EXPERTS)])
    tot = w1 + w2
    g1, g2 = w1 / tot, w2 / tot
    width = logits_t.shape[1]
    rowi = lax.broadcasted_iota(jnp.int32, (LANE, width), 0)
    i1b = jnp.broadcast_to(i1, (LANE, width))
    i2b = jnp.broadcast_to(i2, (LANE, width))
    return (jnp.where(rowi == i1b, jnp.broadcast_to(g1, (LANE, width)), 0.0)
            + jnp.where(rowi == i2b, jnp.broadcast_to(g2, (LANE, width)), 0.0))


def _moe_kernel(x_ref, g_ref, sh_ref, sc_ref, gate_ref, wrt_ref, br_ref, wg_ref, wu_ref, wd_ref,
                o_ref, hn_ref, gt_ref, acc_ref):
    e = pl.program_id(2)

    @pl.when(e == 0)
    def _():
        h = _modulated_norm(x_ref[0], g_ref[...], sh_ref[0], sc_ref[0])
        hn_ref[...] = h.astype(BF16)
        logits_t = lax.dot_general(wrt_ref[...], h, NT_DIMS, precision=lax.Precision.HIGHEST,
                                   preferred_element_type=F32)
        gt_ref[...] = _route(logits_t, br_ref[...]).T
        acc_ref[...] = jnp.zeros(acc_ref.shape, F32)

    hn = hn_ref[...]
    a = jnp.dot(hn, wg_ref[0], preferred_element_type=F32)
    u = jnp.dot(hn, wu_ref[0], preferred_element_type=F32)
    hid = ((a * _sigmoid(a)) * u).astype(BF16)
    y = jnp.dot(hid, wd_ref[0], preferred_element_type=F32)
    lane = lax.broadcasted_iota(jnp.int32, gt_ref.shape, 1)
    col = jnp.sum(jnp.where(lane == e, gt_ref[...], 0.0), axis=-1, keepdims=True)
    acc_ref[...] += col * y

    @pl.when(e == N_EXPERTS - 1)
    def _():
        o_ref[0] = x_ref[0] + gate_ref[0] * acc_ref[...]


def _moe(x, g, sh, sc, gate, w_router_t, b_router, w_gate, w_up, w_down):
    nb, r, d = x.shape
    dff = w_gate.shape[2]
    tm = _tile(r, 512)
    vec = pl.BlockSpec((1, 1, d), lambda b, t, e: (b, 0, 0))
    return pl.pallas_call(
        _moe_kernel,
        grid=(nb, r // tm, N_EXPERTS),
        in_specs=[
            pl.BlockSpec((1, tm, d), lambda b, t, e: (b, t, 0)),
            pl.BlockSpec((1, d), lambda b, t, e: (0, 0)),
            vec, vec, vec,
            pl.BlockSpec((N_EXPERTS, d), lambda b, t, e: (0, 0)),
            pl.BlockSpec((N_EXPERTS, 1), lambda b, t, e: (0, 0)),
            pl.BlockSpec((1, d, dff), lambda b, t, e: (e, 0, 0)),
            pl.BlockSpec((1, d, dff), lambda b, t, e: (e, 0, 0)),
            pl.BlockSpec((1, dff, d), lambda b, t, e: (e, 0, 0)),
        ],
        out_specs=pl.BlockSpec((1, tm, d), lambda b, t, e: (b, t, 0)),
        out_shape=jax.ShapeDtypeStruct((nb, r, d), F32),
        scratch_shapes=[
            pltpu.VMEM((tm, d), BF16),
            pltpu.VMEM((tm, LANE), F32),
            pltpu.VMEM((tm, d), F32),
        ],
        compiler_params=_cparams("parallel", "parallel", "arbitrary"),
        name="moe_ffn",
    )(x, g, sh, sc, gate, w_router_t, b_router.reshape(N_EXPERTS, 1), w_gate, w_up, w_down)


def kernel(x, c, ctx, c_ctx, w_ada, b_ada, norm1_g, norm2_g, w_in, qn_a, kn_a, sink_a, qn_b, kn_b,
           lam_b, subln_b, conv_w, conv_b, w_rg, b_rg, w_ig, b_ig, lru_lambda, lb_d, onorm_d,
           w_branch, w_out, w_router, b_router, w_gate, w_up, w_down):
    bsz, seq, d = x.shape
    lc = ctx.shape[1]
    depth = w_in.shape[0]

    n_rows = -(-(bsz + 1) // SUBLANE) * SUBLANE
    c_pad = jnp.zeros((n_rows, d), F32).at[:bsz].set(c).at[bsz].set(c_ctx)
    mod = _ada(c_pad, w_ada, b_ada).reshape(depth, n_rows, 6, d)

    lb_w = jax.nn.softmax(lb_d.astype(F32), axis=0)
    lb_all = jnp.cumsum(lb_w, axis=0) - lb_w[0:1]
    rope_tabs = _rope_tables(seq)
    w_router_t = w_router.T

    xl = x
    xc = ctx.reshape(1, bsz * lc, d)
    for l in range(depth):
        ctx_out = l < depth - 1
        mod_l = [mod[l, :bsz, k][:, None, :] for k in range(6)]
        mod_c = [mod[l, bsz:bsz + 1, k][:, None, :] for k in range(6)]
        w_in_l = w_in[l].astype(BF16)
        g1 = norm1_g[l].reshape(1, d)
        g2 = norm2_g[l].reshape(1, d)

        proj_l = _norm_proj(xl, g1, mod_l[0], mod_l[1], w_in_l)
        proj_c = _norm_proj(xc, g1, mod_c[0], mod_c[1], w_in_l).reshape(bsz, lc, -1)

        qa = qn_a[l] * (HEAD_DIM ** -0.5)
        qb = jnp.tile(qn_b[l], 2) * (B_DIM ** -0.5)
        kb = jnp.tile(kn_b[l], 2)
        one = jnp.ones((LANE,), F32)
        gains = jnp.stack([qa] * 8 + [kn_a[l]] * 2 + [one] * 2 + [qb] * 8 + [kb] * 8 + [one] * 8)
        gains = gains.reshape(N_QKV_BLOCKS, 1, LANE)
        p_l = _prep(proj_l, gains, rope_tabs)
        p_c = _prep(proj_c, gains, None)

        oa_c, oa_l = _attn_a(sink_a[l], p_l, p_c, ctx_out)

        lq1, lk1, lq2, lk2 = lam_b[l].astype(F32)
        lam_init = 0.8 - 0.6 * math.exp(-0.3 * l)
        lam = (jnp.exp(jnp.sum(lq1 * lk1)) - jnp.exp(jnp.sum(lq2 * lk2)) + lam_init).reshape(1)
        ob_c, ob_l = _attn_b(lam, p_l, p_c, subln_b[l].reshape(1, LANE), 1.0 - lam_init, ctx_out)

        oc_c, oc_l = _lru(proj_l, proj_c, conv_w[l], conv_b[l], w_rg[l].astype(BF16), b_rg[l],
                          w_ig[l].astype(BF16), b_ig[l], lru_lambda[l], ctx_out)
        od_c, od_l = _gla(proj_l, proj_c, lb_all[l], onorm_d[l].reshape(1, LANE), ctx_out)

        w_branch_l = w_branch[l].astype(BF16)
        w_out_l = w_out[l].astype(BF16)
        moe_w = (w_router_t, b_router, w_gate[l].astype(BF16), w_up[l].astype(BF16),
                 w_down[l].astype(BF16))

        y_l = _merge((oa_l, ob_l, oc_l, od_l), proj_l, w_branch_l)
        xl = _out_proj(y_l, w_out_l, xl, mod_l[2])
        xl = _moe(xl, g2, mod_l[3], mod_l[4], mod_l[5], *moe_w)
        if ctx_out:
            flat = lambda t: t.reshape(1, bsz * lc, -1)
            y_c = _merge(tuple(flat(t) for t in (oa_c, ob_c, oc_c, od_c)), flat(proj_c), w_branch_l)
            xc = _out_proj(y_c, w_out_l, xc, mod_c[2])
            xc = _moe(xc, g2, mod_c[3], mod_c[4], mod_c[5], *moe_w)
    return xl
```

```python
import functools
import math

import jax
import jax.numpy as jnp
from jax import lax
from jax.experimental import pallas as pl
from jax.experimental.pallas import tpu as pltpu

F32 = jnp.float32
BF16 = jnp.bfloat16

LANE = 128
SUBLANE = 8
VMEM_LIMIT_BYTES = 56 * 1024 * 1024

EPS = 1e-6
ROPE_THETA = 10000.0
GRID_W = 64
HEAD_DIM = 128
WINDOW = 128
Q_BLOCK = 128
N_HEADS = 8
A_KV_HEADS = 2
A_GROUP = N_HEADS // A_KV_HEADS
B_DIM = 64
LRU_C = 8.0
GLA_CHUNK = 64
N_EXPERTS = 16
N_GROUPS = 4
EXPERTS_PER_GROUP = N_EXPERTS // N_GROUPS
NEG_BIG = -1e30
TINY = 1e-37

COL_AQ, COL_AK, COL_AV = 0, 8, 10
COL_BQ, COL_BK, COL_BV = 12, 20, 28
COL_CX, COL_CY = 36, 44
COL_DQ, COL_DFF, COL_DFB, COL_DI, COL_DG = 52, 60, 68, 76, 84
COL_GATE = 92
N_QKV_BLOCKS = 36

NT_DIMS = (((1,), (1,)), ((), ()))
TN_DIMS = (((0,), (0,)), ((), ()))


def _cparams(*sem):
    return pltpu.CompilerParams(dimension_semantics=sem, vmem_limit_bytes=VMEM_LIMIT_BYTES)


def _tile(n, pref):
    t = min(n, pref)
    while n % t:
        t //= 2
    return t


def _sigmoid(x):
    return jax.nn.sigmoid(x)


def _modulated_norm(x, g, sh, sc):
    ms = jnp.mean(x * x, axis=-1, keepdims=True)
    return (x * lax.rsqrt(ms + EPS) * g) * (1.0 + sc) + sh


def _ada_kernel(c_ref, w_ref, b_ref, o_ref):
    c = c_ref[...]
    s = (c * _sigmoid(c)).astype(BF16)
    o_ref[0] = jnp.dot(s, w_ref[0].astype(BF16), preferred_element_type=F32) + b_ref[0]


def _ada(c_pad, w_ada, b_ada):
    nl, d, n = w_ada.shape
    rows = c_pad.shape[0]
    tn = _tile(n, 1024)
    return pl.pallas_call(
        _ada_kernel,
        grid=(nl, n // tn),
        in_specs=[
            pl.BlockSpec((rows, d), lambda l, j: (0, 0)),
            pl.BlockSpec((1, d, tn), lambda l, j: (l, 0, j)),
            pl.BlockSpec((1, 1, tn), lambda l, j: (l, 0, j)),
        ],
        out_specs=pl.BlockSpec((1, rows, tn), lambda l, j: (l, 0, j)),
        out_shape=jax.ShapeDtypeStruct((nl, rows, n), F32),
        compiler_params=_cparams("parallel", "parallel"),
        name="ada_mod",
    )(c_pad, w_ada, b_ada.reshape(nl, 1, n))


def _norm_proj_kernel(x_ref, g_ref, sh_ref, sc_ref, w_ref, o_ref, hn_ref):
    @pl.when(pl.program_id(2) == 0)
    def _():
        hn_ref[...] = _modulated_norm(x_ref[0], g_ref[...], sh_ref[0], sc_ref[0]).astype(BF16)

    o_ref[0] = jnp.dot(hn_ref[...], w_ref[...], preferred_element_type=F32)


def _norm_proj(x, g, sh, sc, w):
    nb, r, d = x.shape
    n = w.shape[1]
    tm = _tile(r, 1024)
    tn = _tile(n, 512)
    return pl.pallas_call(
        _norm_proj_kernel,
        grid=(nb, r // tm, n // tn),
        in_specs=[
            pl.BlockSpec((1, tm, d), lambda b, t, j: (b, t, 0)),
            pl.BlockSpec((1, d), lambda b, t, j: (0, 0)),
            pl.BlockSpec((1, 1, d), lambda b, t, j: (b, 0, 0)),
            pl.BlockSpec((1, 1, d), lambda b, t, j: (b, 0, 0)),
            pl.BlockSpec((d, tn), lambda b, t, j: (0, j)),
        ],
        out_specs=pl.BlockSpec((1, tm, tn), lambda b, t, j: (b, t, j)),
        out_shape=jax.ShapeDtypeStruct((nb, r, n), F32),
        scratch_shapes=[pltpu.VMEM((tm, d), BF16)],
        compiler_params=_cparams("parallel", "parallel", "arbitrary"),
        name="norm_proj",
    )(x, g, sh, sc, w)


def _rope_partner(y, lane, half):
    return jnp.where((lane % (2 * half)) < half,
                     pltpu.roll(y, LANE - half, 1), pltpu.roll(y, half, 1))


def _prep_kernel(*refs, rope):
    if rope:
        x_ref, g_ref, ca_ref, sa_ref, cb_ref, sb_ref, o_ref = refs
    else:
        x_ref, g_ref, o_ref = refs
    lane = lax.broadcasted_iota(jnp.int32, (x_ref.shape[1], LANE), 1)
    lo = lane < B_DIM
    for j in range(N_QKV_BLOCKS):
        cols = slice(j * LANE, (j + 1) * LANE)
        x = x_ref[0, :, cols]
        if j < COL_AV:
            ms = jnp.mean(x * x, axis=-1, keepdims=True)
            y = x * lax.rsqrt(ms + EPS) * g_ref[j]
            if rope:
                y = y * ca_ref[...] + _rope_partner(y, lane, 32) * sa_ref[...]
        elif COL_BQ <= j < COL_BV:
            x2 = x * x
            s_lo = jnp.sum(jnp.where(lo, x2, 0.0), axis=-1, keepdims=True)
            s_hi = jnp.sum(jnp.where(lo, 0.0, x2), axis=-1, keepdims=True)
            inv = jnp.where(lo, lax.rsqrt(s_lo / B_DIM + EPS), lax.rsqrt(s_hi / B_DIM + EPS))
            y = x * inv * g_ref[j]
            if rope:
                y = y * cb_ref[...] + _rope_partner(y, lane, 16) * sb_ref[...]
        else:
            y = x
        o_ref[0, :, cols] = y.astype(BF16)


def _prep(proj, gains, rope_tabs):
    nb, r, _ = proj.shape
    tm = _tile(r, 256)
    width = N_QKV_BLOCKS * LANE
    rope = rope_tabs is not None
    in_specs = [
        pl.BlockSpec((1, tm, width), lambda b, t: (b, t, 0)),
        pl.BlockSpec((N_QKV_BLOCKS, 1, LANE), lambda b, t: (0, 0, 0)),
    ]
    args = [proj, gains]
    if rope:
        in_specs += [pl.BlockSpec((tm, LANE), lambda b, t: (t, 0))] * 4
        args += list(rope_tabs)
    return pl.pallas_call(
        functools.partial(_prep_kernel, rope=rope),
        grid=(nb, r // tm),
        in_specs=in_specs,
        out_specs=pl.BlockSpec((1, tm, width), lambda b, t: (b, t, 0)),
        out_shape=jax.ShapeDtypeStruct((nb, r, width), BF16),
        compiler_params=_cparams("parallel", "parallel"),
        name="qkv_prep",
    )(*args)


def _rope_tables(seq):
    pos = jnp.arange(seq)
    rows = (pos // GRID_W).astype(F32)[:, None]
    cols = (pos % GRID_W).astype(F32)[:, None]
    lane = jnp.arange(LANE)

    def tables(half):
        inv = ROPE_THETA ** (-(lane % half).astype(F32) / half)
        use_rows = (lane % (4 * half)) < 2 * half
        ang = jnp.where(use_rows[None, :], rows, cols) * inv[None, :]
        sign = jnp.where((lane % (2 * half)) < half, -1.0, 1.0)
        return jnp.cos(ang), jnp.sin(ang) * sign[None, :]

    ca, sa = tables(32)
    cb, sb = tables(16)
    return ca, sa, cb, sb


def _stack_heads(q):
    return jnp.concatenate([q[:, g * HEAD_DIM:(g + 1) * HEAD_DIM] for g in range(A_GROUP)], axis=0)


def _unstack_heads(o, rows):
    return jnp.concatenate([o[g * rows:(g + 1) * rows] for g in range(A_GROUP)], axis=1)


def _sink_column(sink_ref, kvh, rows):
    return jnp.concatenate(
        [jnp.full((rows, 1), sink_ref[kvh * A_GROUP + g], F32) for g in range(A_GROUP)], axis=0)


def _attn_a_lat_kernel(sink_ref, q_ref, kl_ref, vl_ref, kc_ref, vc_ref, o_ref, *, seq):
    kvh = pl.program_id(1)
    i = pl.program_id(2)
    band = 3 * Q_BLOCK
    qs = _stack_heads(q_ref[0])
    start = pl.multiple_of(jnp.clip((i - 1) * Q_BLOCK, 0, seq - band), Q_BLOCK)
    kb = kl_ref[0, pl.ds(start, band), :]
    vb = vl_ref[0, pl.ds(start, band), :]
    s_loc = lax.dot_general(qs, kb, NT_DIMS, preferred_element_type=F32)
    s_ctx = lax.dot_general(qs, kc_ref[0], NT_DIMS, preferred_element_type=F32)
    row = lax.broadcasted_iota(jnp.int32, s_loc.shape, 0)
    col = lax.broadcasted_iota(jnp.int32, s_loc.shape, 1)
    qpos = i * Q_BLOCK + (row % Q_BLOCK)
    kpos = start + col
    s_loc = jnp.where(jnp.abs(kpos - qpos) <= WINDOW, s_loc, NEG_BIG)
    sk = _sink_column(sink_ref, kvh, Q_BLOCK)
    m = jnp.maximum(jnp.maximum(jnp.max(s_loc, axis=-1, keepdims=True),
                                jnp.max(s_ctx, axis=-1, keepdims=True)), sk)
    p_loc = jnp.exp(s_loc - m)
    p_ctx = jnp.exp(s_ctx - m)
    den = (jnp.sum(p_loc, axis=-1, keepdims=True) + jnp.sum(p_ctx, axis=-1, keepdims=True)
           + jnp.exp(sk - m))
    o = (jnp.dot(p_loc.astype(BF16), vb, preferred_element_type=F32)
         + jnp.dot(p_ctx.astype(BF16), vc_ref[0], preferred_element_type=F32)) / den
    o_ref[0] = _unstack_heads(o, Q_BLOCK).astype(BF16)


def _attn_a_ctx_kernel(sink_ref, q_ref, kc_ref, vc_ref, o_ref):
    kvh = pl.program_id(1)
    rows = q_ref.shape[1]
    qs = _stack_heads(q_ref[0])
    s = lax.dot_general(qs, kc_ref[0], NT_DIMS, preferred_element_type=F32)
    sk = _sink_column(sink_ref, kvh, rows)
    m = jnp.maximum(jnp.max(s, axis=-1, keepdims=True), sk)
    p = jnp.exp(s - m)
    den = jnp.sum(p, axis=-1, keepdims=True) + jnp.exp(sk - m)
    o = jnp.dot(p.astype(BF16), vc_ref[0], preferred_element_type=F32) / den
    o_ref[0] = _unstack_heads(o, rows).astype(BF16)


def _attn_a(sink, p_l, p_c, ctx_out):
    b, seq, _ = p_l.shape
    lc = p_c.shape[1]
    gw = A_GROUP * HEAD_DIM
    smem = pl.BlockSpec(memory_space=pltpu.SMEM)
    o_l = pl.pallas_call(
        functools.partial(_attn_a_lat_kernel, seq=seq),
        grid=(b, A_KV_HEADS, seq // Q_BLOCK),
        in_specs=[
            smem,
            pl.BlockSpec((1, Q_BLOCK, gw), lambda n, h, i: (n, i, h)),
            pl.BlockSpec((1, seq, LANE), lambda n, h, i: (n, 0, COL_AK + h)),
            pl.BlockSpec((1, seq, LANE), lambda n, h, i: (n, 0, COL_AV + h)),
            pl.BlockSpec((1, lc, LANE), lambda n, h, i: (n, 0, COL_AK + h)),
            pl.BlockSpec((1, lc, LANE), lambda n, h, i: (n, 0, COL_AV + h)),
        ],
        out_specs=pl.BlockSpec((1, Q_BLOCK, gw), lambda n, h, i: (n, i, h)),
        out_shape=jax.ShapeDtypeStruct((b, seq, N_HEADS * HEAD_DIM), BF16),
        compiler_params=_cparams("parallel", "parallel", "arbitrary"),
        name="attn_a_latent",
    )(sink, p_l, p_l, p_l, p_c, p_c)
    o_c = None
    if ctx_out:
        o_c = pl.pallas_call(
            _attn_a_ctx_kernel,
            grid=(b, A_KV_HEADS),
            in_specs=[
                smem,
                pl.BlockSpec((1, lc, gw), lambda n, h: (n, 0, h)),
                pl.BlockSpec((1, lc, LANE), lambda n, h: (n, 0, COL_AK + h)),
                pl.BlockSpec((1, lc, LANE), lambda n, h: (n, 0, COL_AV + h)),
            ],
            out_specs=pl.BlockSpec((1, lc, gw), lambda n, h: (n, 0, h)),
            out_shape=jax.ShapeDtypeStruct((b, lc, N_HEADS * HEAD_DIM), BF16),
            compiler_params=_cparams("parallel", "parallel"),
            name="attn_a_ctx",
        )(sink, p_c, p_c, p_c)
    return o_c, o_l


def _attn_b_kernel(lam_ref, q_ref, *refs, nseg, post_scale):
    k_refs = refs[:nseg]
    v_refs = refs[nseg:2 * nseg]
    g_ref, o_ref = refs[2 * nseg], refs[2 * nseg + 1]
    q = q_ref[0]
    tq = q.shape[0]
    lane = lax.broadcasted_iota(jnp.int32, q.shape, 1)
    zero = jnp.zeros_like(q)
    q2 = jnp.concatenate([jnp.where(lane < B_DIM, q, zero), jnp.where(lane < B_DIM, zero, q)], axis=0)
    ss = [lax.dot_general(q2, k[0], NT_DIMS, preferred_element_type=F32) for k in k_refs]
    m = functools.reduce(jnp.maximum, [jnp.max(s, axis=-1, keepdims=True) for s in ss])
    ps = [jnp.exp(s - m) for s in ss]
    den = functools.reduce(jnp.add, [jnp.sum(p, axis=-1, keepdims=True) for p in ps])
    acc = functools.reduce(
        jnp.add, [jnp.dot(p.astype(BF16), v[0], preferred_element_type=F32) for p, v in zip(ps, v_refs)])
    o2 = acc / den
    o = o2[:tq] - lam_ref[0] * o2[tq:]
    ms = jnp.mean(o * o, axis=-1, keepdims=True)
    o_ref[0] = ((o * lax.rsqrt(ms + EPS) * g_ref[...]) * post_scale).astype(BF16)


def _attn_b_call(lam, q_src, k_srcs, subln, post_scale, name):
    b, rq, _ = q_src.shape
    tq = _tile(rq, 256)
    smem = pl.BlockSpec(memory_space=pltpu.SMEM)
    in_specs = [smem, pl.BlockSpec((1, tq, LANE), lambda n, h, i: (n, i, COL_BQ + h))]
    in_specs += [pl.BlockSpec((1, s.shape[1], LANE), lambda n, h, i: (n, 0, COL_BK + h)) for s in k_srcs]
    in_specs += [pl.BlockSpec((1, s.shape[1], LANE), lambda n, h, i: (n, 0, COL_BV + h)) for s in k_srcs]
    in_specs += [pl.BlockSpec((1, LANE), lambda n, h, i: (0, 0))]
    return pl.pallas_call(
        functools.partial(_attn_b_kernel, nseg=len(k_srcs), post_scale=post_scale),
        grid=(b, N_HEADS, rq // tq),
        in_specs=in_specs,
        out_specs=pl.BlockSpec((1, tq, LANE), lambda n, h, i: (n, i, h)),
        out_shape=jax.ShapeDtypeStruct((b, rq, N_HEADS * HEAD_DIM), BF16),
        compiler_params=_cparams("parallel", "parallel", "arbitrary"),
        name=name,
    )(lam, q_src, *k_srcs, *k_srcs, subln)


def _attn_b(lam, p_l, p_c, subln, post_scale, ctx_out):
    o_l = _attn_b_call(lam, p_l, [p_c, p_l], subln, post_scale, "attn_b_latent")
    o_c = _attn_b_call(lam, p_c, [p_c], subln, post_scale, "attn_b_ctx") if ctx_out else None
    return o_c, o_l


def _centred_conv(x_ref, pad_ref, w_ref, b_ref):
    n = x_ref.shape[1]
    zeros = jnp.zeros((SUBLANE, LANE), F32)
    pad_ref[0:SUBLANE, :] = zeros
    pad_ref[SUBLANE:SUBLANE + n, :] = x_ref[0]
    pad_ref[SUBLANE + n:2 * SUBLANE + n, :] = zeros
    out = b_ref[...]
    for tap in range(4):
        out = out + pad_ref[pl.ds(SUBLANE - 2 + tap, n), :] * w_ref[tap:tap + 1, :]
    return out


def _tile_scan(a, v, reverse):
    n = a.shape[0]
    row = lax.broadcasted_iota(jnp.int32, a.shape, 0) % SUBLANE
    for s in (1, 2, 4):
        if reverse:
            keep = row < SUBLANE - s
            a_sh = jnp.where(keep, pltpu.roll(a, n - s, 0), 1.0)
            v_sh = jnp.where(keep, pltpu.roll(v, n - s, 0), 0.0)
        else:
            keep = row >= s
            a_sh = jnp.where(keep, pltpu.roll(a, s, 0), 1.0)
            v_sh = jnp.where(keep, pltpu.roll(v, s, 0), 0.0)
        v = v + a * v_sh
        a = a * a_sh
    return a, v


def _seg_tile(t, ntc, ntl, reverse):
    if not reverse:
        return t
    return jnp.where(t < ntc, ntc - 1 - t, 2 * ntc + ntl - 1 - t)


def _lru_kernel(xc_ref, yc_ref, xl_ref, yl_ref, cw_ref, cb_ref, wr_ref, br_ref, wi_ref, bi_ref,
                lam_ref, *refs, ctx_out):
    if ctx_out:
        oc_ref, ol_ref, ac_ref, hl_ref, hs_ref, pad_ref = refs
    else:
        ol_ref, ac_ref, hl_ref, hs_ref, pad_ref = refs
    lc, seq = xc_ref.shape[1], xl_ref.shape[1]
    ntc, ntl = lc // SUBLANE, seq // SUBLANE
    for x_ref, off in ((xc_ref, 0), (xl_ref, lc)):
        n = x_ref.shape[1]
        u = _centred_conv(x_ref, pad_ref, cw_ref, cb_ref)
        ub = u.astype(BF16)
        for d in range(2):
            lam = lam_ref[d:d + 1, :]
            sp = jnp.maximum(-lam, 0.0) + jnp.log1p(jnp.exp(-jnp.abs(lam)))
            r = _sigmoid(jnp.dot(ub, wr_ref[d, 0], preferred_element_type=F32) + br_ref[d:d + 1, :])
            gi = _sigmoid(jnp.dot(ub, wi_ref[d, 0], preferred_element_type=F32) + bi_ref[d:d + 1, :])
            log_a = -LRU_C * r * sp
            a = jnp.exp(log_a)
            y = jnp.tanh(-log_a) * (1.0 + a * a)
            v = (y * lax.rsqrt(jnp.maximum(y, TINY))) * gi * u
            a_cum, h_loc = _tile_scan(a, v, reverse=(d == 1))
            ac_ref[d, off:off + n, :] = a_cum
            hl_ref[d, off:off + n, :] = h_loc

    def step(t, carry):
        new = []
        for d in range(2):
            tile = _seg_tile(t, ntc, ntl, d == 1)
            r0 = pl.multiple_of(tile * SUBLANE, SUBLANE)
            h = hl_ref[d, pl.ds(r0, SUBLANE), :] + ac_ref[d, pl.ds(r0, SUBLANE), :] * carry[d]
            if d == 0:
                hs_ref[pl.ds(r0, SUBLANE), :] = h
                new.append(h[SUBLANE - 1:SUBLANE, :])
            else:
                hl_ref[d, pl.ds(r0, SUBLANE), :] = h
                new.append(h[0:1, :])
        return tuple(new)

    zero = jnp.zeros((1, LANE), F32)
    lax.fori_loop(0, ntc + ntl, step, (zero, zero))
    if ctx_out:
        hc = hs_ref[0:lc, :] + hl_ref[1, 0:lc, :]
        oc_ref[0] = (hc * jax.nn.gelu(yc_ref[0])).astype(BF16)
    hl = hs_ref[lc:lc + seq, :] + hl_ref[1, lc:lc + seq, :]
    ol_ref[0] = (hl * jax.nn.gelu(yl_ref[0])).astype(BF16)


def _lru(proj_l, proj_c, conv_w, conv_b, w_r, b_r, w_i, b_i, lam, ctx_out):
    b, seq, _ = proj_l.shape
    lc = proj_c.shape[1]
    nblk = w_r.shape[1]
    width = nblk * LANE

    def seg(n, col):
        return pl.BlockSpec((1, n, LANE), lambda i, j: (i, 0, col + j))

    vec2 = pl.BlockSpec((2, LANE), lambda i, j: (0, j))
    wspec = pl.BlockSpec((2, 1, LANE, LANE), lambda i, j: (0, j, 0, 0))
    out_specs = [pl.BlockSpec((1, seq, LANE), lambda i, j: (i, 0, j))]
    out_shape = [jax.ShapeDtypeStruct((b, seq, width), BF16)]
    if ctx_out:
        out_specs.insert(0, pl.BlockSpec((1, lc, LANE), lambda i, j: (i, 0, j)))
        out_shape.insert(0, jax.ShapeDtypeStruct((b, lc, width), BF16))
    outs = pl.pallas_call(
        functools.partial(_lru_kernel, ctx_out=ctx_out),
        grid=(b, nblk),
        in_specs=[
            seg(lc, COL_CX), seg(lc, COL_CY), seg(seq, COL_CX), seg(seq, COL_CY),
            pl.BlockSpec((4, LANE), lambda i, j: (0, j)),
            pl.BlockSpec((1, LANE), lambda i, j: (0, j)),
            wspec, vec2, wspec, vec2, vec2,
        ],
        out_specs=out_specs,
        out_shape=out_shape,
        scratch_shapes=[
            pltpu.VMEM((2, lc + seq, LANE), F32),
            pltpu.VMEM((2, lc + seq, LANE), F32),
            pltpu.VMEM((lc + seq, LANE), F32),
            pltpu.VMEM((max(lc, seq) + 2 * SUBLANE, LANE), F32),
        ],
        compiler_params=_cparams("parallel", "parallel"),
        name="rglru",
    )(proj_c, proj_c, proj_l, proj_l, conv_w, conv_b.reshape(1, width), w_r, b_r, w_i, b_i, lam)
    return (outs[0], outs[1]) if ctx_out else (None, outs[0])


def _gla_level_masks(reverse):
    c = GLA_CHUNK
    row = lax.broadcasted_iota(jnp.int32, (c, c), 0)
    col = lax.broadcasted_iota(jnp.int32, (c, c), 1)
    masks = {}
    for s in (32, 16, 8):
        same = (row // (2 * s)) == (col // (2 * s))
        if reverse:
            masks[s] = same & ((row % (2 * s)) < s) & ((col % (2 * s)) >= s)
        else:
            masks[s] = same & ((row % (2 * s)) >= s) & ((col % (2 * s)) < s)
    tri = (row <= col) if reverse else (row >= col)
    return masks, tri.astype(F32)


def _gla_chunk(q, k, v, g, st, masks, tri, reverse):
    c = GLA_CHUNK
    b = jnp.dot(tri, g, precision=lax.Precision.HIGHEST, preferred_element_type=F32)
    att = jnp.zeros((c, c), F32)
    for s in (32, 16, 8):
        b3 = b.reshape(c // (2 * s), 2 * s, LANE)
        rr = s if reverse else s - 1
        rho = jnp.broadcast_to(b3[:, rr:rr + 1, :], b3.shape).reshape(c, LANE)
        e = jnp.exp(-jnp.abs(b - rho))
        a = lax.dot_general((q * e).astype(BF16), (k * e).astype(BF16), NT_DIMS,
                            preferred_element_type=F32)
        att = att + jnp.where(masks[s], a, 0.0)
    vb = v.astype(BF16)
    o = jnp.dot(att.astype(BF16), vb, preferred_element_type=F32)
    nb = c // SUBLANE
    b3 = b.reshape(nb, SUBLANE, LANE)
    q3 = q.reshape(nb, SUBLANE, LANE)
    k3 = k.reshape(nb, SUBLANE, LANE)
    v3 = v.reshape(nb, SUBLANE, LANE)
    rowi = lax.broadcasted_iota(jnp.int32, b3.shape, 1)
    od = jnp.zeros(b3.shape, F32)
    for jj in range(SUBLANE):
        keep = (rowi <= jj) if reverse else (rowi >= jj)
        e = jnp.where(keep, jnp.exp(jnp.minimum(b3 - b3[:, jj:jj + 1, :], 0.0)), 0.0)
        sj = jnp.sum(q3 * e * k3[:, jj:jj + 1, :], axis=-1, keepdims=True)
        od = od + sj * v3[:, jj:jj + 1, :]
    o = o + od.reshape(c, LANE)
    o = o + lax.dot_general((q * jnp.exp(b)).astype(BF16), st.astype(BF16), NT_DIMS,
                            preferred_element_type=F32)
    b_end = b[0:1, :] if reverse else b[c - 1:c, :]
    khat = (k * jnp.exp(b_end - b)).astype(BF16)
    st_new = st * jnp.exp(b_end) + lax.dot_general(vb, khat, TN_DIMS, preferred_element_type=F32)
    return o, st_new


def _gla_kernel(lb_ref, on_ref, qc_ref, ffc_ref, fbc_ref, ic_ref, gc_ref,
                ql_ref, ffl_ref, fbl_ref, il_ref, gl_ref, *refs, ctx_out):
    if ctx_out:
        oc_ref, ol_ref, q_s, v_s, g_s, k_s, o_s, st_s = refs
    else:
        ol_ref, q_s, v_s, g_s, k_s, o_s, st_s = refs
    lc, seq = qc_ref.shape[1], ql_ref.shape[1]
    ncc, ncl = lc // GLA_CHUNK, seq // GLA_CHUNK
    for off, n, q_ref, i_ref, f_refs in ((0, lc, qc_ref, ic_ref, (ffc_ref, fbc_ref)),
                                         (lc, seq, ql_ref, il_ref, (ffl_ref, fbl_ref))):
        q_s[off:off + n, :] = q_ref[0]
        v_s[off:off + n, :] = i_ref[0]
        for d in range(2):
            z = f_refs[d][0]
            lbd = lb_ref[d:d + 1, :]
            ez = jnp.exp(-jnp.abs(z))
            r = 1.0 / (1.0 + ez)
            pos = z >= 0.0
            sig_p = jnp.where(pos, r, ez * r)
            sig_n = jnp.where(pos, ez * r, r)
            g_s[d, off:off + n, :] = jnp.log(lbd + (1.0 - lbd) * sig_p)
            k_s[d, off:off + n, :] = (1.0 - lbd) * sig_n
    st_s[...] = jnp.zeros(st_s.shape, F32)
    consts = [_gla_level_masks(False), _gla_level_masks(True)]

    def step(cidx, carry):
        for d in range(2):
            chunk = _seg_tile(cidx, ncc, ncl, d == 1)
            r0 = pl.multiple_of(chunk * GLA_CHUNK, GLA_CHUNK)
            rows = pl.ds(r0, GLA_CHUNK)
            o, st_new = _gla_chunk(q_s[rows, :], k_s[d, rows, :], v_s[rows, :], g_s[d, rows, :],
                                   st_s[d], consts[d][0], consts[d][1], d == 1)
            st_s[d] = st_new
            o_s[d, rows, :] = o
        return carry

    lax.fori_loop(0, ncc + ncl, step, 0, unroll=4)

    def finish(o, gate):
        ms = jnp.mean(o * o, axis=-1, keepdims=True)
        return ((o * lax.rsqrt(ms + EPS) * on_ref[...]) * (gate * _sigmoid(gate))).astype(BF16)

    if ctx_out:
        oc_ref[0] = finish(o_s[0, 0:lc, :] + o_s[1, 0:lc, :], gc_ref[0])
    ol_ref[0] = finish(o_s[0, lc:lc + seq, :] + o_s[1, lc:lc + seq, :], gl_ref[0])


def _gla(proj_l, proj_c, lb, onorm, ctx_out):
    b, seq, _ = proj_l.shape
    lc = proj_c.shape[1]
    width = N_HEADS * HEAD_DIM
    nt = lc + seq

    def seg(n, col):
        return pl.BlockSpec((1, n, LANE), lambda i, h: (i, 0, col + h))

    cols = (COL_DQ, COL_DFF, COL_DFB, COL_DI, COL_DG)
    out_specs = [pl.BlockSpec((1, seq, LANE), lambda i, h: (i, 0, h))]
    out_shape = [jax.ShapeDtypeStruct((b, seq, width), BF16)]
    if ctx_out:
        out_specs.insert(0, pl.BlockSpec((1, lc, LANE), lambda i, h: (i, 0, h)))
        out_shape.insert(0, jax.ShapeDtypeStruct((b, lc, width), BF16))
    outs = pl.pallas_call(
        functools.partial(_gla_kernel, ctx_out=ctx_out),
        grid=(b, N_HEADS),
        in_specs=[pl.BlockSpec((2, LANE), lambda i, h: (0, h)),
                  pl.BlockSpec((1, LANE), lambda i, h: (0, 0))]
        + [seg(lc, c) for c in cols] + [seg(seq, c) for c in cols],
        out_specs=out_specs,
        out_shape=out_shape,
        scratch_shapes=[
            pltpu.VMEM((nt, LANE), F32),
            pltpu.VMEM((nt, LANE), F32),
            pltpu.VMEM((2, nt, LANE), F32),
            pltpu.VMEM((2, nt, LANE), F32),
            pltpu.VMEM((2, nt, LANE), F32),
            pltpu.VMEM((2, LANE, LANE), F32),
        ],
        compiler_params=_cparams("parallel", "parallel"),
        name="hgrn2",
    )(lb, onorm, *([proj_c] * 5), *([proj_l] * 5))
    return (outs[0], outs[1]) if ctx_out else (None, outs[0])


def _merge_kernel(oa_ref, ob_ref, oc_ref, od_ref, g0_ref, g1_ref, g2_ref, g3_ref,
                  w0_ref, w1_ref, w2_ref, w3_ref, y_ref):
    acc = None
    for o_ref, g_ref, w_ref in ((oa_ref, g0_ref, w0_ref), (ob_ref, g1_ref, w1_ref),
                                (oc_ref, g2_ref, w2_ref), (od_ref, g3_ref, w3_ref)):
        t = _sigmoid(g_ref[0]) * jnp.dot(o_ref[0], w_ref[0], preferred_element_type=F32)
        acc = t if acc is None else acc + t
    y_ref[0] = acc.astype(BF16)


def _merge(outs, proj, w_branch):
    nb, r, mw = outs[0].shape
    d = w_branch.shape[2]
    tm = _tile(r, 512)
    tn = 512
    nj = d // tn
    o_spec = pl.BlockSpec((1, tm, mw), lambda b, t, j: (b, t, 0))
    g_specs = [pl.BlockSpec((1, tm, tn), functools.partial(
        lambda b, t, j, n: (b, t, (COL_GATE * LANE) // tn + n * nj + j), n=n)) for n in range(4)]
    w_specs = [pl.BlockSpec((1, mw, tn), functools.partial(lambda b, t, j, n: (n, 0, j), n=n))
               for n in range(4)]
    return pl.pallas_call(
        _merge_kernel,
        grid=(nb, r // tm, nj),
        in_specs=[o_spec] * 4 + g_specs + w_specs,
        out_specs=pl.BlockSpec((1, tm, tn), lambda b, t, j: (b, t, j)),
        out_shape=jax.ShapeDtypeStruct((nb, r, d), BF16),
        compiler_params=_cparams("parallel", "parallel", "arbitrary"),
        name="branch_merge",
    )(*outs, *([proj] * 4), *([w_branch] * 4))


def _out_proj_kernel(y_ref, w_ref, x_ref, g_ref, o_ref):
    o_ref[0] = x_ref[0] + g_ref[0] * jnp.dot(y_ref[0], w_ref[...], preferred_element_type=F32)


def _out_proj(y, w, x, gate):
    nb, r, d = x.shape
    tm = _tile(r, 1024)
    tn = 512
    return pl.pallas_call(
        _out_proj_kernel,
        grid=(nb, r // tm, d // tn),
        in_specs=[
            pl.BlockSpec((1, tm, d), lambda b, t, j: (b, t, 0)),
            pl.BlockSpec((d, tn), lambda b, t, j: (0, j)),
            pl.BlockSpec((1, tm, tn), lambda b, t, j: (b, t, j)),
            pl.BlockSpec((1, 1, tn), lambda b, t, j: (b, 0, j)),
        ],
        out_specs=pl.BlockSpec((1, tm, tn), lambda b, t, j: (b, t, j)),
        out_shape=jax.ShapeDtypeStruct((nb, r, d), F32),
        compiler_params=_cparams("parallel", "parallel", "arbitrary"),
        name="out_proj",
    )(y, w, x, gate)


def _route(logits_t, bias_col):
    aff = _sigmoid(logits_t)
    sel = aff + bias_col
    aff_r = [aff[e:e + 1, :] for e in range(N_EXPERTS)]
    sel_r = [sel[e:e + 1, :] for e in range(N_EXPERTS)]
    scores = []
    for g in range(N_GROUPS):
        v = sel_r[g * EXPERTS_PER_GROUP:(g + 1) * EXPERTS_PER_GROUP]
        m1 = functools.reduce(jnp.maximum, v)
        taken = jnp.zeros(m1.shape, jnp.bool_)
        second = jnp.full(m1.shape, -jnp.inf, F32)
        for x in v:
            first = (x == m1) & jnp.logical_not(taken)
            taken = taken | first
            second = jnp.where(first, second, jnp.maximum(second, x))
        scores.append(m1 + second)
    best, gidx = scores[0], jnp.zeros(scores[0].shape, jnp.int32)
    for g in range(1, N_GROUPS):
        better = scores[g] > best
        gidx = jnp.where(better, g, gidx)
        best = jnp.where(better, scores[g], best)
    masked = [jnp.where(gidx == e // EXPERTS_PER_GROUP, sel_r[e], -jnp.inf) for e in range(N_EXPERTS)]

    def first_argmax(vals, exclude):
        bv = jnp.full(vals[0].shape, -jnp.inf, F32)
        bi = jnp.full(vals[0].shape, -1, jnp.int32)
        for e, x in enumerate(vals):
            better = x > bv
            if exclude is not None:
                better = better & (exclude != e)
            bi = jnp.where(better, e, bi)
            bv = jnp.where(better, x, bv)
        return bi

    i1 = first_argmax(masked, None)
    i2 = first_argmax(masked, i1)
    w1 = functools.reduce(jnp.add, [jnp.where(i1 == e, aff_r[e], 0.0) for e in range(N_EXPERTS)])
    w2 = functools.reduce(jnp.add, [jnp.where(i2 == e, aff_r[e], 0.0) for e in range(N_EXPERTS)])
    tot = w1 + w2
    g1, g2 = w1 / tot, w2 / tot
    width = logits_t.shape[1]
    rowi = lax.broadcasted_iota(jnp.int32, (SUBLANE, width), 0)
    local1 = jnp.broadcast_to(i1 - gidx * EXPERTS_PER_GROUP, (SUBLANE, width))
    local2 = jnp.broadcast_to(i2 - gidx * EXPERTS_PER_GROUP, (SUBLANE, width))
    info = (jnp.where(rowi == local1, jnp.broadcast_to(g1, (SUBLANE, width)), 0.0)
            + jnp.where(rowi == local2, jnp.broadcast_to(g2, (SUBLANE, width)), 0.0))
    return jnp.where(rowi == EXPERTS_PER_GROUP,
                     jnp.broadcast_to(gidx.astype(F32), (SUBLANE, width)), info)


def _moe_route_kernel(x_ref, g_ref, sh_ref, sc_ref, wrt_ref, br_ref, hn_ref, info_ref):
    h = _modulated_norm(x_ref[0], g_ref[...], sh_ref[0], sc_ref[0])
    hn_ref[0] = h
    logits_t = lax.dot_general(wrt_ref[...], h, NT_DIMS, precision=lax.Precision.HIGHEST,
                               preferred_element_type=F32)
    info_ref[0] = _route(logits_t, br_ref[...])


def _moe_route(x, g, sh, sc, w_router_t, b_router):
    nb, r, d = x.shape
    tm = _tile(r, 512)
    vec = pl.BlockSpec((1, 1, d), lambda b, t: (b, 0, 0))
    return pl.pallas_call(
        _moe_route_kernel,
        grid=(nb, r // tm),
        in_specs=[
            pl.BlockSpec((1, tm, d), lambda b, t: (b, t, 0)),
            pl.BlockSpec((1, d), lambda b, t: (0, 0)),
            vec, vec,
            pl.BlockSpec((N_EXPERTS, d), lambda b, t: (0, 0)),
            pl.BlockSpec((N_EXPERTS, 1), lambda b, t: (0, 0)),
        ],
        out_specs=[pl.BlockSpec((1, tm, d), lambda b, t: (b, t, 0)),
                   pl.BlockSpec((1, SUBLANE, tm), lambda b, t: (b, 0, t))],
        out_shape=[jax.ShapeDtypeStruct((nb, r, d), F32),
                   jax.ShapeDtypeStruct((nb, SUBLANE, r), F32)],
        compiler_params=_cparams("parallel", "parallel"),
        name="moe_route",
    )(x, g, sh, sc, w_router_t, b_router.reshape(N_EXPERTS, 1))


def _row_copy(src, src_row, dst, dst_row, sem):
    return pltpu.make_async_copy(src.at[pl.ds(src_row, 1), :], dst.at[pl.ds(dst_row, 1), :], sem)


def _moe_experts_kernel(src_ref, tgrp_ref, nval_ref, hn_hbm, gates_ref, wg_ref, wu_ref, wd_ref,
                        y_hbm, xbuf, hb, acc, gsem, ssem, *, tm):
    i = pl.program_id(0)
    e = pl.program_id(1)
    nt = pl.num_programs(0)
    slot = i % 2
    per = tm // EXPERTS_PER_GROUP

    def wait_scatter(tile, sl):
        def body(r, c):
            _row_copy(acc.at[sl], 0, y_hbm, 0, ssem.at[sl]).wait()
            return c
        lax.fori_loop(0, nval_ref[tile], body, 0)

    @pl.when((i == 0) & (e == 0))
    def _():
        def body(r, c):
            _row_copy(hn_hbm, src_ref[r], xbuf.at[0], r, gsem.at[0]).start()
            return c
        lax.fori_loop(0, tm, body, 0)

    @pl.when(e == 0)
    def _():
        pltpu.make_async_copy(hn_hbm.at[pl.ds(0, tm), :], xbuf.at[slot], gsem.at[slot]).wait()
        hb[...] = xbuf[slot].astype(BF16)

        @pl.when(i >= 2)
        def _():
            wait_scatter(i - 2, slot)
        acc[slot] = jnp.zeros(acc.shape[1:], F32)

    @pl.when(i + 1 < nt)
    def _():
        base = (i + 1) * tm + e * per
        for k in range(per):
            _row_copy(hn_hbm, src_ref[base + k], xbuf.at[1 - slot], e * per + k, gsem.at[1 - slot]).start()

    @pl.when(nval_ref[i] > 0)
    def _():
        hv = hb[...]
        a = jnp.dot(hv, wg_ref[0], preferred_element_type=F32)
        u = jnp.dot(hv, wu_ref[0], preferred_element_type=F32)
        hid = ((a * _sigmoid(a)) * u).astype(BF16)
        y = jnp.dot(hid, wd_ref[0], preferred_element_type=F32)
        lane = lax.broadcasted_iota(jnp.int32, gates_ref.shape, 1)
        col = jnp.sum(jnp.where(lane == e, gates_ref[...], 0.0), axis=-1, keepdims=True)
        acc[slot] = acc[slot] + col * y

    @pl.when(e == EXPERTS_PER_GROUP - 1)
    def _():
        def body(r, c):
            _row_copy(acc.at[slot], r, y_hbm, src_ref[i * tm + r], ssem.at[slot]).start()
            return c
        lax.fori_loop(0, nval_ref[i], body, 0)

        @pl.when(i == nt - 1)
        def _():
            wait_scatter(i, slot)

            @pl.when(i >= 1)
            def _():
                wait_scatter(i - 1, 1 - slot)


def _moe_experts(hn, src, tile_group, tile_nvalid, gates_sorted, w_gate, w_up, w_down, tm):
    t, d = hn.shape
    dff = w_gate.shape[2]
    ntiles = tile_group.shape[0]

    def wmap(i, e, src_r, tgrp_r, nval_r):
        return (tgrp_r[i] * EXPERTS_PER_GROUP + e, 0, 0)

    grid_spec = pltpu.PrefetchScalarGridSpec(
        num_scalar_prefetch=3,
        grid=(ntiles, EXPERTS_PER_GROUP),
        in_specs=[
            pl.BlockSpec(memory_space=pl.ANY),
            pl.BlockSpec((tm, EXPERTS_PER_GROUP), lambda i, e, *_: (i, 0)),
            pl.BlockSpec((1, d, dff), wmap),
            pl.BlockSpec((1, d, dff), wmap),
            pl.BlockSpec((1, dff, d), wmap),
        ],
        out_specs=pl.BlockSpec(memory_space=pl.ANY),
        scratch_shapes=[
            pltpu.VMEM((2, tm, d), F32),
            pltpu.VMEM((tm, d), BF16),
            pltpu.VMEM((2, tm, d), F32),
            pltpu.SemaphoreType.DMA((2,)),
            pltpu.SemaphoreType.DMA((2,)),
        ],
    )
    return pl.pallas_call(
        functools.partial(_moe_experts_kernel, tm=tm),
        grid_spec=grid_spec,
        out_shape=jax.ShapeDtypeStruct((t, d), F32),
        compiler_params=_cparams("arbitrary", "arbitrary"),
        name="moe_experts",
    )(src, tile_group, tile_nvalid, hn, gates_sorted, w_gate, w_up, w_down)


def _residual_kernel(x_ref, y_ref, g_ref, o_ref):
    o_ref[0] = x_ref[0] + g_ref[0] * y_ref[0]


def _residual(x, y, gate):
    nb, r, d = x.shape
    tm = _tile(r, 512)
    blk = pl.BlockSpec((1, tm, d), lambda b, t: (b, t, 0))
    return pl.pallas_call(
        _residual_kernel,
        grid=(nb, r // tm),
        in_specs=[blk, blk, pl.BlockSpec((1, 1, d), lambda b, t: (b, 0, 0))],
        out_specs=blk,
        out_shape=jax.ShapeDtypeStruct((nb, r, d), F32),
        compiler_params=_cparams("parallel", "parallel"),
        name="moe_residual",
    )(x, y, gate)


def _group_layout(grp, gates4, tm):
    t = grp.shape[0]
    ntiles = t // tm + N_GROUPS
    onehot = (grp[:, None] == jnp.arange(N_GROUPS)[None, :]).astype(jnp.int32)
    counts = jnp.sum(onehot, axis=0)
    rank = jnp.sum((jnp.cumsum(onehot, axis=0) - onehot) * onehot, axis=1)
    padded = ((counts + tm - 1) // tm) * tm
    start = jnp.cumsum(padded) - padded
    slot_of_token = start[grp] + rank
    nslots = ntiles * tm
    src = jnp.zeros((nslots,), jnp.int32).at[slot_of_token].set(jnp.arange(t, dtype=jnp.int32))
    gates_sorted = jnp.zeros((nslots, EXPERTS_PER_GROUP), F32).at[slot_of_token].set(gates4)
    tile_start = jnp.arange(ntiles, dtype=jnp.int32) * tm
    ends = jnp.cumsum(padded)
    tile_group = jnp.minimum(jnp.sum((tile_start[:, None] >= ends[None, :]).astype(jnp.int32), axis=1),
                             N_GROUPS - 1).astype(jnp.int32)
    tile_nvalid = jnp.clip(counts[tile_group] - (tile_start - start[tile_group]), 0, tm).astype(jnp.int32)
    return src, tile_group, tile_nvalid, gates_sorted


def _moe(x, g, sh, sc, gate, w_router_t, b_router, w_gate, w_up, w_down):
    nb, r, d = x.shape
    t = nb * r
    tm = 512 if t >= 8192 else (256 if t >= 2048 else 128)
    hn, info = _moe_route(x, g, sh, sc, w_router_t, b_router)
    grp = info[:, EXPERTS_PER_GROUP, :].reshape(t).astype(jnp.int32)
    gates4 = jnp.swapaxes(info[:, :EXPERTS_PER_GROUP, :], 1, 2).reshape(t, EXPERTS_PER_GROUP)
    src, tile_group, tile_nvalid, gates_sorted = _group_layout(grp, gates4, tm)
    y = _moe_experts(hn.reshape(t, d), src, tile_group, tile_nvalid, gates_sorted,
                     w_gate, w_up, w_down, tm)
    return _residual(x, y.reshape(nb, r, d), gate)


def kernel(x, c, ctx, c_ctx, w_ada, b_ada, norm1_g, norm2_g, w_in, qn_a, kn_a, sink_a, qn_b, kn_b,
           lam_b, subln_b, conv_w, conv_b, w_rg, b_rg, w_ig, b_ig, lru_lambda, lb_d, onorm_d,
           w_branch, w_out, w_router, b_router, w_gate, w_up, w_down):
    bsz, seq, d = x.shape
    lc = ctx.shape[1]
    depth = w_in.shape[0]

    n_rows = -(-(bsz + 1) // SUBLANE) * SUBLANE
    c_pad = jnp.zeros((n_rows, d), F32).at[:bsz].set(c).at[bsz].set(c_ctx)
    mod = _ada(c_pad, w_ada, b_ada).reshape(depth, n_rows, 6, d)

    lb_w = jax.nn.softmax(lb_d.astype(F32), axis=0)
    lb_all = jnp.cumsum(lb_w, axis=0) - lb_w[0:1]
    rope_tabs = _rope_tables(seq)
    w_router_t = w_router.T

    xl = x
    xc = ctx.reshape(1, bsz * lc, d)
    for l in range(depth):
        ctx_out = l < depth - 1
        mod_l = [mod[l, :bsz, k][:, None, :] for k in range(6)]
        mod_c = [mod[l, bsz:bsz + 1, k][:, None, :] for k in range(6)]
        w_in_l = w_in[l].astype(BF16)
        g1 = norm1_g[l].reshape(1, d)
        g2 = norm2_g[l].reshape(1, d)

        proj_l = _norm_proj(xl, g1, mod_l[0], mod_l[1], w_in_l)
        proj_c = _norm_proj(xc, g1, mod_c[0], mod_c[1], w_in_l).reshape(bsz, lc, -1)

        qa = qn_a[l] * (HEAD_DIM ** -0.5)
        qb = jnp.tile(qn_b[l], 2) * (B_DIM ** -0.5)
        kb = jnp.tile(kn_b[l], 2)
        one = jnp.ones((LANE,), F32)
        gains = jnp.stack([qa] * 8 + [kn_a[l]] * 2 + [one] * 2 + [qb] * 8 + [kb] * 8 + [one] * 8)
        gains = gains.reshape(N_QKV_BLOCKS, 1, LANE)
        p_l = _prep(proj_l, gains, rope_tabs)
        p_c = _prep(proj_c, gains, None)

        oa_c, oa_l = _attn_a(sink_a[l], p_l, p_c, ctx_out)

        lq1, lk1, lq2, lk2 = lam_b[l].astype(F32)
        lam_init = 0.8 - 0.6 * math.exp(-0.3 * l)
        lam = (jnp.exp(jnp.sum(lq1 * lk1)) - jnp.exp(jnp.sum(lq2 * lk2)) + lam_init).reshape(1)
        ob_c, ob_l = _attn_b(lam, p_l, p_c, subln_b[l].reshape(1, LANE), 1.0 - lam_init, ctx_out)

        oc_c, oc_l = _lru(proj_l, proj_c, conv_w[l], conv_b[l], w_rg[l].astype(BF16), b_rg[l],
                          w_ig[l].astype(BF16), b_ig[l], lru_lambda[l], ctx_out)
        od_c, od_l = _gla(proj_l, proj_c, lb_all[l], onorm_d[l].reshape(1, LANE), ctx_out)

        w_branch_l = w_branch[l].astype(BF16)
        w_out_l = w_out[l].astype(BF16)
        moe_w = (w_router_t, b_router, w_gate[l].astype(BF16), w_up[l].astype(BF16),
                 w_down[l].astype(BF16))

        y_l = _merge((oa_l, ob_l, oc_l, od_l), proj_l, w_branch_l)
        xl = _out_proj(y_l, w_out_l, xl, mod_l[2])
        xl = _moe(xl, g2, mod_l[3], mod_l[4], mod_l[5], *moe_w)
        if ctx_out:
            flat = lambda t: t.reshape(1, bsz * lc, -1)
            y_c = _merge(tuple(flat(t) for t in (oa_c, ob_c, oc_c, od_c)), flat(proj_c), w_branch_l)
            xc = _out_proj(y_c, w_out_l, xc, mod_c[2])
            xc = _moe(xc, g2, mod_c[3], mod_c[4], mod_c[5], *moe_w)
    return xl
```

```python
import functools
import math

import jax
import jax.numpy as jnp
from jax import lax
from jax.experimental import pallas as pl
from jax.experimental.pallas import tpu as pltpu

F32 = jnp.float32
BF16 = jnp.bfloat16

LANE = 128
SUBLANE = 8
VMEM_LIMIT_BYTES = 56 * 1024 * 1024

EPS = 1e-6
ROPE_THETA = 10000.0
GRID_W = 64
HEAD_DIM = 128
WINDOW = 128
Q_BLOCK = 128
N_HEADS = 8
A_KV_HEADS = 2
A_GROUP = N_HEADS // A_KV_HEADS
B_DIM = 64
LRU_C = 8.0
GLA_CHUNK = 64
GLA_SAFE_DECAY = 80.0
N_EXPERTS = 16
N_GROUPS = 4
EXPERTS_PER_GROUP = N_EXPERTS // N_GROUPS
NEG_BIG = -1e30
TINY = 1e-37

COL_AQ, COL_AK, COL_AV = 0, 8, 10
COL_BQ, COL_BK, COL_BV = 12, 20, 28
COL_CX, COL_CY = 36, 44
COL_DQ, COL_DFF, COL_DFB, COL_DI, COL_DG = 52, 60, 68, 76, 84
COL_GATE = 92
N_QKV_BLOCKS = 36

NT_DIMS = (((1,), (1,)), ((), ()))
TN_DIMS = (((0,), (0,)), ((), ()))


def _cparams(*sem):
    return pltpu.CompilerParams(dimension_semantics=sem, vmem_limit_bytes=VMEM_LIMIT_BYTES)


def _tile(n, pref):
    t = min(n, pref)
    while n % t:
        t //= 2
    return t


def _sigmoid(x):
    return jax.nn.sigmoid(x)


def _modulated_norm(x, g, sh, sc):
    ms = jnp.mean(x * x, axis=-1, keepdims=True)
    return (x * lax.rsqrt(ms + EPS) * g) * (1.0 + sc) + sh


def _ada_kernel(c_ref, w_ref, b_ref, o_ref):
    c = c_ref[...]
    s = (c * _sigmoid(c)).astype(BF16)
    o_ref[0] = jnp.dot(s, w_ref[0].astype(BF16), preferred_element_type=F32) + b_ref[0]


def _ada(c_pad, w_ada, b_ada):
    nl, d, n = w_ada.shape
    rows = c_pad.shape[0]
    tn = _tile(n, 1024)
    return pl.pallas_call(
        _ada_kernel,
        grid=(nl, n // tn),
        in_specs=[
            pl.BlockSpec((rows, d), lambda l, j: (0, 0)),
            pl.BlockSpec((1, d, tn), lambda l, j: (l, 0, j)),
            pl.BlockSpec((1, 1, tn), lambda l, j: (l, 0, j)),
        ],
        out_specs=pl.BlockSpec((1, rows, tn), lambda l, j: (l, 0, j)),
        out_shape=jax.ShapeDtypeStruct((nl, rows, n), F32),
        compiler_params=_cparams("parallel", "parallel"),
        name="ada_mod",
    )(c_pad, w_ada, b_ada.reshape(nl, 1, n))


def _norm_proj_kernel(x_ref, g_ref, sh_ref, sc_ref, w_ref, o_ref, hn_ref):
    @pl.when(pl.program_id(2) == 0)
    def _():
        hn_ref[...] = _modulated_norm(x_ref[0], g_ref[...], sh_ref[0], sc_ref[0]).astype(BF16)

    o_ref[0] = jnp.dot(hn_ref[...], w_ref[...], preferred_element_type=F32)


def _norm_proj(x, g, sh, sc, w):
    nb, r, d = x.shape
    n = w.shape[1]
    tm = _tile(r, 1024)
    tn = _tile(n, 512)
    return pl.pallas_call(
        _norm_proj_kernel,
        grid=(nb, r // tm, n // tn),
        in_specs=[
            pl.BlockSpec((1, tm, d), lambda b, t, j: (b, t, 0)),
            pl.BlockSpec((1, d), lambda b, t, j: (0, 0)),
            pl.BlockSpec((1, 1, d), lambda b, t, j: (b, 0, 0)),
            pl.BlockSpec((1, 1, d), lambda b, t, j: (b, 0, 0)),
            pl.BlockSpec((d, tn), lambda b, t, j: (0, j)),
        ],
        out_specs=pl.BlockSpec((1, tm, tn), lambda b, t, j: (b, t, j)),
        out_shape=jax.ShapeDtypeStruct((nb, r, n), F32),
        scratch_shapes=[pltpu.VMEM((tm, d), BF16)],
        compiler_params=_cparams("parallel", "parallel", "arbitrary"),
        name="norm_proj",
    )(x, g, sh, sc, w)


def _rope_partner(y, lane, half):
    return jnp.where((lane % (2 * half)) < half,
                     pltpu.roll(y, LANE - half, 1), pltpu.roll(y, half, 1))


def _prep_kernel(*refs, rope):
    if rope:
        x_ref, g_ref, ca_ref, sa_ref, cb_ref, sb_ref, o_ref = refs
    else:
        x_ref, g_ref, o_ref = refs
    lane = lax.broadcasted_iota(jnp.int32, (x_ref.shape[1], LANE), 1)
    lo = lane < B_DIM
    for j in range(N_QKV_BLOCKS):
        cols = slice(j * LANE, (j + 1) * LANE)
        x = x_ref[0, :, cols]
        if j < COL_AV:
            ms = jnp.mean(x * x, axis=-1, keepdims=True)
            y = x * lax.rsqrt(ms + EPS) * g_ref[j]
            if rope:
                y = y * ca_ref[...] + _rope_partner(y, lane, 32) * sa_ref[...]
        elif COL_BQ <= j < COL_BV:
            x2 = x * x
            s_lo = jnp.sum(jnp.where(lo, x2, 0.0), axis=-1, keepdims=True)
            s_hi = jnp.sum(jnp.where(lo, 0.0, x2), axis=-1, keepdims=True)
            inv = jnp.where(lo, lax.rsqrt(s_lo / B_DIM + EPS), lax.rsqrt(s_hi / B_DIM + EPS))
            y = x * inv * g_ref[j]
            if rope:
                y = y * cb_ref[...] + _rope_partner(y, lane, 16) * sb_ref[...]
        else:
            y = x
        o_ref[0, :, cols] = y.astype(BF16)


def _prep(proj, gains, rope_tabs):
    nb, r, _ = proj.shape
    tm = _tile(r, 256)
    width = N_QKV_BLOCKS * LANE
    rope = rope_tabs is not None
    in_specs = [
        pl.BlockSpec((1, tm, width), lambda b, t: (b, t, 0)),
        pl.BlockSpec((N_QKV_BLOCKS, 1, LANE), lambda b, t: (0, 0, 0)),
    ]
    args = [proj, gains]
    if rope:
        in_specs += [pl.BlockSpec((tm, LANE), lambda b, t: (t, 0))] * 4
        args += list(rope_tabs)
    return pl.pallas_call(
        functools.partial(_prep_kernel, rope=rope),
        grid=(nb, r // tm),
        in_specs=in_specs,
        out_specs=pl.BlockSpec((1, tm, width), lambda b, t: (b, t, 0)),
        out_shape=jax.ShapeDtypeStruct((nb, r, width), BF16),
        compiler_params=_cparams("parallel", "parallel"),
        name="qkv_prep",
    )(*args)


def _rope_tables(seq):
    pos = jnp.arange(seq)
    rows = (pos // GRID_W).astype(F32)[:, None]
    cols = (pos % GRID_W).astype(F32)[:, None]
    lane = jnp.arange(LANE)

    def tables(half):
        inv = ROPE_THETA ** (-(lane % half).astype(F32) / half)
        use_rows = (lane % (4 * half)) < 2 * half
        ang = jnp.where(use_rows[None, :], rows, cols) * inv[None, :]
        sign = jnp.where((lane % (2 * half)) < half, -1.0, 1.0)
        return jnp.cos(ang), jnp.sin(ang) * sign[None, :]

    ca, sa = tables(32)
    cb, sb = tables(16)
    return ca, sa, cb, sb


def _stack_heads(q):
    return jnp.concatenate([q[:, g * HEAD_DIM:(g + 1) * HEAD_DIM] for g in range(A_GROUP)], axis=0)


def _unstack_heads(o, rows):
    return jnp.concatenate([o[g * rows:(g + 1) * rows] for g in range(A_GROUP)], axis=1)


def _sink_column(sink_ref, kvh, rows):
    return jnp.concatenate(
        [jnp.full((rows, 1), sink_ref[kvh * A_GROUP + g], F32) for g in range(A_GROUP)], axis=0)


def _attn_a_lat_kernel(sink_ref, q_ref, kl_ref, vl_ref, kc_ref, vc_ref, o_ref, *, seq):
    kvh = pl.program_id(1)
    i = pl.program_id(2)
    band = 3 * Q_BLOCK
    qs = _stack_heads(q_ref[0])
    start = pl.multiple_of(jnp.clip((i - 1) * Q_BLOCK, 0, seq - band), Q_BLOCK)
    kb = kl_ref[0, pl.ds(start, band), :]
    vb = vl_ref[0, pl.ds(start, band), :]
    s_loc = lax.dot_general(qs, kb, NT_DIMS, preferred_element_type=F32)
    s_ctx = lax.dot_general(qs, kc_ref[0], NT_DIMS, preferred_element_type=F32)
    row = lax.broadcasted_iota(jnp.int32, s_loc.shape, 0)
    col = lax.broadcasted_iota(jnp.int32, s_loc.shape, 1)
    qpos = i * Q_BLOCK + (row % Q_BLOCK)
    kpos = start + col
    s_loc = jnp.where(jnp.abs(kpos - qpos) <= WINDOW, s_loc, NEG_BIG)
    sk = _sink_column(sink_ref, kvh, Q_BLOCK)
    m = jnp.maximum(jnp.maximum(jnp.max(s_loc, axis=-1, keepdims=True),
                                jnp.max(s_ctx, axis=-1, keepdims=True)), sk)
    p_loc = jnp.exp(s_loc - m)
    p_ctx = jnp.exp(s_ctx - m)
    den = (jnp.sum(p_loc, axis=-1, keepdims=True) + jnp.sum(p_ctx, axis=-1, keepdims=True)
           + jnp.exp(sk - m))
    o = (jnp.dot(p_loc.astype(BF16), vb, preferred_element_type=F32)
         + jnp.dot(p_ctx.astype(BF16), vc_ref[0], preferred_element_type=F32)) / den
    o_ref[0] = _unstack_heads(o, Q_BLOCK).astype(BF16)


def _attn_a_ctx_kernel(sink_ref, q_ref, kc_ref, vc_ref, o_ref):
    kvh = pl.program_id(1)
    rows = q_ref.shape[1]
    qs = _stack_heads(q_ref[0])
    s = lax.dot_general(qs, kc_ref[0], NT_DIMS, preferred_element_type=F32)
    sk = _sink_column(sink_ref, kvh, rows)
    m = jnp.maximum(jnp.max(s, axis=-1, keepdims=True), sk)
    p = jnp.exp(s - m)
    den = jnp.sum(p, axis=-1, keepdims=True) + jnp.exp(sk - m)
    o = jnp.dot(p.astype(BF16), vc_ref[0], preferred_element_type=F32) / den
    o_ref[0] = _unstack_heads(o, rows).astype(BF16)


def _attn_a(sink, p_l, p_c, ctx_out):
    b, seq, _ = p_l.shape
    lc = p_c.shape[1]
    gw = A_GROUP * HEAD_DIM
    smem = pl.BlockSpec(memory_space=pltpu.SMEM)
    o_l = pl.pallas_call(
        functools.partial(_attn_a_lat_kernel, seq=seq),
        grid=(b, A_KV_HEADS, seq // Q_BLOCK),
        in_specs=[
            smem,
            pl.BlockSpec((1, Q_BLOCK, gw), lambda n, h, i: (n, i, h)),
            pl.BlockSpec((1, seq, LANE), lambda n, h, i: (n, 0, COL_AK + h)),
            pl.BlockSpec((1, seq, LANE), lambda n, h, i: (n, 0, COL_AV + h)),
            pl.BlockSpec((1, lc, LANE), lambda n, h, i: (n, 0, COL_AK + h)),
            pl.BlockSpec((1, lc, LANE), lambda n, h, i: (n, 0, COL_AV + h)),
        ],
        out_specs=pl.BlockSpec((1, Q_BLOCK, gw), lambda n, h, i: (n, i, h)),
        out_shape=jax.ShapeDtypeStruct((b, seq, N_HEADS * HEAD_DIM), BF16),
        compiler_params=_cparams("parallel", "parallel", "arbitrary"),
        name="attn_a_latent",
    )(sink, p_l, p_l, p_l, p_c, p_c)
    o_c = None
    if ctx_out:
        o_c = pl.pallas_call(
            _attn_a_ctx_kernel,
            grid=(b, A_KV_HEADS),
            in_specs=[
                smem,
                pl.BlockSpec((1, lc, gw), lambda n, h: (n, 0, h)),
                pl.BlockSpec((1, lc, LANE), lambda n, h: (n, 0, COL_AK + h)),
                pl.BlockSpec((1, lc, LANE), lambda n, h: (n, 0, COL_AV + h)),
            ],
            out_specs=pl.BlockSpec((1, lc, gw), lambda n, h: (n, 0, h)),
            out_shape=jax.ShapeDtypeStruct((b, lc, N_HEADS * HEAD_DIM), BF16),
            compiler_params=_cparams("parallel", "parallel"),
            name="attn_a_ctx",
        )(sink, p_c, p_c, p_c)
    return o_c, o_l


def _attn_b_kernel(lam_ref, q_ref, *refs, nseg, post_scale):
    k_refs = refs[:nseg]
    v_refs = refs[nseg:2 * nseg]
    g_ref, o_ref = refs[2 * nseg], refs[2 * nseg + 1]
    q = q_ref[0]
    tq = q.shape[0]
    lane = lax.broadcasted_iota(jnp.int32, q.shape, 1)
    zero = jnp.zeros_like(q)
    q2 = jnp.concatenate([jnp.where(lane < B_DIM, q, zero), jnp.where(lane < B_DIM, zero, q)], axis=0)
    ss = [lax.dot_general(q2, k[0], NT_DIMS, preferred_element_type=F32) for k in k_refs]
    m = functools.reduce(jnp.maximum, [jnp.max(s, axis=-1, keepdims=True) for s in ss])
    ps = [jnp.exp(s - m) for s in ss]
    den = functools.reduce(jnp.add, [jnp.sum(p, axis=-1, keepdims=True) for p in ps])
    acc = functools.reduce(
        jnp.add, [jnp.dot(p.astype(BF16), v[0], preferred_element_type=F32) for p, v in zip(ps, v_refs)])
    o2 = acc / den
    o = o2[:tq] - lam_ref[0] * o2[tq:]
    ms = jnp.mean(o * o, axis=-1, keepdims=True)
    o_ref[0] = ((o * lax.rsqrt(ms + EPS) * g_ref[...]) * post_scale).astype(BF16)


def _attn_b_call(lam, q_src, k_srcs, subln, post_scale, name):
    b, rq, _ = q_src.shape
    tq = _tile(rq, 256)
    smem = pl.BlockSpec(memory_space=pltpu.SMEM)
    in_specs = [smem, pl.BlockSpec((1, tq, LANE), lambda n, h, i: (n, i, COL_BQ + h))]
    in_specs += [pl.BlockSpec((1, s.shape[1], LANE), lambda n, h, i: (n, 0, COL_BK + h)) for s in k_srcs]
    in_specs += [pl.BlockSpec((1, s.shape[1], LANE), lambda n, h, i: (n, 0, COL_BV + h)) for s in k_srcs]
    in_specs += [pl.BlockSpec((1, LANE), lambda n, h, i: (0, 0))]
    return pl.pallas_call(
        functools.partial(_attn_b_kernel, nseg=len(k_srcs), post_scale=post_scale),
        grid=(b, N_HEADS, rq // tq),
        in_specs=in_specs,
        out_specs=pl.BlockSpec((1, tq, LANE), lambda n, h, i: (n, i, h)),
        out_shape=jax.ShapeDtypeStruct((b, rq, N_HEADS * HEAD_DIM), BF16),
        compiler_params=_cparams("parallel", "parallel", "arbitrary"),
        name=name,
    )(lam, q_src, *k_srcs, *k_srcs, subln)


def _attn_b(lam, p_l, p_c, subln, post_scale, ctx_out):
    o_l = _attn_b_call(lam, p_l, [p_c, p_l], subln, post_scale, "attn_b_latent")
    o_c = _attn_b_call(lam, p_c, [p_c], subln, post_scale, "attn_b_ctx") if ctx_out else None
    return o_c, o_l


def _centred_conv(x_ref, pad_ref, w_ref, b_ref):
    n = x_ref.shape[1]
    zeros = jnp.zeros((SUBLANE, LANE), F32)
    pad_ref[0:SUBLANE, :] = zeros
    pad_ref[SUBLANE:SUBLANE + n, :] = x_ref[0]
    pad_ref[SUBLANE + n:2 * SUBLANE + n, :] = zeros
    out = b_ref[...]
    for tap in range(4):
        out = out + pad_ref[pl.ds(SUBLANE - 2 + tap, n), :] * w_ref[tap:tap + 1, :]
    return out


def _tile_scan(a, v, reverse):
    n = a.shape[0]
    row = lax.broadcasted_iota(jnp.int32, a.shape, 0) % SUBLANE
    for s in (1, 2, 4):
        if reverse:
            keep = row < SUBLANE - s
            a_sh = jnp.where(keep, pltpu.roll(a, n - s, 0), 1.0)
            v_sh = jnp.where(keep, pltpu.roll(v, n - s, 0), 0.0)
        else:
            keep = row >= s
            a_sh = jnp.where(keep, pltpu.roll(a, s, 0), 1.0)
            v_sh = jnp.where(keep, pltpu.roll(v, s, 0), 0.0)
        v = v + a * v_sh
        a = a * a_sh
    return a, v


def _seg_tile(t, ntc, ntl, reverse):
    if not reverse:
        return t
    return jnp.where(t < ntc, ntc - 1 - t, 2 * ntc + ntl - 1 - t)


def _lru_kernel(xc_ref, yc_ref, xl_ref, yl_ref, cw_ref, cb_ref, wr_ref, br_ref, wi_ref, bi_ref,
                lam_ref, *refs, ctx_out):
    if ctx_out:
        oc_ref, ol_ref, ac_ref, hl_ref, hs_ref, pad_ref = refs
    else:
        ol_ref, ac_ref, hl_ref, hs_ref, pad_ref = refs
    lc, seq = xc_ref.shape[1], xl_ref.shape[1]
    ntc, ntl = lc // SUBLANE, seq // SUBLANE
    for x_ref, off in ((xc_ref, 0), (xl_ref, lc)):
        n = x_ref.shape[1]
        u = _centred_conv(x_ref, pad_ref, cw_ref, cb_ref)
        ub = u.astype(BF16)
        for d in range(2):
            lam = lam_ref[d:d + 1, :]
            sp = jnp.maximum(-lam, 0.0) + jnp.log1p(jnp.exp(-jnp.abs(lam)))
            r = _sigmoid(jnp.dot(ub, wr_ref[d, 0], preferred_element_type=F32) + br_ref[d:d + 1, :])
            gi = _sigmoid(jnp.dot(ub, wi_ref[d, 0], preferred_element_type=F32) + bi_ref[d:d + 1, :])
            log_a = -LRU_C * r * sp
            a = jnp.exp(log_a)
            y = jnp.tanh(-log_a) * (1.0 + a * a)
            v = (y * lax.rsqrt(jnp.maximum(y, TINY))) * gi * u
            a_cum, h_loc = _tile_scan(a, v, reverse=(d == 1))
            ac_ref[d, off:off + n, :] = a_cum
            hl_ref[d, off:off + n, :] = h_loc

    def step(t, carry):
        new = []
        for d in range(2):
            tile = _seg_tile(t, ntc, ntl, d == 1)
            r0 = pl.multiple_of(tile * SUBLANE, SUBLANE)
            h = hl_ref[d, pl.ds(r0, SUBLANE), :] + ac_ref[d, pl.ds(r0, SUBLANE), :] * carry[d]
            if d == 0:
                hs_ref[pl.ds(r0, SUBLANE), :] = h
                new.append(h[SUBLANE - 1:SUBLANE, :])
            else:
                hl_ref[d, pl.ds(r0, SUBLANE), :] = h
                new.append(h[0:1, :])
        return tuple(new)

    zero = jnp.zeros((1, LANE), F32)
    lax.fori_loop(0, ntc + ntl, step, (zero, zero))
    if ctx_out:
        hc = hs_ref[0:lc, :] + hl_ref[1, 0:lc, :]
        oc_ref[0] = (hc * jax.nn.gelu(yc_ref[0])).astype(BF16)
    hl = hs_ref[lc:lc + seq, :] + hl_ref[1, lc:lc + seq, :]
    ol_ref[0] = (hl * jax.nn.gelu(yl_ref[0])).astype(BF16)


def _lru(proj_l, proj_c, conv_w, conv_b, w_r, b_r, w_i, b_i, lam, ctx_out):
    b, seq, _ = proj_l.shape
    lc = proj_c.shape[1]
    nblk = w_r.shape[1]
    width = nblk * LANE

    def seg(n, col):
        return pl.BlockSpec((1, n, LANE), lambda i, j: (i, 0, col + j))

    vec2 = pl.BlockSpec((2, LANE), lambda i, j: (0, j))
    wspec = pl.BlockSpec((2, 1, LANE, LANE), lambda i, j: (0, j, 0, 0))
    out_specs = [pl.BlockSpec((1, seq, LANE), lambda i, j: (i, 0, j))]
    out_shape = [jax.ShapeDtypeStruct((b, seq, width), BF16)]
    if ctx_out:
        out_specs.insert(0, pl.BlockSpec((1, lc, LANE), lambda i, j: (i, 0, j)))
        out_shape.insert(0, jax.ShapeDtypeStruct((b, lc, width), BF16))
    outs = pl.pallas_call(
        functools.partial(_lru_kernel, ctx_out=ctx_out),
        grid=(b, nblk),
        in_specs=[
            seg(lc, COL_CX), seg(lc, COL_CY), seg(seq, COL_CX), seg(seq, COL_CY),
            pl.BlockSpec((4, LANE), lambda i, j: (0, j)),
            pl.BlockSpec((1, LANE), lambda i, j: (0, j)),
            wspec, vec2, wspec, vec2, vec2,
        ],
        out_specs=out_specs,
        out_shape=out_shape,
        scratch_shapes=[
            pltpu.VMEM((2, lc + seq, LANE), F32),
            pltpu.VMEM((2, lc + seq, LANE), F32),
            pltpu.VMEM((lc + seq, LANE), F32),
            pltpu.VMEM((max(lc, seq) + 2 * SUBLANE, LANE), F32),
        ],
        compiler_params=_cparams("parallel", "parallel"),
        name="rglru",
    )(proj_c, proj_c, proj_l, proj_l, conv_w, conv_b.reshape(1, width), w_r, b_r, w_i, b_i, lam)
    return (outs[0], outs[1]) if ctx_out else (None, outs[0])


def _gla_level_masks(reverse):
    c = GLA_CHUNK
    row = lax.broadcasted_iota(jnp.int32, (c, c), 0)
    col = lax.broadcasted_iota(jnp.int32, (c, c), 1)
    masks = {}
    for s in (32, 16, 8):
        same = (row // (2 * s)) == (col // (2 * s))
        if reverse:
            masks[s] = same & ((row % (2 * s)) < s) & ((col % (2 * s)) >= s)
        else:
            masks[s] = same & ((row % (2 * s)) >= s) & ((col % (2 * s)) < s)
    tri = (row <= col) if reverse else (row >= col)
    return masks, tri.astype(F32)


def _gla_chunk(q, k, v, g, st, masks, tri, reverse):
    c = GLA_CHUNK
    b = jnp.dot(tri, g, precision=lax.Precision.HIGHEST, preferred_element_type=F32)
    att = jnp.zeros((c, c), F32)
    for s in (32, 16, 8):
        b3 = b.reshape(c // (2 * s), 2 * s, LANE)
        rr = s if reverse else s - 1
        rho = jnp.broadcast_to(b3[:, rr:rr + 1, :], b3.shape).reshape(c, LANE)
        e = jnp.exp(-jnp.abs(b - rho))
        a = lax.dot_general((q * e).astype(BF16), (k * e).astype(BF16), NT_DIMS,
                            preferred_element_type=F32)
        att = att + jnp.where(masks[s], a, 0.0)
    vb = v.astype(BF16)
    o = jnp.dot(att.astype(BF16), vb, preferred_element_type=F32)
    nb = c // SUBLANE
    b3 = b.reshape(nb, SUBLANE, LANE)
    q3 = q.reshape(nb, SUBLANE, LANE)
    k3 = k.reshape(nb, SUBLANE, LANE)
    v3 = v.reshape(nb, SUBLANE, LANE)
    rowi = lax.broadcasted_iota(jnp.int32, b3.shape, 1)
    od = jnp.zeros(b3.shape, F32)
    for jj in range(SUBLANE):
        keep = (rowi <= jj) if reverse else (rowi >= jj)
        e = jnp.where(keep, jnp.exp(jnp.minimum(b3 - b3[:, jj:jj + 1, :], 0.0)), 0.0)
        sj = jnp.sum(q3 * e * k3[:, jj:jj + 1, :], axis=-1, keepdims=True)
        od = od + sj * v3[:, jj:jj + 1, :]
    o = o + od.reshape(c, LANE)
    o = o + lax.dot_general((q * jnp.exp(b)).astype(BF16), st.astype(BF16), NT_DIMS,
                            preferred_element_type=F32)
    b_end = b[0:1, :] if reverse else b[c - 1:c, :]
    khat = (k * jnp.exp(b_end - b)).astype(BF16)
    st_new = st * jnp.exp(b_end) + lax.dot_general(vb, khat, TN_DIMS, preferred_element_type=F32)
    return o, st_new


def _gla_fast_intra(q, k, v, g, tri, reverse):
    c = GLA_CHUNK
    g_hi = g.astype(BF16)
    g_lo = (g - g_hi.astype(F32)).astype(BF16)
    b2 = jnp.dot(tri.astype(BF16), jnp.concatenate([g_hi, g_lo], axis=1), preferred_element_type=F32)
    b = b2[:, :LANE] + b2[:, LANE:]
    rr = c // 2 if reverse else c // 2 - 1
    rho = b[rr:rr + 1, :]
    qt = q * jnp.exp(b - rho)
    kt = k * jnp.exp(rho - b)
    a = lax.dot_general(qt.astype(BF16), kt.astype(BF16), NT_DIMS, preferred_element_type=F32)
    att = jnp.where(tri > 0.0, a, 0.0)
    vb = v.astype(BF16)
    o_intra = jnp.dot(att.astype(BF16), vb, preferred_element_type=F32)
    b_end = b[0:1, :] if reverse else b[c - 1:c, :]
    qe = (qt * jnp.exp(rho)).astype(BF16)
    khat = (kt * jnp.exp(b_end - rho)).astype(BF16)
    return o_intra, qe, khat, vb, jnp.exp(b_end)


def _gla_fast_inter(intra, st):
    o_intra, qe, khat, vb, decay = intra
    o = o_intra + lax.dot_general(qe, st.astype(BF16), NT_DIMS, preferred_element_type=F32)
    st_new = st * decay + lax.dot_general(vb, khat, TN_DIMS, preferred_element_type=F32)
    return o, st_new


def _gla_kernel(lb_ref, on_ref, qc_ref, ffc_ref, fbc_ref, ic_ref, gc_ref,
                ql_ref, ffl_ref, fbl_ref, il_ref, gl_ref, *refs, ctx_out):
    if ctx_out:
        oc_ref, ol_ref, q_s, v_s, g_s, k_s, o_s, st_s = refs
    else:
        ol_ref, q_s, v_s, g_s, k_s, o_s, st_s = refs
    lc, seq = qc_ref.shape[1], ql_ref.shape[1]
    ncc, ncl = lc // GLA_CHUNK, seq // GLA_CHUNK
    for off, n, q_ref, i_ref, f_refs in ((0, lc, qc_ref, ic_ref, (ffc_ref, fbc_ref)),
                                         (lc, seq, ql_ref, il_ref, (ffl_ref, fbl_ref))):
        q_s[off:off + n, :] = q_ref[0]
        v_s[off:off + n, :] = i_ref[0]
        for d in range(2):
            z = f_refs[d][0]
            lbd = lb_ref[d:d + 1, :]
            ez = jnp.exp(-jnp.abs(z))
            r = 1.0 / (1.0 + ez)
            pos = z >= 0.0
            sig_p = jnp.where(pos, r, ez * r)
            sig_n = jnp.where(pos, ez * r, r)
            g_s[d, off:off + n, :] = jnp.log(lbd + (1.0 - lbd) * sig_p)
            k_s[d, off:off + n, :] = (1.0 - lbd) * sig_n
    st_s[...] = jnp.zeros(st_s.shape, F32)
    consts = [_gla_level_masks(False), _gla_level_masks(True)]

    half = GLA_CHUNK // 2
    worst = jnp.zeros((1, LANE), F32)
    for d in range(2):
        hs = jnp.sum(g_s[d].reshape((lc + seq) // half, half, LANE), axis=1)
        worst = jnp.maximum(worst, jnp.max(-hs, axis=0, keepdims=True))
    safe = jnp.max(worst) < GLA_SAFE_DECAY

    def chunk_rows(cidx, d):
        chunk = _seg_tile(cidx, ncc, ncl, d == 1)
        return pl.ds(pl.multiple_of(chunk * GLA_CHUNK, GLA_CHUNK), GLA_CHUNK)

    def robust_step(cidx, carry):
        for d in range(2):
            rows = chunk_rows(cidx, d)
            o, st_new = _gla_chunk(q_s[rows, :], k_s[d, rows, :], v_s[rows, :], g_s[d, rows, :],
                                   st_s[d], consts[d][0], consts[d][1], d == 1)
            st_s[d] = st_new
            o_s[d, rows, :] = o
        return carry

    nchunks = ncc + ncl
    group = next(u for u in (4, 3, 2, 1) if nchunks % u == 0)

    def fast_step(t, carry):
        work = [[] for _ in range(2)]
        for d in range(2):
            for u in range(group):
                rows = chunk_rows(t * group + u, d)
                work[d].append((rows, _gla_fast_intra(q_s[rows, :], k_s[d, rows, :], v_s[rows, :],
                                                      g_s[d, rows, :], consts[d][1], d == 1)))
        for d in range(2):
            st = st_s[d]
            outs = []
            for rows, intra in work[d]:
                o, st = _gla_fast_inter(intra, st)
                outs.append((rows, o))
            st_s[d] = st
            for rows, o in outs:
                o_s[d, rows, :] = o
        return carry

    @pl.when(safe)
    def _():
        lax.fori_loop(0, nchunks // group, fast_step, 0)

    @pl.when(jnp.logical_not(safe))
    def _():
        lax.fori_loop(0, nchunks, robust_step, 0)

    def finish(o, gate):
        ms = jnp.mean(o * o, axis=-1, keepdims=True)
        return ((o * lax.rsqrt(ms + EPS) * on_ref[...]) * (gate * _sigmoid(gate))).astype(BF16)

    if ctx_out:
        oc_ref[0] = finish(o_s[0, 0:lc, :] + o_s[1, 0:lc, :], gc_ref[0])
    ol_ref[0] = finish(o_s[0, lc:lc + seq, :] + o_s[1, lc:lc + seq, :], gl_ref[0])


def _gla(proj_l, proj_c, lb, onorm, ctx_out):
    b, seq, _ = proj_l.shape
    lc = proj_c.shape[1]
    width = N_HEADS * HEAD_DIM
    nt = lc + seq

    def seg(n, col):
        return pl.BlockSpec((1, n, LANE), lambda i, h: (i, 0, col + h))

    cols = (COL_DQ, COL_DFF, COL_DFB, COL_DI, COL_DG)
    out_specs = [pl.BlockSpec((1, seq, LANE), lambda i, h: (i, 0, h))]
    out_shape = [jax.ShapeDtypeStruct((b, seq, width), BF16)]
    if ctx_out:
        out_specs.insert(0, pl.BlockSpec((1, lc, LANE), lambda i, h: (i, 0, h)))
        out_shape.insert(0, jax.ShapeDtypeStruct((b, lc, width), BF16))
    outs = pl.pallas_call(
        functools.partial(_gla_kernel, ctx_out=ctx_out),
        grid=(b, N_HEADS),
        in_specs=[pl.BlockSpec((2, LANE), lambda i, h: (0, h)),
                  pl.BlockSpec((1, LANE), lambda i, h: (0, 0))]
        + [seg(lc, c) for c in cols] + [seg(seq, c) for c in cols],
        out_specs=out_specs,
        out_shape=out_shape,
        scratch_shapes=[
            pltpu.VMEM((nt, LANE), F32),
            pltpu.VMEM((nt, LANE), F32),
            pltpu.VMEM((2, nt, LANE), F32),
            pltpu.VMEM((2, nt, LANE), F32),
            pltpu.VMEM((2, nt, LANE), F32),
            pltpu.VMEM((2, LANE, LANE), F32),
        ],
        compiler_params=_cparams("parallel", "parallel"),
        name="hgrn2",
    )(lb, onorm, *([proj_c] * 5), *([proj_l] * 5))
    return (outs[0], outs[1]) if ctx_out else (None, outs[0])


def _merge_kernel(oa_ref, ob_ref, oc_ref, od_ref, g0_ref, g1_ref, g2_ref, g3_ref,
                  w0_ref, w1_ref, w2_ref, w3_ref, y_ref):
    acc = None
    for o_ref, g_ref, w_ref in ((oa_ref, g0_ref, w0_ref), (ob_ref, g1_ref, w1_ref),
                                (oc_ref, g2_ref, w2_ref), (od_ref, g3_ref, w3_ref)):
        t = _sigmoid(g_ref[0]) * jnp.dot(o_ref[0], w_ref[0], preferred_element_type=F32)
        acc = t if acc is None else acc + t
    y_ref[0] = acc.astype(BF16)


def _merge(outs, proj, w_branch):
    nb, r, mw = outs[0].shape
    d = w_branch.shape[2]
    tm = _tile(r, 512)
    tn = 512
    nj = d // tn
    o_spec = pl.BlockSpec((1, tm, mw), lambda b, t, j: (b, t, 0))
    g_specs = [pl.BlockSpec((1, tm, tn), functools.partial(
        lambda b, t, j, n: (b, t, (COL_GATE * LANE) // tn + n * nj + j), n=n)) for n in range(4)]
    w_specs = [pl.BlockSpec((1, mw, tn), functools.partial(lambda b, t, j, n: (n, 0, j), n=n))
               for n in range(4)]
    return pl.pallas_call(
        _merge_kernel,
        grid=(nb, r // tm, nj),
        in_specs=[o_spec] * 4 + g_specs + w_specs,
        out_specs=pl.BlockSpec((1, tm, tn), lambda b, t, j: (b, t, j)),
        out_shape=jax.ShapeDtypeStruct((nb, r, d), BF16),
        compiler_params=_cparams("parallel", "parallel", "arbitrary"),
        name="branch_merge",
    )(*outs, *([proj] * 4), *([w_branch] * 4))


def _out_proj_kernel(y_ref, w_ref, x_ref, g_ref, o_ref):
    o_ref[0] = x_ref[0] + g_ref[0] * jnp.dot(y_ref[0], w_ref[...], preferred_element_type=F32)


def _out_proj(y, w, x, gate):
    nb, r, d = x.shape
    tm = _tile(r, 1024)
    tn = 512
    return pl.pallas_call(
        _out_proj_kernel,
        grid=(nb, r // tm, d // tn),
        in_specs=[
            pl.BlockSpec((1, tm, d), lambda b, t, j: (b, t, 0)),
            pl.BlockSpec((d, tn), lambda b, t, j: (0, j)),
            pl.BlockSpec((1, tm, tn), lambda b, t, j: (b, t, j)),
            pl.BlockSpec((1, 1, tn), lambda b, t, j: (b, 0, j)),
        ],
        out_specs=pl.BlockSpec((1, tm, tn), lambda b, t, j: (b, t, j)),
        out_shape=jax.ShapeDtypeStruct((nb, r, d), F32),
        compiler_params=_cparams("parallel", "parallel", "arbitrary"),
        name="out_proj",
    )(y, w, x, gate)


def _route(logits_t, bias_col):
    aff = _sigmoid(logits_t)
    sel = aff + bias_col
    aff_r = [aff[e:e + 1, :] for e in range(N_EXPERTS)]
    sel_r = [sel[e:e + 1, :] for e in range(N_EXPERTS)]
    scores = []
    for g in range(N_GROUPS):
        v = sel_r[g * EXPERTS_PER_GROUP:(g + 1) * EXPERTS_PER_GROUP]
        m1 = functools.reduce(jnp.maximum, v)
        taken = jnp.zeros(m1.shape, jnp.bool_)
        second = jnp.full(m1.shape, -jnp.inf, F32)
        for x in v:
            first = (x == m1) & jnp.logical_not(taken)
            taken = taken | first
            second = jnp.where(first, second, jnp.maximum(second, x))
        scores.append(m1 + second)
    best, gidx = scores[0], jnp.zeros(scores[0].shape, jnp.int32)
    for g in range(1, N_GROUPS):
        better = scores[g] > best
        gidx = jnp.where(better, g, gidx)
        best = jnp.where(better, scores[g], best)
    masked = [jnp.where(gidx == e // EXPERTS_PER_GROUP, sel_r[e], -jnp.inf) for e in range(N_EXPERTS)]

    def first_argmax(vals, exclude):
        bv = jnp.full(vals[0].shape, -jnp.inf, F32)
        bi = jnp.full(vals[0].shape, -1, jnp.int32)
        for e, x in enumerate(vals):
            better = x > bv
            if exclude is not None:
                better = better & (exclude != e)
            bi = jnp.where(better, e, bi)
            bv = jnp.where(better, x, bv)
        return bi

    i1 = first_argmax(masked, None)
    i2 = first_argmax(masked, i1)
    w1 = functools.reduce(jnp.add, [jnp.where(i1 == e, aff_r[e], 0.0) for e in range(N_EXPERTS)])
    w2 = functools.reduce(jnp.add, [jnp.where(i2 == e, aff_r[e], 0.0) for e in range(N_EXPERTS)])
    tot = w1 + w2
    g1, g2 = w1 / tot, w2 / tot
    width = logits_t.shape[1]
    rowi = lax.broadcasted_iota(jnp.int32, (SUBLANE, width), 0)
    local1 = jnp.broadcast_to(i1 - gidx * EXPERTS_PER_GROUP, (SUBLANE, width))
    local2 = jnp.broadcast_to(i2 - gidx * EXPERTS_PER_GROUP, (SUBLANE, width))
    info = (jnp.where(rowi == local1, jnp.broadcast_to(g1, (SUBLANE, width)), 0.0)
            + jnp.where(rowi == local2, jnp.broadcast_to(g2, (SUBLANE, width)), 0.0))
    return jnp.where(rowi == EXPERTS_PER_GROUP,
                     jnp.broadcast_to(gidx.astype(F32), (SUBLANE, width)), info)


def _moe_route_kernel(x_ref, g_ref, sh_ref, sc_ref, wrt_ref, br_ref, hn_ref, info_ref):
    h = _modulated_norm(x_ref[0], g_ref[...], sh_ref[0], sc_ref[0])
    hn_ref[0] = h
    logits_t = lax.dot_general(wrt_ref[...], h, NT_DIMS, precision=lax.Precision.HIGHEST,
                               preferred_element_type=F32)
    info_ref[0] = _route(logits_t, br_ref[...])


def _moe_route(x, g, sh, sc, w_router_t, b_router):
    nb, r, d = x.shape
    tm = _tile(r, 512)
    vec = pl.BlockSpec((1, 1, d), lambda b, t: (b, 0, 0))
    return pl.pallas_call(
        _moe_route_kernel,
        grid=(nb, r // tm),
        in_specs=[
            pl.BlockSpec((1, tm, d), lambda b, t: (b, t, 0)),
            pl.BlockSpec((1, d), lambda b, t: (0, 0)),
            vec, vec,
            pl.BlockSpec((N_EXPERTS, d), lambda b, t: (0, 0)),
            pl.BlockSpec((N_EXPERTS, 1), lambda b, t: (0, 0)),
        ],
        out_specs=[pl.BlockSpec((1, tm, d), lambda b, t: (b, t, 0)),
                   pl.BlockSpec((1, SUBLANE, tm), lambda b, t: (b, 0, t))],
        out_shape=[jax.ShapeDtypeStruct((nb, r, d), F32),
                   jax.ShapeDtypeStruct((nb, SUBLANE, r), F32)],
        compiler_params=_cparams("parallel", "parallel"),
        name="moe_route",
    )(x, g, sh, sc, w_router_t, b_router.reshape(N_EXPERTS, 1))


def _row_copy(src, src_row, dst, dst_row, sem):
    return pltpu.make_async_copy(src.at[pl.ds(src_row, 1), :], dst.at[pl.ds(dst_row, 1), :], sem)


def _moe_experts_kernel(src_ref, tgrp_ref, nval_ref, hn_hbm, gates_ref, wg_ref, wu_ref, wd_ref,
                        y_hbm, xbuf, hb, acc, gsem, ssem, *, tm):
    i = pl.program_id(0)
    e = pl.program_id(1)
    nt = pl.num_programs(0)
    slot = i % 2
    per = tm // EXPERTS_PER_GROUP

    def wait_scatter(tile, sl):
        def body(r, c):
            _row_copy(acc.at[sl], 0, y_hbm, 0, ssem.at[sl]).wait()
            return c
        lax.fori_loop(0, nval_ref[tile], body, 0)

    @pl.when((i == 0) & (e == 0))
    def _():
        def body(r, c):
            _row_copy(hn_hbm, src_ref[r], xbuf.at[0], r, gsem.at[0]).start()
            return c
        lax.fori_loop(0, tm, body, 0)

    @pl.when(e == 0)
    def _():
        pltpu.make_async_copy(hn_hbm.at[pl.ds(0, tm), :], xbuf.at[slot], gsem.at[slot]).wait()
        hb[...] = xbuf[slot].astype(BF16)

        @pl.when(i >= 2)
        def _():
            wait_scatter(i - 2, slot)
        acc[slot] = jnp.zeros(acc.shape[1:], F32)

    @pl.when(i + 1 < nt)
    def _():
        base = (i + 1) * tm + e * per
        for k in range(per):
            _row_copy(hn_hbm, src_ref[base + k], xbuf.at[1 - slot], e * per + k, gsem.at[1 - slot]).start()

    @pl.when(nval_ref[i] > 0)
    def _():
        hv = hb[...]
        a = jnp.dot(hv, wg_ref[0], preferred_element_type=F32)
        u = jnp.dot(hv, wu_ref[0], preferred_element_type=F32)
        hid = ((a * _sigmoid(a)) * u).astype(BF16)
        y = jnp.dot(hid, wd_ref[0], preferred_element_type=F32)
        lane = lax.broadcasted_iota(jnp.int32, gates_ref.shape, 1)
        col = jnp.sum(jnp.where(lane == e, gates_ref[...], 0.0), axis=-1, keepdims=True)
        acc[slot] = acc[slot] + col * y

    @pl.when(e == EXPERTS_PER_GROUP - 1)
    def _():
        def body(r, c):
            _row_copy(acc.at[slot], r, y_hbm, src_ref[i * tm + r], ssem.at[slot]).start()
            return c
        lax.fori_loop(0, nval_ref[i], body, 0)

        @pl.when(i == nt - 1)
        def _():
            wait_scatter(i, slot)

            @pl.when(i >= 1)
            def _():
                wait_scatter(i - 1, 1 - slot)


def _moe_experts(hn, src, tile_group, tile_nvalid, gates_sorted, w_gate, w_up, w_down, tm):
    t, d = hn.shape
    dff = w_gate.shape[2]
    ntiles = tile_group.shape[0]

    def wmap(i, e, src_r, tgrp_r, nval_r):
        return (tgrp_r[i] * EXPERTS_PER_GROUP + e, 0, 0)

    grid_spec = pltpu.PrefetchScalarGridSpec(
        num_scalar_prefetch=3,
        grid=(ntiles, EXPERTS_PER_GROUP),
        in_specs=[
            pl.BlockSpec(memory_space=pl.ANY),
            pl.BlockSpec((tm, EXPERTS_PER_GROUP), lambda i, e, *_: (i, 0)),
            pl.BlockSpec((1, d, dff), wmap),
            pl.BlockSpec((1, d, dff), wmap),
            pl.BlockSpec((1, dff, d), wmap),
        ],
        out_specs=pl.BlockSpec(memory_space=pl.ANY),
        scratch_shapes=[
            pltpu.VMEM((2, tm, d), F32),
            pltpu.VMEM((tm, d), BF16),
            pltpu.VMEM((2, tm, d), F32),
            pltpu.SemaphoreType.DMA((2,)),
            pltpu.SemaphoreType.DMA((2,)),
        ],
    )
    return pl.pallas_call(
        functools.partial(_moe_experts_kernel, tm=tm),
        grid_spec=grid_spec,
        out_shape=jax.ShapeDtypeStruct((t, d), F32),
        compiler_params=_cparams("arbitrary", "arbitrary"),
        name="moe_experts",
    )(src, tile_group, tile_nvalid, hn, gates_sorted, w_gate, w_up, w_down)


def _residual_kernel(x_ref, y_ref, g_ref, o_ref):
    o_ref[0] = x_ref[0] + g_ref[0] * y_ref[0]


def _residual(x, y, gate):
    nb, r, d = x.shape
    tm = _tile(r, 512)
    blk = pl.BlockSpec((1, tm, d), lambda b, t: (b, t, 0))
    return pl.pallas_call(
        _residual_kernel,
        grid=(nb, r // tm),
        in_specs=[blk, blk, pl.BlockSpec((1, 1, d), lambda b, t: (b, 0, 0))],
        out_specs=blk,
        out_shape=jax.ShapeDtypeStruct((nb, r, d), F32),
        compiler_params=_cparams("parallel", "parallel"),
        name="moe_residual",
    )(x, y, gate)


def _group_layout(grp, gates4, tm):
    t = grp.shape[0]
    ntiles = t // tm + N_GROUPS
    onehot = (grp[:, None] == jnp.arange(N_GROUPS)[None, :]).astype(jnp.int32)
    counts = jnp.sum(onehot, axis=0)
    rank = jnp.sum((jnp.cumsum(onehot, axis=0) - onehot) * onehot, axis=1)
    padded = ((counts + tm - 1) // tm) * tm
    start = jnp.cumsum(padded) - padded
    slot_of_token = start[grp] + rank
    nslots = ntiles * tm
    src = jnp.zeros((nslots,), jnp.int32).at[slot_of_token].set(jnp.arange(t, dtype=jnp.int32))
    gates_sorted = jnp.zeros((nslots, EXPERTS_PER_GROUP), F32).at[slot_of_token].set(gates4)
    tile_start = jnp.arange(ntiles, dtype=jnp.int32) * tm
    ends = jnp.cumsum(padded)
    tile_group = jnp.minimum(jnp.sum((tile_start[:, None] >= ends[None, :]).astype(jnp.int32), axis=1),
                             N_GROUPS - 1).astype(jnp.int32)
    tile_nvalid = jnp.clip(counts[tile_group] - (tile_start - start[tile_group]), 0, tm).astype(jnp.int32)
    return src, tile_group, tile_nvalid, gates_sorted


def _moe(x, g, sh, sc, gate, w_router_t, b_router, w_gate, w_up, w_down):
    nb, r, d = x.shape
    t = nb * r
    tm = 512 if t >= 8192 else (256 if t >= 2048 else 128)
    hn, info = _moe_route(x, g, sh, sc, w_router_t, b_router)
    grp = info[:, EXPERTS_PER_GROUP, :].reshape(t).astype(jnp.int32)
    gates4 = jnp.swapaxes(info[:, :EXPERTS_PER_GROUP, :], 1, 2).reshape(t, EXPERTS_PER_GROUP)
    src, tile_group, tile_nvalid, gates_sorted = _group_layout(grp, gates4, tm)
    y = _moe_experts(hn.reshape(t, d), src, tile_group, tile_nvalid, gates_sorted,
                     w_gate, w_up, w_down, tm)
    return _residual(x, y.reshape(nb, r, d), gate)


def kernel(x, c, ctx, c_ctx, w_ada, b_ada, norm1_g, norm2_g, w_in, qn_a, kn_a, sink_a, qn_b, kn_b,
           lam_b, subln_b, conv_w, conv_b, w_rg, b_rg, w_ig, b_ig, lru_lambda, lb_d, onorm_d,
           w_branch, w_out, w_router, b_router, w_gate, w_up, w_down):
    bsz, seq, d = x.shape
    lc = ctx.shape[1]
    depth = w_in.shape[0]

    n_rows = -(-(bsz + 1) // SUBLANE) * SUBLANE
    c_pad = jnp.zeros((n_rows, d), F32).at[:bsz].set(c).at[bsz].set(c_ctx)
    mod = _ada(c_pad, w_ada, b_ada).reshape(depth, n_rows, 6, d)

    lb_w = jax.nn.softmax(lb_d.astype(F32), axis=0)
    lb_all = jnp.cumsum(lb_w, axis=0) - lb_w[0:1]
    rope_tabs = _rope_tables(seq)
    w_router_t = w_router.T

    xl = x
    xc = ctx.reshape(1, bsz * lc, d)
    for l in range(depth):
        ctx_out = l < depth - 1
        mod_l = [mod[l, :bsz, k][:, None, :] for k in range(6)]
        mod_c = [mod[l, bsz:bsz + 1, k][:, None, :] for k in range(6)]
        w_in_l = w_in[l].astype(BF16)
        g1 = norm1_g[l].reshape(1, d)
        g2 = norm2_g[l].reshape(1, d)

        proj_l = _norm_proj(xl, g1, mod_l[0], mod_l[1], w_in_l)
        proj_c = _norm_proj(xc, g1, mod_c[0], mod_c[1], w_in_l).reshape(bsz, lc, -1)

        qa = qn_a[l] * (HEAD_DIM ** -0.5)
        qb = jnp.tile(qn_b[l], 2) * (B_DIM ** -0.5)
        kb = jnp.tile(kn_b[l], 2)
        one = jnp.ones((LANE,), F32)
        gains = jnp.stack([qa] * 8 + [kn_a[l]] * 2 + [one] * 2 + [qb] * 8 + [kb] * 8 + [one] * 8)
        gains = gains.reshape(N_QKV_BLOCKS, 1, LANE)
        p_l = _prep(proj_l, gains, rope_tabs)
        p_c = _prep(proj_c, gains, None)

        oa_c, oa_l = _attn_a(sink_a[l], p_l, p_c, ctx_out)

        lq1, lk1, lq2, lk2 = lam_b[l].astype(F32)
        lam_init = 0.8 - 0.6 * math.exp(-0.3 * l)
        lam = (jnp.exp(jnp.sum(lq1 * lk1)) - jnp.exp(jnp.sum(lq2 * lk2)) + lam_init).reshape(1)
        ob_c, ob_l = _attn_b(lam, p_l, p_c, subln_b[l].reshape(1, LANE), 1.0 - lam_init, ctx_out)

        oc_c, oc_l = _lru(proj_l, proj_c, conv_w[l], conv_b[l], w_rg[l].astype(BF16), b_rg[l],
                          w_ig[l].astype(BF16), b_ig[l], lru_lambda[l], ctx_out)
        od_c, od_l = _gla(proj_l, proj_c, lb_all[l], onorm_d[l].reshape(1, LANE), ctx_out)

        w_branch_l = w_branch[l].astype(BF16)
        w_out_l = w_out[l].astype(BF16)
        moe_w = (w_router_t, b_router, w_gate[l].astype(BF16), w_up[l].astype(BF16),
                 w_down[l].astype(BF16))

        y_l = _merge((oa_l, ob_l, oc_l, od_l), proj_l, w_branch_l)
        xl = _out_proj(y_l, w_out_l, xl, mod_l[2])
        xl = _moe(xl, g2, mod_l[3], mod_l[4], mod_l[5], *moe_w)
        if ctx_out:
            flat = lambda t: t.reshape(1, bsz * lc, -1)
            y_c = _merge(tuple(flat(t) for t in (oa_c, ob_c, oc_c, od_c)), flat(proj_c), w_branch_l)
            xc = _out_proj(y_c, w_out_l, xc, mod_c[2])
            xc = _moe(xc, g2, mod_c[3], mod_c[4], mod_c[5], *moe_w)
    return xl
```

```python
import functools
import math

import jax
import jax.numpy as jnp
from jax import lax
from jax.experimental import pallas as pl
from jax.experimental.pallas import tpu as pltpu

F32 = jnp.float32
BF16 = jnp.bfloat16

LANE = 128
SUBLANE = 8
VMEM_LIMIT_BYTES = 56 * 1024 * 1024

EPS = 1e-6
ROPE_THETA = 10000.0
GRID_W = 64
HEAD_DIM = 128
WINDOW = 128
Q_BLOCK = 128
N_HEADS = 8
A_KV_HEADS = 2
A_GROUP = N_HEADS // A_KV_HEADS
B_DIM = 64
B_KEY_CHUNK = 1024
LRU_C = 8.0
GLA_CHUNK = 64
GLA_SAFE_DECAY = 80.0
N_EXPERTS = 16
N_GROUPS = 4
EXPERTS_PER_GROUP = N_EXPERTS // N_GROUPS
NEG_BIG = -1e30
TINY = 1e-37

COL_AQ, COL_AK, COL_AV = 0, 8, 10
COL_BQ, COL_BK, COL_BV = 12, 20, 28
N_QKV_BLOCKS = 36
COL_GATE = N_QKV_BLOCKS
COL_CX, COL_CY = 0, 8
COL_DQ, COL_DFF, COL_DFB, COL_DI, COL_DG = 16, 24, 32, 40, 48
W_IN_QKV = (0, 4608)
W_IN_REC = (4608, 11776)
W_IN_GATE = (11776, 19968)
PROJ_TILE = 512

NT_DIMS = (((1,), (1,)), ((), ()))
TN_DIMS = (((0,), (0,)), ((), ()))


def _cparams(*sem):
    return pltpu.CompilerParams(dimension_semantics=sem, vmem_limit_bytes=VMEM_LIMIT_BYTES)


def _tile(n, pref):
    t = min(n, pref)
    while n % t:
        t //= 2
    return t


def _sigmoid(x):
    return jax.nn.sigmoid(x)


def _modulated_norm(x, g, sh, sc):
    ms = jnp.mean(x * x, axis=-1, keepdims=True)
    return (x * lax.rsqrt(ms + EPS) * g) * (1.0 + sc) + sh


def _ada_kernel(c_ref, w_ref, b_ref, o_ref):
    c = c_ref[...]
    s = (c * _sigmoid(c)).astype(BF16)
    o_ref[0] = jnp.dot(s, w_ref[0].astype(BF16), preferred_element_type=F32) + b_ref[0]


def _ada(c_pad, w_ada, b_ada):
    nl, d, n = w_ada.shape
    rows = c_pad.shape[0]
    tn = _tile(n, 1024)
    return pl.pallas_call(
        _ada_kernel,
        grid=(nl, n // tn),
        in_specs=[
            pl.BlockSpec((rows, d), lambda l, j: (0, 0)),
            pl.BlockSpec((1, d, tn), lambda l, j: (l, 0, j)),
            pl.BlockSpec((1, 1, tn), lambda l, j: (l, 0, j)),
        ],
        out_specs=pl.BlockSpec((1, rows, tn), lambda l, j: (l, 0, j)),
        out_shape=jax.ShapeDtypeStruct((nl, rows, n), F32),
        compiler_params=_cparams("parallel", "parallel"),
        name="ada_mod",
    )(c_pad, w_ada, b_ada.reshape(nl, 1, n))


def _norm_proj_kernel(x_ref, g_ref, sh_ref, sc_ref, w_ref, lo_ref, hi_ref, hn_ref, *, n_lo):
    j = pl.program_id(2)

    @pl.when(j == 0)
    def _():
        hn_ref[...] = _modulated_norm(x_ref[0], g_ref[...], sh_ref[0], sc_ref[0]).astype(BF16)

    @pl.when(j < n_lo)
    def _():
        lo_ref[0] = jnp.dot(hn_ref[...], w_ref[0], preferred_element_type=F32).astype(BF16)

    @pl.when(j >= n_lo)
    def _():
        hi_ref[0] = jnp.dot(hn_ref[...], w_ref[0], preferred_element_type=F32)


def _norm_proj(x, g, sh, sc, w_tiles, n_lo):
    nb, r, d = x.shape
    ntiles, _, tn = w_tiles.shape
    tm = _tile(r, 1024)
    n_hi = ntiles - n_lo
    return pl.pallas_call(
        functools.partial(_norm_proj_kernel, n_lo=n_lo),
        grid=(nb, r // tm, ntiles),
        in_specs=[
            pl.BlockSpec((1, tm, d), lambda b, t, j: (b, t, 0)),
            pl.BlockSpec((1, d), lambda b, t, j: (0, 0)),
            pl.BlockSpec((1, 1, d), lambda b, t, j: (b, 0, 0)),
            pl.BlockSpec((1, 1, d), lambda b, t, j: (b, 0, 0)),
            pl.BlockSpec((1, d, tn), lambda b, t, j: (j, 0, 0)),
        ],
        out_specs=[
            pl.BlockSpec((1, tm, tn), lambda b, t, j: (b, t, jnp.minimum(j, n_lo - 1))),
            pl.BlockSpec((1, tm, tn), lambda b, t, j: (b, t, jnp.maximum(j - n_lo, 0))),
        ],
        out_shape=[jax.ShapeDtypeStruct((nb, r, n_lo * tn), BF16),
                   jax.ShapeDtypeStruct((nb, r, n_hi * tn), F32)],
        scratch_shapes=[pltpu.VMEM((tm, d), BF16)],
        compiler_params=_cparams("parallel", "parallel", "arbitrary"),
        name="norm_proj",
    )(x, g, sh, sc, w_tiles)


def _rope_partner(y, lane, half):
    return jnp.where((lane % (2 * half)) < half,
                     pltpu.roll(y, LANE - half, 1), pltpu.roll(y, half, 1))


def _prep_kernel(*refs, rope):
    if rope:
        x_ref, g_ref, ca_ref, sa_ref, cb_ref, sb_ref, o_ref = refs
    else:
        x_ref, g_ref, o_ref = refs
    lane = lax.broadcasted_iota(jnp.int32, (x_ref.shape[1], LANE), 1)
    lo = lane < B_DIM
    for j in range(N_QKV_BLOCKS):
        cols = slice(j * LANE, (j + 1) * LANE)
        x = x_ref[0, :, cols].astype(F32)
        if j < COL_AV:
            ms = jnp.mean(x * x, axis=-1, keepdims=True)
            y = x * lax.rsqrt(ms + EPS) * g_ref[j]
            if rope:
                y = y * ca_ref[...] + _rope_partner(y, lane, 32) * sa_ref[...]
        elif COL_BQ <= j < COL_BV:
            x2 = x * x
            s_lo = jnp.sum(jnp.where(lo, x2, 0.0), axis=-1, keepdims=True)
            s_hi = jnp.sum(jnp.where(lo, 0.0, x2), axis=-1, keepdims=True)
            inv = jnp.where(lo, lax.rsqrt(s_lo / B_DIM + EPS), lax.rsqrt(s_hi / B_DIM + EPS))
            y = x * inv * g_ref[j]
            if rope:
                y = y * cb_ref[...] + _rope_partner(y, lane, 16) * sb_ref[...]
        else:
            y = x
        o_ref[0, :, cols] = y.astype(BF16)


def _prep(proj, gains, rope_tabs):
    nb, r, _ = proj.shape
    tm = _tile(r, 256)
    width = N_QKV_BLOCKS * LANE
    rope = rope_tabs is not None
    in_specs = [
        pl.BlockSpec((1, tm, width), lambda b, t: (b, t, 0)),
        pl.BlockSpec((N_QKV_BLOCKS, 1, LANE), lambda b, t: (0, 0, 0)),
    ]
    args = [proj, gains]
    if rope:
        in_specs += [pl.BlockSpec((tm, LANE), lambda b, t: (t, 0))] * 4
        args += list(rope_tabs)
    return pl.pallas_call(
        functools.partial(_prep_kernel, rope=rope),
        grid=(nb, r // tm),
        in_specs=in_specs,
        out_specs=pl.BlockSpec((1, tm, width), lambda b, t: (b, t, 0)),
        out_shape=jax.ShapeDtypeStruct((nb, r, width), BF16),
        compiler_params=_cparams("parallel", "parallel"),
        name="qkv_prep",
    )(*args)


def _rope_tables(seq):
    pos = jnp.arange(seq)
    rows = (pos // GRID_W).astype(F32)[:, None]
    cols = (pos % GRID_W).astype(F32)[:, None]
    lane = jnp.arange(LANE)

    def tables(half):
        inv = ROPE_THETA ** (-(lane % half).astype(F32) / half)
        use_rows = (lane % (4 * half)) < 2 * half
        ang = jnp.where(use_rows[None, :], rows, cols) * inv[None, :]
        sign = jnp.where((lane % (2 * half)) < half, -1.0, 1.0)
        return jnp.cos(ang), jnp.sin(ang) * sign[None, :]

    ca, sa = tables(32)
    cb, sb = tables(16)
    return ca, sa, cb, sb


def _stack_heads(q):
    return jnp.concatenate([q[:, g * HEAD_DIM:(g + 1) * HEAD_DIM] for g in range(A_GROUP)], axis=0)


def _unstack_heads(o, rows):
    return jnp.concatenate([o[g * rows:(g + 1) * rows] for g in range(A_GROUP)], axis=1)


def _sink_column(sink_ref, kvh, rows):
    return jnp.concatenate(
        [jnp.full((rows, 1), sink_ref[kvh * A_GROUP + g], F32) for g in range(A_GROUP)], axis=0)


def _attn_a_lat_kernel(sink_ref, q_ref, kl_ref, vl_ref, kc_ref, vc_ref, o_ref, *, seq):
    kvh = pl.program_id(1)
    i = pl.program_id(2)
    band = 3 * Q_BLOCK
    qs = _stack_heads(q_ref[0])
    start = pl.multiple_of(jnp.clip((i - 1) * Q_BLOCK, 0, seq - band), Q_BLOCK)
    kb = kl_ref[0, pl.ds(start, band), :]
    vb = vl_ref[0, pl.ds(start, band), :]
    s_loc = lax.dot_general(qs, kb, NT_DIMS, preferred_element_type=F32)
    s_ctx = lax.dot_general(qs, kc_ref[0], NT_DIMS, preferred_element_type=F32)
    row = lax.broadcasted_iota(jnp.int32, s_loc.shape, 0)
    col = lax.broadcasted_iota(jnp.int32, s_loc.shape, 1)
    qpos = i * Q_BLOCK + (row % Q_BLOCK)
    kpos = start + col
    s_loc = jnp.where(jnp.abs(kpos - qpos) <= WINDOW, s_loc, NEG_BIG)
    sk = _sink_column(sink_ref, kvh, Q_BLOCK)
    m = jnp.maximum(jnp.maximum(jnp.max(s_loc, axis=-1, keepdims=True),
                                jnp.max(s_ctx, axis=-1, keepdims=True)), sk)
    p_loc = jnp.exp(s_loc - m)
    p_ctx = jnp.exp(s_ctx - m)
    den = (jnp.sum(p_loc, axis=-1, keepdims=True) + jnp.sum(p_ctx, axis=-1, keepdims=True)
           + jnp.exp(sk - m))
    o = (jnp.dot(p_loc.astype(BF16), vb, preferred_element_type=F32)
         + jnp.dot(p_ctx.astype(BF16), vc_ref[0], preferred_element_type=F32)) / den
    o_ref[0] = _unstack_heads(o, Q_BLOCK).astype(BF16)


def _attn_a_ctx_kernel(sink_ref, q_ref, kc_ref, vc_ref, o_ref):
    kvh = pl.program_id(1)
    rows = q_ref.shape[1]
    qs = _stack_heads(q_ref[0])
    s = lax.dot_general(qs, kc_ref[0], NT_DIMS, preferred_element_type=F32)
    sk = _sink_column(sink_ref, kvh, rows)
    m = jnp.maximum(jnp.max(s, axis=-1, keepdims=True), sk)
    p = jnp.exp(s - m)
    den = jnp.sum(p, axis=-1, keepdims=True) + jnp.exp(sk - m)
    o = jnp.dot(p.astype(BF16), vc_ref[0], preferred_element_type=F32) / den
    o_ref[0] = _unstack_heads(o, rows).astype(BF16)


def _attn_a(sink, p_l, p_c, ctx_out):
    b, seq, _ = p_l.shape
    lc = p_c.shape[1]
    gw = A_GROUP * HEAD_DIM
    smem = pl.BlockSpec(memory_space=pltpu.SMEM)
    o_l = pl.pallas_call(
        functools.partial(_attn_a_lat_kernel, seq=seq),
        grid=(b, A_KV_HEADS, seq // Q_BLOCK),
        in_specs=[
            smem,
            pl.BlockSpec((1, Q_BLOCK, gw), lambda n, h, i: (n, i, h)),
            pl.BlockSpec((1, seq, LANE), lambda n, h, i: (n, 0, COL_AK + h)),
            pl.BlockSpec((1, seq, LANE), lambda n, h, i: (n, 0, COL_AV + h)),
            pl.BlockSpec((1, lc, LANE), lambda n, h, i: (n, 0, COL_AK + h)),
            pl.BlockSpec((1, lc, LANE), lambda n, h, i: (n, 0, COL_AV + h)),
        ],
        out_specs=pl.BlockSpec((1, Q_BLOCK, gw), lambda n, h, i: (n, i, h)),
        out_shape=jax.ShapeDtypeStruct((b, seq, N_HEADS * HEAD_DIM), BF16),
        compiler_params=_cparams("parallel", "parallel", "arbitrary"),
        name="attn_a_latent",
    )(sink, p_l, p_l, p_l, p_c, p_c)
    o_c = None
    if ctx_out:
        o_c = pl.pallas_call(
            _attn_a_ctx_kernel,
            grid=(b, A_KV_HEADS),
            in_specs=[
                smem,
                pl.BlockSpec((1, lc, gw), lambda n, h: (n, 0, h)),
                pl.BlockSpec((1, lc, LANE), lambda n, h: (n, 0, COL_AK + h)),
                pl.BlockSpec((1, lc, LANE), lambda n, h: (n, 0, COL_AV + h)),
            ],
            out_specs=pl.BlockSpec((1, lc, gw), lambda n, h: (n, 0, h)),
            out_shape=jax.ShapeDtypeStruct((b, lc, N_HEADS * HEAD_DIM), BF16),
            compiler_params=_cparams("parallel", "parallel"),
            name="attn_a_ctx",
        )(sink, p_c, p_c, p_c)
    return o_c, o_l


def _attn_b_kernel(lam_ref, q_ref, *refs, nseg, post_scale):
    k_refs = refs[:nseg]
    v_refs = refs[nseg:2 * nseg]
    g_ref, o_ref = refs[2 * nseg], refs[2 * nseg + 1]
    q = q_ref[0]
    tq = q.shape[0]
    lane = lax.broadcasted_iota(jnp.int32, q.shape, 1)
    zero = jnp.zeros_like(q)
    q2 = jnp.concatenate([jnp.where(lane < B_DIM, q, zero), jnp.where(lane < B_DIM, zero, q)], axis=0)
    m = jnp.full((2 * tq, 1), NEG_BIG, F32)
    den = jnp.zeros((2 * tq, 1), F32)
    acc = jnp.zeros((2 * tq, LANE), F32)
    for k_ref, v_ref in zip(k_refs, v_refs):
        nk = k_ref.shape[1]
        kc = _tile(nk, B_KEY_CHUNK)
        for c0 in range(0, nk, kc):
            s = lax.dot_general(q2, k_ref[0, c0:c0 + kc, :], NT_DIMS, preferred_element_type=F32)
            m_new = jnp.maximum(m, jnp.max(s, axis=-1, keepdims=True))
            alpha = jnp.exp2(m - m_new)
            p = jnp.exp2(s - m_new)
            den = alpha * den + jnp.sum(p, axis=-1, keepdims=True)
            acc = alpha * acc + jnp.dot(p.astype(BF16), v_ref[0, c0:c0 + kc, :],
                                        preferred_element_type=F32)
            m = m_new
    o2 = acc / den
    o = o2[:tq] - lam_ref[0] * o2[tq:]
    ms = jnp.mean(o * o, axis=-1, keepdims=True)
    o_ref[0] = ((o * lax.rsqrt(ms + EPS) * g_ref[...]) * post_scale).astype(BF16)


def _attn_b_call(lam, q_src, k_srcs, subln, post_scale, name):
    b, rq, _ = q_src.shape
    tq = _tile(rq, 512)
    smem = pl.BlockSpec(memory_space=pltpu.SMEM)
    in_specs = [smem, pl.BlockSpec((1, tq, LANE), lambda n, h, i: (n, i, COL_BQ + h))]
    in_specs += [pl.BlockSpec((1, s.shape[1], LANE), lambda n, h, i: (n, 0, COL_BK + h)) for s in k_srcs]
    in_specs += [pl.BlockSpec((1, s.shape[1], LANE), lambda n, h, i: (n, 0, COL_BV + h)) for s in k_srcs]
    in_specs += [pl.BlockSpec((1, LANE), lambda n, h, i: (0, 0))]
    return pl.pallas_call(
        functools.partial(_attn_b_kernel, nseg=len(k_srcs), post_scale=post_scale),
        grid=(b, N_HEADS, rq // tq),
        in_specs=in_specs,
        out_specs=pl.BlockSpec((1, tq, LANE), lambda n, h, i: (n, i, h)),
        out_shape=jax.ShapeDtypeStruct((b, rq, N_HEADS * HEAD_DIM), BF16),
        compiler_params=_cparams("parallel", "parallel", "arbitrary"),
        name=name,
    )(lam, q_src, *k_srcs, *k_srcs, subln)


def _attn_b(lam, p_l, p_c, subln, post_scale, ctx_out):
    o_l = _attn_b_call(lam, p_l, [p_c, p_l], subln, post_scale, "attn_b_latent")
    o_c = _attn_b_call(lam, p_c, [p_c], subln, post_scale, "attn_b_ctx") if ctx_out else None
    return o_c, o_l


def _centred_conv(x_ref, pad_ref, w_ref, b_ref):
    n = x_ref.shape[1]
    zeros = jnp.zeros((SUBLANE, LANE), F32)
    pad_ref[0:SUBLANE, :] = zeros
    pad_ref[SUBLANE:SUBLANE + n, :] = x_ref[0]
    pad_ref[SUBLANE + n:2 * SUBLANE + n, :] = zeros
    out = b_ref[...]
    for tap in range(4):
        out = out + pad_ref[pl.ds(SUBLANE - 2 + tap, n), :] * w_ref[tap:tap + 1, :]
    return out


def _tile_scan(a, v, reverse):
    n = a.shape[0]
    row = lax.broadcasted_iota(jnp.int32, a.shape, 0) % SUBLANE
    for s in (1, 2, 4):
        if reverse:
            keep = row < SUBLANE - s
            a_sh = jnp.where(keep, pltpu.roll(a, n - s, 0), 1.0)
            v_sh = jnp.where(keep, pltpu.roll(v, n - s, 0), 0.0)
        else:
            keep = row >= s
            a_sh = jnp.where(keep, pltpu.roll(a, s, 0), 1.0)
            v_sh = jnp.where(keep, pltpu.roll(v, s, 0), 0.0)
        v = v + a * v_sh
        a = a * a_sh
    return a, v


def _seg_tile(t, ntc, ntl, reverse):
    if not reverse:
        return t
    return jnp.where(t < ntc, ntc - 1 - t, 2 * ntc + ntl - 1 - t)


def _lru_kernel(xc_ref, yc_ref, xl_ref, yl_ref, cw_ref, cb_ref, wr_ref, br_ref, wi_ref, bi_ref,
                lam_ref, *refs, ctx_out):
    if ctx_out:
        oc_ref, ol_ref, ac_ref, hl_ref, hs_ref, pad_ref = refs
    else:
        ol_ref, ac_ref, hl_ref, hs_ref, pad_ref = refs
    lc, seq = xc_ref.shape[1], xl_ref.shape[1]
    ntc, ntl = lc // SUBLANE, seq // SUBLANE
    for x_ref, off in ((xc_ref, 0), (xl_ref, lc)):
        n = x_ref.shape[1]
        u = _centred_conv(x_ref, pad_ref, cw_ref, cb_ref)
        ub = u.astype(BF16)
        for d in range(2):
            lam = lam_ref[d:d + 1, :]
            sp = jnp.maximum(-lam, 0.0) + jnp.log1p(jnp.exp(-jnp.abs(lam)))
            r = _sigmoid(jnp.dot(ub, wr_ref[d, 0], preferred_element_type=F32) + br_ref[d:d + 1, :])
            gi = _sigmoid(jnp.dot(ub, wi_ref[d, 0], preferred_element_type=F32) + bi_ref[d:d + 1, :])
            log_a = -LRU_C * r * sp
            a = jnp.exp(log_a)
            y = jnp.tanh(-log_a) * (1.0 + a * a)
            v = (y * lax.rsqrt(jnp.maximum(y, TINY))) * gi * u
            a_cum, h_loc = _tile_scan(a, v, reverse=(d == 1))
            ac_ref[d, off:off + n, :] = a_cum
            hl_ref[d, off:off + n, :] = h_loc

    def step(t, carry):
        new = []
        for d in range(2):
            tile = _seg_tile(t, ntc, ntl, d == 1)
            r0 = pl.multiple_of(tile * SUBLANE, SUBLANE)
            h = hl_ref[d, pl.ds(r0, SUBLANE), :] + ac_ref[d, pl.ds(r0, SUBLANE), :] * carry[d]
            if d == 0:
                hs_ref[pl.ds(r0, SUBLANE), :] = h
                new.append(h[SUBLANE - 1:SUBLANE, :])
            else:
                hl_ref[d, pl.ds(r0, SUBLANE), :] = h
                new.append(h[0:1, :])
        return tuple(new)

    zero = jnp.zeros((1, LANE), F32)
    lax.fori_loop(0, ntc + ntl, step, (zero, zero))
    if ctx_out:
        hc = hs_ref[0:lc, :] + hl_ref[1, 0:lc, :]
        oc_ref[0] = (hc * jax.nn.gelu(yc_ref[0])).astype(BF16)
    hl = hs_ref[lc:lc + seq, :] + hl_ref[1, lc:lc + seq, :]
    ol_ref[0] = (hl * jax.nn.gelu(yl_ref[0])).astype(BF16)


def _lru(proj_l, proj_c, conv_w, conv_b, w_r, b_r, w_i, b_i, lam, ctx_out):
    b, seq, _ = proj_l.shape
    lc = proj_c.shape[1]
    nblk = w_r.shape[1]
    width = nblk * LANE

    def seg(n, col):
        return pl.BlockSpec((1, n, LANE), lambda i, j: (i, 0, col + j))

    vec2 = pl.BlockSpec((2, LANE), lambda i, j: (0, j))
    wspec = pl.BlockSpec((2, 1, LANE, LANE), lambda i, j: (0, j, 0, 0))
    out_specs = [pl.BlockSpec((1, seq, LANE), lambda i, j: (i, 0, j))]
    out_shape = [jax.ShapeDtypeStruct((b, seq, width), BF16)]
    if ctx_out:
        out_specs.insert(0, pl.BlockSpec((1, lc, LANE), lambda i, j: (i, 0, j)))
        out_shape.insert(0, jax.ShapeDtypeStruct((b, lc, width), BF16))
    outs = pl.pallas_call(
        functools.partial(_lru_kernel, ctx_out=ctx_out),
        grid=(b, nblk),
        in_specs=[
            seg(lc, COL_CX), seg(lc, COL_CY), seg(seq, COL_CX), seg(seq, COL_CY),
            pl.BlockSpec((4, LANE), lambda i, j: (0, j)),
            pl.BlockSpec((1, LANE), lambda i, j: (0, j)),
            wspec, vec2, wspec, vec2, vec2,
        ],
        out_specs=out_specs,
        out_shape=out_shape,
        scratch_shapes=[
            pltpu.VMEM((2, lc + seq, LANE), F32),
            pltpu.VMEM((2, lc + seq, LANE), F32),
            pltpu.VMEM((lc + seq, LANE), F32),
            pltpu.VMEM((max(lc, seq) + 2 * SUBLANE, LANE), F32),
        ],
        compiler_params=_cparams("parallel", "parallel"),
        name="rglru",
    )(proj_c, proj_c, proj_l, proj_l, conv_w, conv_b.reshape(1, width), w_r, b_r, w_i, b_i, lam)
    return (outs[0], outs[1]) if ctx_out else (None, outs[0])


def _gla_level_masks(reverse):
    c = GLA_CHUNK
    row = lax.broadcasted_iota(jnp.int32, (c, c), 0)
    col = lax.broadcasted_iota(jnp.int32, (c, c), 1)
    masks = {}
    for s in (32, 16, 8):
        same = (row // (2 * s)) == (col // (2 * s))
        if reverse:
            masks[s] = same & ((row % (2 * s)) < s) & ((col % (2 * s)) >= s)
        else:
            masks[s] = same & ((row % (2 * s)) >= s) & ((col % (2 * s)) < s)
    tri = (row <= col) if reverse else (row >= col)
    return masks, tri.astype(F32)


def _gla_chunk(q, k, v, g, st, masks, tri, reverse):
    c = GLA_CHUNK
    b = jnp.dot(tri, g, precision=lax.Precision.HIGHEST, preferred_element_type=F32)
    att = jnp.zeros((c, c), F32)
    for s in (32, 16, 8):
        b3 = b.reshape(c // (2 * s), 2 * s, LANE)
        rr = s if reverse else s - 1
        rho = jnp.broadcast_to(b3[:, rr:rr + 1, :], b3.shape).reshape(c, LANE)
        e = jnp.exp(-jnp.abs(b - rho))
        a = lax.dot_general((q * e).astype(BF16), (k * e).astype(BF16), NT_DIMS,
                            preferred_element_type=F32)
        att = att + jnp.where(masks[s], a, 0.0)
    vb = v.astype(BF16)
    o = jnp.dot(att.astype(BF16), vb, preferred_element_type=F32)
    nb = c // SUBLANE
    b3 = b.reshape(nb, SUBLANE, LANE)
    q3 = q.reshape(nb, SUBLANE, LANE)
    k3 = k.reshape(nb, SUBLANE, LANE)
    v3 = v.reshape(nb, SUBLANE, LANE)
    rowi = lax.broadcasted_iota(jnp.int32, b3.shape, 1)
    od = jnp.zeros(b3.shape, F32)
    for jj in range(SUBLANE):
        keep = (rowi <= jj) if reverse else (rowi >= jj)
        e = jnp.where(keep, jnp.exp(jnp.minimum(b3 - b3[:, jj:jj + 1, :], 0.0)), 0.0)
        sj = jnp.sum(q3 * e * k3[:, jj:jj + 1, :], axis=-1, keepdims=True)
        od = od + sj * v3[:, jj:jj + 1, :]
    o = o + od.reshape(c, LANE)
    o = o + lax.dot_general((q * jnp.exp(b)).astype(BF16), st.astype(BF16), NT_DIMS,
                            preferred_element_type=F32)
    b_end = b[0:1, :] if reverse else b[c - 1:c, :]
    khat = (k * jnp.exp(b_end - b)).astype(BF16)
    st_new = st * jnp.exp(b_end) + lax.dot_general(vb, khat, TN_DIMS, preferred_element_type=F32)
    return o, st_new


def _gla_fast_intra(q, k, v, g, tri, reverse):
    c = GLA_CHUNK
    g_hi = g.astype(BF16)
    g_lo = (g - g_hi.astype(F32)).astype(BF16)
    b2 = jnp.dot(tri.astype(BF16), jnp.concatenate([g_hi, g_lo], axis=1), preferred_element_type=F32)
    b = b2[:, :LANE] + b2[:, LANE:]
    rr = c // 2 if reverse else c // 2 - 1
    rho = b[rr:rr + 1, :]
    qt = q * jnp.exp(b - rho)
    kt = k * jnp.exp(rho - b)
    a = lax.dot_general(qt.astype(BF16), kt.astype(BF16), NT_DIMS, preferred_element_type=F32)
    att = jnp.where(tri > 0.0, a, 0.0)
    vb = v.astype(BF16)
    o_intra = jnp.dot(att.astype(BF16), vb, preferred_element_type=F32)
    b_end = b[0:1, :] if reverse else b[c - 1:c, :]
    qe = (qt * jnp.exp(rho)).astype(BF16)
    khat = (kt * jnp.exp(b_end - rho)).astype(BF16)
    return o_intra, qe, khat, vb, jnp.exp(b_end)


def _gla_fast_inter(intra, st):
    o_intra, qe, khat, vb, decay = intra
    o = o_intra + lax.dot_general(qe, st.astype(BF16), NT_DIMS, preferred_element_type=F32)
    st_new = st * decay + lax.dot_general(vb, khat, TN_DIMS, preferred_element_type=F32)
    return o, st_new


def _gla_kernel(lb_ref, on_ref, qc_ref, ffc_ref, fbc_ref, ic_ref, gc_ref,
                ql_ref, ffl_ref, fbl_ref, il_ref, gl_ref, *refs, ctx_out):
    if ctx_out:
        oc_ref, ol_ref, q_s, v_s, g_s, k_s, o_s, st_s = refs
    else:
        ol_ref, q_s, v_s, g_s, k_s, o_s, st_s = refs
    lc, seq = qc_ref.shape[1], ql_ref.shape[1]
    ncc, ncl = lc // GLA_CHUNK, seq // GLA_CHUNK
    for off, n, q_ref, i_ref, f_refs in ((0, lc, qc_ref, ic_ref, (ffc_ref, fbc_ref)),
                                         (lc, seq, ql_ref, il_ref, (ffl_ref, fbl_ref))):
        q_s[off:off + n, :] = q_ref[0]
        v_s[off:off + n, :] = i_ref[0]
        for d in range(2):
            z = f_refs[d][0]
            lbd = lb_ref[d:d + 1, :]
            ez = jnp.exp(-jnp.abs(z))
            r = 1.0 / (1.0 + ez)
            pos = z >= 0.0
            sig_p = jnp.where(pos, r, ez * r)
            sig_n = jnp.where(pos, ez * r, r)
            g_s[d, off:off + n, :] = jnp.log(lbd + (1.0 - lbd) * sig_p)
            k_s[d, off:off + n, :] = (1.0 - lbd) * sig_n
    st_s[...] = jnp.zeros(st_s.shape, F32)
    consts = [_gla_level_masks(False), _gla_level_masks(True)]

    half = GLA_CHUNK // 2
    worst = jnp.zeros((1, LANE), F32)
    for d in range(2):
        hs = jnp.sum(g_s[d].reshape((lc + seq) // half, half, LANE), axis=1)
        worst = jnp.maximum(worst, jnp.max(-hs, axis=0, keepdims=True))
    safe = jnp.max(worst) < GLA_SAFE_DECAY

    def chunk_rows(cidx, d):
        chunk = _seg_tile(cidx, ncc, ncl, d == 1)
        return pl.ds(pl.multiple_of(chunk * GLA_CHUNK, GLA_CHUNK), GLA_CHUNK)

    def robust_step(cidx, carry):
        for d in range(2):
            rows = chunk_rows(cidx, d)
            o, st_new = _gla_chunk(q_s[rows, :], k_s[d, rows, :], v_s[rows, :], g_s[d, rows, :],
                                   st_s[d], consts[d][0], consts[d][1], d == 1)
            st_s[d] = st_new
            o_s[d, rows, :] = o
        return carry

    nchunks = ncc + ncl
    group = next(u for u in (4, 3, 2, 1) if nchunks % u == 0)

    def fast_step(t, carry):
        work = [[] for _ in range(2)]
        for d in range(2):
            for u in range(group):
                rows = chunk_rows(t * group + u, d)
                work[d].append((rows, _gla_fast_intra(q_s[rows, :], k_s[d, rows, :], v_s[rows, :],
                                                      g_s[d, rows, :], consts[d][1], d == 1)))
        for d in range(2):
            st = st_s[d]
            outs = []
            for rows, intra in work[d]:
                o, st = _gla_fast_inter(intra, st)
                outs.append((rows, o))
            st_s[d] = st
            for rows, o in outs:
                o_s[d, rows, :] = o
        return carry

    @pl.when(safe)
    def _():
        lax.fori_loop(0, nchunks // group, fast_step, 0)

    @pl.when(jnp.logical_not(safe))
    def _():
        lax.fori_loop(0, nchunks, robust_step, 0)

    def finish(o, gate):
        ms = jnp.mean(o * o, axis=-1, keepdims=True)
        return ((o * lax.rsqrt(ms + EPS) * on_ref[...]) * (gate * _sigmoid(gate))).astype(BF16)

    if ctx_out:
        oc_ref[0] = finish(o_s[0, 0:lc, :] + o_s[1, 0:lc, :], gc_ref[0])
    ol_ref[0] = finish(o_s[0, lc:lc + seq, :] + o_s[1, lc:lc + seq, :], gl_ref[0])


def _gla(proj_l, proj_c, lb, onorm, ctx_out):
    b, seq, _ = proj_l.shape
    lc = proj_c.shape[1]
    width = N_HEADS * HEAD_DIM
    nt = lc + seq

    def seg(n, col):
        return pl.BlockSpec((1, n, LANE), lambda i, h: (i, 0, col + h))

    cols = (COL_DQ, COL_DFF, COL_DFB, COL_DI, COL_DG)
    out_specs = [pl.BlockSpec((1, seq, LANE), lambda i, h: (i, 0, h))]
    out_shape = [jax.ShapeDtypeStruct((b, seq, width), BF16)]
    if ctx_out:
        out_specs.insert(0, pl.BlockSpec((1, lc, LANE), lambda i, h: (i, 0, h)))
        out_shape.insert(0, jax.ShapeDtypeStruct((b, lc, width), BF16))
    outs = pl.pallas_call(
        functools.partial(_gla_kernel, ctx_out=ctx_out),
        grid=(b, N_HEADS),
        in_specs=[pl.BlockSpec((2, LANE), lambda i, h: (0, h)),
                  pl.BlockSpec((1, LANE), lambda i, h: (0, 0))]
        + [seg(lc, c) for c in cols] + [seg(seq, c) for c in cols],
        out_specs=out_specs,
        out_shape=out_shape,
        scratch_shapes=[
            pltpu.VMEM((nt, LANE), F32),
            pltpu.VMEM((nt, LANE), F32),
            pltpu.VMEM((2, nt, LANE), F32),
            pltpu.VMEM((2, nt, LANE), F32),
            pltpu.VMEM((2, nt, LANE), F32),
            pltpu.VMEM((2, LANE, LANE), F32),
        ],
        compiler_params=_cparams("parallel", "parallel"),
        name="hgrn2",
    )(lb, onorm, *([proj_c] * 5), *([proj_l] * 5))
    return (outs[0], outs[1]) if ctx_out else (None, outs[0])


def _merge_kernel(oa_ref, ob_ref, oc_ref, od_ref, g0_ref, g1_ref, g2_ref, g3_ref,
                  w0_ref, w1_ref, w2_ref, w3_ref, y_ref):
    acc = None
    for o_ref, g_ref, w_ref in ((oa_ref, g0_ref, w0_ref), (ob_ref, g1_ref, w1_ref),
                                (oc_ref, g2_ref, w2_ref), (od_ref, g3_ref, w3_ref)):
        t = _sigmoid(g_ref[0].astype(F32)) * jnp.dot(o_ref[0], w_ref[0], preferred_element_type=F32)
        acc = t if acc is None else acc + t
    y_ref[0] = acc.astype(BF16)


def _merge(outs, proj, w_branch):
    nb, r, mw = outs[0].shape
    d = w_branch.shape[2]
    tm = _tile(r, 512)
    tn = 512
    nj = d // tn
    o_spec = pl.BlockSpec((1, tm, mw), lambda b, t, j: (b, t, 0))
    g_specs = [pl.BlockSpec((1, tm, tn), functools.partial(
        lambda b, t, j, n: (b, t, (COL_GATE * LANE) // tn + n * nj + j), n=n)) for n in range(4)]
    w_specs = [pl.BlockSpec((1, mw, tn), functools.partial(lambda b, t, j, n: (n, 0, j), n=n))
               for n in range(4)]
    return pl.pallas_call(
        _merge_kernel,
        grid=(nb, r // tm, nj),
        in_specs=[o_spec] * 4 + g_specs + w_specs,
        out_specs=pl.BlockSpec((1, tm, tn), lambda b, t, j: (b, t, j)),
        out_shape=jax.ShapeDtypeStruct((nb, r, d), BF16),
        compiler_params=_cparams("parallel", "parallel", "arbitrary"),
        name="branch_merge",
    )(*outs, *([proj] * 4), *([w_branch] * 4))


def _out_proj_kernel(y_ref, w_ref, x_ref, g_ref, o_ref):
    o_ref[0] = x_ref[0] + g_ref[0] * jnp.dot(y_ref[0], w_ref[...], preferred_element_type=F32)


def _out_proj(y, w, x, gate):
    nb, r, d = x.shape
    tm = _tile(r, 1024)
    tn = 512
    return pl.pallas_call(
        _out_proj_kernel,
        grid=(nb, r // tm, d // tn),
        in_specs=[
            pl.BlockSpec((1, tm, d), lambda b, t, j: (b, t, 0)),
            pl.BlockSpec((d, tn), lambda b, t, j: (0, j)),
            pl.BlockSpec((1, tm, tn), lambda b, t, j: (b, t, j)),
            pl.BlockSpec((1, 1, tn), lambda b, t, j: (b, 0, j)),
        ],
        out_specs=pl.BlockSpec((1, tm, tn), lambda b, t, j: (b, t, j)),
        out_shape=jax.ShapeDtypeStruct((nb, r, d), F32),
        compiler_params=_cparams("parallel", "parallel", "arbitrary"),
        name="out_proj",
    )(y, w, x, gate)


def _route(logits_t, bias_col):
    aff = _sigmoid(logits_t)
    sel = aff + bias_col
    aff_r = [aff[e:e + 1, :] for e in range(N_EXPERTS)]
    sel_r = [sel[e:e + 1, :] for e in range(N_EXPERTS)]
    scores = []
    for g in range(N_GROUPS):
        v = sel_r[g * EXPERTS_PER_GROUP:(g + 1) * EXPERTS_PER_GROUP]
        m1 = functools.reduce(jnp.maximum, v)
        taken = jnp.zeros(m1.shape, jnp.bool_)
        second = jnp.full(m1.shape, -jnp.inf, F32)
        for x in v:
            first = (x == m1) & jnp.logical_not(taken)
            taken = taken | first
            second = jnp.where(first, second, jnp.maximum(second, x))
        scores.append(m1 + second)
    best, gidx = scores[0], jnp.zeros(scores[0].shape, jnp.int32)
    for g in range(1, N_GROUPS):
        better = scores[g] > best
        gidx = jnp.where(better, g, gidx)
        best = jnp.where(better, scores[g], best)
    masked = [jnp.where(gidx == e // EXPERTS_PER_GROUP, sel_r[e], -jnp.inf) for e in range(N_EXPERTS)]

    def first_argmax(vals, exclude):
        bv = jnp.full(vals[0].shape, -jnp.inf, F32)
        bi = jnp.full(vals[0].shape, -1, jnp.int32)
        for e, x in enumerate(vals):
            better = x > bv
            if exclude is not None:
                better = better & (exclude != e)
            bi = jnp.where(better, e, bi)
            bv = jnp.where(better, x, bv)
        return bi

    i1 = first_argmax(masked, None)
    i2 = first_argmax(masked, i1)
    w1 = functools.reduce(jnp.add, [jnp.where(i1 == e, aff_r[e], 0.0) for e in range(N_EXPERTS)])
    w2 = functools.reduce(jnp.add, [jnp.where(i2 == e, aff_r[e], 0.0) for e in range(N_EXPERTS)])
    tot = w1 + w2
    g1, g2 = w1 / tot, w2 / tot
    width = logits_t.shape[1]
    rowi = lax.broadcasted_iota(jnp.int32, (SUBLANE, width), 0)
    local1 = jnp.broadcast_to(i1 - gidx * EXPERTS_PER_GROUP, (SUBLANE, width))
    local2 = jnp.broadcast_to(i2 - gidx * EXPERTS_PER_GROUP, (SUBLANE, width))
    info = (jnp.where(rowi == local1, jnp.broadcast_to(g1, (SUBLANE, width)), 0.0)
            + jnp.where(rowi == local2, jnp.broadcast_to(g2, (SUBLANE, width)), 0.0))
    return jnp.where(rowi == EXPERTS_PER_GROUP,
                     jnp.broadcast_to(gidx.astype(F32), (SUBLANE, width)), info)


def _moe_route_kernel(x_ref, g_ref, sh_ref, sc_ref, wrt_ref, br_ref, hn_ref, info_ref):
    h = _modulated_norm(x_ref[0], g_ref[...], sh_ref[0], sc_ref[0])
    hn_ref[0] = h
    logits_t = lax.dot_general(wrt_ref[...], h, NT_DIMS, precision=lax.Precision.HIGHEST,
                               preferred_element_type=F32)
    info_ref[0] = _route(logits_t, br_ref[...])


def _moe_route(x, g, sh, sc, w_router_t, b_router):
    nb, r, d = x.shape
    tm = _tile(r, 512)
    vec = pl.BlockSpec((1, 1, d), lambda b, t: (b, 0, 0))
    return pl.pallas_call(
        _moe_route_kernel,
        grid=(nb, r // tm),
        in_specs=[
            pl.BlockSpec((1, tm, d), lambda b, t: (b, t, 0)),
            pl.BlockSpec((1, d), lambda b, t: (0, 0)),
            vec, vec,
            pl.BlockSpec((N_EXPERTS, d), lambda b, t: (0, 0)),
            pl.BlockSpec((N_EXPERTS, 1), lambda b, t: (0, 0)),
        ],
        out_specs=[pl.BlockSpec((1, tm, d), lambda b, t: (b, t, 0)),
                   pl.BlockSpec((1, SUBLANE, tm), lambda b, t: (b, 0, t))],
        out_shape=[jax.ShapeDtypeStruct((nb, r, d), F32),
                   jax.ShapeDtypeStruct((nb, SUBLANE, r), F32)],
        compiler_params=_cparams("parallel", "parallel"),
        name="moe_route",
    )(x, g, sh, sc, w_router_t, b_router.reshape(N_EXPERTS, 1))


def _row_copy(src, src_row, dst, dst_row, sem):
    return pltpu.make_async_copy(src.at[pl.ds(src_row, 1), :], dst.at[pl.ds(dst_row, 1), :], sem)


def _moe_experts_kernel(src_ref, tgrp_ref, nval_ref, hn_hbm, gates_ref, wg_ref, wu_ref, wd_ref,
                        y_hbm, xbuf, hb, acc, gsem, ssem, *, tm):
    i = pl.program_id(0)
    e = pl.program_id(1)
    nt = pl.num_programs(0)
    slot = i % 2
    per = tm // EXPERTS_PER_GROUP

    def wait_scatter(tile, sl):
        def body(r, c):
            _row_copy(acc.at[sl], 0, y_hbm, 0, ssem.at[sl]).wait()
            return c
        lax.fori_loop(0, nval_ref[tile], body, 0)

    @pl.when((i == 0) & (e == 0))
    def _():
        def body(r, c):
            _row_copy(hn_hbm, src_ref[r], xbuf.at[0], r, gsem.at[0]).start()
            return c
        lax.fori_loop(0, tm, body, 0)

    @pl.when(e == 0)
    def _():
        pltpu.make_async_copy(hn_hbm.at[pl.ds(0, tm), :], xbuf.at[slot], gsem.at[slot]).wait()
        hb[...] = xbuf[slot].astype(BF16)

        @pl.when(i >= 2)
        def _():
            wait_scatter(i - 2, slot)
        acc[slot] = jnp.zeros(acc.shape[1:], F32)

    @pl.when(i + 1 < nt)
    def _():
        base = (i + 1) * tm + e * per
        for k in range(per):
            _row_copy(hn_hbm, src_ref[base + k], xbuf.at[1 - slot], e * per + k, gsem.at[1 - slot]).start()

    @pl.when(nval_ref[i] > 0)
    def _():
        hv = hb[...]
        a = jnp.dot(hv, wg_ref[0], preferred_element_type=F32)
        u = jnp.dot(hv, wu_ref[0], preferred_element_type=F32)
        hid = ((a * _sigmoid(a)) * u).astype(BF16)
        y = jnp.dot(hid, wd_ref[0], preferred_element_type=F32)
        lane = lax.broadcasted_iota(jnp.int32, gates_ref.shape, 1)
        col = jnp.sum(jnp.where(lane == e, gates_ref[...], 0.0), axis=-1, keepdims=True)
        acc[slot] = acc[slot] + col * y

    @pl.when(e == EXPERTS_PER_GROUP - 1)
    def _():
        def body(r, c):
            _row_copy(acc.at[slot], r, y_hbm, src_ref[i * tm + r], ssem.at[slot]).start()
            return c
        lax.fori_loop(0, nval_ref[i], body, 0)

        @pl.when(i == nt - 1)
        def _():
            wait_scatter(i, slot)

            @pl.when(i >= 1)
            def _():
                wait_scatter(i - 1, 1 - slot)


def _moe_experts(hn, src, tile_group, tile_nvalid, gates_sorted, w_gate, w_up, w_down, tm):
    t, d = hn.shape
    dff = w_gate.shape[2]
    ntiles = tile_group.shape[0]

    def wmap(i, e, src_r, tgrp_r, nval_r):
        return (tgrp_r[i] * EXPERTS_PER_GROUP + e, 0, 0)

    grid_spec = pltpu.PrefetchScalarGridSpec(
        num_scalar_prefetch=3,
        grid=(ntiles, EXPERTS_PER_GROUP),
        in_specs=[
            pl.BlockSpec(memory_space=pl.ANY),
            pl.BlockSpec((tm, EXPERTS_PER_GROUP), lambda i, e, *_: (i, 0)),
            pl.BlockSpec((1, d, dff), wmap),
            pl.BlockSpec((1, d, dff), wmap),
            pl.BlockSpec((1, dff, d), wmap),
        ],
        out_specs=pl.BlockSpec(memory_space=pl.ANY),
        scratch_shapes=[
            pltpu.VMEM((2, tm, d), F32),
            pltpu.VMEM((tm, d), BF16),
            pltpu.VMEM((2, tm, d), F32),
            pltpu.SemaphoreType.DMA((2,)),
            pltpu.SemaphoreType.DMA((2,)),
        ],
    )
    return pl.pallas_call(
        functools.partial(_moe_experts_kernel, tm=tm),
        grid_spec=grid_spec,
        out_shape=jax.ShapeDtypeStruct((t, d), F32),
        compiler_params=_cparams("arbitrary", "arbitrary"),
        name="moe_experts",
    )(src, tile_group, tile_nvalid, hn, gates_sorted, w_gate, w_up, w_down)


def _residual_kernel(x_ref, y_ref, g_ref, o_ref):
    o_ref[0] = x_ref[0] + g_ref[0] * y_ref[0]


def _residual(x, y, gate):
    nb, r, d = x.shape
    tm = _tile(r, 512)
    blk = pl.BlockSpec((1, tm, d), lambda b, t: (b, t, 0))
    return pl.pallas_call(
        _residual_kernel,
        grid=(nb, r // tm),
        in_specs=[blk, blk, pl.BlockSpec((1, 1, d), lambda b, t: (b, 0, 0))],
        out_specs=blk,
        out_shape=jax.ShapeDtypeStruct((nb, r, d), F32),
        compiler_params=_cparams("parallel", "parallel"),
        name="moe_residual",
    )(x, y, gate)


def _group_layout(grp, gates4, tm):
    t = grp.shape[0]
    ntiles = t // tm + N_GROUPS
    onehot = (grp[:, None] == jnp.arange(N_GROUPS)[None, :]).astype(jnp.int32)
    counts = jnp.sum(onehot, axis=0)
    rank = jnp.sum((jnp.cumsum(onehot, axis=0) - onehot) * onehot, axis=1)
    padded = ((counts + tm - 1) // tm) * tm
    start = jnp.cumsum(padded) - padded
    slot_of_token = start[grp] + rank
    nslots = ntiles * tm
    src = jnp.zeros((nslots,), jnp.int32).at[slot_of_token].set(jnp.arange(t, dtype=jnp.int32))
    gates_sorted = jnp.zeros((nslots, EXPERTS_PER_GROUP), F32).at[slot_of_token].set(gates4)
    tile_start = jnp.arange(ntiles, dtype=jnp.int32) * tm
    ends = jnp.cumsum(padded)
    tile_group = jnp.minimum(jnp.sum((tile_start[:, None] >= ends[None, :]).astype(jnp.int32), axis=1),
                             N_GROUPS - 1).astype(jnp.int32)
    tile_nvalid = jnp.clip(counts[tile_group] - (tile_start - start[tile_group]), 0, tm).astype(jnp.int32)
    return src, tile_group, tile_nvalid, gates_sorted


def _moe(x, g, sh, sc, gate, w_router_t, b_router, w_gate, w_up, w_down):
    nb, r, d = x.shape
    t = nb * r
    tm = 512 if t >= 8192 else (256 if t >= 2048 else 128)
    hn, info = _moe_route(x, g, sh, sc, w_router_t, b_router)
    grp = info[:, EXPERTS_PER_GROUP, :].reshape(t).astype(jnp.int32)
    gates4 = jnp.swapaxes(info[:, :EXPERTS_PER_GROUP, :], 1, 2).reshape(t, EXPERTS_PER_GROUP)
    src, tile_group, tile_nvalid, gates_sorted = _group_layout(grp, gates4, tm)
    y = _moe_experts(hn.reshape(t, d), src, tile_group, tile_nvalid, gates_sorted,
                     w_gate, w_up, w_down, tm)
    return _residual(x, y.reshape(nb, r, d), gate)


def kernel(x, c, ctx, c_ctx, w_ada, b_ada, norm1_g, norm2_g, w_in, qn_a, kn_a, sink_a, qn_b, kn_b,
           lam_b, subln_b, conv_w, conv_b, w_rg, b_rg, w_ig, b_ig, lru_lambda, lb_d, onorm_d,
           w_branch, w_out, w_router, b_router, w_gate, w_up, w_down):
    bsz, seq, d = x.shape
    lc = ctx.shape[1]
    depth = w_in.shape[0]

    n_rows = -(-(bsz + 1) // SUBLANE) * SUBLANE
    c_pad = jnp.zeros((n_rows, d), F32).at[:bsz].set(c).at[bsz].set(c_ctx)
    mod = _ada(c_pad, w_ada, b_ada).reshape(depth, n_rows, 6, d)

    lb_w = jax.nn.softmax(lb_d.astype(F32), axis=0)
    lb_all = jnp.cumsum(lb_w, axis=0) - lb_w[0:1]
    rope_tabs = _rope_tables(seq)
    w_router_t = w_router.T

    xl = x
    xc = ctx.reshape(1, bsz * lc, d)
    for l in range(depth):
        ctx_out = l < depth - 1
        mod_l = [mod[l, :bsz, k][:, None, :] for k in range(6)]
        mod_c = [mod[l, bsz:bsz + 1, k][:, None, :] for k in range(6)]
        w_cols = jnp.concatenate([w_in[l][:, a:b] for a, b in (W_IN_QKV, W_IN_GATE, W_IN_REC)], axis=1)
        w_in_l = w_cols.astype(BF16).reshape(d, -1, PROJ_TILE).transpose(1, 0, 2)
        n_lo = (W_IN_QKV[1] - W_IN_QKV[0] + W_IN_GATE[1] - W_IN_GATE[0]) // PROJ_TILE
        g1 = norm1_g[l].reshape(1, d)
        g2 = norm2_g[l].reshape(1, d)

        lo_l, hi_l = _norm_proj(xl, g1, mod_l[0], mod_l[1], w_in_l, n_lo)
        lo_c, hi_c = _norm_proj(xc, g1, mod_c[0], mod_c[1], w_in_l, n_lo)
        lo_c = lo_c.reshape(bsz, lc, -1)
        hi_c = hi_c.reshape(bsz, lc, -1)

        qa = qn_a[l] * (HEAD_DIM ** -0.5)
        qb = jnp.tile(qn_b[l], 2) * (B_DIM ** -0.5 * math.log2(math.e))
        kb = jnp.tile(kn_b[l], 2)
        one = jnp.ones((LANE,), F32)
        gains = jnp.stack([qa] * 8 + [kn_a[l]] * 2 + [one] * 2 + [qb] * 8 + [kb] * 8 + [one] * 8)
        gains = gains.reshape(N_QKV_BLOCKS, 1, LANE)
        p_l = _prep(lo_l, gains, rope_tabs)
        p_c = _prep(lo_c, gains, None)

        oa_c, oa_l = _attn_a(sink_a[l], p_l, p_c, ctx_out)

        lq1, lk1, lq2, lk2 = lam_b[l].astype(F32)
        lam_init = 0.8 - 0.6 * math.exp(-0.3 * l)
        lam = (jnp.exp(jnp.sum(lq1 * lk1)) - jnp.exp(jnp.sum(lq2 * lk2)) + lam_init).reshape(1)
        ob_c, ob_l = _attn_b(lam, p_l, p_c, subln_b[l].reshape(1, LANE), 1.0 - lam_init, ctx_out)

        oc_c, oc_l = _lru(hi_l, hi_c, conv_w[l], conv_b[l], w_rg[l].astype(BF16), b_rg[l],
                          w_ig[l].astype(BF16), b_ig[l], lru_lambda[l], ctx_out)
        od_c, od_l = _gla(hi_l, hi_c, lb_all[l], onorm_d[l].reshape(1, LANE), ctx_out)

        w_branch_l = w_branch[l].astype(BF16)
        w_out_l = w_out[l].astype(BF16)
        moe_w = (w_router_t, b_router, w_gate[l].astype(BF16), w_up[l].astype(BF16),
                 w_down[l].astype(BF16))

        y_l = _merge((oa_l, ob_l, oc_l, od_l), lo_l, w_branch_l)
        xl = _out_proj(y_l, w_out_l, xl, mod_l[2])
        xl = _moe(xl, g2, mod_l[3], mod_l[4], mod_l[5], *moe_w)
        if ctx_out:
            flat = lambda t: t.reshape(1, bsz * lc, -1)
            y_c = _merge(tuple(flat(t) for t in (oa_c, ob_c, oc_c, od_c)), flat(lo_c), w_branch_l)
            xc = _out_proj(y_c, w_out_l, xc, mod_c[2])
            xc = _moe(xc, g2, mod_c[3], mod_c[4], mod_c[5], *moe_w)
    return xl
```

```python
import functools
import math

import jax
import jax.numpy as jnp
from jax import lax
from jax.experimental import pallas as pl
from jax.experimental.pallas import tpu as pltpu

F32 = jnp.float32
BF16 = jnp.bfloat16

LANE = 128
SUBLANE = 8
VMEM_LIMIT_BYTES = 56 * 1024 * 1024

EPS = 1e-6
ROPE_THETA = 10000.0
GRID_W = 64
HEAD_DIM = 128
WINDOW = 128
Q_BLOCK = 128
N_HEADS = 8
A_KV_HEADS = 2
A_GROUP = N_HEADS // A_KV_HEADS
B_DIM = 64
B_KEY_CHUNK = 1024
LRU_C = 8.0
GLA_CHUNK = 64
GLA_SAFE_DECAY = 80.0
N_EXPERTS = 16
N_GROUPS = 4
EXPERTS_PER_GROUP = N_EXPERTS // N_GROUPS
NEG_BIG = -1e30
TINY = 1e-37

COL_AQ, COL_AK, COL_AV = 0, 8, 10
COL_BQ, COL_BK, COL_BV = 12, 20, 28
N_QKV_BLOCKS = 36
COL_GATE = N_QKV_BLOCKS
COL_CX, COL_CY = 0, 8
COL_DQ, COL_DFF, COL_DFB, COL_DI, COL_DG = 16, 24, 32, 40, 48
W_IN_QKV = (0, 4608)
W_IN_REC = (4608, 11776)
W_IN_GATE = (11776, 19968)
PROJ_TILE = 512

NT_DIMS = (((1,), (1,)), ((), ()))
TN_DIMS = (((0,), (0,)), ((), ()))


def _cparams(*sem):
    return pltpu.CompilerParams(dimension_semantics=sem, vmem_limit_bytes=VMEM_LIMIT_BYTES)


def _tile(n, pref):
    t = min(n, pref)
    while n % t:
        t //= 2
    return t


def _sigmoid(x):
    return jax.nn.sigmoid(x)


def _modulated_norm(x, g, sh, sc):
    ms = jnp.mean(x * x, axis=-1, keepdims=True)
    return (x * lax.rsqrt(ms + EPS) * g) * (1.0 + sc) + sh


def _ada_kernel(c_ref, w_ref, b_ref, o_ref):
    c = c_ref[...]
    s = (c * _sigmoid(c)).astype(BF16)
    o_ref[0] = jnp.dot(s, w_ref[0].astype(BF16), preferred_element_type=F32) + b_ref[0]


def _ada(c_pad, w_ada, b_ada):
    nl, d, n = w_ada.shape
    rows = c_pad.shape[0]
    tn = _tile(n, 1024)
    return pl.pallas_call(
        _ada_kernel,
        grid=(nl, n // tn),
        in_specs=[
            pl.BlockSpec((rows, d), lambda l, j: (0, 0)),
            pl.BlockSpec((1, d, tn), lambda l, j: (l, 0, j)),
            pl.BlockSpec((1, 1, tn), lambda l, j: (l, 0, j)),
        ],
        out_specs=pl.BlockSpec((1, rows, tn), lambda l, j: (l, 0, j)),
        out_shape=jax.ShapeDtypeStruct((nl, rows, n), F32),
        compiler_params=_cparams("parallel", "parallel"),
        name="ada_mod",
    )(c_pad, w_ada, b_ada.reshape(nl, 1, n))


def _norm_proj_kernel(x_ref, g_ref, sh_ref, sc_ref, w_ref, lo_ref, hi_ref, hn_ref, *, n_lo):
    j = pl.program_id(2)

    @pl.when(j == 0)
    def _():
        hn_ref[...] = _modulated_norm(x_ref[0], g_ref[...], sh_ref[0], sc_ref[0]).astype(BF16)

    @pl.when(j < n_lo)
    def _():
        lo_ref[0] = jnp.dot(hn_ref[...], w_ref[...], preferred_element_type=F32).astype(BF16)

    @pl.when(j >= n_lo)
    def _():
        hi_ref[0] = jnp.dot(hn_ref[...], w_ref[...], preferred_element_type=F32)


def _proj_weight_tile(j):
    n_qkv = (W_IN_QKV[1] - W_IN_QKV[0]) // PROJ_TILE
    n_gate = (W_IN_GATE[1] - W_IN_GATE[0]) // PROJ_TILE
    gate0 = W_IN_GATE[0] // PROJ_TILE
    rec0 = W_IN_REC[0] // PROJ_TILE
    return jnp.where(j < n_qkv, j, jnp.where(j < n_qkv + n_gate, j - n_qkv + gate0, j - n_qkv - n_gate + rec0))


def _norm_proj(x, g, sh, sc, w):
    nb, r, d = x.shape
    tn = PROJ_TILE
    ntiles = w.shape[1] // tn
    n_lo = (W_IN_QKV[1] - W_IN_QKV[0] + W_IN_GATE[1] - W_IN_GATE[0]) // tn
    tm = _tile(r, 1024)
    n_hi = ntiles - n_lo
    return pl.pallas_call(
        functools.partial(_norm_proj_kernel, n_lo=n_lo),
        grid=(nb, r // tm, ntiles),
        in_specs=[
            pl.BlockSpec((1, tm, d), lambda b, t, j: (b, t, 0)),
            pl.BlockSpec((1, d), lambda b, t, j: (0, 0)),
            pl.BlockSpec((1, 1, d), lambda b, t, j: (b, 0, 0)),
            pl.BlockSpec((1, 1, d), lambda b, t, j: (b, 0, 0)),
            pl.BlockSpec((d, tn), lambda b, t, j: (0, _proj_weight_tile(j))),
        ],
        out_specs=[
            pl.BlockSpec((1, tm, tn), lambda b, t, j: (b, t, jnp.minimum(j, n_lo - 1))),
            pl.BlockSpec((1, tm, tn), lambda b, t, j: (b, t, jnp.maximum(j - n_lo, 0))),
        ],
        out_shape=[jax.ShapeDtypeStruct((nb, r, n_lo * tn), BF16),
                   jax.ShapeDtypeStruct((nb, r, n_hi * tn), F32)],
        scratch_shapes=[pltpu.VMEM((tm, d), BF16)],
        compiler_params=_cparams("parallel", "parallel", "arbitrary"),
        name="norm_proj",
    )(x, g, sh, sc, w)


def _rope_partner(y, lane, half):
    return jnp.where((lane % (2 * half)) < half,
                     pltpu.roll(y, LANE - half, 1), pltpu.roll(y, half, 1))


def _prep_kernel(*refs, rope):
    if rope:
        x_ref, g_ref, ca_ref, sa_ref, cb_ref, sb_ref, o_ref = refs
    else:
        x_ref, g_ref, o_ref = refs
    lane = lax.broadcasted_iota(jnp.int32, (x_ref.shape[1], LANE), 1)
    lo = lane < B_DIM
    for j in range(N_QKV_BLOCKS):
        cols = slice(j * LANE, (j + 1) * LANE)
        x = x_ref[0, :, cols].astype(F32)
        if j < COL_AV:
            ms = jnp.mean(x * x, axis=-1, keepdims=True)
            y = x * lax.rsqrt(ms + EPS) * g_ref[j]
            if rope:
                y = y * ca_ref[...] + _rope_partner(y, lane, 32) * sa_ref[...]
        elif COL_BQ <= j < COL_BV:
            x2 = x * x
            s_lo = jnp.sum(jnp.where(lo, x2, 0.0), axis=-1, keepdims=True)
            s_hi = jnp.sum(jnp.where(lo, 0.0, x2), axis=-1, keepdims=True)
            inv = jnp.where(lo, lax.rsqrt(s_lo / B_DIM + EPS), lax.rsqrt(s_hi / B_DIM + EPS))
            y = x * inv * g_ref[j]
            if rope:
                y = y * cb_ref[...] + _rope_partner(y, lane, 16) * sb_ref[...]
        else:
            y = x
        o_ref[0, :, cols] = y.astype(BF16)


def _prep(proj, gains, rope_tabs):
    nb, r, _ = proj.shape
    tm = _tile(r, 256)
    width = N_QKV_BLOCKS * LANE
    rope = rope_tabs is not None
    in_specs = [
        pl.BlockSpec((1, tm, width), lambda b, t: (b, t, 0)),
        pl.BlockSpec((N_QKV_BLOCKS, 1, LANE), lambda b, t: (0, 0, 0)),
    ]
    args = [proj, gains]
    if rope:
        in_specs += [pl.BlockSpec((tm, LANE), lambda b, t: (t, 0))] * 4
        args += list(rope_tabs)
    return pl.pallas_call(
        functools.partial(_prep_kernel, rope=rope),
        grid=(nb, r // tm),
        in_specs=in_specs,
        out_specs=pl.BlockSpec((1, tm, width), lambda b, t: (b, t, 0)),
        out_shape=jax.ShapeDtypeStruct((nb, r, width), BF16),
        compiler_params=_cparams("parallel", "parallel"),
        name="qkv_prep",
    )(*args)


def _rope_tables(seq):
    pos = jnp.arange(seq)
    rows = (pos // GRID_W).astype(F32)[:, None]
    cols = (pos % GRID_W).astype(F32)[:, None]
    lane = jnp.arange(LANE)

    def tables(half):
        inv = ROPE_THETA ** (-(lane % half).astype(F32) / half)
        use_rows = (lane % (4 * half)) < 2 * half
        ang = jnp.where(use_rows[None, :], rows, cols) * inv[None, :]
        sign = jnp.where((lane % (2 * half)) < half, -1.0, 1.0)
        return jnp.cos(ang), jnp.sin(ang) * sign[None, :]

    ca, sa = tables(32)
    cb, sb = tables(16)
    return ca, sa, cb, sb


def _stack_heads(q):
    return jnp.concatenate([q[:, g * HEAD_DIM:(g + 1) * HEAD_DIM] for g in range(A_GROUP)], axis=0)


def _unstack_heads(o, rows):
    return jnp.concatenate([o[g * rows:(g + 1) * rows] for g in range(A_GROUP)], axis=1)


def _sink_column(sink_ref, kvh, rows):
    return jnp.concatenate(
        [jnp.full((rows, 1), sink_ref[kvh * A_GROUP + g], F32) for g in range(A_GROUP)], axis=0)


def _attn_a_lat_kernel(sink_ref, q_ref, kl_ref, vl_ref, kc_ref, vc_ref, o_ref, *, seq):
    kvh = pl.program_id(1)
    i = pl.program_id(2)
    band = 3 * Q_BLOCK
    qs = _stack_heads(q_ref[0])
    start = pl.multiple_of(jnp.clip((i - 1) * Q_BLOCK, 0, seq - band), Q_BLOCK)
    kb = kl_ref[0, pl.ds(start, band), :]
    vb = vl_ref[0, pl.ds(start, band), :]
    s_loc = lax.dot_general(qs, kb, NT_DIMS, preferred_element_type=F32)
    s_ctx = lax.dot_general(qs, kc_ref[0], NT_DIMS, preferred_element_type=F32)
    row = lax.broadcasted_iota(jnp.int32, s_loc.shape, 0)
    col = lax.broadcasted_iota(jnp.int32, s_loc.shape, 1)
    qpos = i * Q_BLOCK + (row % Q_BLOCK)
    kpos = start + col
    s_loc = jnp.where(jnp.abs(kpos - qpos) <= WINDOW, s_loc, NEG_BIG)
    sk = _sink_column(sink_ref, kvh, Q_BLOCK)
    m = jnp.maximum(jnp.maximum(jnp.max(s_loc, axis=-1, keepdims=True),
                                jnp.max(s_ctx, axis=-1, keepdims=True)), sk)
    p_loc = jnp.exp(s_loc - m)
    p_ctx = jnp.exp(s_ctx - m)
    den = (jnp.sum(p_loc, axis=-1, keepdims=True) + jnp.sum(p_ctx, axis=-1, keepdims=True)
           + jnp.exp(sk - m))
    o = (jnp.dot(p_loc.astype(BF16), vb, preferred_element_type=F32)
         + jnp.dot(p_ctx.astype(BF16), vc_ref[0], preferred_element_type=F32)) / den
    o_ref[0] = _unstack_heads(o, Q_BLOCK).astype(BF16)


def _attn_a_ctx_kernel(sink_ref, q_ref, kc_ref, vc_ref, o_ref):
    kvh = pl.program_id(1)
    rows = q_ref.shape[1]
    qs = _stack_heads(q_ref[0])
    s = lax.dot_general(qs, kc_ref[0], NT_DIMS, preferred_element_type=F32)
    sk = _sink_column(sink_ref, kvh, rows)
    m = jnp.maximum(jnp.max(s, axis=-1, keepdims=True), sk)
    p = jnp.exp(s - m)
    den = jnp.sum(p, axis=-1, keepdims=True) + jnp.exp(sk - m)
    o = jnp.dot(p.astype(BF16), vc_ref[0], preferred_element_type=F32) / den
    o_ref[0] = _unstack_heads(o, rows).astype(BF16)


def _attn_a(sink, p_l, p_c, ctx_out):
    b, seq, _ = p_l.shape
    lc = p_c.shape[1]
    gw = A_GROUP * HEAD_DIM
    smem = pl.BlockSpec(memory_space=pltpu.SMEM)
    o_l = pl.pallas_call(
        functools.partial(_attn_a_lat_kernel, seq=seq),
        grid=(b, A_KV_HEADS, seq // Q_BLOCK),
        in_specs=[
            smem,
            pl.BlockSpec((1, Q_BLOCK, gw), lambda n, h, i: (n, i, h)),
            pl.BlockSpec((1, seq, LANE), lambda n, h, i: (n, 0, COL_AK + h)),
            pl.BlockSpec((1, seq, LANE), lambda n, h, i: (n, 0, COL_AV + h)),
            pl.BlockSpec((1, lc, LANE), lambda n, h, i: (n, 0, COL_AK + h)),
            pl.BlockSpec((1, lc, LANE), lambda n, h, i: (n, 0, COL_AV + h)),
        ],
        out_specs=pl.BlockSpec((1, Q_BLOCK, gw), lambda n, h, i: (n, i, h)),
        out_shape=jax.ShapeDtypeStruct((b, seq, N_HEADS * HEAD_DIM), BF16),
        compiler_params=_cparams("parallel", "parallel", "arbitrary"),
        name="attn_a_latent",
    )(sink, p_l, p_l, p_l, p_c, p_c)
    o_c = None
    if ctx_out:
        o_c = pl.pallas_call(
            _attn_a_ctx_kernel,
            grid=(b, A_KV_HEADS),
            in_specs=[
                smem,
                pl.BlockSpec((1, lc, gw), lambda n, h: (n, 0, h)),
                pl.BlockSpec((1, lc, LANE), lambda n, h: (n, 0, COL_AK + h)),
                pl.BlockSpec((1, lc, LANE), lambda n, h: (n, 0, COL_AV + h)),
            ],
            out_specs=pl.BlockSpec((1, lc, gw), lambda n, h: (n, 0, h)),
            out_shape=jax.ShapeDtypeStruct((b, lc, N_HEADS * HEAD_DIM), BF16),
            compiler_params=_cparams("parallel", "parallel"),
            name="attn_a_ctx",
        )(sink, p_c, p_c, p_c)
    return o_c, o_l


def _attn_b_kernel(lam_ref, q_ref, *refs, nseg, post_scale):
    k_refs = refs[:nseg]
    v_refs = refs[nseg:2 * nseg]
    g_ref, o_ref = refs[2 * nseg], refs[2 * nseg + 1]
    q = q_ref[0]
    tq = q.shape[0]
    lane = lax.broadcasted_iota(jnp.int32, q.shape, 1)
    zero = jnp.zeros_like(q)
    q2 = jnp.concatenate([jnp.where(lane < B_DIM, q, zero), jnp.where(lane < B_DIM, zero, q)], axis=0)
    m = jnp.full((2 * tq, 1), NEG_BIG, F32)
    den = jnp.zeros((2 * tq, 1), F32)
    acc = jnp.zeros((2 * tq, LANE), F32)
    for k_ref, v_ref in zip(k_refs, v_refs):
        nk = k_ref.shape[1]
        kc = _tile(nk, B_KEY_CHUNK)
        for c0 in range(0, nk, kc):
            s = lax.dot_general(q2, k_ref[0, c0:c0 + kc, :], NT_DIMS, preferred_element_type=F32)
            m_new = jnp.maximum(m, jnp.max(s, axis=-1, keepdims=True))
            alpha = jnp.exp2(m - m_new)
            p = jnp.exp2(s - m_new)
            den = alpha * den + jnp.sum(p, axis=-1, keepdims=True)
            acc = alpha * acc + jnp.dot(p.astype(BF16), v_ref[0, c0:c0 + kc, :],
                                        preferred_element_type=F32)
            m = m_new
    o2 = acc / den
    o = o2[:tq] - lam_ref[0] * o2[tq:]
    ms = jnp.mean(o * o, axis=-1, keepdims=True)
    o_ref[0] = ((o * lax.rsqrt(ms + EPS) * g_ref[...]) * post_scale).astype(BF16)


def _attn_b_call(lam, q_src, k_srcs, subln, post_scale, name):
    b, rq, _ = q_src.shape
    tq = _tile(rq, 512)
    smem = pl.BlockSpec(memory_space=pltpu.SMEM)
    in_specs = [smem, pl.BlockSpec((1, tq, LANE), lambda n, h, i: (n, i, COL_BQ + h))]
    in_specs += [pl.BlockSpec((1, s.shape[1], LANE), lambda n, h, i: (n, 0, COL_BK + h)) for s in k_srcs]
    in_specs += [pl.BlockSpec((1, s.shape[1], LANE), lambda n, h, i: (n, 0, COL_BV + h)) for s in k_srcs]
    in_specs += [pl.BlockSpec((1, LANE), lambda n, h, i: (0, 0))]
    return pl.pallas_call(
        functools.partial(_attn_b_kernel, nseg=len(k_srcs), post_scale=post_scale),
        grid=(b, N_HEADS, rq // tq),
        in_specs=in_specs,
        out_specs=pl.BlockSpec((1, tq, LANE), lambda n, h, i: (n, i, h)),
        out_shape=jax.ShapeDtypeStruct((b, rq, N_HEADS * HEAD_DIM), BF16),
        compiler_params=_cparams("parallel", "parallel", "arbitrary"),
        name=name,
    )(lam, q_src, *k_srcs, *k_srcs, subln)


def _attn_b(lam, p_l, p_c, subln, post_scale, ctx_out):
    o_l = _attn_b_call(lam, p_l, [p_c, p_l], subln, post_scale, "attn_b_latent")
    o_c = _attn_b_call(lam, p_c, [p_c], subln, post_scale, "attn_b_ctx") if ctx_out else None
    return o_c, o_l


def _centred_conv(x_ref, pad_ref, w_ref, b_ref):
    n = x_ref.shape[1]
    zeros = jnp.zeros((SUBLANE, LANE), F32)
    pad_ref[0:SUBLANE, :] = zeros
    pad_ref[SUBLANE:SUBLANE + n, :] = x_ref[0]
    pad_ref[SUBLANE + n:2 * SUBLANE + n, :] = zeros
    out = b_ref[...]
    for tap in range(4):
        out = out + pad_ref[pl.ds(SUBLANE - 2 + tap, n), :] * w_ref[tap:tap + 1, :]
    return out


def _tile_scan(a, v, reverse):
    n = a.shape[0]
    row = lax.broadcasted_iota(jnp.int32, a.shape, 0) % SUBLANE
    for s in (1, 2, 4):
        if reverse:
            keep = row < SUBLANE - s
            a_sh = jnp.where(keep, pltpu.roll(a, n - s, 0), 1.0)
            v_sh = jnp.where(keep, pltpu.roll(v, n - s, 0), 0.0)
        else:
            keep = row >= s
            a_sh = jnp.where(keep, pltpu.roll(a, s, 0), 1.0)
            v_sh = jnp.where(keep, pltpu.roll(v, s, 0), 0.0)
        v = v + a * v_sh
        a = a * a_sh
    return a, v


def _seg_tile(t, ntc, ntl, reverse):
    if not reverse:
        return t
    return jnp.where(t < ntc, ntc - 1 - t, 2 * ntc + ntl - 1 - t)


def _lru_kernel(xc_ref, yc_ref, xl_ref, yl_ref, cw_ref, cb_ref, wr_ref, br_ref, wi_ref, bi_ref,
                lam_ref, *refs, ctx_out):
    if ctx_out:
        oc_ref, ol_ref, ac_ref, hl_ref, hs_ref, pad_ref = refs
    else:
        ol_ref, ac_ref, hl_ref, hs_ref, pad_ref = refs
    lc, seq = xc_ref.shape[1], xl_ref.shape[1]
    ntc, ntl = lc // SUBLANE, seq // SUBLANE
    for x_ref, off in ((xc_ref, 0), (xl_ref, lc)):
        n = x_ref.shape[1]
        u = _centred_conv(x_ref, pad_ref, cw_ref, cb_ref)
        ub = u.astype(BF16)
        for d in range(2):
            lam = lam_ref[d:d + 1, :]
            sp = jnp.maximum(-lam, 0.0) + jnp.log1p(jnp.exp(-jnp.abs(lam)))
            r = _sigmoid(jnp.dot(ub, wr_ref[d, 0], preferred_element_type=F32) + br_ref[d:d + 1, :])
            gi = _sigmoid(jnp.dot(ub, wi_ref[d, 0], preferred_element_type=F32) + bi_ref[d:d + 1, :])
            log_a = -LRU_C * r * sp
            a = jnp.exp(log_a)
            y = jnp.tanh(-log_a) * (1.0 + a * a)
            v = (y * lax.rsqrt(jnp.maximum(y, TINY))) * gi * u
            a_cum, h_loc = _tile_scan(a, v, reverse=(d == 1))
            ac_ref[d, off:off + n, :] = a_cum
            hl_ref[d, off:off + n, :] = h_loc

    def step(t, carry):
        new = []
        for d in range(2):
            tile = _seg_tile(t, ntc, ntl, d == 1)
            r0 = pl.multiple_of(tile * SUBLANE, SUBLANE)
            h = hl_ref[d, pl.ds(r0, SUBLANE), :] + ac_ref[d, pl.ds(r0, SUBLANE), :] * carry[d]
            if d == 0:
                hs_ref[pl.ds(r0, SUBLANE), :] = h
                new.append(h[SUBLANE - 1:SUBLANE, :])
            else:
                hl_ref[d, pl.ds(r0, SUBLANE), :] = h
                new.append(h[0:1, :])
        return tuple(new)

    zero = jnp.zeros((1, LANE), F32)
    lax.fori_loop(0, ntc + ntl, step, (zero, zero))
    if ctx_out:
        hc = hs_ref[0:lc, :] + hl_ref[1, 0:lc, :]
        oc_ref[0] = (hc * jax.nn.gelu(yc_ref[0])).astype(BF16)
    hl = hs_ref[lc:lc + seq, :] + hl_ref[1, lc:lc + seq, :]
    ol_ref[0] = (hl * jax.nn.gelu(yl_ref[0])).astype(BF16)


def _lru(proj_l, proj_c, conv_w, conv_b, w_r, b_r, w_i, b_i, lam, ctx_out):
    b, seq, _ = proj_l.shape
    lc = proj_c.shape[1]
    nblk = w_r.shape[1]
    width = nblk * LANE

    def seg(n, col):
        return pl.BlockSpec((1, n, LANE), lambda i, j: (i, 0, col + j))

    vec2 = pl.BlockSpec((2, LANE), lambda i, j: (0, j))
    wspec = pl.BlockSpec((2, 1, LANE, LANE), lambda i, j: (0, j, 0, 0))
    out_specs = [pl.BlockSpec((1, seq, LANE), lambda i, j: (i, 0, j))]
    out_shape = [jax.ShapeDtypeStruct((b, seq, width), BF16)]
    if ctx_out:
        out_specs.insert(0, pl.BlockSpec((1, lc, LANE), lambda i, j: (i, 0, j)))
        out_shape.insert(0, jax.ShapeDtypeStruct((b, lc, width), BF16))
    outs = pl.pallas_call(
        functools.partial(_lru_kernel, ctx_out=ctx_out),
        grid=(b, nblk),
        in_specs=[
            seg(lc, COL_CX), seg(lc, COL_CY), seg(seq, COL_CX), seg(seq, COL_CY),
            pl.BlockSpec((4, LANE), lambda i, j: (0, j)),
            pl.BlockSpec((1, LANE), lambda i, j: (0, j)),
            wspec, vec2, wspec, vec2, vec2,
        ],
        out_specs=out_specs,
        out_shape=out_shape,
        scratch_shapes=[
            pltpu.VMEM((2, lc + seq, LANE), F32),
            pltpu.VMEM((2, lc + seq, LANE), F32),
            pltpu.VMEM((lc + seq, LANE), F32),
            pltpu.VMEM((max(lc, seq) + 2 * SUBLANE, LANE), F32),
        ],
        compiler_params=_cparams("parallel", "parallel"),
        name="rglru",
    )(proj_c, proj_c, proj_l, proj_l, conv_w, conv_b.reshape(1, width), w_r, b_r, w_i, b_i, lam)
    return (outs[0], outs[1]) if ctx_out else (None, outs[0])


def _gla_level_masks(reverse):
    c = GLA_CHUNK
    row = lax.broadcasted_iota(jnp.int32, (c, c), 0)
    col = lax.broadcasted_iota(jnp.int32, (c, c), 1)
    masks = {}
    for s in (32, 16, 8):
        same = (row // (2 * s)) == (col // (2 * s))
        if reverse:
            masks[s] = same & ((row % (2 * s)) < s) & ((col % (2 * s)) >= s)
        else:
            masks[s] = same & ((row % (2 * s)) >= s) & ((col % (2 * s)) < s)
    tri = (row <= col) if reverse else (row >= col)
    return masks, tri.astype(F32)


def _gla_chunk(q, k, v, g, st, masks, tri, reverse):
    c = GLA_CHUNK
    b = jnp.dot(tri, g, precision=lax.Precision.HIGHEST, preferred_element_type=F32)
    att = jnp.zeros((c, c), F32)
    for s in (32, 16, 8):
        b3 = b.reshape(c // (2 * s), 2 * s, LANE)
        rr = s if reverse else s - 1
        rho = jnp.broadcast_to(b3[:, rr:rr + 1, :], b3.shape).reshape(c, LANE)
        e = jnp.exp(-jnp.abs(b - rho))
        a = lax.dot_general((q * e).astype(BF16), (k * e).astype(BF16), NT_DIMS,
                            preferred_element_type=F32)
        att = att + jnp.where(masks[s], a, 0.0)
    vb = v.astype(BF16)
    o = jnp.dot(att.astype(BF16), vb, preferred_element_type=F32)
    nb = c // SUBLANE
    b3 = b.reshape(nb, SUBLANE, LANE)
    q3 = q.reshape(nb, SUBLANE, LANE)
    k3 = k.reshape(nb, SUBLANE, LANE)
    v3 = v.reshape(nb, SUBLANE, LANE)
    rowi = lax.broadcasted_iota(jnp.int32, b3.shape, 1)
    od = jnp.zeros(b3.shape, F32)
    for jj in range(SUBLANE):
        keep = (rowi <= jj) if reverse else (rowi >= jj)
        e = jnp.where(keep, jnp.exp(jnp.minimum(b3 - b3[:, jj:jj + 1, :], 0.0)), 0.0)
        sj = jnp.sum(q3 * e * k3[:, jj:jj + 1, :], axis=-1, keepdims=True)
        od = od + sj * v3[:, jj:jj + 1, :]
    o = o + od.reshape(c, LANE)
    o = o + lax.dot_general((q * jnp.exp(b)).astype(BF16), st.astype(BF16), NT_DIMS,
                            preferred_element_type=F32)
    b_end = b[0:1, :] if reverse else b[c - 1:c, :]
    khat = (k * jnp.exp(b_end - b)).astype(BF16)
    st_new = st * jnp.exp(b_end) + lax.dot_general(vb, khat, TN_DIMS, preferred_element_type=F32)
    return o, st_new


def _gla_fast_intra(q, k, v, g, tri, reverse):
    c = GLA_CHUNK
    g_hi = g.astype(BF16)
    g_lo = (g - g_hi.astype(F32)).astype(BF16)
    b2 = jnp.dot(tri.astype(BF16), jnp.concatenate([g_hi, g_lo], axis=1), preferred_element_type=F32)
    b = b2[:, :LANE] + b2[:, LANE:]
    rr = c // 2 if reverse else c // 2 - 1
    rho = b[rr:rr + 1, :]
    qt = q * jnp.exp(b - rho)
    kt = k * jnp.exp(rho - b)
    a = lax.dot_general(qt.astype(BF16), kt.astype(BF16), NT_DIMS, preferred_element_type=F32)
    att = jnp.where(tri > 0.0, a, 0.0)
    vb = v.astype(BF16)
    o_intra = jnp.dot(att.astype(BF16), vb, preferred_element_type=F32)
    b_end = b[0:1, :] if reverse else b[c - 1:c, :]
    qe = (qt * jnp.exp(rho)).astype(BF16)
    khat = (kt * jnp.exp(b_end - rho)).astype(BF16)
    return o_intra, qe, khat, vb, jnp.exp(b_end)


def _gla_fast_inter(intra, st):
    o_intra, qe, khat, vb, decay = intra
    o = o_intra + lax.dot_general(qe, st.astype(BF16), NT_DIMS, preferred_element_type=F32)
    st_new = st * decay + lax.dot_general(vb, khat, TN_DIMS, preferred_element_type=F32)
    return o, st_new


def _gla_kernel(lb_ref, on_ref, qc_ref, ffc_ref, fbc_ref, ic_ref, gc_ref,
                ql_ref, ffl_ref, fbl_ref, il_ref, gl_ref, *refs, ctx_out):
    if ctx_out:
        oc_ref, ol_ref, q_s, v_s, g_s, k_s, o_s, st_s = refs
    else:
        ol_ref, q_s, v_s, g_s, k_s, o_s, st_s = refs
    lc, seq = qc_ref.shape[1], ql_ref.shape[1]
    ncc, ncl = lc // GLA_CHUNK, seq // GLA_CHUNK
    for off, n, q_ref, i_ref, f_refs in ((0, lc, qc_ref, ic_ref, (ffc_ref, fbc_ref)),
                                         (lc, seq, ql_ref, il_ref, (ffl_ref, fbl_ref))):
        q_s[off:off + n, :] = q_ref[0]
        v_s[off:off + n, :] = i_ref[0]
        for d in range(2):
            z = f_refs[d][0]
            lbd = lb_ref[d:d + 1, :]
            ez = jnp.exp(-jnp.abs(z))
            r = 1.0 / (1.0 + ez)
            pos = z >= 0.0
            sig_p = jnp.where(pos, r, ez * r)
            sig_n = jnp.where(pos, ez * r, r)
            g_s[d, off:off + n, :] = jnp.log(lbd + (1.0 - lbd) * sig_p)
            k_s[d, off:off + n, :] = (1.0 - lbd) * sig_n
    st_s[...] = jnp.zeros(st_s.shape, F32)
    consts = [_gla_level_masks(False), _gla_level_masks(True)]

    half = GLA_CHUNK // 2
    worst = jnp.zeros((1, LANE), F32)
    for d in range(2):
        hs = jnp.sum(g_s[d].reshape((lc + seq) // half, half, LANE), axis=1)
        worst = jnp.maximum(worst, jnp.max(-hs, axis=0, keepdims=True))
    safe = jnp.max(worst) < GLA_SAFE_DECAY

    def chunk_rows(cidx, d):
        chunk = _seg_tile(cidx, ncc, ncl, d == 1)
        return pl.ds(pl.multiple_of(chunk * GLA_CHUNK, GLA_CHUNK), GLA_CHUNK)

    def robust_step(cidx, carry):
        for d in range(2):
            rows = chunk_rows(cidx, d)
            o, st_new = _gla_chunk(q_s[rows, :], k_s[d, rows, :], v_s[rows, :], g_s[d, rows, :],
                                   st_s[d], consts[d][0], consts[d][1], d == 1)
            st_s[d] = st_new
            o_s[d, rows, :] = o
        return carry

    nchunks = ncc + ncl
    group = next(u for u in (4, 3, 2, 1) if nchunks % u == 0)

    def fast_step(t, carry):
        work = [[] for _ in range(2)]
        for d in range(2):
            for u in range(group):
                rows = chunk_rows(t * group + u, d)
                work[d].append((rows, _gla_fast_intra(q_s[rows, :], k_s[d, rows, :], v_s[rows, :],
                                                      g_s[d, rows, :], consts[d][1], d == 1)))
        for d in range(2):
            st = st_s[d]
            outs = []
            for rows, intra in work[d]:
                o, st = _gla_fast_inter(intra, st)
                outs.append((rows, o))
            st_s[d] = st
            for rows, o in outs:
                o_s[d, rows, :] = o
        return carry

    @pl.when(safe)
    def _():
        lax.fori_loop(0, nchunks // group, fast_step, 0)

    @pl.when(jnp.logical_not(safe))
    def _():
        lax.fori_loop(0, nchunks, robust_step, 0)

    def finish(o, gate):
        ms = jnp.mean(o * o, axis=-1, keepdims=True)
        return ((o * lax.rsqrt(ms + EPS) * on_ref[...]) * (gate * _sigmoid(gate))).astype(BF16)

    if ctx_out:
        oc_ref[0] = finish(o_s[0, 0:lc, :] + o_s[1, 0:lc, :], gc_ref[0])
    ol_ref[0] = finish(o_s[0, lc:lc + seq, :] + o_s[1, lc:lc + seq, :], gl_ref[0])


def _gla(proj_l, proj_c, lb, onorm, ctx_out):
    b, seq, _ = proj_l.shape
    lc = proj_c.shape[1]
    width = N_HEADS * HEAD_DIM
    nt = lc + seq

    def seg(n, col):
        return pl.BlockSpec((1, n, LANE), lambda i, h: (i, 0, col + h))

    cols = (COL_DQ, COL_DFF, COL_DFB, COL_DI, COL_DG)
    out_specs = [pl.BlockSpec((1, seq, LANE), lambda i, h: (i, 0, h))]
    out_shape = [jax.ShapeDtypeStruct((b, seq, width), BF16)]
    if ctx_out:
        out_specs.insert(0, pl.BlockSpec((1, lc, LANE), lambda i, h: (i, 0, h)))
        out_shape.insert(0, jax.ShapeDtypeStruct((b, lc, width), BF16))
    outs = pl.pallas_call(
        functools.partial(_gla_kernel, ctx_out=ctx_out),
        grid=(b, N_HEADS),
        in_specs=[pl.BlockSpec((2, LANE), lambda i, h: (0, h)),
                  pl.BlockSpec((1, LANE), lambda i, h: (0, 0))]
        + [seg(lc, c) for c in cols] + [seg(seq, c) for c in cols],
        out_specs=out_specs,
        out_shape=out_shape,
        scratch_shapes=[
            pltpu.VMEM((nt, LANE), F32),
            pltpu.VMEM((nt, LANE), F32),
            pltpu.VMEM((2, nt, LANE), F32),
            pltpu.VMEM((2, nt, LANE), F32),
            pltpu.VMEM((2, nt, LANE), F32),
            pltpu.VMEM((2, LANE, LANE), F32),
        ],
        compiler_params=_cparams("parallel", "parallel"),
        name="hgrn2",
    )(lb, onorm, *([proj_c] * 5), *([proj_l] * 5))
    return (outs[0], outs[1]) if ctx_out else (None, outs[0])


def _merge_kernel(oa_ref, ob_ref, oc_ref, od_ref, g0_ref, g1_ref, g2_ref, g3_ref,
                  w0_ref, w1_ref, w2_ref, w3_ref, y_ref):
    acc = None
    for o_ref, g_ref, w_ref in ((oa_ref, g0_ref, w0_ref), (ob_ref, g1_ref, w1_ref),
                                (oc_ref, g2_ref, w2_ref), (od_ref, g3_ref, w3_ref)):
        t = _sigmoid(g_ref[0].astype(F32)) * jnp.dot(o_ref[0], w_ref[0], preferred_element_type=F32)
        acc = t if acc is None else acc + t
    y_ref[0] = acc.astype(BF16)


def _merge(outs, proj, w_branch):
    nb, r, mw = outs[0].shape
    d = w_branch.shape[2]
    tm = _tile(r, 512)
    tn = 512
    nj = d // tn
    o_spec = pl.BlockSpec((1, tm, mw), lambda b, t, j: (b, t, 0))
    g_specs = [pl.BlockSpec((1, tm, tn), functools.partial(
        lambda b, t, j, n: (b, t, (COL_GATE * LANE) // tn + n * nj + j), n=n)) for n in range(4)]
    w_specs = [pl.BlockSpec((1, mw, tn), functools.partial(lambda b, t, j, n: (n, 0, j), n=n))
               for n in range(4)]
    return pl.pallas_call(
        _merge_kernel,
        grid=(nb, r // tm, nj),
        in_specs=[o_spec] * 4 + g_specs + w_specs,
        out_specs=pl.BlockSpec((1, tm, tn), lambda b, t, j: (b, t, j)),
        out_shape=jax.ShapeDtypeStruct((nb, r, d), BF16),
        compiler_params=_cparams("parallel", "parallel", "arbitrary"),
        name="branch_merge",
    )(*outs, *([proj] * 4), *([w_branch] * 4))


def _out_proj_kernel(y_ref, w_ref, x_ref, g_ref, o_ref):
    o_ref[0] = x_ref[0] + g_ref[0] * jnp.dot(y_ref[0], w_ref[...], preferred_element_type=F32)


def _out_proj(y, w, x, gate):
    nb, r, d = x.shape
    tm = _tile(r, 1024)
    tn = 512
    return pl.pallas_call(
        _out_proj_kernel,
        grid=(nb, r // tm, d // tn),
        in_specs=[
            pl.BlockSpec((1, tm, d), lambda b, t, j: (b, t, 0)),
            pl.BlockSpec((d, tn), lambda b, t, j: (0, j)),
            pl.BlockSpec((1, tm, tn), lambda b, t, j: (b, t, j)),
            pl.BlockSpec((1, 1, tn), lambda b, t, j: (b, 0, j)),
        ],
        out_specs=pl.BlockSpec((1, tm, tn), lambda b, t, j: (b, t, j)),
        out_shape=jax.ShapeDtypeStruct((nb, r, d), F32),
        compiler_params=_cparams("parallel", "parallel", "arbitrary"),
        name="out_proj",
    )(y, w, x, gate)


def _route(logits_t, bias_col):
    aff = _sigmoid(logits_t)
    sel = aff + bias_col
    aff_r = [aff[e:e + 1, :] for e in range(N_EXPERTS)]
    sel_r = [sel[e:e + 1, :] for e in range(N_EXPERTS)]
    scores = []
    for g in range(N_GROUPS):
        v = sel_r[g * EXPERTS_PER_GROUP:(g + 1) * EXPERTS_PER_GROUP]
        m1 = functools.reduce(jnp.maximum, v)
        taken = jnp.zeros(m1.shape, jnp.bool_)
        second = jnp.full(m1.shape, -jnp.inf, F32)
        for x in v:
            first = (x == m1) & jnp.logical_not(taken)
            taken = taken | first
            second = jnp.where(first, second, jnp.maximum(second, x))
        scores.append(m1 + second)
    best, gidx = scores[0], jnp.zeros(scores[0].shape, jnp.int32)
    for g in range(1, N_GROUPS):
        better = scores[g] > best
        gidx = jnp.where(better, g, gidx)
        best = jnp.where(better, scores[g], best)
    masked = [jnp.where(gidx == e // EXPERTS_PER_GROUP, sel_r[e], -jnp.inf) for e in range(N_EXPERTS)]

    def first_argmax(vals, exclude):
        bv = jnp.full(vals[0].shape, -jnp.inf, F32)
        bi = jnp.full(vals[0].shape, -1, jnp.int32)
        for e, x in enumerate(vals):
            better = x > bv
            if exclude is not None:
                better = better & (exclude != e)
            bi = jnp.where(better, e, bi)
            bv = jnp.where(better, x, bv)
        return bi

    i1 = first_argmax(masked, None)
    i2 = first_argmax(masked, i1)
    w1 = functools.reduce(jnp.add, [jnp.where(i1 == e, aff_r[e], 0.0) for e in range(N_EXPERTS)])
    w2 = functools.reduce(jnp.add, [jnp.where(i2 == e, aff_r[e], 0.0) for e in range(N_EXPERTS)])
    tot = w1 + w2
    g1, g2 = w1 / tot, w2 / tot
    width = logits_t.shape[1]
    rowi = lax.broadcasted_iota(jnp.int32, (SUBLANE, width), 0)
    local1 = jnp.broadcast_to(i1 - gidx * EXPERTS_PER_GROUP, (SUBLANE, width))
    local2 = jnp.broadcast_to(i2 - gidx * EXPERTS_PER_GROUP, (SUBLANE, width))
    info = (jnp.where(rowi == local1, jnp.broadcast_to(g1, (SUBLANE, width)), 0.0)
            + jnp.where(rowi == local2, jnp.broadcast_to(g2, (SUBLANE, width)), 0.0))
    return jnp.where(rowi == EXPERTS_PER_GROUP,
                     jnp.broadcast_to(gidx.astype(F32), (SUBLANE, width)), info)


def _moe_route_kernel(x_ref, g_ref, sh_ref, sc_ref, wrt_ref, br_ref, hn_ref, info_ref):
    d = x_ref.shape[2]
    h = _modulated_norm(x_ref[0], g_ref[...], sh_ref[0], sc_ref[0])
    logits_t = lax.dot_general(wrt_ref[...], h, NT_DIMS, precision=lax.Precision.HIGHEST,
                               preferred_element_type=F32)
    info = _route(logits_t, br_ref[...])
    info_ref[0] = info
    hn_ref[0, :, 0:d] = h
    pad = jnp.zeros((LANE - SUBLANE, info.shape[1]), F32)
    hn_ref[0, :, d:d + LANE] = jnp.concatenate([info, pad], axis=0).T


def _moe_route(x, g, sh, sc, w_router_t, b_router):
    nb, r, d = x.shape
    tm = _tile(r, 512)
    vec = pl.BlockSpec((1, 1, d), lambda b, t: (b, 0, 0))
    return pl.pallas_call(
        _moe_route_kernel,
        grid=(nb, r // tm),
        in_specs=[
            pl.BlockSpec((1, tm, d), lambda b, t: (b, t, 0)),
            pl.BlockSpec((1, d), lambda b, t: (0, 0)),
            vec, vec,
            pl.BlockSpec((N_EXPERTS, d), lambda b, t: (0, 0)),
            pl.BlockSpec((N_EXPERTS, 1), lambda b, t: (0, 0)),
        ],
        out_specs=[pl.BlockSpec((1, tm, d + LANE), lambda b, t: (b, t, 0)),
                   pl.BlockSpec((1, SUBLANE, tm), lambda b, t: (b, 0, t))],
        out_shape=[jax.ShapeDtypeStruct((nb, r, d + LANE), F32),
                   jax.ShapeDtypeStruct((nb, SUBLANE, r), F32)],
        compiler_params=_cparams("parallel", "parallel"),
        name="moe_route",
    )(x, g, sh, sc, w_router_t, b_router.reshape(N_EXPERTS, 1))


def _row_copy(src, src_row, dst, dst_row, sem):
    return pltpu.make_async_copy(src.at[pl.ds(src_row, 1), :], dst.at[pl.ds(dst_row, 1), :], sem)


def _moe_experts_kernel(dst_ref, tgrp_ref, nval_ref, hn_hbm, wg_ref, wu_ref, wd_ref,
                        y_hbm, xbuf, hb, gbuf, acc, gsem, ssem, *, tm, n_tok):
    i = pl.program_id(0)
    e = pl.program_id(1)
    nt = pl.num_programs(0)
    slot = i % 2
    per = tm // EXPERTS_PER_GROUP
    d = hb.shape[1]

    def gather_row(slot_idx, buf, row):
        s = dst_ref[slot_idx]
        return _row_copy(hn_hbm, jnp.where(s < n_tok, s, 0), xbuf.at[buf], row, gsem.at[buf])

    def wait_scatter(sl):
        pltpu.make_async_copy(acc.at[sl], y_hbm.at[pl.ds(0, tm), :], ssem.at[sl]).wait()

    @pl.when((i == 0) & (e == 0))
    def _():
        def body(r, c):
            gather_row(r, 0, r).start()
            return c
        lax.fori_loop(0, tm, body, 0)
        acc[1] = jnp.zeros(acc.shape[1:], F32)
        for half in range(2):
            fill = pltpu.make_async_copy(acc.at[1], y_hbm.at[pl.ds(n_tok + half * tm, tm), :], ssem.at[1])
            fill.start()
            fill.wait()

    @pl.when(e == 0)
    def _():
        pltpu.make_async_copy(hn_hbm.at[pl.ds(0, tm), :], xbuf.at[slot], gsem.at[slot]).wait()
        hb[...] = xbuf[slot, :, 0:d].astype(BF16)
        gbuf[...] = xbuf[slot, :, d:d + LANE]

        @pl.when(i >= 2)
        def _():
            wait_scatter(slot)
        acc[slot] = jnp.zeros(acc.shape[1:], F32)

    @pl.when(i + 1 < nt)
    def _():
        base = (i + 1) * tm + e * per
        for k in range(per):
            gather_row(base + k, 1 - slot, e * per + k).start()

    @pl.when(nval_ref[i] > 0)
    def _():
        hv = hb[...]
        a = jnp.dot(hv, wg_ref[0], preferred_element_type=F32)
        u = jnp.dot(hv, wu_ref[0], preferred_element_type=F32)
        hid = ((a * _sigmoid(a)) * u).astype(BF16)
        y = jnp.dot(hid, wd_ref[0], preferred_element_type=F32)
        lane = lax.broadcasted_iota(jnp.int32, gbuf.shape, 1)
        col = jnp.sum(jnp.where(lane == e, gbuf[...], 0.0), axis=-1, keepdims=True)
        acc[slot] = acc[slot] + col * y

    @pl.when(e == EXPERTS_PER_GROUP - 1)
    def _():
        def body(r, c):
            _row_copy(acc.at[slot], r, y_hbm, dst_ref[i * tm + r], ssem.at[slot]).start()
            return c
        lax.fori_loop(0, tm, body, 0, unroll=8)

        @pl.when(i == nt - 1)
        def _():
            wait_scatter(slot)

            @pl.when(i >= 1)
            def _():
                wait_scatter(1 - slot)


def _moe_experts(hn_ext, dst, tile_group, tile_nvalid, w_gate, w_up, w_down, tm):
    t, dw = hn_ext.shape
    d = dw - LANE
    dff = w_gate.shape[2]
    ntiles = tile_group.shape[0]

    def wmap(i, e, dst_r, tgrp_r, nval_r):
        return (tgrp_r[i] * EXPERTS_PER_GROUP + e, 0, 0)

    grid_spec = pltpu.PrefetchScalarGridSpec(
        num_scalar_prefetch=3,
        grid=(ntiles, EXPERTS_PER_GROUP),
        in_specs=[
            pl.BlockSpec(memory_space=pl.ANY),
            pl.BlockSpec((1, d, dff), wmap),
            pl.BlockSpec((1, d, dff), wmap),
            pl.BlockSpec((1, dff, d), wmap),
        ],
        out_specs=pl.BlockSpec(memory_space=pl.ANY),
        scratch_shapes=[
            pltpu.VMEM((2, tm, dw), F32),
            pltpu.VMEM((tm, d), BF16),
            pltpu.VMEM((tm, LANE), F32),
            pltpu.VMEM((2, tm, d), F32),
            pltpu.SemaphoreType.DMA((2,)),
            pltpu.SemaphoreType.DMA((2,)),
        ],
    )
    return pl.pallas_call(
        functools.partial(_moe_experts_kernel, tm=tm, n_tok=t),
        grid_spec=grid_spec,
        out_shape=jax.ShapeDtypeStruct((t + 2 * tm, d), F32),
        compiler_params=_cparams("arbitrary", "arbitrary"),
        name="moe_experts",
    )(dst, tile_group, tile_nvalid, hn_ext, w_gate, w_up, w_down)


def _residual_kernel(x_ref, y_ref, g_ref, o_ref):
    o_ref[0] = x_ref[0] + g_ref[0] * y_ref[...]


def _residual(x, y, gate):
    nb, r, d = x.shape
    tm = _tile(r, 512)
    per_b = r // tm
    blk = pl.BlockSpec((1, tm, d), lambda b, t: (b, t, 0))
    return pl.pallas_call(
        _residual_kernel,
        grid=(nb, per_b),
        in_specs=[blk, pl.BlockSpec((tm, d), lambda b, t: (b * per_b + t, 0)),
                  pl.BlockSpec((1, 1, d), lambda b, t: (b, 0, 0))],
        out_specs=blk,
        out_shape=jax.ShapeDtypeStruct((nb, r, d), F32),
        compiler_params=_cparams("parallel", "parallel"),
        name="moe_residual",
    )(x, y, gate)


def _group_layout(grp, tm):
    t = grp.shape[0]
    ntiles = t // tm + N_GROUPS
    nslots = ntiles * tm
    oh = (grp[None, :] == jnp.arange(N_GROUPS)[:, None]).astype(F32).reshape(N_GROUPS, t // LANE, LANE)
    tri = (jnp.arange(LANE)[:, None] <= jnp.arange(LANE)[None, :]).astype(F32)
    within = jnp.einsum('grk,kl->grl', oh, tri)
    row_tot = within[..., -1]
    row_off = jnp.cumsum(row_tot, axis=1) - row_tot
    rank = (jnp.sum((within + row_off[..., None]) * oh, axis=0).reshape(t) - 1.0).astype(jnp.int32)
    counts = jnp.sum(row_tot, axis=1).astype(jnp.int32)
    padded = ((counts + tm - 1) // tm) * tm
    start = jnp.cumsum(padded) - padded
    slot_of_token = start[grp] + rank
    slot_ids = jnp.arange(nslots, dtype=jnp.int32)
    spare = t + ((slot_ids // tm) % 2) * tm + slot_ids % tm
    dst = spare.at[slot_of_token].set(jnp.arange(t, dtype=jnp.int32))
    tile_start = jnp.arange(ntiles, dtype=jnp.int32) * tm
    ends = jnp.cumsum(padded)
    tile_group = jnp.minimum(jnp.sum((tile_start[:, None] >= ends[None, :]).astype(jnp.int32), axis=1),
                             N_GROUPS - 1).astype(jnp.int32)
    tile_nvalid = jnp.clip(counts[tile_group] - (tile_start - start[tile_group]), 0, tm).astype(jnp.int32)
    return dst, tile_group, tile_nvalid


def _moe(x, g, sh, sc, gate, w_router_t, b_router, w_gate, w_up, w_down):
    nb, r, d = x.shape
    t = nb * r
    tm = 512 if t >= 8192 else (256 if t >= 2048 else 128)
    hn_ext, info = _moe_route(x, g, sh, sc, w_router_t, b_router)
    grp = info[:, EXPERTS_PER_GROUP, :].reshape(t).astype(jnp.int32)
    dst, tile_group, tile_nvalid = _group_layout(grp, tm)
    y = _moe_experts(hn_ext.reshape(t, d + LANE), dst, tile_group, tile_nvalid, w_gate, w_up, w_down, tm)
    return _residual(x, y, gate)


def kernel(x, c, ctx, c_ctx, w_ada, b_ada, norm1_g, norm2_g, w_in, qn_a, kn_a, sink_a, qn_b, kn_b,
           lam_b, subln_b, conv_w, conv_b, w_rg, b_rg, w_ig, b_ig, lru_lambda, lb_d, onorm_d,
           w_branch, w_out, w_router, b_router, w_gate, w_up, w_down):
    bsz, seq, d = x.shape
    lc = ctx.shape[1]
    depth = w_in.shape[0]

    n_rows = -(-(bsz + 1) // SUBLANE) * SUBLANE
    c_pad = jnp.zeros((n_rows, d), F32).at[:bsz].set(c).at[bsz].set(c_ctx)
    mod = _ada(c_pad, w_ada, b_ada).reshape(depth, n_rows, 6, d)

    lb_w = jax.nn.softmax(lb_d.astype(F32), axis=0)
    lb_all = jnp.cumsum(lb_w, axis=0) - lb_w[0:1]
    rope_tabs = _rope_tables(seq)
    w_router_t = w_router.T

    xl = x
    xc = ctx.reshape(1, bsz * lc, d)
    for l in range(depth):
        ctx_out = l < depth - 1
        mod_l = [mod[l, :bsz, k][:, None, :] for k in range(6)]
        mod_c = [mod[l, bsz:bsz + 1, k][:, None, :] for k in range(6)]
        w_in_l = w_in[l].astype(BF16)
        g1 = norm1_g[l].reshape(1, d)
        g2 = norm2_g[l].reshape(1, d)

        lo_l, hi_l = _norm_proj(xl, g1, mod_l[0], mod_l[1], w_in_l)
        lo_c, hi_c = _norm_proj(xc, g1, mod_c[0], mod_c[1], w_in_l)
        lo_c = lo_c.reshape(bsz, lc, -1)
        hi_c = hi_c.reshape(bsz, lc, -1)

        qa = qn_a[l] * (HEAD_DIM ** -0.5)
        qb = jnp.tile(qn_b[l], 2) * (B_DIM ** -0.5 * math.log2(math.e))
        kb = jnp.tile(kn_b[l], 2)
        one = jnp.ones((LANE,), F32)
        gains = jnp.stack([qa] * 8 + [kn_a[l]] * 2 + [one] * 2 + [qb] * 8 + [kb] * 8 + [one] * 8)
        gains = gains.reshape(N_QKV_BLOCKS, 1, LANE)
        p_l = _prep(lo_l, gains, rope_tabs)
        p_c = _prep(lo_c, gains, None)

        oa_c, oa_l = _attn_a(sink_a[l], p_l, p_c, ctx_out)

        lq1, lk1, lq2, lk2 = lam_b[l].astype(F32)
        lam_init = 0.8 - 0.6 * math.exp(-0.3 * l)
        lam = (jnp.exp(jnp.sum(lq1 * lk1)) - jnp.exp(jnp.sum(lq2 * lk2)) + lam_init).reshape(1)
        ob_c, ob_l = _attn_b(lam, p_l, p_c, subln_b[l].reshape(1, LANE), 1.0 - lam_init, ctx_out)

        oc_c, oc_l = _lru(hi_l, hi_c, conv_w[l], conv_b[l], w_rg[l].astype(BF16), b_rg[l],
                          w_ig[l].astype(BF16), b_ig[l], lru_lambda[l], ctx_out)
        od_c, od_l = _gla(hi_l, hi_c, lb_all[l], onorm_d[l].reshape(1, LANE), ctx_out)

        w_branch_l = w_branch[l].astype(BF16)
        w_out_l = w_out[l].astype(BF16)
        moe_w = (w_router_t, b_router, w_gate[l].astype(BF16), w_up[l].astype(BF16),
                 w_down[l].astype(BF16))

        y_l = _merge((oa_l, ob_l, oc_l, od_l), lo_l, w_branch_l)
        xl = _out_proj(y_l, w_out_l, xl, mod_l[2])
        xl = _moe(xl, g2, mod_l[3], mod_l[4], mod_l[5], *moe_w)
        if ctx_out:
            flat = lambda t: t.reshape(1, bsz * lc, -1)
            y_c = _merge(tuple(flat(t) for t in (oa_c, ob_c, oc_c, od_c)), flat(lo_c), w_branch_l)
            xc = _out_proj(y_c, w_out_l, xc, mod_c[2])
            xc = _moe(xc, g2, mod_c[3], mod_c[4], mod_c[5], *moe_w)
    return xl
```

```python
import functools
import math

import jax
import jax.numpy as jnp
from jax import lax
from jax.experimental import pallas as pl
from jax.experimental.pallas import tpu as pltpu

F32 = jnp.float32
BF16 = jnp.bfloat16

LANE = 128
SUBLANE = 8
VMEM_LIMIT_BYTES = 56 * 1024 * 1024

EPS = 1e-6
ROPE_THETA = 10000.0
GRID_W = 64
HEAD_DIM = 128
WINDOW = 128
Q_BLOCK = 128
N_HEADS = 8
A_KV_HEADS = 2
A_GROUP = N_HEADS // A_KV_HEADS
B_DIM = 64
B_KEY_CHUNK = 1024
LRU_C = 8.0
GLA_CHUNK = 64
GLA_SAFE_DECAY = 80.0
N_EXPERTS = 16
N_GROUPS = 4
EXPERTS_PER_GROUP = N_EXPERTS // N_GROUPS
NEG_BIG = -1e30
LOG2E = math.log2(math.e)
TINY = 1e-37

COL_AQ, COL_AK, COL_AV = 0, 8, 10
COL_BQ, COL_BK, COL_BV = 12, 20, 28
N_QKV_BLOCKS = 36
COL_GATE = N_QKV_BLOCKS
COL_CX, COL_CY = 0, 8
COL_DQ, COL_DFF, COL_DFB, COL_DI, COL_DG = 16, 24, 32, 40, 48
W_IN_QKV = (0, 4608)
W_IN_REC = (4608, 11776)
W_IN_GATE = (11776, 19968)
PROJ_TILE = 512

NT_DIMS = (((1,), (1,)), ((), ()))
TN_DIMS = (((0,), (0,)), ((), ()))


def _cparams(*sem):
    return pltpu.CompilerParams(dimension_semantics=sem, vmem_limit_bytes=VMEM_LIMIT_BYTES)


def _tile(n, pref):
    t = min(n, pref)
    while n % t:
        t //= 2
    return t


def _sigmoid(x):
    return jax.nn.sigmoid(x)


def _modulated_norm(x, g, sh, sc):
    ms = jnp.mean(x * x, axis=-1, keepdims=True)
    return (x * lax.rsqrt(ms + EPS) * g) * (1.0 + sc) + sh


def _ada_kernel(c_ref, w_ref, b_ref, o_ref):
    c = c_ref[...]
    s = (c * _sigmoid(c)).astype(BF16)
    o_ref[0] = jnp.dot(s, w_ref[0].astype(BF16), preferred_element_type=F32) + b_ref[0]


def _ada(c_pad, w_ada, b_ada):
    nl, d, n = w_ada.shape
    rows = c_pad.shape[0]
    tn = _tile(n, 1024)
    return pl.pallas_call(
        _ada_kernel,
        grid=(nl, n // tn),
        in_specs=[
            pl.BlockSpec((rows, d), lambda l, j: (0, 0)),
            pl.BlockSpec((1, d, tn), lambda l, j: (l, 0, j)),
            pl.BlockSpec((1, 1, tn), lambda l, j: (l, 0, j)),
        ],
        out_specs=pl.BlockSpec((1, rows, tn), lambda l, j: (l, 0, j)),
        out_shape=jax.ShapeDtypeStruct((nl, rows, n), F32),
        compiler_params=_cparams("parallel", "parallel"),
        name="ada_mod",
    )(c_pad, w_ada, b_ada.reshape(nl, 1, n))


def _norm_proj_kernel(x_ref, g_ref, sh_ref, sc_ref, w_ref, lo_ref, hi_ref, hn_ref, *, n_lo):
    j = pl.program_id(2)

    @pl.when(j == 0)
    def _():
        hn_ref[...] = _modulated_norm(x_ref[0], g_ref[...], sh_ref[0], sc_ref[0]).astype(BF16)

    @pl.when(j < n_lo)
    def _():
        lo_ref[0] = jnp.dot(hn_ref[...], w_ref[...], preferred_element_type=F32).astype(BF16)

    @pl.when(j >= n_lo)
    def _():
        hi_ref[0] = jnp.dot(hn_ref[...], w_ref[...], preferred_element_type=F32)


def _proj_weight_tile(j):
    n_qkv = (W_IN_QKV[1] - W_IN_QKV[0]) // PROJ_TILE
    n_gate = (W_IN_GATE[1] - W_IN_GATE[0]) // PROJ_TILE
    gate0 = W_IN_GATE[0] // PROJ_TILE
    rec0 = W_IN_REC[0] // PROJ_TILE
    return jnp.where(j < n_qkv, j, jnp.where(j < n_qkv + n_gate, j - n_qkv + gate0, j - n_qkv - n_gate + rec0))


def _norm_proj(x, g, sh, sc, w):
    nb, r, d = x.shape
    tn = PROJ_TILE
    ntiles = w.shape[1] // tn
    n_lo = (W_IN_QKV[1] - W_IN_QKV[0] + W_IN_GATE[1] - W_IN_GATE[0]) // tn
    tm = _tile(r, 1024)
    n_hi = ntiles - n_lo
    return pl.pallas_call(
        functools.partial(_norm_proj_kernel, n_lo=n_lo),
        grid=(nb, r // tm, ntiles),
        in_specs=[
            pl.BlockSpec((1, tm, d), lambda b, t, j: (b, t, 0)),
            pl.BlockSpec((1, d), lambda b, t, j: (0, 0)),
            pl.BlockSpec((1, 1, d), lambda b, t, j: (b, 0, 0)),
            pl.BlockSpec((1, 1, d), lambda b, t, j: (b, 0, 0)),
            pl.BlockSpec((d, tn), lambda b, t, j: (0, _proj_weight_tile(j))),
        ],
        out_specs=[
            pl.BlockSpec((1, tm, tn), lambda b, t, j: (b, t, jnp.minimum(j, n_lo - 1))),
            pl.BlockSpec((1, tm, tn), lambda b, t, j: (b, t, jnp.maximum(j - n_lo, 0))),
        ],
        out_shape=[jax.ShapeDtypeStruct((nb, r, n_lo * tn), BF16),
                   jax.ShapeDtypeStruct((nb, r, n_hi * tn), F32)],
        scratch_shapes=[pltpu.VMEM((tm, d), BF16)],
        compiler_params=_cparams("parallel", "parallel", "arbitrary"),
        name="norm_proj",
    )(x, g, sh, sc, w)


def _group_mean_matrix(group):
    k = lax.broadcasted_iota(jnp.int32, (LANE, LANE), 0)
    l = lax.broadcasted_iota(jnp.int32, (LANE, LANE), 1)
    return jnp.where(k // group == l // group, 1.0 / group, 0.0).astype(BF16)


def _rope_partner_matrix(half):
    k = lax.broadcasted_iota(jnp.int32, (LANE, LANE), 0)
    l = lax.broadcasted_iota(jnp.int32, (LANE, LANE), 1)
    src = jnp.where((l % (2 * half)) < half, l + half, l - half)
    return jnp.where(k == src, 1.0, 0.0).astype(BF16)


def _group_mean(x2, mean_m):
    hi = x2.astype(BF16)
    lo = (x2 - hi.astype(F32)).astype(BF16)
    return (jnp.dot(hi, mean_m, preferred_element_type=F32)
            + jnp.dot(lo, mean_m, preferred_element_type=F32))


def _prep_kernel(*refs, rope):
    if rope:
        x_ref, g_ref, ca_ref, sa_ref, cb_ref, sb_ref, o_ref = refs
    else:
        x_ref, g_ref, o_ref = refs
    tm = x_ref.shape[1]

    def qk_blocks(col0, nblk, group, half, cos_ref, sin_ref):
        blocks = [slice((col0 + j) * LANE, (col0 + j + 1) * LANE) for j in range(nblk)]
        x = jnp.concatenate([x_ref[0, :, c].astype(F32) for c in blocks], axis=0)
        ms = _group_mean(x * x, _group_mean_matrix(group))
        y = (x * lax.rsqrt(ms + EPS)).reshape(nblk, tm, LANE) * g_ref[col0:col0 + nblk]
        if rope:
            part = jnp.dot(y.reshape(nblk * tm, LANE).astype(BF16), _rope_partner_matrix(half),
                           preferred_element_type=F32).reshape(nblk, tm, LANE)
            y = y * cos_ref[...][None] + part * sin_ref[...][None]
        for j, c in enumerate(blocks):
            o_ref[0, :, c] = y[j].astype(BF16)

    qk_blocks(COL_AQ, COL_AV - COL_AQ, HEAD_DIM, 32, ca_ref if rope else None, sa_ref if rope else None)
    qk_blocks(COL_BQ, COL_BV - COL_BQ, B_DIM, 16, cb_ref if rope else None, sb_ref if rope else None)
    for c0, c1 in ((COL_AV, COL_BQ), (COL_BV, N_QKV_BLOCKS)):
        o_ref[0, :, c0 * LANE:c1 * LANE] = x_ref[0, :, c0 * LANE:c1 * LANE]


def _prep(proj, gains, rope_tabs):
    nb, r, _ = proj.shape
    tm = _tile(r, 256)
    width = N_QKV_BLOCKS * LANE
    rope = rope_tabs is not None
    in_specs = [
        pl.BlockSpec((1, tm, width), lambda b, t: (b, t, 0)),
        pl.BlockSpec((N_QKV_BLOCKS, 1, LANE), lambda b, t: (0, 0, 0)),
    ]
    args = [proj, gains]
    if rope:
        in_specs += [pl.BlockSpec((tm, LANE), lambda b, t: (t, 0))] * 4
        args += list(rope_tabs)
    return pl.pallas_call(
        functools.partial(_prep_kernel, rope=rope),
        grid=(nb, r // tm),
        in_specs=in_specs,
        out_specs=pl.BlockSpec((1, tm, width), lambda b, t: (b, t, 0)),
        out_shape=jax.ShapeDtypeStruct((nb, r, width), BF16),
        compiler_params=_cparams("parallel", "parallel"),
        name="qkv_prep",
    )(*args)


def _rope_tables(seq):
    pos = jnp.arange(seq)
    rows = (pos // GRID_W).astype(F32)[:, None]
    cols = (pos % GRID_W).astype(F32)[:, None]
    lane = jnp.arange(LANE)

    def tables(half):
        inv = ROPE_THETA ** (-(lane % half).astype(F32) / half)
        use_rows = (lane % (4 * half)) < 2 * half
        ang = jnp.where(use_rows[None, :], rows, cols) * inv[None, :]
        sign = jnp.where((lane % (2 * half)) < half, -1.0, 1.0)
        return jnp.cos(ang), jnp.sin(ang) * sign[None, :]

    ca, sa = tables(32)
    cb, sb = tables(16)
    return ca, sa, cb, sb


def _stack_heads(q):
    return jnp.concatenate([q[:, g * HEAD_DIM:(g + 1) * HEAD_DIM] for g in range(A_GROUP)], axis=0)


def _unstack_heads(o, rows):
    return jnp.concatenate([o[g * rows:(g + 1) * rows] for g in range(A_GROUP)], axis=1)


def _sink_column(sink_ref, kvh, rows):
    return jnp.concatenate(
        [jnp.full((rows, 1), sink_ref[kvh * A_GROUP + g] * LOG2E, F32) for g in range(A_GROUP)], axis=0)


def _attn_a_lat_kernel(sink_ref, q_ref, bias_ref, kl_ref, vl_ref, kc_ref, vc_ref, o_ref, *, seq):
    i = pl.program_id(1)
    band = 3 * Q_BLOCK
    start = pl.multiple_of(jnp.clip((i - 1) * Q_BLOCK, 0, seq - band), Q_BLOCK)
    bias = bias_ref[0]
    outs = []
    for kvh in range(A_KV_HEADS):
        hs = slice(kvh * HEAD_DIM, (kvh + 1) * HEAD_DIM)
        qs = _stack_heads(q_ref[0, :, kvh * A_GROUP * HEAD_DIM:(kvh + 1) * A_GROUP * HEAD_DIM])
        kb = kl_ref[0, pl.ds(start, band), hs]
        vb = vl_ref[0, pl.ds(start, band), hs]
        s_loc = lax.dot_general(qs, kb, NT_DIMS, preferred_element_type=F32)
        s_loc = (s_loc.reshape(A_GROUP, Q_BLOCK, band) + bias[None]).reshape(A_GROUP * Q_BLOCK, band)
        s_ctx = lax.dot_general(qs, kc_ref[0, :, hs], NT_DIMS, preferred_element_type=F32)
        sk = _sink_column(sink_ref, kvh, Q_BLOCK)
        m = jnp.maximum(jnp.maximum(jnp.max(s_loc, axis=-1, keepdims=True),
                                    jnp.max(s_ctx, axis=-1, keepdims=True)), sk)
        p_loc = jnp.exp2(s_loc - m)
        p_ctx = jnp.exp2(s_ctx - m)
        den = (jnp.sum(p_loc, axis=-1, keepdims=True) + jnp.sum(p_ctx, axis=-1, keepdims=True)
               + jnp.exp2(sk - m))
        o = (jnp.dot(p_loc.astype(BF16), vb, preferred_element_type=F32)
             + jnp.dot(p_ctx.astype(BF16), vc_ref[0, :, hs], preferred_element_type=F32)) / den
        outs.append(_unstack_heads(o, Q_BLOCK))
    o_ref[0] = jnp.concatenate(outs, axis=1).astype(BF16)


def _attn_a_ctx_kernel(sink_ref, q_ref, kc_ref, vc_ref, o_ref):
    kvh = pl.program_id(1)
    rows = q_ref.shape[1]
    qs = _stack_heads(q_ref[0])
    s = lax.dot_general(qs, kc_ref[0], NT_DIMS, preferred_element_type=F32)
    sk = _sink_column(sink_ref, kvh, rows)
    m = jnp.maximum(jnp.max(s, axis=-1, keepdims=True), sk)
    p = jnp.exp2(s - m)
    den = jnp.sum(p, axis=-1, keepdims=True) + jnp.exp2(sk - m)
    o = jnp.dot(p.astype(BF16), vc_ref[0], preferred_element_type=F32) / den
    o_ref[0] = _unstack_heads(o, rows).astype(BF16)


def _window_bias():
    band = 3 * Q_BLOCK
    row = jnp.arange(Q_BLOCK)[None, :, None]
    col = jnp.arange(band)[None, None, :]
    delta = (jnp.arange(3) * Q_BLOCK)[:, None, None]
    return jnp.where(jnp.abs(col - delta - row) <= WINDOW, 0.0, NEG_BIG).astype(F32)


def _attn_a(sink, p_l, p_c, ctx_out):
    b, seq, _ = p_l.shape
    lc = p_c.shape[1]
    gw = A_GROUP * HEAD_DIM
    nblk = seq // Q_BLOCK
    kvw = A_KV_HEADS * HEAD_DIM
    smem = pl.BlockSpec(memory_space=pltpu.SMEM)

    def bias_map(n, i):
        return (i - jnp.clip(i - 1, 0, nblk - 3), 0, 0)

    o_l = pl.pallas_call(
        functools.partial(_attn_a_lat_kernel, seq=seq),
        grid=(b, nblk),
        in_specs=[
            smem,
            pl.BlockSpec((1, Q_BLOCK, N_HEADS * HEAD_DIM), lambda n, i: (n, i, 0)),
            pl.BlockSpec((1, Q_BLOCK, 3 * Q_BLOCK), bias_map),
            pl.BlockSpec((1, seq, kvw), lambda n, i: (n, 0, COL_AK // A_KV_HEADS)),
            pl.BlockSpec((1, seq, kvw), lambda n, i: (n, 0, COL_AV // A_KV_HEADS)),
            pl.BlockSpec((1, lc, kvw), lambda n, i: (n, 0, COL_AK // A_KV_HEADS)),
            pl.BlockSpec((1, lc, kvw), lambda n, i: (n, 0, COL_AV // A_KV_HEADS)),
        ],
        out_specs=pl.BlockSpec((1, Q_BLOCK, N_HEADS * HEAD_DIM), lambda n, i: (n, i, 0)),
        out_shape=jax.ShapeDtypeStruct((b, seq, N_HEADS * HEAD_DIM), BF16),
        compiler_params=_cparams("parallel", "arbitrary"),
        name="attn_a_latent",
    )(sink, p_l, _window_bias(), p_l, p_l, p_c, p_c)
    o_c = None
    if ctx_out:
        o_c = pl.pallas_call(
            _attn_a_ctx_kernel,
            grid=(b, A_KV_HEADS),
            in_specs=[
                smem,
                pl.BlockSpec((1, lc, gw), lambda n, h: (n, 0, h)),
                pl.BlockSpec((1, lc, LANE), lambda n, h: (n, 0, COL_AK + h)),
                pl.BlockSpec((1, lc, LANE), lambda n, h: (n, 0, COL_AV + h)),
            ],
            out_specs=pl.BlockSpec((1, lc, gw), lambda n, h: (n, 0, h)),
            out_shape=jax.ShapeDtypeStruct((b, lc, N_HEADS * HEAD_DIM), BF16),
            compiler_params=_cparams("parallel", "parallel"),
            name="attn_a_ctx",
        )(sink, p_c, p_c, p_c)
    return o_c, o_l


def _attn_b_kernel(lam_ref, q_ref, *refs, nseg, post_scale):
    k_refs = refs[:nseg]
    v_refs = refs[nseg:2 * nseg]
    g_ref, o_ref = refs[2 * nseg], refs[2 * nseg + 1]
    q = q_ref[0]
    tq = q.shape[0]
    lane = lax.broadcasted_iota(jnp.int32, q.shape, 1)
    zero = jnp.zeros_like(q)
    q2 = jnp.concatenate([jnp.where(lane < B_DIM, q, zero), jnp.where(lane < B_DIM, zero, q)], axis=0)
    m = jnp.full((2 * tq, 1), NEG_BIG, F32)
    den = jnp.zeros((2 * tq, 1), F32)
    acc = jnp.zeros((2 * tq, LANE), F32)
    for k_ref, v_ref in zip(k_refs, v_refs):
        nk = k_ref.shape[1]
        kc = _tile(nk, B_KEY_CHUNK)
        for c0 in range(0, nk, kc):
            s = lax.dot_general(q2, k_ref[0, c0:c0 + kc, :], NT_DIMS, preferred_element_type=F32)
            m_new = jnp.maximum(m, jnp.max(s, axis=-1, keepdims=True))
            alpha = jnp.exp2(m - m_new)
            p = jnp.exp2(s - m_new)
            den = alpha * den + jnp.sum(p, axis=-1, keepdims=True)
            acc = alpha * acc + jnp.dot(p.astype(BF16), v_ref[0, c0:c0 + kc, :],
                                        preferred_element_type=F32)
            m = m_new
    o2 = acc / den
    o = o2[:tq] - lam_ref[0] * o2[tq:]
    ms = jnp.mean(o * o, axis=-1, keepdims=True)
    o_ref[0] = ((o * lax.rsqrt(ms + EPS) * g_ref[...]) * post_scale).astype(BF16)


def _attn_b_call(lam, q_src, k_srcs, subln, post_scale, name):
    b, rq, _ = q_src.shape
    tq = _tile(rq, 512)
    smem = pl.BlockSpec(memory_space=pltpu.SMEM)
    in_specs = [smem, pl.BlockSpec((1, tq, LANE), lambda n, h, i: (n, i, COL_BQ + h))]
    in_specs += [pl.BlockSpec((1, s.shape[1], LANE), lambda n, h, i: (n, 0, COL_BK + h)) for s in k_srcs]
    in_specs += [pl.BlockSpec((1, s.shape[1], LANE), lambda n, h, i: (n, 0, COL_BV + h)) for s in k_srcs]
    in_specs += [pl.BlockSpec((1, LANE), lambda n, h, i: (0, 0))]
    return pl.pallas_call(
        functools.partial(_attn_b_kernel, nseg=len(k_srcs), post_scale=post_scale),
        grid=(b, N_HEADS, rq // tq),
        in_specs=in_specs,
        out_specs=pl.BlockSpec((1, tq, LANE), lambda n, h, i: (n, i, h)),
        out_shape=jax.ShapeDtypeStruct((b, rq, N_HEADS * HEAD_DIM), BF16),
        compiler_params=_cparams("parallel", "parallel", "arbitrary"),
        name=name,
    )(lam, q_src, *k_srcs, *k_srcs, subln)


def _attn_b(lam, p_l, p_c, subln, post_scale, ctx_out):
    o_l = _attn_b_call(lam, p_l, [p_c, p_l], subln, post_scale, "attn_b_latent")
    o_c = _attn_b_call(lam, p_c, [p_c], subln, post_scale, "attn_b_ctx") if ctx_out else None
    return o_c, o_l


def _centred_conv(x_ref, pad_ref, w_ref, b_ref):
    n = x_ref.shape[1]
    zeros = jnp.zeros((SUBLANE, LANE), F32)
    pad_ref[0:SUBLANE, :] = zeros
    pad_ref[SUBLANE:SUBLANE + n, :] = x_ref[0]
    pad_ref[SUBLANE + n:2 * SUBLANE + n, :] = zeros
    out = b_ref[...]
    for tap in range(4):
        out = out + pad_ref[pl.ds(SUBLANE - 2 + tap, n), :] * w_ref[tap:tap + 1, :]
    return out


def _every_8th(ref, d, j, n):
    return ref[d, pl.ds(j, n, stride=SUBLANE), :]


def _tile_order(reverse):
    return range(SUBLANE - 1, -1, -1) if reverse else range(SUBLANE)


def _scan_summaries(a_ref, v_ref, d, n, reverse):
    h = p = None
    for j in _tile_order(reverse):
        a, v = _every_8th(a_ref, d, j, n), _every_8th(v_ref, d, j, n)
        h, p = (v, a) if h is None else (a * h + v, a * p)
    return p, h


def _scan_entering(a_ref, v_ref, d, n, carry_in, reverse, store):
    e = carry_in
    for j in _tile_order(reverse):
        store(j, e)
        e = _every_8th(a_ref, d, j, n) * e + _every_8th(v_ref, d, j, n)


def _scan_apply(a_ref, v_ref, d, n, carry_in, reverse, store):
    h = carry_in
    for j in _tile_order(reverse):
        h = _every_8th(a_ref, d, j, n) * h + _every_8th(v_ref, d, j, n)
        store(j, h)


def _seg_tile(t, ntc, ntl, reverse):
    if not reverse:
        return t
    return jnp.where(t < ntc, ntc - 1 - t, 2 * ntc + ntl - 1 - t)


def _lru_kernel(xc_ref, yc_ref, xl_ref, yl_ref, cw_ref, cb_ref, wr_ref, br_ref, wi_ref, bi_ref,
                lam_ref, *refs, ctx_out):
    if ctx_out:
        oc_ref, ol_ref, ac_ref, hl_ref, hs_ref, pad_ref, tp_ref, th_ref, sp_ref, sh_ref, c3_ref, e1_ref = refs
    else:
        ol_ref, ac_ref, hl_ref, hs_ref, pad_ref, tp_ref, th_ref, sp_ref, sh_ref, c3_ref, e1_ref = refs
    lc, seq = xc_ref.shape[1], xl_ref.shape[1]
    for x_ref, off in ((xc_ref, 0), (xl_ref, lc)):
        n = x_ref.shape[1]
        u = _centred_conv(x_ref, pad_ref, cw_ref, cb_ref)
        ub = u.astype(BF16)
        for d in range(2):
            lam = lam_ref[d:d + 1, :]
            sp = jnp.maximum(-lam, 0.0) + jnp.log1p(jnp.exp(-jnp.abs(lam)))
            r = _sigmoid(jnp.dot(ub, wr_ref[d, 0], preferred_element_type=F32) + br_ref[d:d + 1, :])
            gi = _sigmoid(jnp.dot(ub, wi_ref[d, 0], preferred_element_type=F32) + bi_ref[d:d + 1, :])
            log_a = -LRU_C * r * sp
            a = jnp.exp(log_a)
            y = jnp.tanh(-log_a) * (1.0 + a * a)
            v = (y * lax.rsqrt(jnp.maximum(y, TINY))) * gi * u
            ac_ref[d, off:off + n, :] = a
            hl_ref[d, off:off + n, :] = v

    nt1 = (lc + seq) // SUBLANE
    nt2 = nt1 // SUBLANE
    group_rows = SUBLANE * SUBLANE
    for d in range(2):
        tp_ref[d], th_ref[d] = _scan_summaries(ac_ref, hl_ref, d, nt1, d == 1)
        sp_ref[d], sh_ref[d] = _scan_summaries(tp_ref, th_ref, d, nt2, d == 1)

    def step(s, carry):
        new = []
        for d in range(2):
            g = pl.ds(_seg_tile(s, lc // group_rows, seq // group_rows, d == 1), 1)
            c3_ref[d, g, :] = carry[d]
            new.append(sp_ref[d, g, :] * carry[d] + sh_ref[d, g, :])
        return tuple(new)

    zero = jnp.zeros((1, LANE), F32)
    lax.fori_loop(0, nt2, step, (zero, zero))

    for d in range(2):
        def store_entering(j, e, d=d):
            e1_ref[d, pl.ds(j, nt2, stride=SUBLANE), :] = e

        def store_h(j, h, d=d):
            if d == 0:
                hs_ref[pl.ds(j, nt1, stride=SUBLANE), :] = h
            else:
                hl_ref[d, pl.ds(j, nt1, stride=SUBLANE), :] = h

        _scan_entering(tp_ref, th_ref, d, nt2, c3_ref[d], d == 1, store_entering)
        _scan_apply(ac_ref, hl_ref, d, nt1, e1_ref[d], d == 1, store_h)
    if ctx_out:
        hc = hs_ref[0:lc, :] + hl_ref[1, 0:lc, :]
        oc_ref[0] = (hc * jax.nn.gelu(yc_ref[0])).astype(BF16)
    hl = hs_ref[lc:lc + seq, :] + hl_ref[1, lc:lc + seq, :]
    ol_ref[0] = (hl * jax.nn.gelu(yl_ref[0])).astype(BF16)


def _lru(proj_l, proj_c, conv_w, conv_b, w_r, b_r, w_i, b_i, lam, ctx_out):
    b, seq, _ = proj_l.shape
    lc = proj_c.shape[1]
    nblk = w_r.shape[1]
    width = nblk * LANE
    nt1 = (lc + seq) // SUBLANE
    nt2 = nt1 // SUBLANE

    def seg(n, col):
        return pl.BlockSpec((1, n, LANE), lambda i, j: (i, 0, col + j))

    vec2 = pl.BlockSpec((2, LANE), lambda i, j: (0, j))
    wspec = pl.BlockSpec((2, 1, LANE, LANE), lambda i, j: (0, j, 0, 0))
    out_specs = [pl.BlockSpec((1, seq, LANE), lambda i, j: (i, 0, j))]
    out_shape = [jax.ShapeDtypeStruct((b, seq, width), BF16)]
    if ctx_out:
        out_specs.insert(0, pl.BlockSpec((1, lc, LANE), lambda i, j: (i, 0, j)))
        out_shape.insert(0, jax.ShapeDtypeStruct((b, lc, width), BF16))
    outs = pl.pallas_call(
        functools.partial(_lru_kernel, ctx_out=ctx_out),
        grid=(b, nblk),
        in_specs=[
            seg(lc, COL_CX), seg(lc, COL_CY), seg(seq, COL_CX), seg(seq, COL_CY),
            pl.BlockSpec((4, LANE), lambda i, j: (0, j)),
            pl.BlockSpec((1, LANE), lambda i, j: (0, j)),
            wspec, vec2, wspec, vec2, vec2,
        ],
        out_specs=out_specs,
        out_shape=out_shape,
        scratch_shapes=[
            pltpu.VMEM((2, lc + seq, LANE), F32),
            pltpu.VMEM((2, lc + seq, LANE), F32),
            pltpu.VMEM((lc + seq, LANE), F32),
            pltpu.VMEM((max(lc, seq) + 2 * SUBLANE, LANE), F32),
            pltpu.VMEM((2, nt1, LANE), F32),
            pltpu.VMEM((2, nt1, LANE), F32),
            pltpu.VMEM((2, nt2, LANE), F32),
            pltpu.VMEM((2, nt2, LANE), F32),
            pltpu.VMEM((2, nt2, LANE), F32),
            pltpu.VMEM((2, nt1, LANE), F32),
        ],
        compiler_params=_cparams("parallel", "parallel"),
        name="rglru",
    )(proj_c, proj_c, proj_l, proj_l, conv_w, conv_b.reshape(1, width), w_r, b_r, w_i, b_i, lam)
    return (outs[0], outs[1]) if ctx_out else (None, outs[0])


def _gla_level_masks(reverse):
    c = GLA_CHUNK
    row = lax.broadcasted_iota(jnp.int32, (c, c), 0)
    col = lax.broadcasted_iota(jnp.int32, (c, c), 1)
    masks = {}
    for s in (32, 16, 8):
        same = (row // (2 * s)) == (col // (2 * s))
        if reverse:
            masks[s] = same & ((row % (2 * s)) < s) & ((col % (2 * s)) >= s)
        else:
            masks[s] = same & ((row % (2 * s)) >= s) & ((col % (2 * s)) < s)
    tri = (row <= col) if reverse else (row >= col)
    return masks, tri.astype(F32)


def _gla_chunk(q, k, v, g, st, masks, tri, reverse):
    c = GLA_CHUNK
    b = jnp.dot(tri, g, precision=lax.Precision.HIGHEST, preferred_element_type=F32)
    att = jnp.zeros((c, c), F32)
    for s in (32, 16, 8):
        b3 = b.reshape(c // (2 * s), 2 * s, LANE)
        rr = s if reverse else s - 1
        rho = jnp.broadcast_to(b3[:, rr:rr + 1, :], b3.shape).reshape(c, LANE)
        e = jnp.exp(-jnp.abs(b - rho))
        a = lax.dot_general((q * e).astype(BF16), (k * e).astype(BF16), NT_DIMS,
                            preferred_element_type=F32)
        att = att + jnp.where(masks[s], a, 0.0)
    vb = v.astype(BF16)
    o = jnp.dot(att.astype(BF16), vb, preferred_element_type=F32)
    nb = c // SUBLANE
    b3 = b.reshape(nb, SUBLANE, LANE)
    q3 = q.reshape(nb, SUBLANE, LANE)
    k3 = k.reshape(nb, SUBLANE, LANE)
    v3 = v.reshape(nb, SUBLANE, LANE)
    rowi = lax.broadcasted_iota(jnp.int32, b3.shape, 1)
    od = jnp.zeros(b3.shape, F32)
    for jj in range(SUBLANE):
        keep = (rowi <= jj) if reverse else (rowi >= jj)
        e = jnp.where(keep, jnp.exp(jnp.minimum(b3 - b3[:, jj:jj + 1, :], 0.0)), 0.0)
        sj = jnp.sum(q3 * e * k3[:, jj:jj + 1, :], axis=-1, keepdims=True)
        od = od + sj * v3[:, jj:jj + 1, :]
    o = o + od.reshape(c, LANE)
    o = o + lax.dot_general((q * jnp.exp(b)).astype(BF16), st.astype(BF16), NT_DIMS,
                            preferred_element_type=F32)
    b_end = b[0:1, :] if reverse else b[c - 1:c, :]
    khat = (k * jnp.exp(b_end - b)).astype(BF16)
    st_new = st * jnp.exp(b_end) + lax.dot_general(vb, khat, TN_DIMS, preferred_element_type=F32)
    return o, st_new


def _gla_fast_intra(q, k, v, g, tri, reverse):
    c = GLA_CHUNK
    g_hi = g.astype(BF16)
    g_lo = (g - g_hi.astype(F32)).astype(BF16)
    b2 = jnp.dot(tri.astype(BF16), jnp.concatenate([g_hi, g_lo], axis=1), preferred_element_type=F32)
    b = b2[:, :LANE] + b2[:, LANE:]
    rr = c // 2 if reverse else c // 2 - 1
    rho = b[rr:rr + 1, :]
    qt = q * jnp.exp(b - rho)
    kt = k * jnp.exp(rho - b)
    a = lax.dot_general(qt.astype(BF16), kt.astype(BF16), NT_DIMS, preferred_element_type=F32)
    att = jnp.where(tri > 0.0, a, 0.0)
    vb = v.astype(BF16)
    o_intra = jnp.dot(att.astype(BF16), vb, preferred_element_type=F32)
    b_end = b[0:1, :] if reverse else b[c - 1:c, :]
    qe = (qt * jnp.exp(rho)).astype(BF16)
    khat = (kt * jnp.exp(b_end - rho)).astype(BF16)
    return o_intra, qe, khat, vb, jnp.exp(b_end)


def _gla_fast_inter(intra, st):
    o_intra, qe, khat, vb, decay = intra
    o = o_intra + lax.dot_general(qe, st.astype(BF16), NT_DIMS, preferred_element_type=F32)
    st_new = st * decay + lax.dot_general(vb, khat, TN_DIMS, preferred_element_type=F32)
    return o, st_new


def _gla_kernel(lb_ref, on_ref, qc_ref, ffc_ref, fbc_ref, ic_ref, gc_ref,
                ql_ref, ffl_ref, fbl_ref, il_ref, gl_ref, *refs, ctx_out):
    if ctx_out:
        oc_ref, ol_ref, q_s, v_s, g_s, k_s, o_s, st_s = refs
    else:
        ol_ref, q_s, v_s, g_s, k_s, o_s, st_s = refs
    lc, seq = qc_ref.shape[1], ql_ref.shape[1]
    ncc, ncl = lc // GLA_CHUNK, seq // GLA_CHUNK
    for off, n, q_ref, i_ref, f_refs in ((0, lc, qc_ref, ic_ref, (ffc_ref, fbc_ref)),
                                         (lc, seq, ql_ref, il_ref, (ffl_ref, fbl_ref))):
        q_s[off:off + n, :] = q_ref[0]
        v_s[off:off + n, :] = i_ref[0]
        for d in range(2):
            z = f_refs[d][0]
            lbd = lb_ref[d:d + 1, :]
            ez = jnp.exp(-jnp.abs(z))
            r = 1.0 / (1.0 + ez)
            pos = z >= 0.0
            sig_p = jnp.where(pos, r, ez * r)
            sig_n = jnp.where(pos, ez * r, r)
            g_s[d, off:off + n, :] = jnp.log(lbd + (1.0 - lbd) * sig_p)
            k_s[d, off:off + n, :] = (1.0 - lbd) * sig_n
    st_s[...] = jnp.zeros(st_s.shape, F32)
    consts = [_gla_level_masks(False), _gla_level_masks(True)]

    half = GLA_CHUNK // 2
    worst = jnp.zeros((1, LANE), F32)
    for d in range(2):
        hs = jnp.sum(g_s[d].reshape((lc + seq) // half, half, LANE), axis=1)
        worst = jnp.maximum(worst, jnp.max(-hs, axis=0, keepdims=True))
    safe = jnp.max(worst) < GLA_SAFE_DECAY

    def chunk_rows(cidx, d):
        chunk = _seg_tile(cidx, ncc, ncl, d == 1)
        return pl.ds(pl.multiple_of(chunk * GLA_CHUNK, GLA_CHUNK), GLA_CHUNK)

    def robust_step(cidx, carry):
        for d in range(2):
            rows = chunk_rows(cidx, d)
            o, st_new = _gla_chunk(q_s[rows, :], k_s[d, rows, :], v_s[rows, :], g_s[d, rows, :],
                                   st_s[d], consts[d][0], consts[d][1], d == 1)
            st_s[d] = st_new
            o_s[d, rows, :] = o
        return carry

    nchunks = ncc + ncl
    group = next(u for u in (4, 3, 2, 1) if nchunks % u == 0)

    def fast_step(t, carry):
        work = [[] for _ in range(2)]
        for d in range(2):
            for u in range(group):
                rows = chunk_rows(t * group + u, d)
                work[d].append((rows, _gla_fast_intra(q_s[rows, :], k_s[d, rows, :], v_s[rows, :],
                                                      g_s[d, rows, :], consts[d][1], d == 1)))
        for d in range(2):
            st = st_s[d]
            outs = []
            for rows, intra in work[d]:
                o, st = _gla_fast_inter(intra, st)
                outs.append((rows, o))
            st_s[d] = st
            for rows, o in outs:
                o_s[d, rows, :] = o
        return carry

    @pl.when(safe)
    def _():
        lax.fori_loop(0, nchunks // group, fast_step, 0)

    @pl.when(jnp.logical_not(safe))
    def _():
        lax.fori_loop(0, nchunks, robust_step, 0)

    def finish(o, gate):
        ms = jnp.mean(o * o, axis=-1, keepdims=True)
        return ((o * lax.rsqrt(ms + EPS) * on_ref[...]) * (gate * _sigmoid(gate))).astype(BF16)

    if ctx_out:
        oc_ref[0] = finish(o_s[0, 0:lc, :] + o_s[1, 0:lc, :], gc_ref[0])
    ol_ref[0] = finish(o_s[0, lc:lc + seq, :] + o_s[1, lc:lc + seq, :], gl_ref[0])


def _gla(proj_l, proj_c, lb, onorm, ctx_out):
    b, seq, _ = proj_l.shape
    lc = proj_c.shape[1]
    width = N_HEADS * HEAD_DIM
    nt = lc + seq

    def seg(n, col):
        return pl.BlockSpec((1, n, LANE), lambda i, h: (i, 0, col + h))

    cols = (COL_DQ, COL_DFF, COL_DFB, COL_DI, COL_DG)
    out_specs = [pl.BlockSpec((1, seq, LANE), lambda i, h: (i, 0, h))]
    out_shape = [jax.ShapeDtypeStruct((b, seq, width), BF16)]
    if ctx_out:
        out_specs.insert(0, pl.BlockSpec((1, lc, LANE), lambda i, h: (i, 0, h)))
        out_shape.insert(0, jax.ShapeDtypeStruct((b, lc, width), BF16))
    outs = pl.pallas_call(
        functools.partial(_gla_kernel, ctx_out=ctx_out),
        grid=(b, N_HEADS),
        in_specs=[pl.BlockSpec((2, LANE), lambda i, h: (0, h)),
                  pl.BlockSpec((1, LANE), lambda i, h: (0, 0))]
        + [seg(lc, c) for c in cols] + [seg(seq, c) for c in cols],
        out_specs=out_specs,
        out_shape=out_shape,
        scratch_shapes=[
            pltpu.VMEM((nt, LANE), F32),
            pltpu.VMEM((nt, LANE), F32),
            pltpu.VMEM((2, nt, LANE), F32),
            pltpu.VMEM((2, nt, LANE), F32),
            pltpu.VMEM((2, nt, LANE), F32),
            pltpu.VMEM((2, LANE, LANE), F32),
        ],
        compiler_params=_cparams("parallel", "parallel"),
        name="hgrn2",
    )(lb, onorm, *([proj_c] * 5), *([proj_l] * 5))
    return (outs[0], outs[1]) if ctx_out else (None, outs[0])


def _merge_kernel(oa_ref, ob_ref, oc_ref, od_ref, g0_ref, g1_ref, g2_ref, g3_ref,
                  w0_ref, w1_ref, w2_ref, w3_ref, y_ref):
    acc = None
    for o_ref, g_ref, w_ref in ((oa_ref, g0_ref, w0_ref), (ob_ref, g1_ref, w1_ref),
                                (oc_ref, g2_ref, w2_ref), (od_ref, g3_ref, w3_ref)):
        t = _sigmoid(g_ref[0].astype(F32)) * jnp.dot(o_ref[0], w_ref[0], preferred_element_type=F32)
        acc = t if acc is None else acc + t
    y_ref[0] = acc.astype(BF16)


def _merge(outs, proj, w_branch):
    nb, r, mw = outs[0].shape
    d = w_branch.shape[2]
    tm = _tile(r, 512)
    tn = 512
    nj = d // tn
    o_spec = pl.BlockSpec((1, tm, mw), lambda b, t, j: (b, t, 0))
    g_specs = [pl.BlockSpec((1, tm, tn), functools.partial(
        lambda b, t, j, n: (b, t, (COL_GATE * LANE) // tn + n * nj + j), n=n)) for n in range(4)]
    w_specs = [pl.BlockSpec((1, mw, tn), functools.partial(lambda b, t, j, n: (n, 0, j), n=n))
               for n in range(4)]
    return pl.pallas_call(
        _merge_kernel,
        grid=(nb, r // tm, nj),
        in_specs=[o_spec] * 4 + g_specs + w_specs,
        out_specs=pl.BlockSpec((1, tm, tn), lambda b, t, j: (b, t, j)),
        out_shape=jax.ShapeDtypeStruct((nb, r, d), BF16),
        compiler_params=_cparams("parallel", "parallel", "arbitrary"),
        name="branch_merge",
    )(*outs, *([proj] * 4), *([w_branch] * 4))


def _out_proj_kernel(y_ref, w_ref, x_ref, g_ref, o_ref):
    o_ref[0] = x_ref[0] + g_ref[0] * jnp.dot(y_ref[0], w_ref[...], preferred_element_type=F32)


def _out_proj(y, w, x, gate):
    nb, r, d = x.shape
    tm = _tile(r, 1024)
    tn = 512
    return pl.pallas_call(
        _out_proj_kernel,
        grid=(nb, r // tm, d // tn),
        in_specs=[
            pl.BlockSpec((1, tm, d), lambda b, t, j: (b, t, 0)),
            pl.BlockSpec((d, tn), lambda b, t, j: (0, j)),
            pl.BlockSpec((1, tm, tn), lambda b, t, j: (b, t, j)),
            pl.BlockSpec((1, 1, tn), lambda b, t, j: (b, 0, j)),
        ],
        out_specs=pl.BlockSpec((1, tm, tn), lambda b, t, j: (b, t, j)),
        out_shape=jax.ShapeDtypeStruct((nb, r, d), F32),
        compiler_params=_cparams("parallel", "parallel", "arbitrary"),
        name="out_proj",
    )(y, w, x, gate)


def _route(logits_t, bias_col):
    aff = _sigmoid(logits_t)
    sel = aff + bias_col
    aff_r = [aff[e:e + 1, :] for e in range(N_EXPERTS)]
    sel_r = [sel[e:e + 1, :] for e in range(N_EXPERTS)]
    scores = []
    for g in range(N_GROUPS):
        v = sel_r[g * EXPERTS_PER_GROUP:(g + 1) * EXPERTS_PER_GROUP]
        m1 = functools.reduce(jnp.maximum, v)
        taken = jnp.zeros(m1.shape, jnp.bool_)
        second = jnp.full(m1.shape, -jnp.inf, F32)
        for x in v:
            first = (x == m1) & jnp.logical_not(taken)
            taken = taken | first
            second = jnp.where(first, second, jnp.maximum(second, x))
        scores.append(m1 + second)
    best, gidx = scores[0], jnp.zeros(scores[0].shape, jnp.int32)
    for g in range(1, N_GROUPS):
        better = scores[g] > best
        gidx = jnp.where(better, g, gidx)
        best = jnp.where(better, scores[g], best)
    masked = [jnp.where(gidx == e // EXPERTS_PER_GROUP, sel_r[e], -jnp.inf) for e in range(N_EXPERTS)]

    def first_argmax(vals, exclude):
        bv = jnp.full(vals[0].shape, -jnp.inf, F32)
        bi = jnp.full(vals[0].shape, -1, jnp.int32)
        for e, x in enumerate(vals):
            better = x > bv
            if exclude is not None:
                better = better & (exclude != e)
            bi = jnp.where(better, e, bi)
            bv = jnp.where(better, x, bv)
        return bi

    i1 = first_argmax(masked, None)
    i2 = first_argmax(masked, i1)
    w1 = functools.reduce(jnp.add, [jnp.where(i1 == e, aff_r[e], 0.0) for e in range(N_EXPERTS)])
    w2 = functools.reduce(jnp.add, [jnp.where(i2 == e, aff_r[e], 0.0) for e in range(N_EXPERTS)])
    tot = w1 + w2
    g1, g2 = w1 / tot, w2 / tot
    width = logits_t.shape[1]
    rowi = lax.broadcasted_iota(jnp.int32, (SUBLANE, width), 0)
    local1 = jnp.broadcast_to(i1 - gidx * EXPERTS_PER_GROUP, (SUBLANE, width))
    local2 = jnp.broadcast_to(i2 - gidx * EXPERTS_PER_GROUP, (SUBLANE, width))
    info = (jnp.where(rowi == local1, jnp.broadcast_to(g1, (SUBLANE, width)), 0.0)
            + jnp.where(rowi == local2, jnp.broadcast_to(g2, (SUBLANE, width)), 0.0))
    return jnp.where(rowi == EXPERTS_PER_GROUP,
                     jnp.broadcast_to(gidx.astype(F32), (SUBLANE, width)), info)


def _moe_route_kernel(x_ref, g_ref, sh_ref, sc_ref, wrt_ref, br_ref, hn_ref, info_ref):
    d = x_ref.shape[2]
    h = _modulated_norm(x_ref[0], g_ref[...], sh_ref[0], sc_ref[0])
    logits_t = lax.dot_general(wrt_ref[...], h, NT_DIMS, precision=lax.Precision.HIGHEST,
                               preferred_element_type=F32)
    info = _route(logits_t, br_ref[...])
    info_ref[0] = info
    hn_ref[0, :, 0:d] = h
    pad = jnp.zeros((LANE - SUBLANE, info.shape[1]), F32)
    hn_ref[0, :, d:d + LANE] = jnp.concatenate([info, pad], axis=0).T


def _moe_route(x, g, sh, sc, w_router_t, b_router):
    nb, r, d = x.shape
    tm = _tile(r, 512)
    vec = pl.BlockSpec((1, 1, d), lambda b, t: (b, 0, 0))
    return pl.pallas_call(
        _moe_route_kernel,
        grid=(nb, r // tm),
        in_specs=[
            pl.BlockSpec((1, tm, d), lambda b, t: (b, t, 0)),
            pl.BlockSpec((1, d), lambda b, t: (0, 0)),
            vec, vec,
            pl.BlockSpec((N_EXPERTS, d), lambda b, t: (0, 0)),
            pl.BlockSpec((N_EXPERTS, 1), lambda b, t: (0, 0)),
        ],
        out_specs=[pl.BlockSpec((1, tm, d + LANE), lambda b, t: (b, t, 0)),
                   pl.BlockSpec((1, SUBLANE, tm), lambda b, t: (b, 0, t))],
        out_shape=[jax.ShapeDtypeStruct((nb, r, d + LANE), F32),
                   jax.ShapeDtypeStruct((nb, SUBLANE, r), F32)],
        compiler_params=_cparams("parallel", "parallel"),
        name="moe_route",
    )(x, g, sh, sc, w_router_t, b_router.reshape(N_EXPERTS, 1))


def _row_copy(src, src_row, dst, dst_row, sem):
    return pltpu.make_async_copy(src.at[pl.ds(src_row, 1), :], dst.at[pl.ds(dst_row, 1), :], sem)


def _moe_experts_kernel(dst_ref, tgrp_ref, nval_ref, hn_hbm, wg_ref, wu_ref, wd_ref,
                        y_hbm, xbuf, hb, gbuf, acc, gsem, ssem, *, tm, n_tok):
    i = pl.program_id(0)
    e = pl.program_id(1)
    nt = pl.num_programs(0)
    slot = i % 2
    per = tm // EXPERTS_PER_GROUP
    d = hb.shape[1]

    def gather_row(slot_idx, buf, row):
        s = dst_ref[slot_idx]
        return _row_copy(hn_hbm, jnp.where(s < n_tok, s, 0), xbuf.at[buf], row, gsem.at[buf])

    def wait_scatter(sl):
        pltpu.make_async_copy(acc.at[sl], y_hbm.at[pl.ds(0, tm), :], ssem.at[sl]).wait()

    @pl.when((i == 0) & (e == 0))
    def _():
        def body(r, c):
            gather_row(r, 0, r).start()
            return c
        lax.fori_loop(0, tm, body, 0)
        acc[1] = jnp.zeros(acc.shape[1:], F32)
        for half in range(2):
            fill = pltpu.make_async_copy(acc.at[1], y_hbm.at[pl.ds(n_tok + half * tm, tm), :], ssem.at[1])
            fill.start()
            fill.wait()

    @pl.when(e == 0)
    def _():
        pltpu.make_async_copy(hn_hbm.at[pl.ds(0, tm), :], xbuf.at[slot], gsem.at[slot]).wait()
        hb[...] = xbuf[slot, :, 0:d].astype(BF16)
        gbuf[...] = xbuf[slot, :, d:d + LANE]

        @pl.when(i >= 2)
        def _():
            wait_scatter(slot)
        acc[slot] = jnp.zeros(acc.shape[1:], F32)

    @pl.when(i + 1 < nt)
    def _():
        base = (i + 1) * tm + e * per
        for k in range(per):
            gather_row(base + k, 1 - slot, e * per + k).start()

    @pl.when(nval_ref[i] > 0)
    def _():
        hv = hb[...]
        a = jnp.dot(hv, wg_ref[0], preferred_element_type=F32)
        u = jnp.dot(hv, wu_ref[0], preferred_element_type=F32)
        hid = ((a * _sigmoid(a)) * u).astype(BF16)
        y = jnp.dot(hid, wd_ref[0], preferred_element_type=F32)
        lane = lax.broadcasted_iota(jnp.int32, gbuf.shape, 1)
        col = jnp.sum(jnp.where(lane == e, gbuf[...], 0.0), axis=-1, keepdims=True)
        acc[slot] = acc[slot] + col * y

    @pl.when(e == EXPERTS_PER_GROUP - 1)
    def _():
        def body(r, c):
            _row_copy(acc.at[slot], r, y_hbm, dst_ref[i * tm + r], ssem.at[slot]).start()
            return c
        lax.fori_loop(0, tm, body, 0, unroll=8)

        @pl.when(i == nt - 1)
        def _():
            wait_scatter(slot)

            @pl.when(i >= 1)
            def _():
                wait_scatter(1 - slot)


def _moe_experts(hn_ext, dst, tile_group, tile_nvalid, w_gate, w_up, w_down, tm):
    t, dw = hn_ext.shape
    d = dw - LANE
    dff = w_gate.shape[2]
    ntiles = tile_group.shape[0]

    def wmap(i, e, dst_r, tgrp_r, nval_r):
        return (tgrp_r[i] * EXPERTS_PER_GROUP + e, 0, 0)

    grid_spec = pltpu.PrefetchScalarGridSpec(
        num_scalar_prefetch=3,
        grid=(ntiles, EXPERTS_PER_GROUP),
        in_specs=[
            pl.BlockSpec(memory_space=pl.ANY),
            pl.BlockSpec((1, d, dff), wmap),
            pl.BlockSpec((1, d, dff), wmap),
            pl.BlockSpec((1, dff, d), wmap),
        ],
        out_specs=pl.BlockSpec(memory_space=pl.ANY),
        scratch_shapes=[
            pltpu.VMEM((2, tm, dw), F32),
            pltpu.VMEM((tm, d), BF16),
            pltpu.VMEM((tm, LANE), F32),
            pltpu.VMEM((2, tm, d), F32),
            pltpu.SemaphoreType.DMA((2,)),
            pltpu.SemaphoreType.DMA((2,)),
        ],
    )
    return pl.pallas_call(
        functools.partial(_moe_experts_kernel, tm=tm, n_tok=t),
        grid_spec=grid_spec,
        out_shape=jax.ShapeDtypeStruct((t + 2 * tm, d), F32),
        compiler_params=_cparams("arbitrary", "arbitrary"),
        name="moe_experts",
    )(dst, tile_group, tile_nvalid, hn_ext, w_gate, w_up, w_down)


def _residual_kernel(x_ref, y_ref, g_ref, o_ref):
    o_ref[0] = x_ref[0] + g_ref[0] * y_ref[...]


def _residual(x, y, gate):
    nb, r, d = x.shape
    tm = _tile(r, 512)
    per_b = r // tm
    blk = pl.BlockSpec((1, tm, d), lambda b, t: (b, t, 0))
    return pl.pallas_call(
        _residual_kernel,
        grid=(nb, per_b),
        in_specs=[blk, pl.BlockSpec((tm, d), lambda b, t: (b * per_b + t, 0)),
                  pl.BlockSpec((1, 1, d), lambda b, t: (b, 0, 0))],
        out_specs=blk,
        out_shape=jax.ShapeDtypeStruct((nb, r, d), F32),
        compiler_params=_cparams("parallel", "parallel"),
        name="moe_residual",
    )(x, y, gate)


def _group_layout(grp, tm):
    t = grp.shape[0]
    ntiles = t // tm + N_GROUPS
    nslots = ntiles * tm
    oh = (grp[None, :] == jnp.arange(N_GROUPS)[:, None]).astype(F32).reshape(N_GROUPS, t // LANE, LANE)
    tri = (jnp.arange(LANE)[:, None] <= jnp.arange(LANE)[None, :]).astype(F32)
    within = jnp.einsum('grk,kl->grl', oh, tri)
    row_tot = within[..., -1]
    row_off = jnp.cumsum(row_tot, axis=1) - row_tot
    rank = (jnp.sum((within + row_off[..., None]) * oh, axis=0).reshape(t) - 1.0).astype(jnp.int32)
    counts = jnp.sum(row_tot, axis=1).astype(jnp.int32)
    padded = ((counts + tm - 1) // tm) * tm
    start = jnp.cumsum(padded) - padded
    slot_of_token = start[grp] + rank
    slot_ids = jnp.arange(nslots, dtype=jnp.int32)
    spare = t + ((slot_ids // tm) % 2) * tm + slot_ids % tm
    dst = spare.at[slot_of_token].set(jnp.arange(t, dtype=jnp.int32))
    tile_start = jnp.arange(ntiles, dtype=jnp.int32) * tm
    ends = jnp.cumsum(padded)
    tile_group = jnp.minimum(jnp.sum((tile_start[:, None] >= ends[None, :]).astype(jnp.int32), axis=1),
                             N_GROUPS - 1).astype(jnp.int32)
    tile_nvalid = jnp.clip(counts[tile_group] - (tile_start - start[tile_group]), 0, tm).astype(jnp.int32)
    return dst, tile_group, tile_nvalid


def _moe(x, g, sh, sc, gate, w_router_t, b_router, w_gate, w_up, w_down):
    nb, r, d = x.shape
    t = nb * r
    tm = 512 if t >= 8192 else (256 if t >= 2048 else 128)
    hn_ext, info = _moe_route(x, g, sh, sc, w_router_t, b_router)
    grp = info[:, EXPERTS_PER_GROUP, :].reshape(t).astype(jnp.int32)
    dst, tile_group, tile_nvalid = _group_layout(grp, tm)
    y = _moe_experts(hn_ext.reshape(t, d + LANE), dst, tile_group, tile_nvalid, w_gate, w_up, w_down, tm)
    return _residual(x, y, gate)


def kernel(x, c, ctx, c_ctx, w_ada, b_ada, norm1_g, norm2_g, w_in, qn_a, kn_a, sink_a, qn_b, kn_b,
           lam_b, subln_b, conv_w, conv_b, w_rg, b_rg, w_ig, b_ig, lru_lambda, lb_d, onorm_d,
           w_branch, w_out, w_router, b_router, w_gate, w_up, w_down):
    bsz, seq, d = x.shape
    lc = ctx.shape[1]
    depth = w_in.shape[0]

    n_rows = -(-(bsz + 1) // SUBLANE) * SUBLANE
    c_pad = jnp.zeros((n_rows, d), F32).at[:bsz].set(c).at[bsz].set(c_ctx)
    mod = _ada(c_pad, w_ada, b_ada).reshape(depth, n_rows, 6, d)

    lb_w = jax.nn.softmax(lb_d.astype(F32), axis=0)
    lb_all = jnp.cumsum(lb_w, axis=0) - lb_w[0:1]
    rope_tabs = _rope_tables(seq)
    w_router_t = w_router.T

    xl = x
    xc = ctx.reshape(1, bsz * lc, d)
    for l in range(depth):
        ctx_out = l < depth - 1
        mod_l = [mod[l, :bsz, k][:, None, :] for k in range(6)]
        mod_c = [mod[l, bsz:bsz + 1, k][:, None, :] for k in range(6)]
        w_in_l = w_in[l].astype(BF16)
        g1 = norm1_g[l].reshape(1, d)
        g2 = norm2_g[l].reshape(1, d)

        lo_l, hi_l = _norm_proj(xl, g1, mod_l[0], mod_l[1], w_in_l)
        lo_c, hi_c = _norm_proj(xc, g1, mod_c[0], mod_c[1], w_in_l)
        lo_c = lo_c.reshape(bsz, lc, -1)
        hi_c = hi_c.reshape(bsz, lc, -1)

        qa = qn_a[l] * (HEAD_DIM ** -0.5 * LOG2E)
        qb = jnp.tile(qn_b[l], 2) * (B_DIM ** -0.5 * LOG2E)
        kb = jnp.tile(kn_b[l], 2)
        one = jnp.ones((LANE,), F32)
        gains = jnp.stack([qa] * 8 + [kn_a[l]] * 2 + [one] * 2 + [qb] * 8 + [kb] * 8 + [one] * 8)
        gains = gains.reshape(N_QKV_BLOCKS, 1, LANE)
        p_l = _prep(lo_l, gains, rope_tabs)
        p_c = _prep(lo_c, gains, None)

        oa_c, oa_l = _attn_a(sink_a[l], p_l, p_c, ctx_out)

        lq1, lk1, lq2, lk2 = lam_b[l].astype(F32)
        lam_init = 0.8 - 0.6 * math.exp(-0.3 * l)
        lam = (jnp.exp(jnp.sum(lq1 * lk1)) - jnp.exp(jnp.sum(lq2 * lk2)) + lam_init).reshape(1)
        ob_c, ob_l = _attn_b(lam, p_l, p_c, subln_b[l].reshape(1, LANE), 1.0 - lam_init, ctx_out)

        oc_c, oc_l = _lru(hi_l, hi_c, conv_w[l], conv_b[l], w_rg[l].astype(BF16), b_rg[l],
                          w_ig[l].astype(BF16), b_ig[l], lru_lambda[l], ctx_out)
        od_c, od_l = _gla(hi_l, hi_c, lb_all[l], onorm_d[l].reshape(1, LANE), ctx_out)

        w_branch_l = w_branch[l].astype(BF16)
        w_out_l = w_out[l].astype(BF16)
        moe_w = (w_router_t, b_router, w_gate[l].astype(BF16), w_up[l].astype(BF16),
                 w_down[l].astype(BF16))

        y_l = _merge((oa_l, ob_l, oc_l, od_l), lo_l, w_branch_l)
        xl = _out_proj(y_l, w_out_l, xl, mod_l[2])
        xl = _moe(xl, g2, mod_l[3], mod_l[4], mod_l[5], *moe_w)
        if ctx_out:
            flat = lambda t: t.reshape(1, bsz * lc, -1)
            y_c = _merge(tuple(flat(t) for t in (oa_c, ob_c, oc_c, od_c)), flat(lo_c), w_branch_l)
            xc = _out_proj(y_c, w_out_l, xc, mod_c[2])
            xc = _moe(xc, g2, mod_c[3], mod_c[4], mod_c[5], *moe_w)
    return xl
```

```python
import functools
import math

import jax
import jax.numpy as jnp
from jax import lax
from jax.experimental import pallas as pl
from jax.experimental.pallas import tpu as pltpu

F32 = jnp.float32
BF16 = jnp.bfloat16

LANE = 128
SUBLANE = 8
VMEM_LIMIT_BYTES = 56 * 1024 * 1024

EPS = 1e-6
ROPE_THETA = 10000.0
GRID_W = 64
HEAD_DIM = 128
WINDOW = 128
Q_BLOCK = 128
N_HEADS = 8
A_KV_HEADS = 2
A_GROUP = N_HEADS // A_KV_HEADS
B_DIM = 64
B_KEY_CHUNK = 1024
LRU_C = 8.0
GLA_CHUNK = 64
GLA_SAFE_DECAY = 80.0
N_EXPERTS = 16
N_GROUPS = 4
EXPERTS_PER_GROUP = N_EXPERTS // N_GROUPS
NEG_BIG = -1e30
LOG2E = math.log2(math.e)
TINY = 1e-37

COL_AQ, COL_AK, COL_AV = 0, 8, 10
COL_BQ, COL_BK, COL_BV = 12, 20, 28
N_QKV_BLOCKS = 36
COL_GATE = N_QKV_BLOCKS
COL_CX, COL_CY = 0, 8
COL_DQ, COL_DFF, COL_DFB, COL_DI, COL_DG = 16, 24, 32, 40, 48
W_IN_QKV = (0, 4608)
W_IN_REC = (4608, 11776)
W_IN_GATE = (11776, 19968)
PROJ_TILE = 512

NT_DIMS = (((1,), (1,)), ((), ()))
TN_DIMS = (((0,), (0,)), ((), ()))


def _cparams(*sem):
    return pltpu.CompilerParams(dimension_semantics=sem, vmem_limit_bytes=VMEM_LIMIT_BYTES)


def _tile(n, pref):
    t = min(n, pref)
    while n % t:
        t //= 2
    return t


def _sigmoid(x):
    return jax.nn.sigmoid(x)


def _modulated_norm(x, g, sh, sc):
    ms = jnp.mean(x * x, axis=-1, keepdims=True)
    return (x * lax.rsqrt(ms + EPS) * g) * (1.0 + sc) + sh


def _ada_kernel(c_ref, w_ref, b_ref, o_ref):
    c = c_ref[...]
    s = (c * _sigmoid(c)).astype(BF16)
    o_ref[0] = jnp.dot(s, w_ref[0].astype(BF16), preferred_element_type=F32) + b_ref[0]


def _ada(c_pad, w_ada, b_ada):
    nl, d, n = w_ada.shape
    rows = c_pad.shape[0]
    tn = _tile(n, 1024)
    return pl.pallas_call(
        _ada_kernel,
        grid=(nl, n // tn),
        in_specs=[
            pl.BlockSpec((rows, d), lambda l, j: (0, 0)),
            pl.BlockSpec((1, d, tn), lambda l, j: (l, 0, j)),
            pl.BlockSpec((1, 1, tn), lambda l, j: (l, 0, j)),
        ],
        out_specs=pl.BlockSpec((1, rows, tn), lambda l, j: (l, 0, j)),
        out_shape=jax.ShapeDtypeStruct((nl, rows, n), F32),
        compiler_params=_cparams("parallel", "parallel"),
        name="ada_mod",
    )(c_pad, w_ada, b_ada.reshape(nl, 1, n))


def _norm_proj_kernel(x_ref, g_ref, sh_ref, sc_ref, w_ref, lo_ref, hi_ref, hn_ref, *, n_lo):
    j = pl.program_id(2)

    @pl.when(j == 0)
    def _():
        hn_ref[...] = _modulated_norm(x_ref[0], g_ref[...], sh_ref[0], sc_ref[0]).astype(BF16)

    @pl.when(j < n_lo)
    def _():
        lo_ref[0] = jnp.dot(hn_ref[...], w_ref[...], preferred_element_type=F32).astype(BF16)

    @pl.when(j >= n_lo)
    def _():
        hi_ref[0] = jnp.dot(hn_ref[...], w_ref[...], preferred_element_type=F32)


def _proj_weight_tile(j):
    n_qkv = (W_IN_QKV[1] - W_IN_QKV[0]) // PROJ_TILE
    n_gate = (W_IN_GATE[1] - W_IN_GATE[0]) // PROJ_TILE
    gate0 = W_IN_GATE[0] // PROJ_TILE
    rec0 = W_IN_REC[0] // PROJ_TILE
    return jnp.where(j < n_qkv, j, jnp.where(j < n_qkv + n_gate, j - n_qkv + gate0, j - n_qkv - n_gate + rec0))


def _norm_proj(x, g, sh, sc, w):
    nb, r, d = x.shape
    tn = PROJ_TILE
    ntiles = w.shape[1] // tn
    n_lo = (W_IN_QKV[1] - W_IN_QKV[0] + W_IN_GATE[1] - W_IN_GATE[0]) // tn
    tm = _tile(r, 1024)
    n_hi = ntiles - n_lo
    return pl.pallas_call(
        functools.partial(_norm_proj_kernel, n_lo=n_lo),
        grid=(nb, r // tm, ntiles),
        in_specs=[
            pl.BlockSpec((1, tm, d), lambda b, t, j: (b, t, 0)),
            pl.BlockSpec((1, d), lambda b, t, j: (0, 0)),
            pl.BlockSpec((1, 1, d), lambda b, t, j: (b, 0, 0)),
            pl.BlockSpec((1, 1, d), lambda b, t, j: (b, 0, 0)),
            pl.BlockSpec((d, tn), lambda b, t, j: (0, _proj_weight_tile(j))),
        ],
        out_specs=[
            pl.BlockSpec((1, tm, tn), lambda b, t, j: (b, t, jnp.minimum(j, n_lo - 1))),
            pl.BlockSpec((1, tm, tn), lambda b, t, j: (b, t, jnp.maximum(j - n_lo, 0))),
        ],
        out_shape=[jax.ShapeDtypeStruct((nb, r, n_lo * tn), BF16),
                   jax.ShapeDtypeStruct((nb, r, n_hi * tn), F32)],
        scratch_shapes=[pltpu.VMEM((tm, d), BF16)],
        compiler_params=_cparams("parallel", "parallel", "arbitrary"),
        name="norm_proj",
    )(x, g, sh, sc, w)


def _group_mean_matrix(group):
    k = lax.broadcasted_iota(jnp.int32, (LANE, LANE), 0)
    l = lax.broadcasted_iota(jnp.int32, (LANE, LANE), 1)
    return jnp.where(k // group == l // group, 1.0 / group, 0.0).astype(BF16)


def _rope_partner_matrix(half):
    k = lax.broadcasted_iota(jnp.int32, (LANE, LANE), 0)
    l = lax.broadcasted_iota(jnp.int32, (LANE, LANE), 1)
    src = jnp.where((l % (2 * half)) < half, l + half, l - half)
    return jnp.where(k == src, 1.0, 0.0).astype(BF16)


def _group_mean(x2, mean_m):
    hi = x2.astype(BF16)
    lo = (x2 - hi.astype(F32)).astype(BF16)
    return (jnp.dot(hi, mean_m, preferred_element_type=F32)
            + jnp.dot(lo, mean_m, preferred_element_type=F32))


def _prep_kernel(*refs, rope):
    if rope:
        x_ref, g_ref, ca_ref, sa_ref, cb_ref, sb_ref, o_ref = refs
    else:
        x_ref, g_ref, o_ref = refs
    tm = x_ref.shape[1]

    def qk_blocks(col0, nblk, group, half, cos_ref, sin_ref):
        blocks = [slice((col0 + j) * LANE, (col0 + j + 1) * LANE) for j in range(nblk)]
        x = jnp.concatenate([x_ref[0, :, c].astype(F32) for c in blocks], axis=0)
        ms = _group_mean(x * x, _group_mean_matrix(group))
        y = (x * lax.rsqrt(ms + EPS)).reshape(nblk, tm, LANE) * g_ref[col0:col0 + nblk]
        if rope:
            part = jnp.dot(y.reshape(nblk * tm, LANE).astype(BF16), _rope_partner_matrix(half),
                           preferred_element_type=F32).reshape(nblk, tm, LANE)
            y = y * cos_ref[...][None] + part * sin_ref[...][None]
        for j, c in enumerate(blocks):
            o_ref[0, :, c] = y[j].astype(BF16)

    qk_blocks(COL_AQ, COL_AV - COL_AQ, HEAD_DIM, 32, ca_ref if rope else None, sa_ref if rope else None)
    qk_blocks(COL_BQ, COL_BV - COL_BQ, B_DIM, 16, cb_ref if rope else None, sb_ref if rope else None)
    for c0, c1 in ((COL_AV, COL_BQ), (COL_BV, N_QKV_BLOCKS)):
        o_ref[0, :, c0 * LANE:c1 * LANE] = x_ref[0, :, c0 * LANE:c1 * LANE]


def _prep(proj, gains, rope_tabs):
    nb, r, _ = proj.shape
    tm = _tile(r, 256)
    width = N_QKV_BLOCKS * LANE
    rope = rope_tabs is not None
    in_specs = [
        pl.BlockSpec((1, tm, width), lambda b, t: (b, t, 0)),
        pl.BlockSpec((N_QKV_BLOCKS, 1, LANE), lambda b, t: (0, 0, 0)),
    ]
    args = [proj, gains]
    if rope:
        in_specs += [pl.BlockSpec((tm, LANE), lambda b, t: (t, 0))] * 4
        args += list(rope_tabs)
    return pl.pallas_call(
        functools.partial(_prep_kernel, rope=rope),
        grid=(nb, r // tm),
        in_specs=in_specs,
        out_specs=pl.BlockSpec((1, tm, width), lambda b, t: (b, t, 0)),
        out_shape=jax.ShapeDtypeStruct((nb, r, width), BF16),
        compiler_params=_cparams("parallel", "parallel"),
        name="qkv_prep",
    )(*args)


def _rope_tables(seq):
    pos = jnp.arange(seq)
    rows = (pos // GRID_W).astype(F32)[:, None]
    cols = (pos % GRID_W).astype(F32)[:, None]
    lane = jnp.arange(LANE)

    def tables(half):
        inv = ROPE_THETA ** (-(lane % half).astype(F32) / half)
        use_rows = (lane % (4 * half)) < 2 * half
        ang = jnp.where(use_rows[None, :], rows, cols) * inv[None, :]
        sign = jnp.where((lane % (2 * half)) < half, -1.0, 1.0)
        return jnp.cos(ang), jnp.sin(ang) * sign[None, :]

    ca, sa = tables(32)
    cb, sb = tables(16)
    return ca, sa, cb, sb


def _stack_heads(q):
    return jnp.concatenate([q[:, g * HEAD_DIM:(g + 1) * HEAD_DIM] for g in range(A_GROUP)], axis=0)


def _unstack_heads(o, rows):
    return jnp.concatenate([o[g * rows:(g + 1) * rows] for g in range(A_GROUP)], axis=1)


def _sink_column(sink_ref, kvh, rows):
    return jnp.concatenate(
        [jnp.full((rows, 1), sink_ref[kvh * A_GROUP + g] * LOG2E, F32) for g in range(A_GROUP)], axis=0)


def _attn_a_lat_kernel(sink_ref, q_ref, bias_ref, kl_ref, vl_ref, kc_ref, vc_ref, o_ref, *, seq):
    i = pl.program_id(1)
    band = 3 * Q_BLOCK
    start = pl.multiple_of(jnp.clip((i - 1) * Q_BLOCK, 0, seq - band), Q_BLOCK)
    bias = bias_ref[0]
    outs = []
    for kvh in range(A_KV_HEADS):
        hs = slice(kvh * HEAD_DIM, (kvh + 1) * HEAD_DIM)
        qs = _stack_heads(q_ref[0, :, kvh * A_GROUP * HEAD_DIM:(kvh + 1) * A_GROUP * HEAD_DIM])
        kb = kl_ref[0, pl.ds(start, band), hs]
        vb = vl_ref[0, pl.ds(start, band), hs]
        s_loc = lax.dot_general(qs, kb, NT_DIMS, preferred_element_type=F32)
        s_loc = (s_loc.reshape(A_GROUP, Q_BLOCK, band) + bias[None]).reshape(A_GROUP * Q_BLOCK, band)
        s_ctx = lax.dot_general(qs, kc_ref[0, :, hs], NT_DIMS, preferred_element_type=F32)
        sk = _sink_column(sink_ref, kvh, Q_BLOCK)
        m = jnp.maximum(jnp.maximum(jnp.max(s_loc, axis=-1, keepdims=True),
                                    jnp.max(s_ctx, axis=-1, keepdims=True)), sk)
        p_loc = jnp.exp2(s_loc - m)
        p_ctx = jnp.exp2(s_ctx - m)
        den = (jnp.sum(p_loc, axis=-1, keepdims=True) + jnp.sum(p_ctx, axis=-1, keepdims=True)
               + jnp.exp2(sk - m))
        o = (jnp.dot(p_loc.astype(BF16), vb, preferred_element_type=F32)
             + jnp.dot(p_ctx.astype(BF16), vc_ref[0, :, hs], preferred_element_type=F32)) / den
        outs.append(_unstack_heads(o, Q_BLOCK))
    o_ref[0] = jnp.concatenate(outs, axis=1).astype(BF16)


def _attn_a_ctx_kernel(sink_ref, q_ref, kc_ref, vc_ref, o_ref):
    kvh = pl.program_id(1)
    rows = q_ref.shape[1]
    qs = _stack_heads(q_ref[0])
    s = lax.dot_general(qs, kc_ref[0], NT_DIMS, preferred_element_type=F32)
    sk = _sink_column(sink_ref, kvh, rows)
    m = jnp.maximum(jnp.max(s, axis=-1, keepdims=True), sk)
    p = jnp.exp2(s - m)
    den = jnp.sum(p, axis=-1, keepdims=True) + jnp.exp2(sk - m)
    o = jnp.dot(p.astype(BF16), vc_ref[0], preferred_element_type=F32) / den
    o_ref[0] = _unstack_heads(o, rows).astype(BF16)


def _window_bias():
    band = 3 * Q_BLOCK
    row = jnp.arange(Q_BLOCK)[None, :, None]
    col = jnp.arange(band)[None, None, :]
    delta = (jnp.arange(3) * Q_BLOCK)[:, None, None]
    return jnp.where(jnp.abs(col - delta - row) <= WINDOW, 0.0, NEG_BIG).astype(F32)


def _attn_a(sink, p_l, p_c, ctx_out):
    b, seq, _ = p_l.shape
    lc = p_c.shape[1]
    gw = A_GROUP * HEAD_DIM
    nblk = seq // Q_BLOCK
    kvw = A_KV_HEADS * HEAD_DIM
    smem = pl.BlockSpec(memory_space=pltpu.SMEM)

    def bias_map(n, i):
        return (i - jnp.clip(i - 1, 0, nblk - 3), 0, 0)

    o_l = pl.pallas_call(
        functools.partial(_attn_a_lat_kernel, seq=seq),
        grid=(b, nblk),
        in_specs=[
            smem,
            pl.BlockSpec((1, Q_BLOCK, N_HEADS * HEAD_DIM), lambda n, i: (n, i, 0)),
            pl.BlockSpec((1, Q_BLOCK, 3 * Q_BLOCK), bias_map),
            pl.BlockSpec((1, seq, kvw), lambda n, i: (n, 0, COL_AK // A_KV_HEADS)),
            pl.BlockSpec((1, seq, kvw), lambda n, i: (n, 0, COL_AV // A_KV_HEADS)),
            pl.BlockSpec((1, lc, kvw), lambda n, i: (n, 0, COL_AK // A_KV_HEADS)),
            pl.BlockSpec((1, lc, kvw), lambda n, i: (n, 0, COL_AV // A_KV_HEADS)),
        ],
        out_specs=pl.BlockSpec((1, Q_BLOCK, N_HEADS * HEAD_DIM), lambda n, i: (n, i, 0)),
        out_shape=jax.ShapeDtypeStruct((b, seq, N_HEADS * HEAD_DIM), BF16),
        compiler_params=_cparams("parallel", "arbitrary"),
        name="attn_a_latent",
    )(sink, p_l, _window_bias(), p_l, p_l, p_c, p_c)
    o_c = None
    if ctx_out:
        o_c = pl.pallas_call(
            _attn_a_ctx_kernel,
            grid=(b, A_KV_HEADS),
            in_specs=[
                smem,
                pl.BlockSpec((1, lc, gw), lambda n, h: (n, 0, h)),
                pl.BlockSpec((1, lc, LANE), lambda n, h: (n, 0, COL_AK + h)),
                pl.BlockSpec((1, lc, LANE), lambda n, h: (n, 0, COL_AV + h)),
            ],
            out_specs=pl.BlockSpec((1, lc, gw), lambda n, h: (n, 0, h)),
            out_shape=jax.ShapeDtypeStruct((b, lc, N_HEADS * HEAD_DIM), BF16),
            compiler_params=_cparams("parallel", "parallel"),
            name="attn_a_ctx",
        )(sink, p_c, p_c, p_c)
    return o_c, o_l


def _attn_b_kernel(lam_ref, q_ref, *refs, nseg, post_scale):
    k_refs = refs[:nseg]
    v_refs = refs[nseg:2 * nseg]
    g_ref, o_ref = refs[2 * nseg], refs[2 * nseg + 1]
    q = q_ref[0]
    tq = q.shape[0]
    lane = lax.broadcasted_iota(jnp.int32, q.shape, 1)
    zero = jnp.zeros_like(q)
    q2 = jnp.concatenate([jnp.where(lane < B_DIM, q, zero), jnp.where(lane < B_DIM, zero, q)], axis=0)
    m = jnp.full((2 * tq, 1), NEG_BIG, F32)
    den = jnp.zeros((2 * tq, 1), F32)
    acc = jnp.zeros((2 * tq, LANE), F32)
    for k_ref, v_ref in zip(k_refs, v_refs):
        nk = k_ref.shape[1]
        kc = _tile(nk, B_KEY_CHUNK)
        for c0 in range(0, nk, kc):
            s = lax.dot_general(q2, k_ref[0, c0:c0 + kc, :], NT_DIMS, preferred_element_type=F32)
            m_new = jnp.maximum(m, jnp.max(s, axis=-1, keepdims=True))
            alpha = jnp.exp2(m - m_new)
            p = jnp.exp2(s - m_new)
            den = alpha * den + jnp.sum(p, axis=-1, keepdims=True)
            acc = alpha * acc + jnp.dot(p.astype(BF16), v_ref[0, c0:c0 + kc, :],
                                        preferred_element_type=F32)
            m = m_new
    o2 = acc / den
    o = o2[:tq] - lam_ref[0] * o2[tq:]
    ms = jnp.mean(o * o, axis=-1, keepdims=True)
    o_ref[0] = ((o * lax.rsqrt(ms + EPS) * g_ref[...]) * post_scale).astype(BF16)


def _attn_b_call(lam, q_src, k_srcs, subln, post_scale, name):
    b, rq, _ = q_src.shape
    tq = _tile(rq, 512)
    smem = pl.BlockSpec(memory_space=pltpu.SMEM)
    in_specs = [smem, pl.BlockSpec((1, tq, LANE), lambda n, h, i: (n, i, COL_BQ + h))]
    in_specs += [pl.BlockSpec((1, s.shape[1], LANE), lambda n, h, i: (n, 0, COL_BK + h)) for s in k_srcs]
    in_specs += [pl.BlockSpec((1, s.shape[1], LANE), lambda n, h, i: (n, 0, COL_BV + h)) for s in k_srcs]
    in_specs += [pl.BlockSpec((1, LANE), lambda n, h, i: (0, 0))]
    return pl.pallas_call(
        functools.partial(_attn_b_kernel, nseg=len(k_srcs), post_scale=post_scale),
        grid=(b, N_HEADS, rq // tq),
        in_specs=in_specs,
        out_specs=pl.BlockSpec((1, tq, LANE), lambda n, h, i: (n, i, h)),
        out_shape=jax.ShapeDtypeStruct((b, rq, N_HEADS * HEAD_DIM), BF16),
        compiler_params=_cparams("parallel", "parallel", "arbitrary"),
        name=name,
    )(lam, q_src, *k_srcs, *k_srcs, subln)


def _attn_b(lam, p_l, p_c, subln, post_scale, ctx_out):
    o_l = _attn_b_call(lam, p_l, [p_c, p_l], subln, post_scale, "attn_b_latent")
    o_c = _attn_b_call(lam, p_c, [p_c], subln, post_scale, "attn_b_ctx") if ctx_out else None
    return o_c, o_l


def _centred_conv(x_ref, pad_ref, w_ref, b_ref):
    n = x_ref.shape[1]
    zeros = jnp.zeros((SUBLANE, LANE), F32)
    pad_ref[0:SUBLANE, :] = zeros
    pad_ref[SUBLANE:SUBLANE + n, :] = x_ref[0]
    pad_ref[SUBLANE + n:2 * SUBLANE + n, :] = zeros
    out = b_ref[...]
    for tap in range(4):
        out = out + pad_ref[pl.ds(SUBLANE - 2 + tap, n), :] * w_ref[tap:tap + 1, :]
    return out


def _every_8th(ref, d, j, n):
    return ref[d, pl.ds(j, n, stride=SUBLANE), :]


def _tile_order(reverse):
    return range(SUBLANE - 1, -1, -1) if reverse else range(SUBLANE)


def _scan_summaries(a_ref, v_ref, d, n, reverse):
    h = p = None
    for j in _tile_order(reverse):
        a, v = _every_8th(a_ref, d, j, n), _every_8th(v_ref, d, j, n)
        h, p = (v, a) if h is None else (a * h + v, a * p)
    return p, h


def _scan_entering(a_ref, v_ref, d, n, carry_in, reverse, store):
    e = carry_in
    for j in _tile_order(reverse):
        store(j, e)
        e = _every_8th(a_ref, d, j, n) * e + _every_8th(v_ref, d, j, n)


def _scan_apply(a_ref, v_ref, d, n, carry_in, reverse, store):
    h = carry_in
    for j in _tile_order(reverse):
        h = _every_8th(a_ref, d, j, n) * h + _every_8th(v_ref, d, j, n)
        store(j, h)


def _seg_tile(t, ntc, ntl, reverse):
    if not reverse:
        return t
    return jnp.where(t < ntc, ntc - 1 - t, 2 * ntc + ntl - 1 - t)


def _lru_kernel(xc_ref, yc_ref, xl_ref, yl_ref, cw_ref, cb_ref, wr_ref, br_ref, wi_ref, bi_ref,
                lam_ref, *refs, ctx_out):
    if ctx_out:
        oc_ref, ol_ref, ac_ref, hl_ref, hs_ref, pad_ref, tp_ref, th_ref, sp_ref, sh_ref, c3_ref, e1_ref = refs
    else:
        ol_ref, ac_ref, hl_ref, hs_ref, pad_ref, tp_ref, th_ref, sp_ref, sh_ref, c3_ref, e1_ref = refs
    lc, seq = xc_ref.shape[1], xl_ref.shape[1]
    for x_ref, off in ((xc_ref, 0), (xl_ref, lc)):
        n = x_ref.shape[1]
        u = _centred_conv(x_ref, pad_ref, cw_ref, cb_ref)
        ub = u.astype(BF16)
        for d in range(2):
            lam = lam_ref[d:d + 1, :]
            sp = jnp.maximum(-lam, 0.0) + jnp.log1p(jnp.exp(-jnp.abs(lam)))
            r = _sigmoid(jnp.dot(ub, wr_ref[d, 0], preferred_element_type=F32) + br_ref[d:d + 1, :])
            gi = _sigmoid(jnp.dot(ub, wi_ref[d, 0], preferred_element_type=F32) + bi_ref[d:d + 1, :])
            log_a = -LRU_C * r * sp
            a = jnp.exp(log_a)
            y = jnp.tanh(-log_a) * (1.0 + a * a)
            v = (y * lax.rsqrt(jnp.maximum(y, TINY))) * gi * u
            ac_ref[d, off:off + n, :] = a
            hl_ref[d, off:off + n, :] = v

    nt1 = (lc + seq) // SUBLANE
    nt2 = nt1 // SUBLANE
    group_rows = SUBLANE * SUBLANE
    for d in range(2):
        tp_ref[d], th_ref[d] = _scan_summaries(ac_ref, hl_ref, d, nt1, d == 1)
        sp_ref[d], sh_ref[d] = _scan_summaries(tp_ref, th_ref, d, nt2, d == 1)

    def step(s, carry):
        new = []
        for d in range(2):
            g = pl.ds(_seg_tile(s, lc // group_rows, seq // group_rows, d == 1), 1)
            c3_ref[d, g, :] = carry[d]
            new.append(sp_ref[d, g, :] * carry[d] + sh_ref[d, g, :])
        return tuple(new)

    zero = jnp.zeros((1, LANE), F32)
    lax.fori_loop(0, nt2, step, (zero, zero))

    for d in range(2):
        def store_entering(j, e, d=d):
            e1_ref[d, pl.ds(j, nt2, stride=SUBLANE), :] = e

        def store_h(j, h, d=d):
            if d == 0:
                hs_ref[pl.ds(j, nt1, stride=SUBLANE), :] = h
            else:
                hl_ref[d, pl.ds(j, nt1, stride=SUBLANE), :] = h

        _scan_entering(tp_ref, th_ref, d, nt2, c3_ref[d], d == 1, store_entering)
        _scan_apply(ac_ref, hl_ref, d, nt1, e1_ref[d], d == 1, store_h)
    if ctx_out:
        hc = hs_ref[0:lc, :] + hl_ref[1, 0:lc, :]
        oc_ref[0] = (hc * jax.nn.gelu(yc_ref[0])).astype(BF16)
    hl = hs_ref[lc:lc + seq, :] + hl_ref[1, lc:lc + seq, :]
    ol_ref[0] = (hl * jax.nn.gelu(yl_ref[0])).astype(BF16)


def _lru(proj_l, proj_c, conv_w, conv_b, w_r, b_r, w_i, b_i, lam, ctx_out):
    b, seq, _ = proj_l.shape
    lc = proj_c.shape[1]
    nblk = w_r.shape[1]
    width = nblk * LANE
    nt1 = (lc + seq) // SUBLANE
    nt2 = nt1 // SUBLANE

    def seg(n, col):
        return pl.BlockSpec((1, n, LANE), lambda i, j: (i, 0, col + j))

    vec2 = pl.BlockSpec((2, LANE), lambda i, j: (0, j))
    wspec = pl.BlockSpec((2, 1, LANE, LANE), lambda i, j: (0, j, 0, 0))
    out_specs = [pl.BlockSpec((1, seq, LANE), lambda i, j: (i, 0, j))]
    out_shape = [jax.ShapeDtypeStruct((b, seq, width), BF16)]
    if ctx_out:
        out_specs.insert(0, pl.BlockSpec((1, lc, LANE), lambda i, j: (i, 0, j)))
        out_shape.insert(0, jax.ShapeDtypeStruct((b, lc, width), BF16))
    outs = pl.pallas_call(
        functools.partial(_lru_kernel, ctx_out=ctx_out),
        grid=(b, nblk),
        in_specs=[
            seg(lc, COL_CX), seg(lc, COL_CY), seg(seq, COL_CX), seg(seq, COL_CY),
            pl.BlockSpec((4, LANE), lambda i, j: (0, j)),
            pl.BlockSpec((1, LANE), lambda i, j: (0, j)),
            wspec, vec2, wspec, vec2, vec2,
        ],
        out_specs=out_specs,
        out_shape=out_shape,
        scratch_shapes=[
            pltpu.VMEM((2, lc + seq, LANE), F32),
            pltpu.VMEM((2, lc + seq, LANE), F32),
            pltpu.VMEM((lc + seq, LANE), F32),
            pltpu.VMEM((max(lc, seq) + 2 * SUBLANE, LANE), F32),
            pltpu.VMEM((2, nt1, LANE), F32),
            pltpu.VMEM((2, nt1, LANE), F32),
            pltpu.VMEM((2, nt2, LANE), F32),
            pltpu.VMEM((2, nt2, LANE), F32),
            pltpu.VMEM((2, nt2, LANE), F32),
            pltpu.VMEM((2, nt1, LANE), F32),
        ],
        compiler_params=_cparams("parallel", "parallel"),
        name="rglru",
    )(proj_c, proj_c, proj_l, proj_l, conv_w, conv_b.reshape(1, width), w_r, b_r, w_i, b_i, lam)
    return (outs[0], outs[1]) if ctx_out else (None, outs[0])


def _gla_level_masks(reverse):
    c = GLA_CHUNK
    row = lax.broadcasted_iota(jnp.int32, (c, c), 0)
    col = lax.broadcasted_iota(jnp.int32, (c, c), 1)
    masks = {}
    for s in (32, 16, 8):
        same = (row // (2 * s)) == (col // (2 * s))
        if reverse:
            masks[s] = same & ((row % (2 * s)) < s) & ((col % (2 * s)) >= s)
        else:
            masks[s] = same & ((row % (2 * s)) >= s) & ((col % (2 * s)) < s)
    tri = (row <= col) if reverse else (row >= col)
    return masks, tri.astype(F32)


def _gla_chunk(q, k, v, g, st, masks, tri, reverse):
    c = GLA_CHUNK
    b = jnp.dot(tri, g, precision=lax.Precision.HIGHEST, preferred_element_type=F32)
    att = jnp.zeros((c, c), F32)
    for s in (32, 16, 8):
        b3 = b.reshape(c // (2 * s), 2 * s, LANE)
        rr = s if reverse else s - 1
        rho = jnp.broadcast_to(b3[:, rr:rr + 1, :], b3.shape).reshape(c, LANE)
        e = jnp.exp(-jnp.abs(b - rho))
        a = lax.dot_general((q * e).astype(BF16), (k * e).astype(BF16), NT_DIMS,
                            preferred_element_type=F32)
        att = att + jnp.where(masks[s], a, 0.0)
    vb = v.astype(BF16)
    o = jnp.dot(att.astype(BF16), vb, preferred_element_type=F32)
    nb = c // SUBLANE
    b3 = b.reshape(nb, SUBLANE, LANE)
    q3 = q.reshape(nb, SUBLANE, LANE)
    k3 = k.reshape(nb, SUBLANE, LANE)
    v3 = v.reshape(nb, SUBLANE, LANE)
    rowi = lax.broadcasted_iota(jnp.int32, b3.shape, 1)
    od = jnp.zeros(b3.shape, F32)
    for jj in range(SUBLANE):
        keep = (rowi <= jj) if reverse else (rowi >= jj)
        e = jnp.where(keep, jnp.exp(jnp.minimum(b3 - b3[:, jj:jj + 1, :], 0.0)), 0.0)
        sj = jnp.sum(q3 * e * k3[:, jj:jj + 1, :], axis=-1, keepdims=True)
        od = od + sj * v3[:, jj:jj + 1, :]
    o = o + od.reshape(c, LANE)
    o = o + lax.dot_general((q * jnp.exp(b)).astype(BF16), st.astype(BF16), NT_DIMS,
                            preferred_element_type=F32)
    b_end = b[0:1, :] if reverse else b[c - 1:c, :]
    khat = (k * jnp.exp(b_end - b)).astype(BF16)
    st_new = st * jnp.exp(b_end) + lax.dot_general(vb, khat, TN_DIMS, preferred_element_type=F32)
    return o, st_new


def _gla_fast_intra(q, k, v, g, tri, reverse):
    c = GLA_CHUNK
    g_hi = g.astype(BF16)
    g_lo = (g - g_hi.astype(F32)).astype(BF16)
    b2 = jnp.dot(tri.astype(BF16), jnp.concatenate([g_hi, g_lo], axis=1), preferred_element_type=F32)
    b = b2[:, :LANE] + b2[:, LANE:]
    rr = c // 2 if reverse else c // 2 - 1
    rho = b[rr:rr + 1, :]
    qt = q * jnp.exp(b - rho)
    kt = k * jnp.exp(rho - b)
    a = lax.dot_general(qt.astype(BF16), kt.astype(BF16), NT_DIMS, preferred_element_type=F32)
    att = jnp.where(tri > 0.0, a, 0.0)
    vb = v.astype(BF16)
    o_intra = jnp.dot(att.astype(BF16), vb, preferred_element_type=F32)
    b_end = b[0:1, :] if reverse else b[c - 1:c, :]
    qe = (qt * jnp.exp(rho)).astype(BF16)
    khat = (kt * jnp.exp(b_end - rho)).astype(BF16)
    return o_intra, qe, khat, vb, jnp.exp(b_end)


def _gla_fast_inter(intra, st):
    o_intra, qe, khat, vb, decay = intra
    o = o_intra + lax.dot_general(qe, st.astype(BF16), NT_DIMS, preferred_element_type=F32)
    st_new = st * decay + lax.dot_general(vb, khat, TN_DIMS, preferred_element_type=F32)
    return o, st_new


def _gla_kernel(lb_ref, on_ref, qc_ref, ffc_ref, fbc_ref, ic_ref, gc_ref,
                ql_ref, ffl_ref, fbl_ref, il_ref, gl_ref, *refs, ctx_out):
    if ctx_out:
        oc_ref, ol_ref, q_s, v_s, g_s, k_s, o_s, st_s = refs
    else:
        ol_ref, q_s, v_s, g_s, k_s, o_s, st_s = refs
    lc, seq = qc_ref.shape[1], ql_ref.shape[1]
    ncc, ncl = lc // GLA_CHUNK, seq // GLA_CHUNK
    for off, n, q_ref, i_ref, f_refs in ((0, lc, qc_ref, ic_ref, (ffc_ref, fbc_ref)),
                                         (lc, seq, ql_ref, il_ref, (ffl_ref, fbl_ref))):
        q_s[off:off + n, :] = q_ref[0]
        v_s[off:off + n, :] = i_ref[0]
        for d in range(2):
            z = f_refs[d][0]
            lbd = lb_ref[d:d + 1, :]
            ez = jnp.exp(-jnp.abs(z))
            r = 1.0 / (1.0 + ez)
            pos = z >= 0.0
            sig_p = jnp.where(pos, r, ez * r)
            sig_n = jnp.where(pos, ez * r, r)
            g_s[d, off:off + n, :] = jnp.log(lbd + (1.0 - lbd) * sig_p)
            k_s[d, off:off + n, :] = (1.0 - lbd) * sig_n
    st_s[...] = jnp.zeros(st_s.shape, F32)
    consts = [_gla_level_masks(False), _gla_level_masks(True)]

    half = GLA_CHUNK // 2
    worst = jnp.zeros((1, LANE), F32)
    for d in range(2):
        hs = jnp.sum(g_s[d].reshape((lc + seq) // half, half, LANE), axis=1)
        worst = jnp.maximum(worst, jnp.max(-hs, axis=0, keepdims=True))
    safe = jnp.max(worst) < GLA_SAFE_DECAY

    def chunk_rows(cidx, d):
        chunk = _seg_tile(cidx, ncc, ncl, d == 1)
        return pl.ds(pl.multiple_of(chunk * GLA_CHUNK, GLA_CHUNK), GLA_CHUNK)

    def robust_step(cidx, carry):
        for d in range(2):
            rows = chunk_rows(cidx, d)
            o, st_new = _gla_chunk(q_s[rows, :], k_s[d, rows, :], v_s[rows, :], g_s[d, rows, :],
                                   st_s[d], consts[d][0], consts[d][1], d == 1)
            st_s[d] = st_new
            o_s[d, rows, :] = o
        return carry

    nchunks = ncc + ncl
    group = next(u for u in (12, 6, 4, 3, 2, 1) if nchunks % u == 0)

    def fast_step(t, carry):
        work = [[] for _ in range(2)]
        for d in range(2):
            for u in range(group):
                rows = chunk_rows(t * group + u, d)
                work[d].append((rows, _gla_fast_intra(q_s[rows, :], k_s[d, rows, :], v_s[rows, :],
                                                      g_s[d, rows, :], consts[d][1], d == 1)))
        for d in range(2):
            st = st_s[d]
            outs = []
            for rows, intra in work[d]:
                o, st = _gla_fast_inter(intra, st)
                outs.append((rows, o))
            st_s[d] = st
            for rows, o in outs:
                o_s[d, rows, :] = o
        return carry

    @pl.when(safe)
    def _():
        lax.fori_loop(0, nchunks // group, fast_step, 0)

    @pl.when(jnp.logical_not(safe))
    def _():
        lax.fori_loop(0, nchunks, robust_step, 0)

    def finish(o, gate):
        ms = jnp.mean(o * o, axis=-1, keepdims=True)
        return ((o * lax.rsqrt(ms + EPS) * on_ref[...]) * (gate * _sigmoid(gate))).astype(BF16)

    if ctx_out:
        oc_ref[0] = finish(o_s[0, 0:lc, :] + o_s[1, 0:lc, :], gc_ref[0])
    ol_ref[0] = finish(o_s[0, lc:lc + seq, :] + o_s[1, lc:lc + seq, :], gl_ref[0])


def _gla(proj_l, proj_c, lb, onorm, ctx_out):
    b, seq, _ = proj_l.shape
    lc = proj_c.shape[1]
    width = N_HEADS * HEAD_DIM
    nt = lc + seq

    def seg(n, col):
        return pl.BlockSpec((1, n, LANE), lambda i, h: (i, 0, col + h))

    cols = (COL_DQ, COL_DFF, COL_DFB, COL_DI, COL_DG)
    out_specs = [pl.BlockSpec((1, seq, LANE), lambda i, h: (i, 0, h))]
    out_shape = [jax.ShapeDtypeStruct((b, seq, width), BF16)]
    if ctx_out:
        out_specs.insert(0, pl.BlockSpec((1, lc, LANE), lambda i, h: (i, 0, h)))
        out_shape.insert(0, jax.ShapeDtypeStruct((b, lc, width), BF16))
    outs = pl.pallas_call(
        functools.partial(_gla_kernel, ctx_out=ctx_out),
        grid=(b, N_HEADS),
        in_specs=[pl.BlockSpec((2, LANE), lambda i, h: (0, h)),
                  pl.BlockSpec((1, LANE), lambda i, h: (0, 0))]
        + [seg(lc, c) for c in cols] + [seg(seq, c) for c in cols],
        out_specs=out_specs,
        out_shape=out_shape,
        scratch_shapes=[
            pltpu.VMEM((nt, LANE), F32),
            pltpu.VMEM((nt, LANE), F32),
            pltpu.VMEM((2, nt, LANE), F32),
            pltpu.VMEM((2, nt, LANE), F32),
            pltpu.VMEM((2, nt, LANE), F32),
            pltpu.VMEM((2, LANE, LANE), F32),
        ],
        compiler_params=_cparams("parallel", "parallel"),
        name="hgrn2",
    )(lb, onorm, *([proj_c] * 5), *([proj_l] * 5))
    return (outs[0], outs[1]) if ctx_out else (None, outs[0])


def _merge_kernel(oa_ref, ob_ref, oc_ref, od_ref, g0_ref, g1_ref, g2_ref, g3_ref,
                  w0_ref, w1_ref, w2_ref, w3_ref, y_ref):
    acc = None
    for o_ref, g_ref, w_ref in ((oa_ref, g0_ref, w0_ref), (ob_ref, g1_ref, w1_ref),
                                (oc_ref, g2_ref, w2_ref), (od_ref, g3_ref, w3_ref)):
        t = _sigmoid(g_ref[0].astype(F32)) * jnp.dot(o_ref[0], w_ref[0], preferred_element_type=F32)
        acc = t if acc is None else acc + t
    y_ref[0] = acc.astype(BF16)


def _merge(outs, proj, w_branch):
    nb, r, mw = outs[0].shape
    d = w_branch.shape[2]
    tm = _tile(r, 512)
    tn = 512
    nj = d // tn
    o_spec = pl.BlockSpec((1, tm, mw), lambda b, t, j: (b, t, 0))
    g_specs = [pl.BlockSpec((1, tm, tn), functools.partial(
        lambda b, t, j, n: (b, t, (COL_GATE * LANE) // tn + n * nj + j), n=n)) for n in range(4)]
    w_specs = [pl.BlockSpec((1, mw, tn), functools.partial(lambda b, t, j, n: (n, 0, j), n=n))
               for n in range(4)]
    return pl.pallas_call(
        _merge_kernel,
        grid=(nb, r // tm, nj),
        in_specs=[o_spec] * 4 + g_specs + w_specs,
        out_specs=pl.BlockSpec((1, tm, tn), lambda b, t, j: (b, t, j)),
        out_shape=jax.ShapeDtypeStruct((nb, r, d), BF16),
        compiler_params=_cparams("parallel", "parallel", "arbitrary"),
        name="branch_merge",
    )(*outs, *([proj] * 4), *([w_branch] * 4))


def _out_proj_kernel(y_ref, w_ref, x_ref, g_ref, o_ref):
    o_ref[0] = x_ref[0] + g_ref[0] * jnp.dot(y_ref[0], w_ref[...], preferred_element_type=F32)


def _out_proj(y, w, x, gate):
    nb, r, d = x.shape
    tm = _tile(r, 1024)
    tn = 512
    return pl.pallas_call(
        _out_proj_kernel,
        grid=(nb, r // tm, d // tn),
        in_specs=[
            pl.BlockSpec((1, tm, d), lambda b, t, j: (b, t, 0)),
            pl.BlockSpec((d, tn), lambda b, t, j: (0, j)),
            pl.BlockSpec((1, tm, tn), lambda b, t, j: (b, t, j)),
            pl.BlockSpec((1, 1, tn), lambda b, t, j: (b, 0, j)),
        ],
        out_specs=pl.BlockSpec((1, tm, tn), lambda b, t, j: (b, t, j)),
        out_shape=jax.ShapeDtypeStruct((nb, r, d), F32),
        compiler_params=_cparams("parallel", "parallel", "arbitrary"),
        name="out_proj",
    )(y, w, x, gate)


def _route(logits_t, bias_col):
    aff = _sigmoid(logits_t)
    sel = aff + bias_col
    aff_r = [aff[e:e + 1, :] for e in range(N_EXPERTS)]
    sel_r = [sel[e:e + 1, :] for e in range(N_EXPERTS)]
    scores = []
    for g in range(N_GROUPS):
        v = sel_r[g * EXPERTS_PER_GROUP:(g + 1) * EXPERTS_PER_GROUP]
        m1 = functools.reduce(jnp.maximum, v)
        taken = jnp.zeros(m1.shape, jnp.bool_)
        second = jnp.full(m1.shape, -jnp.inf, F32)
        for x in v:
            first = (x == m1) & jnp.logical_not(taken)
            taken = taken | first
            second = jnp.where(first, second, jnp.maximum(second, x))
        scores.append(m1 + second)
    best, gidx = scores[0], jnp.zeros(scores[0].shape, jnp.int32)
    for g in range(1, N_GROUPS):
        better = scores[g] > best
        gidx = jnp.where(better, g, gidx)
        best = jnp.where(better, scores[g], best)
    masked = [jnp.where(gidx == e // EXPERTS_PER_GROUP, sel_r[e], -jnp.inf) for e in range(N_EXPERTS)]

    def first_argmax(vals, exclude):
        bv = jnp.full(vals[0].shape, -jnp.inf, F32)
        bi = jnp.full(vals[0].shape, -1, jnp.int32)
        for e, x in enumerate(vals):
            better = x > bv
            if exclude is not None:
                better = better & (exclude != e)
            bi = jnp.where(better, e, bi)
            bv = jnp.where(better, x, bv)
        return bi

    i1 = first_argmax(masked, None)
    i2 = first_argmax(masked, i1)
    w1 = functools.reduce(jnp.add, [jnp.where(i1 == e, aff_r[e], 0.0) for e in range(N_EXPERTS)])
    w2 = functools.reduce(jnp.add, [jnp.where(i2 == e, aff_r[e], 0.0) for e in range(N_EXPERTS)])
    tot = w1 + w2
    g1, g2 = w1 / tot, w2 / tot
    width = logits_t.shape[1]
    rowi = lax.broadcasted_iota(jnp.int32, (SUBLANE, width), 0)
    local1 = jnp.broadcast_to(i1 - gidx * EXPERTS_PER_GROUP, (SUBLANE, width))
    local2 = jnp.broadcast_to(i2 - gidx * EXPERTS_PER_GROUP, (SUBLANE, width))
    info = (jnp.where(rowi == local1, jnp.broadcast_to(g1, (SUBLANE, width)), 0.0)
            + jnp.where(rowi == local2, jnp.broadcast_to(g2, (SUBLANE, width)), 0.0))
    return jnp.where(rowi == EXPERTS_PER_GROUP,
                     jnp.broadcast_to(gidx.astype(F32), (SUBLANE, width)), info)


def _moe_route_kernel(x_ref, g_ref, sh_ref, sc_ref, wrt_ref, br_ref, hn_ref, info_ref):
    d = x_ref.shape[2]
    h = _modulated_norm(x_ref[0], g_ref[...], sh_ref[0], sc_ref[0])
    logits_t = lax.dot_general(wrt_ref[...], h, NT_DIMS, precision=lax.Precision.HIGHEST,
                               preferred_element_type=F32)
    info = _route(logits_t, br_ref[...])
    info_ref[0] = info
    hn_ref[0, :, 0:d] = h
    pad = jnp.zeros((LANE - SUBLANE, info.shape[1]), F32)
    hn_ref[0, :, d:d + LANE] = jnp.concatenate([info, pad], axis=0).T


def _moe_route(x, g, sh, sc, w_router_t, b_router):
    nb, r, d = x.shape
    tm = _tile(r, 512)
    vec = pl.BlockSpec((1, 1, d), lambda b, t: (b, 0, 0))
    return pl.pallas_call(
        _moe_route_kernel,
        grid=(nb, r // tm),
        in_specs=[
            pl.BlockSpec((1, tm, d), lambda b, t: (b, t, 0)),
            pl.BlockSpec((1, d), lambda b, t: (0, 0)),
            vec, vec,
            pl.BlockSpec((N_EXPERTS, d), lambda b, t: (0, 0)),
            pl.BlockSpec((N_EXPERTS, 1), lambda b, t: (0, 0)),
        ],
        out_specs=[pl.BlockSpec((1, tm, d + LANE), lambda b, t: (b, t, 0)),
                   pl.BlockSpec((1, SUBLANE, tm), lambda b, t: (b, 0, t))],
        out_shape=[jax.ShapeDtypeStruct((nb, r, d + LANE), F32),
                   jax.ShapeDtypeStruct((nb, SUBLANE, r), F32)],
        compiler_params=_cparams("parallel", "parallel"),
        name="moe_route",
    )(x, g, sh, sc, w_router_t, b_router.reshape(N_EXPERTS, 1))


def _row_copy(src, src_row, dst, dst_row, sem):
    return pltpu.make_async_copy(src.at[pl.ds(src_row, 1), :], dst.at[pl.ds(dst_row, 1), :], sem)


def _moe_experts_kernel(dst_ref, tgrp_ref, hn_hbm, wg_ref, wu_ref, wd_ref,
                        y_hbm, xbuf, hb, gbuf, acc, gsem, ssem, *, tm, n_tok):
    i = pl.program_id(0)
    e = pl.program_id(1)
    nt = pl.num_programs(0)
    slot = i % 2
    other = 1 - slot
    per = tm // EXPERTS_PER_GROUP
    d = hb.shape[1]
    last_step = (i == nt - 1) & (e == EXPERTS_PER_GROUP - 1)

    def gather_row(slot_idx, buf, row):
        s = dst_ref[slot_idx]
        return _row_copy(hn_hbm, jnp.where(s < n_tok, s, 0), xbuf.at[buf], row, gsem.at[buf])

    def wait_gather(sl):
        pltpu.make_async_copy(hn_hbm.at[pl.ds(0, tm), :], xbuf.at[sl], gsem.at[sl]).wait()

    def wait_scatter(sl):
        pltpu.make_async_copy(acc.at[sl], y_hbm.at[pl.ds(0, tm), :], ssem.at[sl]).wait()

    @pl.when((i == 0) & (e == 0))
    def _():
        def body(r, c):
            gather_row(r, 0, r).start()
            return c
        lax.fori_loop(0, tm, body, 0)
        acc[1] = jnp.zeros(acc.shape[1:], F32)
        for half in range(2):
            fill = pltpu.make_async_copy(acc.at[1], y_hbm.at[pl.ds(n_tok + half * tm, tm), :], ssem.at[1])
            fill.start()
            fill.wait()

    @pl.when(e == 0)
    def _():
        wait_gather(slot)
        hb[...] = xbuf[slot, :, 0:d].astype(BF16)
        gbuf[...] = xbuf[slot, :, d:d + LANE]

        @pl.when(i >= 1)
        def _():
            wait_scatter(slot)
        acc[slot] = jnp.zeros(acc.shape[1:], F32)

    gather_base = jnp.minimum(i + 1, nt - 1) * tm + e * per
    for k in range(per):
        gather_row(gather_base + k, other, e * per + k).start()

    hv = hb[...]
    a = jnp.dot(hv, wg_ref[0], preferred_element_type=F32)
    u = jnp.dot(hv, wu_ref[0], preferred_element_type=F32)
    hid = ((a * _sigmoid(a)) * u).astype(BF16)
    y = jnp.dot(hid, wd_ref[0], preferred_element_type=F32)
    lane = lax.broadcasted_iota(jnp.int32, gbuf.shape, 1)
    col = jnp.sum(jnp.where(lane == e, gbuf[...], 0.0), axis=-1, keepdims=True)
    acc[slot] = acc[slot] + col * y

    scatter_base = jnp.maximum(i - 1, 0) * tm + e * per
    for k in range(per):
        row = e * per + k
        to = jnp.where(i == 0, n_tok + row, dst_ref[scatter_base + k])
        _row_copy(acc.at[other], row, y_hbm, to, ssem.at[other]).start()

    @pl.when(last_step)
    def _():
        wait_scatter(other)

        def body(r, c):
            _row_copy(acc.at[slot], r, y_hbm, dst_ref[i * tm + r], ssem.at[slot]).start()
            return c
        lax.fori_loop(0, tm, body, 0, unroll=8)
        wait_scatter(slot)
        wait_gather(other)


def _moe_experts(hn_ext, dst, tile_group, w_gate, w_up, w_down, tm):
    t, dw = hn_ext.shape
    d = dw - LANE
    dff = w_gate.shape[2]
    ntiles = tile_group.shape[0]

    def wmap(i, e, dst_r, tgrp_r):
        return (tgrp_r[i] * EXPERTS_PER_GROUP + e, 0, 0)

    grid_spec = pltpu.PrefetchScalarGridSpec(
        num_scalar_prefetch=2,
        grid=(ntiles, EXPERTS_PER_GROUP),
        in_specs=[
            pl.BlockSpec(memory_space=pl.ANY),
            pl.BlockSpec((1, d, dff), wmap),
            pl.BlockSpec((1, d, dff), wmap),
            pl.BlockSpec((1, dff, d), wmap),
        ],
        out_specs=pl.BlockSpec(memory_space=pl.ANY),
        scratch_shapes=[
            pltpu.VMEM((2, tm, dw), F32),
            pltpu.VMEM((tm, d), BF16),
            pltpu.VMEM((tm, LANE), F32),
            pltpu.VMEM((2, tm, d), F32),
            pltpu.SemaphoreType.DMA((2,)),
            pltpu.SemaphoreType.DMA((2,)),
        ],
    )
    return pl.pallas_call(
        functools.partial(_moe_experts_kernel, tm=tm, n_tok=t),
        grid_spec=grid_spec,
        out_shape=jax.ShapeDtypeStruct((t + 2 * tm, d), F32),
        compiler_params=_cparams("arbitrary", "arbitrary"),
        name="moe_experts",
    )(dst, tile_group, hn_ext, w_gate, w_up, w_down)


def _residual_kernel(x_ref, y_ref, g_ref, o_ref):
    o_ref[0] = x_ref[0] + g_ref[0] * y_ref[...]


def _residual(x, y, gate):
    nb, r, d = x.shape
    tm = _tile(r, 512)
    per_b = r // tm
    blk = pl.BlockSpec((1, tm, d), lambda b, t: (b, t, 0))
    return pl.pallas_call(
        _residual_kernel,
        grid=(nb, per_b),
        in_specs=[blk, pl.BlockSpec((tm, d), lambda b, t: (b * per_b + t, 0)),
                  pl.BlockSpec((1, 1, d), lambda b, t: (b, 0, 0))],
        out_specs=blk,
        out_shape=jax.ShapeDtypeStruct((nb, r, d), F32),
        compiler_params=_cparams("parallel", "parallel"),
        name="moe_residual",
    )(x, y, gate)


def _group_layout(grp, tm):
    t = grp.shape[0]
    ntiles = t // tm + N_GROUPS
    nslots = ntiles * tm
    oh = (grp[None, :] == jnp.arange(N_GROUPS)[:, None]).astype(F32).reshape(N_GROUPS, t // LANE, LANE)
    tri = (jnp.arange(LANE)[:, None] <= jnp.arange(LANE)[None, :]).astype(F32)
    within = jnp.einsum('grk,kl->grl', oh, tri)
    row_tot = within[..., -1]
    row_off = jnp.cumsum(row_tot, axis=1) - row_tot
    rank = (jnp.sum((within + row_off[..., None]) * oh, axis=0).reshape(t) - 1.0).astype(jnp.int32)
    counts = jnp.sum(row_tot, axis=1).astype(jnp.int32)
    padded = ((counts + tm - 1) // tm) * tm
    start = jnp.cumsum(padded) - padded
    slot_of_token = start[grp] + rank
    slot_ids = jnp.arange(nslots, dtype=jnp.int32)
    spare = t + ((slot_ids // tm) % 2) * tm + slot_ids % tm
    dst = spare.at[slot_of_token].set(jnp.arange(t, dtype=jnp.int32))
    tile_start = jnp.arange(ntiles, dtype=jnp.int32) * tm
    ends = jnp.cumsum(padded)
    tile_group = jnp.minimum(jnp.sum((tile_start[:, None] >= ends[None, :]).astype(jnp.int32), axis=1),
                             N_GROUPS - 1).astype(jnp.int32)
    return dst, tile_group


def _moe(x, g, sh, sc, gate, w_router_t, b_router, w_gate, w_up, w_down):
    nb, r, d = x.shape
    t = nb * r
    tm = 512 if t >= 8192 else (256 if t >= 2048 else 128)
    hn_ext, info = _moe_route(x, g, sh, sc, w_router_t, b_router)
    grp = info[:, EXPERTS_PER_GROUP, :].reshape(t).astype(jnp.int32)
    dst, tile_group = _group_layout(grp, tm)
    y = _moe_experts(hn_ext.reshape(t, d + LANE), dst, tile_group, w_gate, w_up, w_down, tm)
    return _residual(x, y, gate)


def kernel(x, c, ctx, c_ctx, w_ada, b_ada, norm1_g, norm2_g, w_in, qn_a, kn_a, sink_a, qn_b, kn_b,
           lam_b, subln_b, conv_w, conv_b, w_rg, b_rg, w_ig, b_ig, lru_lambda, lb_d, onorm_d,
           w_branch, w_out, w_router, b_router, w_gate, w_up, w_down):
    bsz, seq, d = x.shape
    lc = ctx.shape[1]
    depth = w_in.shape[0]

    n_rows = -(-(bsz + 1) // SUBLANE) * SUBLANE
    c_pad = jnp.zeros((n_rows, d), F32).at[:bsz].set(c).at[bsz].set(c_ctx)
    mod = _ada(c_pad, w_ada, b_ada).reshape(depth, n_rows, 6, d)

    lb_w = jax.nn.softmax(lb_d.astype(F32), axis=0)
    lb_all = jnp.cumsum(lb_w, axis=0) - lb_w[0:1]
    rope_tabs = _rope_tables(seq)
    w_router_t = w_router.T

    xl = x
    xc = ctx.reshape(1, bsz * lc, d)
    for l in range(depth):
        ctx_out = l < depth - 1
        mod_l = [mod[l, :bsz, k][:, None, :] for k in range(6)]
        mod_c = [mod[l, bsz:bsz + 1, k][:, None, :] for k in range(6)]
        w_in_l = w_in[l].astype(BF16)
        g1 = norm1_g[l].reshape(1, d)
        g2 = norm2_g[l].reshape(1, d)

        lo_l, hi_l = _norm_proj(xl, g1, mod_l[0], mod_l[1], w_in_l)
        lo_c, hi_c = _norm_proj(xc, g1, mod_c[0], mod_c[1], w_in_l)
        lo_c = lo_c.reshape(bsz, lc, -1)
        hi_c = hi_c.reshape(bsz, lc, -1)

        qa = qn_a[l] * (HEAD_DIM ** -0.5 * LOG2E)
        qb = jnp.tile(qn_b[l], 2) * (B_DIM ** -0.5 * LOG2E)
        kb = jnp.tile(kn_b[l], 2)
        one = jnp.ones((LANE,), F32)
        gains = jnp.stack([qa] * 8 + [kn_a[l]] * 2 + [one] * 2 + [qb] * 8 + [kb] * 8 + [one] * 8)
        gains = gains.reshape(N_QKV_BLOCKS, 1, LANE)
        p_l = _prep(lo_l, gains, rope_tabs)
        p_c = _prep(lo_c, gains, None)

        oa_c, oa_l = _attn_a(sink_a[l], p_l, p_c, ctx_out)

        lq1, lk1, lq2, lk2 = lam_b[l].astype(F32)
        lam_init = 0.8 - 0.6 * math.exp(-0.3 * l)
        lam = (jnp.exp(jnp.sum(lq1 * lk1)) - jnp.exp(jnp.sum(lq2 * lk2)) + lam_init).reshape(1)
        ob_c, ob_l = _attn_b(lam, p_l, p_c, subln_b[l].reshape(1, LANE), 1.0 - lam_init, ctx_out)

        oc_c, oc_l = _lru(hi_l, hi_c, conv_w[l], conv_b[l], w_rg[l].astype(BF16), b_rg[l],
                          w_ig[l].astype(BF16), b_ig[l], lru_lambda[l], ctx_out)
        od_c, od_l = _gla(hi_l, hi_c, lb_all[l], onorm_d[l].reshape(1, LANE), ctx_out)

        w_branch_l = w_branch[l].astype(BF16)
        w_out_l = w_out[l].astype(BF16)
        moe_w = (w_router_t, b_router, w_gate[l].astype(BF16), w_up[l].astype(BF16),
                 w_down[l].astype(BF16))

        y_l = _merge((oa_l, ob_l, oc_l, od_l), lo_l, w_branch_l)
        xl = _out_proj(y_l, w_out_l, xl, mod_l[2])
        xl = _moe(xl, g2, mod_l[3], mod_l[4], mod_l[5], *moe_w)
        if ctx_out:
            flat = lambda t: t.reshape(1, bsz * lc, -1)
            y_c = _merge(tuple(flat(t) for t in (oa_c, ob_c, oc_c, od_c)), flat(lo_c), w_branch_l)
            xc = _out_proj(y_c, w_out_l, xc, mod_c[2])
            xc = _moe(xc, g2, mod_c[3], mod_c[4], mod_c[5], *moe_w)
    return xl
```

```python
import functools
import math

import jax
import jax.numpy as jnp
from jax import lax
from jax.experimental import pallas as pl
from jax.experimental.pallas import tpu as pltpu

F32 = jnp.float32
BF16 = jnp.bfloat16

LANE = 128
SUBLANE = 8
VMEM_LIMIT_BYTES = 56 * 1024 * 1024

EPS = 1e-6
ROPE_THETA = 10000.0
GRID_W = 64
HEAD_DIM = 128
WINDOW = 128
Q_BLOCK = 128
N_HEADS = 8
A_KV_HEADS = 2
A_GROUP = N_HEADS // A_KV_HEADS
A_Q_PER_STEP = 2
B_DIM = 64
B_KEY_CHUNK = 1024
LRU_C = 8.0
GLA_CHUNK = 64
GLA_SAFE_DECAY = 80.0
N_EXPERTS = 16
N_GROUPS = 4
EXPERTS_PER_GROUP = N_EXPERTS // N_GROUPS
NEG_BIG = -1e30
LOG2E = math.log2(math.e)
TINY = 1e-37

COL_AQ, COL_AK, COL_AV = 0, 8, 10
COL_BQ, COL_BK, COL_BV = 12, 20, 28
N_QKV_BLOCKS = 36
COL_GATE = N_QKV_BLOCKS
COL_CX, COL_CY = 0, 8
COL_DQ, COL_DFF, COL_DFB, COL_DI, COL_DG = 16, 24, 32, 40, 48
W_IN_QKV = (0, 4608)
W_IN_REC = (4608, 11776)
W_IN_GATE = (11776, 19968)
PROJ_TILE = 512

NT_DIMS = (((1,), (1,)), ((), ()))
TN_DIMS = (((0,), (0,)), ((), ()))


def _cparams(*sem):
    return pltpu.CompilerParams(dimension_semantics=sem, vmem_limit_bytes=VMEM_LIMIT_BYTES)


def _tile(n, pref):
    t = min(n, pref)
    while n % t:
        t //= 2
    return t


def _sigmoid(x):
    return jax.nn.sigmoid(x)


def _modulated_norm(x, g, sh, sc):
    ms = jnp.mean(x * x, axis=-1, keepdims=True)
    return (x * lax.rsqrt(ms + EPS) * g) * (1.0 + sc) + sh


def _ada_kernel(c_ref, w_ref, b_ref, o_ref):
    c = c_ref[...]
    s = (c * _sigmoid(c)).astype(BF16)
    o_ref[0] = jnp.dot(s, w_ref[0].astype(BF16), preferred_element_type=F32) + b_ref[0]


def _ada(c_pad, w_ada, b_ada):
    nl, d, n = w_ada.shape
    rows = c_pad.shape[0]
    tn = _tile(n, 1024)
    return pl.pallas_call(
        _ada_kernel,
        grid=(nl, n // tn),
        in_specs=[
            pl.BlockSpec((rows, d), lambda l, j: (0, 0)),
            pl.BlockSpec((1, d, tn), lambda l, j: (l, 0, j)),
            pl.BlockSpec((1, 1, tn), lambda l, j: (l, 0, j)),
        ],
        out_specs=pl.BlockSpec((1, rows, tn), lambda l, j: (l, 0, j)),
        out_shape=jax.ShapeDtypeStruct((nl, rows, n), F32),
        compiler_params=_cparams("parallel", "parallel"),
        name="ada_mod",
    )(c_pad, w_ada, b_ada.reshape(nl, 1, n))


def _norm_proj_kernel(x_ref, g_ref, sh_ref, sc_ref, w_ref, lo_ref, hi_ref, hn_ref, *, n_lo):
    j = pl.program_id(2)

    @pl.when(j == 0)
    def _():
        hn_ref[...] = _modulated_norm(x_ref[0], g_ref[...], sh_ref[0], sc_ref[0]).astype(BF16)

    @pl.when(j < n_lo)
    def _():
        lo_ref[0] = jnp.dot(hn_ref[...], w_ref[...], preferred_element_type=F32).astype(BF16)

    @pl.when(j >= n_lo)
    def _():
        hi_ref[0] = jnp.dot(hn_ref[...], w_ref[...], preferred_element_type=F32)


def _proj_weight_tile(j):
    n_qkv = (W_IN_QKV[1] - W_IN_QKV[0]) // PROJ_TILE
    n_gate = (W_IN_GATE[1] - W_IN_GATE[0]) // PROJ_TILE
    gate0 = W_IN_GATE[0] // PROJ_TILE
    rec0 = W_IN_REC[0] // PROJ_TILE
    return jnp.where(j < n_qkv, j, jnp.where(j < n_qkv + n_gate, j - n_qkv + gate0, j - n_qkv - n_gate + rec0))


def _norm_proj(x, g, sh, sc, w):
    nb, r, d = x.shape
    tn = PROJ_TILE
    ntiles = w.shape[1] // tn
    n_lo = (W_IN_QKV[1] - W_IN_QKV[0] + W_IN_GATE[1] - W_IN_GATE[0]) // tn
    tm = _tile(r, 1024)
    n_hi = ntiles - n_lo
    return pl.pallas_call(
        functools.partial(_norm_proj_kernel, n_lo=n_lo),
        grid=(nb, r // tm, ntiles),
        in_specs=[
            pl.BlockSpec((1, tm, d), lambda b, t, j: (b, t, 0)),
            pl.BlockSpec((1, d), lambda b, t, j: (0, 0)),
            pl.BlockSpec((1, 1, d), lambda b, t, j: (b, 0, 0)),
            pl.BlockSpec((1, 1, d), lambda b, t, j: (b, 0, 0)),
            pl.BlockSpec((d, tn), lambda b, t, j: (0, _proj_weight_tile(j))),
        ],
        out_specs=[
            pl.BlockSpec((1, tm, tn), lambda b, t, j: (b, t, jnp.minimum(j, n_lo - 1))),
            pl.BlockSpec((1, tm, tn), lambda b, t, j: (b, t, jnp.maximum(j - n_lo, 0))),
        ],
        out_shape=[jax.ShapeDtypeStruct((nb, r, n_lo * tn), BF16),
                   jax.ShapeDtypeStruct((nb, r, n_hi * tn), F32)],
        scratch_shapes=[pltpu.VMEM((tm, d), BF16)],
        compiler_params=_cparams("parallel", "parallel", "arbitrary"),
        name="norm_proj",
    )(x, g, sh, sc, w)


def _group_mean_matrix(group):
    k = lax.broadcasted_iota(jnp.int32, (LANE, LANE), 0)
    l = lax.broadcasted_iota(jnp.int32, (LANE, LANE), 1)
    return jnp.where(k // group == l // group, 1.0 / group, 0.0).astype(BF16)


def _rope_partner_matrix(half):
    k = lax.broadcasted_iota(jnp.int32, (LANE, LANE), 0)
    l = lax.broadcasted_iota(jnp.int32, (LANE, LANE), 1)
    src = jnp.where((l % (2 * half)) < half, l + half, l - half)
    return jnp.where(k == src, 1.0, 0.0).astype(BF16)


def _group_mean(x2, mean_m):
    hi = x2.astype(BF16)
    lo = (x2 - hi.astype(F32)).astype(BF16)
    return (jnp.dot(hi, mean_m, preferred_element_type=F32)
            + jnp.dot(lo, mean_m, preferred_element_type=F32))


def _prep_kernel(*refs, rope):
    if rope:
        x_ref, g_ref, ca_ref, sa_ref, cb_ref, sb_ref, o_ref = refs
    else:
        x_ref, g_ref, o_ref = refs
    tm = x_ref.shape[1]

    def qk_blocks(col0, nblk, group, half, cos_ref, sin_ref):
        blocks = [slice((col0 + j) * LANE, (col0 + j + 1) * LANE) for j in range(nblk)]
        x = jnp.concatenate([x_ref[0, :, c].astype(F32) for c in blocks], axis=0)
        ms = _group_mean(x * x, _group_mean_matrix(group))
        y = (x * lax.rsqrt(ms + EPS)).reshape(nblk, tm, LANE) * g_ref[col0:col0 + nblk]
        if rope:
            part = jnp.dot(y.reshape(nblk * tm, LANE).astype(BF16), _rope_partner_matrix(half),
                           preferred_element_type=F32).reshape(nblk, tm, LANE)
            y = y * cos_ref[...][None] + part * sin_ref[...][None]
        for j, c in enumerate(blocks):
            o_ref[0, :, c] = y[j].astype(BF16)

    qk_blocks(COL_AQ, COL_AV - COL_AQ, HEAD_DIM, 32, ca_ref if rope else None, sa_ref if rope else None)
    qk_blocks(COL_BQ, COL_BV - COL_BQ, B_DIM, 16, cb_ref if rope else None, sb_ref if rope else None)
    for c0, c1 in ((COL_AV, COL_BQ), (COL_BV, N_QKV_BLOCKS)):
        o_ref[0, :, c0 * LANE:c1 * LANE] = x_ref[0, :, c0 * LANE:c1 * LANE]


def _prep(proj, gains, rope_tabs):
    nb, r, _ = proj.shape
    tm = _tile(r, 256)
    width = N_QKV_BLOCKS * LANE
    rope = rope_tabs is not None
    in_specs = [
        pl.BlockSpec((1, tm, width), lambda b, t: (b, t, 0)),
        pl.BlockSpec((N_QKV_BLOCKS, 1, LANE), lambda b, t: (0, 0, 0)),
    ]
    args = [proj, gains]
    if rope:
        in_specs += [pl.BlockSpec((tm, LANE), lambda b, t: (t, 0))] * 4
        args += list(rope_tabs)
    return pl.pallas_call(
        functools.partial(_prep_kernel, rope=rope),
        grid=(nb, r // tm),
        in_specs=in_specs,
        out_specs=pl.BlockSpec((1, tm, width), lambda b, t: (b, t, 0)),
        out_shape=jax.ShapeDtypeStruct((nb, r, width), BF16),
        compiler_params=_cparams("parallel", "parallel"),
        name="qkv_prep",
    )(*args)


def _rope_tables(seq):
    pos = jnp.arange(seq)
    rows = (pos // GRID_W).astype(F32)[:, None]
    cols = (pos % GRID_W).astype(F32)[:, None]
    lane = jnp.arange(LANE)

    def tables(half):
        inv = ROPE_THETA ** (-(lane % half).astype(F32) / half)
        use_rows = (lane % (4 * half)) < 2 * half
        ang = jnp.where(use_rows[None, :], rows, cols) * inv[None, :]
        sign = jnp.where((lane % (2 * half)) < half, -1.0, 1.0)
        return jnp.cos(ang), jnp.sin(ang) * sign[None, :]

    ca, sa = tables(32)
    cb, sb = tables(16)
    return ca, sa, cb, sb


def _stack_heads(q):
    return jnp.concatenate([q[:, g * HEAD_DIM:(g + 1) * HEAD_DIM] for g in range(A_GROUP)], axis=0)


def _unstack_heads(o, rows):
    return jnp.concatenate([o[g * rows:(g + 1) * rows] for g in range(A_GROUP)], axis=1)


def _sink_column(sink_ref, kvh, rows):
    return jnp.concatenate(
        [jnp.full((rows, 1), sink_ref[kvh * A_GROUP + g] * LOG2E, F32) for g in range(A_GROUP)], axis=0)


def _attn_a_lat_kernel(sink_ref, q_ref, bias_ref, kl_ref, vl_ref, kc_ref, vc_ref, o_ref, *, seq):
    band = 3 * Q_BLOCK
    nblk = seq // Q_BLOCK
    for qi in range(q_ref.shape[1] // Q_BLOCK):
        i = pl.program_id(1) * (q_ref.shape[1] // Q_BLOCK) + qi
        rows = slice(qi * Q_BLOCK, (qi + 1) * Q_BLOCK)
        first = jnp.clip(i - 1, 0, nblk - 3)
        start = pl.multiple_of(first * Q_BLOCK, Q_BLOCK)
        bias = bias_ref[i - first]
        outs = []
        for kvh in range(A_KV_HEADS):
            hs = slice(kvh * HEAD_DIM, (kvh + 1) * HEAD_DIM)
            qs = _stack_heads(q_ref[0, rows, kvh * A_GROUP * HEAD_DIM:(kvh + 1) * A_GROUP * HEAD_DIM])
            kb = kl_ref[0, pl.ds(start, band), hs]
            vb = vl_ref[0, pl.ds(start, band), hs]
            s_loc = lax.dot_general(qs, kb, NT_DIMS, preferred_element_type=F32)
            s_loc = (s_loc.reshape(A_GROUP, Q_BLOCK, band) + bias[None]).reshape(A_GROUP * Q_BLOCK, band)
            s_ctx = lax.dot_general(qs, kc_ref[0, :, hs], NT_DIMS, preferred_element_type=F32)
            sk = _sink_column(sink_ref, kvh, Q_BLOCK)
            m = jnp.maximum(jnp.maximum(jnp.max(s_loc, axis=-1, keepdims=True),
                                        jnp.max(s_ctx, axis=-1, keepdims=True)), sk)
            p_loc = jnp.exp2(s_loc - m)
            p_ctx = jnp.exp2(s_ctx - m)
            den = (jnp.sum(p_loc, axis=-1, keepdims=True) + jnp.sum(p_ctx, axis=-1, keepdims=True)
                   + jnp.exp2(sk - m))
            o = (jnp.dot(p_loc.astype(BF16), vb, preferred_element_type=F32)
                 + jnp.dot(p_ctx.astype(BF16), vc_ref[0, :, hs], preferred_element_type=F32)) / den
            outs.append(_unstack_heads(o, Q_BLOCK))
        o_ref[0, rows, :] = jnp.concatenate(outs, axis=1).astype(BF16)


def _attn_a_ctx_kernel(sink_ref, q_ref, kc_ref, vc_ref, o_ref):
    kvh = pl.program_id(1)
    rows = q_ref.shape[1]
    qs = _stack_heads(q_ref[0])
    s = lax.dot_general(qs, kc_ref[0], NT_DIMS, preferred_element_type=F32)
    sk = _sink_column(sink_ref, kvh, rows)
    m = jnp.maximum(jnp.max(s, axis=-1, keepdims=True), sk)
    p = jnp.exp2(s - m)
    den = jnp.sum(p, axis=-1, keepdims=True) + jnp.exp2(sk - m)
    o = jnp.dot(p.astype(BF16), vc_ref[0], preferred_element_type=F32) / den
    o_ref[0] = _unstack_heads(o, rows).astype(BF16)


def _window_bias():
    band = 3 * Q_BLOCK
    row = jnp.arange(Q_BLOCK)[None, :, None]
    col = jnp.arange(band)[None, None, :]
    delta = (jnp.arange(3) * Q_BLOCK)[:, None, None]
    return jnp.where(jnp.abs(col - delta - row) <= WINDOW, 0.0, NEG_BIG).astype(F32)


def _attn_a(sink, p_l, p_c, ctx_out):
    b, seq, _ = p_l.shape
    lc = p_c.shape[1]
    gw = A_GROUP * HEAD_DIM
    nblk = seq // Q_BLOCK
    kvw = A_KV_HEADS * HEAD_DIM
    smem = pl.BlockSpec(memory_space=pltpu.SMEM)

    qrows = A_Q_PER_STEP * Q_BLOCK if nblk % A_Q_PER_STEP == 0 else Q_BLOCK
    o_l = pl.pallas_call(
        functools.partial(_attn_a_lat_kernel, seq=seq),
        grid=(b, seq // qrows),
        in_specs=[
            smem,
            pl.BlockSpec((1, qrows, N_HEADS * HEAD_DIM), lambda n, i: (n, i, 0)),
            pl.BlockSpec((3, Q_BLOCK, 3 * Q_BLOCK), lambda n, i: (0, 0, 0)),
            pl.BlockSpec((1, seq, kvw), lambda n, i: (n, 0, COL_AK // A_KV_HEADS)),
            pl.BlockSpec((1, seq, kvw), lambda n, i: (n, 0, COL_AV // A_KV_HEADS)),
            pl.BlockSpec((1, lc, kvw), lambda n, i: (n, 0, COL_AK // A_KV_HEADS)),
            pl.BlockSpec((1, lc, kvw), lambda n, i: (n, 0, COL_AV // A_KV_HEADS)),
        ],
        out_specs=pl.BlockSpec((1, qrows, N_HEADS * HEAD_DIM), lambda n, i: (n, i, 0)),
        out_shape=jax.ShapeDtypeStruct((b, seq, N_HEADS * HEAD_DIM), BF16),
        compiler_params=_cparams("parallel", "arbitrary"),
        name="attn_a_latent",
    )(sink, p_l, _window_bias(), p_l, p_l, p_c, p_c)
    o_c = None
    if ctx_out:
        o_c = pl.pallas_call(
            _attn_a_ctx_kernel,
            grid=(b, A_KV_HEADS),
            in_specs=[
                smem,
                pl.BlockSpec((1, lc, gw), lambda n, h: (n, 0, h)),
                pl.BlockSpec((1, lc, LANE), lambda n, h: (n, 0, COL_AK + h)),
                pl.BlockSpec((1, lc, LANE), lambda n, h: (n, 0, COL_AV + h)),
            ],
            out_specs=pl.BlockSpec((1, lc, gw), lambda n, h: (n, 0, h)),
            out_shape=jax.ShapeDtypeStruct((b, lc, N_HEADS * HEAD_DIM), BF16),
            compiler_params=_cparams("parallel", "parallel"),
            name="attn_a_ctx",
        )(sink, p_c, p_c, p_c)
    return o_c, o_l


def _attn_b_kernel(lam_ref, q_ref, *refs, nseg, post_scale):
    k_refs = refs[:nseg]
    v_refs = refs[nseg:2 * nseg]
    g_ref, o_ref = refs[2 * nseg], refs[2 * nseg + 1]
    q = q_ref[0]
    tq = q.shape[0]
    lane = lax.broadcasted_iota(jnp.int32, q.shape, 1)
    zero = jnp.zeros_like(q)
    q2 = jnp.concatenate([jnp.where(lane < B_DIM, q, zero), jnp.where(lane < B_DIM, zero, q)], axis=0)
    m = jnp.full((2 * tq, 1), NEG_BIG, F32)
    den = jnp.zeros((2 * tq, 1), F32)
    acc = jnp.zeros((2 * tq, LANE), F32)
    for k_ref, v_ref in zip(k_refs, v_refs):
        nk = k_ref.shape[1]
        kc = _tile(nk, B_KEY_CHUNK)
        for c0 in range(0, nk, kc):
            s = lax.dot_general(q2, k_ref[0, c0:c0 + kc, :], NT_DIMS, preferred_element_type=F32)
            m_new = jnp.maximum(m, jnp.max(s, axis=-1, keepdims=True))
            alpha = jnp.exp2(m - m_new)
            p = jnp.exp2(s - m_new)
            den = alpha * den + jnp.sum(p, axis=-1, keepdims=True)
            acc = alpha * acc + jnp.dot(p.astype(BF16), v_ref[0, c0:c0 + kc, :],
                                        preferred_element_type=F32)
            m = m_new
    o2 = acc / den
    o = o2[:tq] - lam_ref[0] * o2[tq:]
    ms = jnp.mean(o * o, axis=-1, keepdims=True)
    o_ref[0] = ((o * lax.rsqrt(ms + EPS) * g_ref[...]) * post_scale).astype(BF16)


def _attn_b_call(lam, q_src, k_srcs, subln, post_scale, name):
    b, rq, _ = q_src.shape
    tq = _tile(rq, 512)
    smem = pl.BlockSpec(memory_space=pltpu.SMEM)
    in_specs = [smem, pl.BlockSpec((1, tq, LANE), lambda n, h, i: (n, i, COL_BQ + h))]
    in_specs += [pl.BlockSpec((1, s.shape[1], LANE), lambda n, h, i: (n, 0, COL_BK + h)) for s in k_srcs]
    in_specs += [pl.BlockSpec((1, s.shape[1], LANE), lambda n, h, i: (n, 0, COL_BV + h)) for s in k_srcs]
    in_specs += [pl.BlockSpec((1, LANE), lambda n, h, i: (0, 0))]
    return pl.pallas_call(
        functools.partial(_attn_b_kernel, nseg=len(k_srcs), post_scale=post_scale),
        grid=(b, N_HEADS, rq // tq),
        in_specs=in_specs,
        out_specs=pl.BlockSpec((1, tq, LANE), lambda n, h, i: (n, i, h)),
        out_shape=jax.ShapeDtypeStruct((b, rq, N_HEADS * HEAD_DIM), BF16),
        compiler_params=_cparams("parallel", "parallel", "arbitrary"),
        name=name,
    )(lam, q_src, *k_srcs, *k_srcs, subln)


def _attn_b(lam, p_l, p_c, subln, post_scale, ctx_out):
    o_l = _attn_b_call(lam, p_l, [p_c, p_l], subln, post_scale, "attn_b_latent")
    o_c = _attn_b_call(lam, p_c, [p_c], subln, post_scale, "attn_b_ctx") if ctx_out else None
    return o_c, o_l


def _centred_conv(x_ref, pad_ref, w_ref, b_ref):
    n = x_ref.shape[1]
    zeros = jnp.zeros((SUBLANE, LANE), F32)
    pad_ref[0:SUBLANE, :] = zeros
    pad_ref[SUBLANE:SUBLANE + n, :] = x_ref[0]
    pad_ref[SUBLANE + n:2 * SUBLANE + n, :] = zeros
    out = b_ref[...]
    for tap in range(4):
        out = out + pad_ref[pl.ds(SUBLANE - 2 + tap, n), :] * w_ref[tap:tap + 1, :]
    return out


def _every_8th(ref, d, j, n):
    return ref[d, pl.ds(j, n, stride=SUBLANE), :]


def _tile_order(reverse):
    return range(SUBLANE - 1, -1, -1) if reverse else range(SUBLANE)


def _scan_summaries(a_ref, v_ref, d, n, reverse):
    h = p = None
    for j in _tile_order(reverse):
        a, v = _every_8th(a_ref, d, j, n), _every_8th(v_ref, d, j, n)
        h, p = (v, a) if h is None else (a * h + v, a * p)
    return p, h


def _scan_entering(a_ref, v_ref, d, n, carry_in, reverse, store):
    e = carry_in
    for j in _tile_order(reverse):
        store(j, e)
        e = _every_8th(a_ref, d, j, n) * e + _every_8th(v_ref, d, j, n)


def _scan_apply(a_ref, v_ref, d, n, carry_in, reverse, store):
    h = carry_in
    for j in _tile_order(reverse):
        h = _every_8th(a_ref, d, j, n) * h + _every_8th(v_ref, d, j, n)
        store(j, h)


def _seg_tile(t, ntc, ntl, reverse):
    if not reverse:
        return t
    return jnp.where(t < ntc, ntc - 1 - t, 2 * ntc + ntl - 1 - t)


def _lru_kernel(xc_ref, yc_ref, xl_ref, yl_ref, cw_ref, cb_ref, wr_ref, br_ref, wi_ref, bi_ref,
                lam_ref, *refs, ctx_out):
    if ctx_out:
        oc_ref, ol_ref, ac_ref, hl_ref, hs_ref, pad_ref, tp_ref, th_ref, sp_ref, sh_ref, c3_ref, e1_ref = refs
    else:
        ol_ref, ac_ref, hl_ref, hs_ref, pad_ref, tp_ref, th_ref, sp_ref, sh_ref, c3_ref, e1_ref = refs
    lc, seq = xc_ref.shape[1], xl_ref.shape[1]
    for x_ref, off in ((xc_ref, 0), (xl_ref, lc)):
        n = x_ref.shape[1]
        u = _centred_conv(x_ref, pad_ref, cw_ref, cb_ref)
        ub = u.astype(BF16)
        for d in range(2):
            lam = lam_ref[d:d + 1, :]
            sp = jnp.maximum(-lam, 0.0) + jnp.log1p(jnp.exp(-jnp.abs(lam)))
            r = _sigmoid(jnp.dot(ub, wr_ref[d, 0], preferred_element_type=F32) + br_ref[d:d + 1, :])
            gi = _sigmoid(jnp.dot(ub, wi_ref[d, 0], preferred_element_type=F32) + bi_ref[d:d + 1, :])
            log_a = -LRU_C * r * sp
            a = jnp.exp(log_a)
            y = jnp.tanh(-log_a) * (1.0 + a * a)
            v = (y * lax.rsqrt(jnp.maximum(y, TINY))) * gi * u
            ac_ref[d, off:off + n, :] = a
            hl_ref[d, off:off + n, :] = v

    nt1 = (lc + seq) // SUBLANE
    nt2 = nt1 // SUBLANE
    group_rows = SUBLANE * SUBLANE
    for d in range(2):
        tp_ref[d], th_ref[d] = _scan_summaries(ac_ref, hl_ref, d, nt1, d == 1)
        sp_ref[d], sh_ref[d] = _scan_summaries(tp_ref, th_ref, d, nt2, d == 1)

    def step(s, carry):
        new = []
        for d in range(2):
            g = pl.ds(_seg_tile(s, lc // group_rows, seq // group_rows, d == 1), 1)
            c3_ref[d, g, :] = carry[d]
            new.append(sp_ref[d, g, :] * carry[d] + sh_ref[d, g, :])
        return tuple(new)

    zero = jnp.zeros((1, LANE), F32)
    lax.fori_loop(0, nt2, step, (zero, zero))

    for d in range(2):
        def store_entering(j, e, d=d):
            e1_ref[d, pl.ds(j, nt2, stride=SUBLANE), :] = e

        def store_h(j, h, d=d):
            if d == 0:
                hs_ref[pl.ds(j, nt1, stride=SUBLANE), :] = h
            else:
                hl_ref[d, pl.ds(j, nt1, stride=SUBLANE), :] = h

        _scan_entering(tp_ref, th_ref, d, nt2, c3_ref[d], d == 1, store_entering)
        _scan_apply(ac_ref, hl_ref, d, nt1, e1_ref[d], d == 1, store_h)
    if ctx_out:
        hc = hs_ref[0:lc, :] + hl_ref[1, 0:lc, :]
        oc_ref[0] = (hc * jax.nn.gelu(yc_ref[0])).astype(BF16)
    hl = hs_ref[lc:lc + seq, :] + hl_ref[1, lc:lc + seq, :]
    ol_ref[0] = (hl * jax.nn.gelu(yl_ref[0])).astype(BF16)


def _lru(proj_l, proj_c, conv_w, conv_b, w_r, b_r, w_i, b_i, lam, ctx_out):
    b, seq, _ = proj_l.shape
    lc = proj_c.shape[1]
    nblk = w_r.shape[1]
    width = nblk * LANE
    nt1 = (lc + seq) // SUBLANE
    nt2 = nt1 // SUBLANE

    def seg(n, col):
        return pl.BlockSpec((1, n, LANE), lambda i, j: (i, 0, col + j))

    vec2 = pl.BlockSpec((2, LANE), lambda i, j: (0, j))
    wspec = pl.BlockSpec((2, 1, LANE, LANE), lambda i, j: (0, j, 0, 0))
    out_specs = [pl.BlockSpec((1, seq, LANE), lambda i, j: (i, 0, j))]
    out_shape = [jax.ShapeDtypeStruct((b, seq, width), BF16)]
    if ctx_out:
        out_specs.insert(0, pl.BlockSpec((1, lc, LANE), lambda i, j: (i, 0, j)))
        out_shape.insert(0, jax.ShapeDtypeStruct((b, lc, width), BF16))
    outs = pl.pallas_call(
        functools.partial(_lru_kernel, ctx_out=ctx_out),
        grid=(b, nblk),
        in_specs=[
            seg(lc, COL_CX), seg(lc, COL_CY), seg(seq, COL_CX), seg(seq, COL_CY),
            pl.BlockSpec((4, LANE), lambda i, j: (0, j)),
            pl.BlockSpec((1, LANE), lambda i, j: (0, j)),
            wspec, vec2, wspec, vec2, vec2,
        ],
        out_specs=out_specs,
        out_shape=out_shape,
        scratch_shapes=[
            pltpu.VMEM((2, lc + seq, LANE), F32),
            pltpu.VMEM((2, lc + seq, LANE), F32),
            pltpu.VMEM((lc + seq, LANE), F32),
            pltpu.VMEM((max(lc, seq) + 2 * SUBLANE, LANE), F32),
            pltpu.VMEM((2, nt1, LANE), F32),
            pltpu.VMEM((2, nt1, LANE), F32),
            pltpu.VMEM((2, nt2, LANE), F32),
            pltpu.VMEM((2, nt2, LANE), F32),
            pltpu.VMEM((2, nt2, LANE), F32),
            pltpu.VMEM((2, nt1, LANE), F32),
        ],
        compiler_params=_cparams("parallel", "parallel"),
        name="rglru",
    )(proj_c, proj_c, proj_l, proj_l, conv_w, conv_b.reshape(1, width), w_r, b_r, w_i, b_i, lam)
    return (outs[0], outs[1]) if ctx_out else (None, outs[0])


def _gla_level_masks(reverse):
    c = GLA_CHUNK
    row = lax.broadcasted_iota(jnp.int32, (c, c), 0)
    col = lax.broadcasted_iota(jnp.int32, (c, c), 1)
    masks = {}
    for s in (32, 16, 8):
        same = (row // (2 * s)) == (col // (2 * s))
        if reverse:
            masks[s] = same & ((row % (2 * s)) < s) & ((col % (2 * s)) >= s)
        else:
            masks[s] = same & ((row % (2 * s)) >= s) & ((col % (2 * s)) < s)
    tri = (row <= col) if reverse else (row >= col)
    return masks, tri.astype(F32)


def _gla_chunk(q, k, v, g, st, masks, tri, reverse):
    c = GLA_CHUNK
    b = jnp.dot(tri, g, precision=lax.Precision.HIGHEST, preferred_element_type=F32)
    att = jnp.zeros((c, c), F32)
    for s in (32, 16, 8):
        b3 = b.reshape(c // (2 * s), 2 * s, LANE)
        rr = s if reverse else s - 1
        rho = jnp.broadcast_to(b3[:, rr:rr + 1, :], b3.shape).reshape(c, LANE)
        e = jnp.exp(-jnp.abs(b - rho))
        a = lax.dot_general((q * e).astype(BF16), (k * e).astype(BF16), NT_DIMS,
                            preferred_element_type=F32)
        att = att + jnp.where(masks[s], a, 0.0)
    vb = v.astype(BF16)
    o = jnp.dot(att.astype(BF16), vb, preferred_element_type=F32)
    nb = c // SUBLANE
    b3 = b.reshape(nb, SUBLANE, LANE)
    q3 = q.reshape(nb, SUBLANE, LANE)
    k3 = k.reshape(nb, SUBLANE, LANE)
    v3 = v.reshape(nb, SUBLANE, LANE)
    rowi = lax.broadcasted_iota(jnp.int32, b3.shape, 1)
    od = jnp.zeros(b3.shape, F32)
    for jj in range(SUBLANE):
        keep = (rowi <= jj) if reverse else (rowi >= jj)
        e = jnp.where(keep, jnp.exp(jnp.minimum(b3 - b3[:, jj:jj + 1, :], 0.0)), 0.0)
        sj = jnp.sum(q3 * e * k3[:, jj:jj + 1, :], axis=-1, keepdims=True)
        od = od + sj * v3[:, jj:jj + 1, :]
    o = o + od.reshape(c, LANE)
    o = o + lax.dot_general((q * jnp.exp(b)).astype(BF16), st.astype(BF16), NT_DIMS,
                            preferred_element_type=F32)
    b_end = b[0:1, :] if reverse else b[c - 1:c, :]
    khat = (k * jnp.exp(b_end - b)).astype(BF16)
    st_new = st * jnp.exp(b_end) + lax.dot_general(vb, khat, TN_DIMS, preferred_element_type=F32)
    return o, st_new


def _gla_fast_intra(q, k, v, g, tri, reverse):
    c = GLA_CHUNK
    g_hi = g.astype(BF16)
    g_lo = (g - g_hi.astype(F32)).astype(BF16)
    b2 = jnp.dot(tri.astype(BF16), jnp.concatenate([g_hi, g_lo], axis=1), preferred_element_type=F32)
    b = b2[:, :LANE] + b2[:, LANE:]
    rr = c // 2 if reverse else c // 2 - 1
    rho = b[rr:rr + 1, :]
    qt = q * jnp.exp(b - rho)
    kt = k * jnp.exp(rho - b)
    a = lax.dot_general(qt.astype(BF16), kt.astype(BF16), NT_DIMS, preferred_element_type=F32)
    att = jnp.where(tri > 0.0, a, 0.0)
    vb = v.astype(BF16)
    o_intra = jnp.dot(att.astype(BF16), vb, preferred_element_type=F32)
    b_end = b[0:1, :] if reverse else b[c - 1:c, :]
    qe = (qt * jnp.exp(rho)).astype(BF16)
    khat = (kt * jnp.exp(b_end - rho)).astype(BF16)
    return o_intra, qe, khat, vb, jnp.exp(b_end)


def _gla_fast_inter(intra, st):
    o_intra, qe, khat, vb, decay = intra
    o = o_intra + lax.dot_general(qe, st.astype(BF16), NT_DIMS, preferred_element_type=F32)
    st_new = st * decay + lax.dot_general(vb, khat, TN_DIMS, preferred_element_type=F32)
    return o, st_new


def _gla_kernel(lb_ref, on_ref, qc_ref, ffc_ref, fbc_ref, ic_ref, gc_ref,
                ql_ref, ffl_ref, fbl_ref, il_ref, gl_ref, *refs, ctx_out):
    if ctx_out:
        oc_ref, ol_ref, q_s, v_s, g_s, k_s, o_s, st_s = refs
    else:
        ol_ref, q_s, v_s, g_s, k_s, o_s, st_s = refs
    lc, seq = qc_ref.shape[1], ql_ref.shape[1]
    ncc, ncl = lc // GLA_CHUNK, seq // GLA_CHUNK
    for off, n, q_ref, i_ref, f_refs in ((0, lc, qc_ref, ic_ref, (ffc_ref, fbc_ref)),
                                         (lc, seq, ql_ref, il_ref, (ffl_ref, fbl_ref))):
        q_s[off:off + n, :] = q_ref[0]
        v_s[off:off + n, :] = i_ref[0]
        for d in range(2):
            z = f_refs[d][0]
            lbd = lb_ref[d:d + 1, :]
            ez = jnp.exp(-jnp.abs(z))
            r = 1.0 / (1.0 + ez)
            pos = z >= 0.0
            sig_p = jnp.where(pos, r, ez * r)
            sig_n = jnp.where(pos, ez * r, r)
            g_s[d, off:off + n, :] = jnp.log(lbd + (1.0 - lbd) * sig_p)
            k_s[d, off:off + n, :] = (1.0 - lbd) * sig_n
    st_s[...] = jnp.zeros(st_s.shape, F32)
    consts = [_gla_level_masks(False), _gla_level_masks(True)]

    half = GLA_CHUNK // 2
    worst = jnp.zeros((1, LANE), F32)
    for d in range(2):
        hs = jnp.sum(g_s[d].reshape((lc + seq) // half, half, LANE), axis=1)
        worst = jnp.maximum(worst, jnp.max(-hs, axis=0, keepdims=True))
    safe = jnp.max(worst) < GLA_SAFE_DECAY

    def chunk_rows(cidx, d):
        chunk = _seg_tile(cidx, ncc, ncl, d == 1)
        return pl.ds(pl.multiple_of(chunk * GLA_CHUNK, GLA_CHUNK), GLA_CHUNK)

    def robust_step(cidx, carry):
        for d in range(2):
            rows = chunk_rows(cidx, d)
            o, st_new = _gla_chunk(q_s[rows, :], k_s[d, rows, :], v_s[rows, :], g_s[d, rows, :],
                                   st_s[d], consts[d][0], consts[d][1], d == 1)
            st_s[d] = st_new
            o_s[d, rows, :] = o
        return carry

    nchunks = ncc + ncl
    group = next(u for u in (12, 6, 4, 3, 2, 1) if nchunks % u == 0)

    def fast_step(t, carry):
        work = [[] for _ in range(2)]
        for d in range(2):
            for u in range(group):
                rows = chunk_rows(t * group + u, d)
                work[d].append((rows, _gla_fast_intra(q_s[rows, :], k_s[d, rows, :], v_s[rows, :],
                                                      g_s[d, rows, :], consts[d][1], d == 1)))
        for d in range(2):
            st = st_s[d]
            outs = []
            for rows, intra in work[d]:
                o, st = _gla_fast_inter(intra, st)
                outs.append((rows, o))
            st_s[d] = st
            for rows, o in outs:
                o_s[d, rows, :] = o
        return carry

    @pl.when(safe)
    def _():
        lax.fori_loop(0, nchunks // group, fast_step, 0)

    @pl.when(jnp.logical_not(safe))
    def _():
        lax.fori_loop(0, nchunks, robust_step, 0)

    def finish(o, gate):
        ms = jnp.mean(o * o, axis=-1, keepdims=True)
        return ((o * lax.rsqrt(ms + EPS) * on_ref[...]) * (gate * _sigmoid(gate))).astype(BF16)

    if ctx_out:
        oc_ref[0] = finish(o_s[0, 0:lc, :] + o_s[1, 0:lc, :], gc_ref[0])
    ol_ref[0] = finish(o_s[0, lc:lc + seq, :] + o_s[1, lc:lc + seq, :], gl_ref[0])


def _gla(proj_l, proj_c, lb, onorm, ctx_out):
    b, seq, _ = proj_l.shape
    lc = proj_c.shape[1]
    width = N_HEADS * HEAD_DIM
    nt = lc + seq

    def seg(n, col):
        return pl.BlockSpec((1, n, LANE), lambda i, h: (i, 0, col + h))

    cols = (COL_DQ, COL_DFF, COL_DFB, COL_DI, COL_DG)
    out_specs = [pl.BlockSpec((1, seq, LANE), lambda i, h: (i, 0, h))]
    out_shape = [jax.ShapeDtypeStruct((b, seq, width), BF16)]
    if ctx_out:
        out_specs.insert(0, pl.BlockSpec((1, lc, LANE), lambda i, h: (i, 0, h)))
        out_shape.insert(0, jax.ShapeDtypeStruct((b, lc, width), BF16))
    outs = pl.pallas_call(
        functools.partial(_gla_kernel, ctx_out=ctx_out),
        grid=(b, N_HEADS),
        in_specs=[pl.BlockSpec((2, LANE), lambda i, h: (0, h)),
                  pl.BlockSpec((1, LANE), lambda i, h: (0, 0))]
        + [seg(lc, c) for c in cols] + [seg(seq, c) for c in cols],
        out_specs=out_specs,
        out_shape=out_shape,
        scratch_shapes=[
            pltpu.VMEM((nt, LANE), F32),
            pltpu.VMEM((nt, LANE), F32),
            pltpu.VMEM((2, nt, LANE), F32),
            pltpu.VMEM((2, nt, LANE), F32),
            pltpu.VMEM((2, nt, LANE), F32),
            pltpu.VMEM((2, LANE, LANE), F32),
        ],
        compiler_params=_cparams("parallel", "parallel"),
        name="hgrn2",
    )(lb, onorm, *([proj_c] * 5), *([proj_l] * 5))
    return (outs[0], outs[1]) if ctx_out else (None, outs[0])


def _merge_kernel(oa_ref, ob_ref, oc_ref, od_ref, g0_ref, g1_ref, g2_ref, g3_ref,
                  w0_ref, w1_ref, w2_ref, w3_ref, y_ref):
    acc = None
    for o_ref, g_ref, w_ref in ((oa_ref, g0_ref, w0_ref), (ob_ref, g1_ref, w1_ref),
                                (oc_ref, g2_ref, w2_ref), (od_ref, g3_ref, w3_ref)):
        t = _sigmoid(g_ref[0].astype(F32)) * jnp.dot(o_ref[0], w_ref[0], preferred_element_type=F32)
        acc = t if acc is None else acc + t
    y_ref[0] = acc.astype(BF16)


def _merge(outs, proj, w_branch):
    nb, r, mw = outs[0].shape
    d = w_branch.shape[2]
    tm = _tile(r, 512)
    tn = 512
    nj = d // tn
    o_spec = pl.BlockSpec((1, tm, mw), lambda b, t, j: (b, t, 0))
    g_specs = [pl.BlockSpec((1, tm, tn), functools.partial(
        lambda b, t, j, n: (b, t, (COL_GATE * LANE) // tn + n * nj + j), n=n)) for n in range(4)]
    w_specs = [pl.BlockSpec((1, mw, tn), functools.partial(lambda b, t, j, n: (n, 0, j), n=n))
               for n in range(4)]
    return pl.pallas_call(
        _merge_kernel,
        grid=(nb, r // tm, nj),
        in_specs=[o_spec] * 4 + g_specs + w_specs,
        out_specs=pl.BlockSpec((1, tm, tn), lambda b, t, j: (b, t, j)),
        out_shape=jax.ShapeDtypeStruct((nb, r, d), BF16),
        compiler_params=_cparams("parallel", "parallel", "arbitrary"),
        name="branch_merge",
    )(*outs, *([proj] * 4), *([w_branch] * 4))


def _out_proj_kernel(y_ref, w_ref, x_ref, g_ref, o_ref):
    o_ref[0] = x_ref[0] + g_ref[0] * jnp.dot(y_ref[0], w_ref[...], preferred_element_type=F32)


def _out_proj(y, w, x, gate):
    nb, r, d = x.shape
    tm = _tile(r, 1024)
    tn = 512
    return pl.pallas_call(
        _out_proj_kernel,
        grid=(nb, r // tm, d // tn),
        in_specs=[
            pl.BlockSpec((1, tm, d), lambda b, t, j: (b, t, 0)),
            pl.BlockSpec((d, tn), lambda b, t, j: (0, j)),
            pl.BlockSpec((1, tm, tn), lambda b, t, j: (b, t, j)),
            pl.BlockSpec((1, 1, tn), lambda b, t, j: (b, 0, j)),
        ],
        out_specs=pl.BlockSpec((1, tm, tn), lambda b, t, j: (b, t, j)),
        out_shape=jax.ShapeDtypeStruct((nb, r, d), F32),
        compiler_params=_cparams("parallel", "parallel", "arbitrary"),
        name="out_proj",
    )(y, w, x, gate)


def _route(logits_t, bias_col):
    aff = _sigmoid(logits_t)
    sel = aff + bias_col
    aff_r = [aff[e:e + 1, :] for e in range(N_EXPERTS)]
    sel_r = [sel[e:e + 1, :] for e in range(N_EXPERTS)]
    scores = []
    for g in range(N_GROUPS):
        v = sel_r[g * EXPERTS_PER_GROUP:(g + 1) * EXPERTS_PER_GROUP]
        m1 = functools.reduce(jnp.maximum, v)
        taken = jnp.zeros(m1.shape, jnp.bool_)
        second = jnp.full(m1.shape, -jnp.inf, F32)
        for x in v:
            first = (x == m1) & jnp.logical_not(taken)
            taken = taken | first
            second = jnp.where(first, second, jnp.maximum(second, x))
        scores.append(m1 + second)
    best, gidx = scores[0], jnp.zeros(scores[0].shape, jnp.int32)
    for g in range(1, N_GROUPS):
        better = scores[g] > best
        gidx = jnp.where(better, g, gidx)
        best = jnp.where(better, scores[g], best)
    masked = [jnp.where(gidx == e // EXPERTS_PER_GROUP, sel_r[e], -jnp.inf) for e in range(N_EXPERTS)]

    def first_argmax(vals, exclude):
        bv = jnp.full(vals[0].shape, -jnp.inf, F32)
        bi = jnp.full(vals[0].shape, -1, jnp.int32)
        for e, x in enumerate(vals):
            better = x > bv
            if exclude is not None:
                better = better & (exclude != e)
            bi = jnp.where(better, e, bi)
            bv = jnp.where(better, x, bv)
        return bi

    i1 = first_argmax(masked, None)
    i2 = first_argmax(masked, i1)
    w1 = functools.reduce(jnp.add, [jnp.where(i1 == e, aff_r[e], 0.0) for e in range(N_EXPERTS)])
    w2 = functools.reduce(jnp.add, [jnp.where(i2 == e, aff_r[e], 0.0) for e in range(N_EXPERTS)])
    tot = w1 + w2
    g1, g2 = w1 / tot, w2 / tot
    width = logits_t.shape[1]
    rowi = lax.broadcasted_iota(jnp.int32, (SUBLANE, width), 0)
    local1 = jnp.broadcast_to(i1 - gidx * EXPERTS_PER_GROUP, (SUBLANE, width))
    local2 = jnp.broadcast_to(i2 - gidx * EXPERTS_PER_GROUP, (SUBLANE, width))
    info = (jnp.where(rowi == local1, jnp.broadcast_to(g1, (SUBLANE, width)), 0.0)
            + jnp.where(rowi == local2, jnp.broadcast_to(g2, (SUBLANE, width)), 0.0))
    return jnp.where(rowi == EXPERTS_PER_GROUP,
                     jnp.broadcast_to(gidx.astype(F32), (SUBLANE, width)), info)


def _moe_route_kernel(x_ref, g_ref, sh_ref, sc_ref, wrt_ref, br_ref, hn_ref, info_ref):
    d = x_ref.shape[2]
    h = _modulated_norm(x_ref[0], g_ref[...], sh_ref[0], sc_ref[0])
    logits_t = lax.dot_general(wrt_ref[...], h, NT_DIMS, precision=lax.Precision.HIGHEST,
                               preferred_element_type=F32)
    info = _route(logits_t, br_ref[...])
    info_ref[0] = info
    hn_ref[0, :, 0:d] = h
    pad = jnp.zeros((LANE - SUBLANE, info.shape[1]), F32)
    hn_ref[0, :, d:d + LANE] = jnp.concatenate([info, pad], axis=0).T


def _moe_route(x, g, sh, sc, w_router_t, b_router):
    nb, r, d = x.shape
    tm = _tile(r, 512)
    vec = pl.BlockSpec((1, 1, d), lambda b, t: (b, 0, 0))
    return pl.pallas_call(
        _moe_route_kernel,
        grid=(nb, r // tm),
        in_specs=[
            pl.BlockSpec((1, tm, d), lambda b, t: (b, t, 0)),
            pl.BlockSpec((1, d), lambda b, t: (0, 0)),
            vec, vec,
            pl.BlockSpec((N_EXPERTS, d), lambda b, t: (0, 0)),
            pl.BlockSpec((N_EXPERTS, 1), lambda b, t: (0, 0)),
        ],
        out_specs=[pl.BlockSpec((1, tm, d + LANE), lambda b, t: (b, t, 0)),
                   pl.BlockSpec((1, SUBLANE, tm), lambda b, t: (b, 0, t))],
        out_shape=[jax.ShapeDtypeStruct((nb, r, d + LANE), F32),
                   jax.ShapeDtypeStruct((nb, SUBLANE, r), F32)],
        compiler_params=_cparams("parallel", "parallel"),
        name="moe_route",
    )(x, g, sh, sc, w_router_t, b_router.reshape(N_EXPERTS, 1))


def _row_copy(src, src_row, dst, dst_row, sem):
    return pltpu.make_async_copy(src.at[pl.ds(src_row, 1), :], dst.at[pl.ds(dst_row, 1), :], sem)


def _moe_experts_kernel(dst_ref, tgrp_ref, nval_ref, hn_hbm, wg_ref, wu_ref, wd_ref,
                        y_hbm, xbuf, hb, gbuf, acc, gsem, ssem, *, tm, n_tok):
    i = pl.program_id(0)
    e = pl.program_id(1)
    nt = pl.num_programs(0)
    slot = i % 2
    per = tm // EXPERTS_PER_GROUP
    d = hb.shape[1]

    def gather_row(slot_idx, buf, row):
        s = dst_ref[slot_idx]
        return _row_copy(hn_hbm, jnp.where(s < n_tok, s, 0), xbuf.at[buf], row, gsem.at[buf])

    def wait_scatter(sl):
        pltpu.make_async_copy(acc.at[sl], y_hbm.at[pl.ds(0, tm), :], ssem.at[sl]).wait()

    @pl.when((i == 0) & (e == 0))
    def _():
        def body(r, c):
            gather_row(r, 0, r).start()
            return c
        lax.fori_loop(0, tm, body, 0)
        acc[1] = jnp.zeros(acc.shape[1:], F32)
        for half in range(2):
            fill = pltpu.make_async_copy(acc.at[1], y_hbm.at[pl.ds(n_tok + half * tm, tm), :], ssem.at[1])
            fill.start()
            fill.wait()

    @pl.when(e == 0)
    def _():
        pltpu.make_async_copy(hn_hbm.at[pl.ds(0, tm), :], xbuf.at[slot], gsem.at[slot]).wait()
        hb[...] = xbuf[slot, :, 0:d].astype(BF16)
        gbuf[...] = xbuf[slot, :, d:d + LANE]

        @pl.when(i >= 2)
        def _():
            wait_scatter(slot)
        acc[slot] = jnp.zeros(acc.shape[1:], F32)

    @pl.when(i + 1 < nt)
    def _():
        base = (i + 1) * tm + e * per
        for k in range(per):
            gather_row(base + k, 1 - slot, e * per + k).start()

    @pl.when(nval_ref[i] > 0)
    def _():
        hv = hb[...]
        a = jnp.dot(hv, wg_ref[0], preferred_element_type=F32)
        u = jnp.dot(hv, wu_ref[0], preferred_element_type=F32)
        hid = ((a * _sigmoid(a)) * u).astype(BF16)
        y = jnp.dot(hid, wd_ref[0], preferred_element_type=F32)
        lane = lax.broadcasted_iota(jnp.int32, gbuf.shape, 1)
        col = jnp.sum(jnp.where(lane == e, gbuf[...], 0.0), axis=-1, keepdims=True)
        acc[slot] = acc[slot] + col * y

    @pl.when(e == EXPERTS_PER_GROUP - 1)
    def _():
        def body(r, c):
            _row_copy(acc.at[slot], r, y_hbm, dst_ref[i * tm + r], ssem.at[slot]).start()
            return c
        lax.fori_loop(0, tm, body, 0, unroll=8)

        @pl.when(i == nt - 1)
        def _():
            wait_scatter(slot)

            @pl.when(i >= 1)
            def _():
                wait_scatter(1 - slot)


def _moe_experts(hn_ext, dst, tile_group, tile_nvalid, w_gate, w_up, w_down, tm):
    t, dw = hn_ext.shape
    d = dw - LANE
    dff = w_gate.shape[2]
    ntiles = tile_group.shape[0]

    def wmap(i, e, dst_r, tgrp_r, nval_r):
        return (tgrp_r[i] * EXPERTS_PER_GROUP + e, 0, 0)

    grid_spec = pltpu.PrefetchScalarGridSpec(
        num_scalar_prefetch=3,
        grid=(ntiles, EXPERTS_PER_GROUP),
        in_specs=[
            pl.BlockSpec(memory_space=pl.ANY),
            pl.BlockSpec((1, d, dff), wmap),
            pl.BlockSpec((1, d, dff), wmap),
            pl.BlockSpec((1, dff, d), wmap),
        ],
        out_specs=pl.BlockSpec(memory_space=pl.ANY),
        scratch_shapes=[
            pltpu.VMEM((2, tm, dw), F32),
            pltpu.VMEM((tm, d), BF16),
            pltpu.VMEM((tm, LANE), F32),
            pltpu.VMEM((2, tm, d), F32),
            pltpu.SemaphoreType.DMA((2,)),
            pltpu.SemaphoreType.DMA((2,)),
        ],
    )
    return pl.pallas_call(
        functools.partial(_moe_experts_kernel, tm=tm, n_tok=t),
        grid_spec=grid_spec,
        out_shape=jax.ShapeDtypeStruct((t + 2 * tm, d), F32),
        compiler_params=_cparams("arbitrary", "arbitrary"),
        name="moe_experts",
    )(dst, tile_group, tile_nvalid, hn_ext, w_gate, w_up, w_down)


def _residual_kernel(x_ref, y_ref, g_ref, o_ref):
    o_ref[0] = x_ref[0] + g_ref[0] * y_ref[...]


def _residual(x, y, gate):
    nb, r, d = x.shape
    tm = _tile(r, 512)
    per_b = r // tm
    blk = pl.BlockSpec((1, tm, d), lambda b, t: (b, t, 0))
    return pl.pallas_call(
        _residual_kernel,
        grid=(nb, per_b),
        in_specs=[blk, pl.BlockSpec((tm, d), lambda b, t: (b * per_b + t, 0)),
                  pl.BlockSpec((1, 1, d), lambda b, t: (b, 0, 0))],
        out_specs=blk,
        out_shape=jax.ShapeDtypeStruct((nb, r, d), F32),
        compiler_params=_cparams("parallel", "parallel"),
        name="moe_residual",
    )(x, y, gate)


def _group_layout(grp, tm):
    t = grp.shape[0]
    ntiles = t // tm + N_GROUPS
    nslots = ntiles * tm
    oh = (grp[None, :] == jnp.arange(N_GROUPS)[:, None]).astype(F32).reshape(N_GROUPS, t // LANE, LANE)
    tri = (jnp.arange(LANE)[:, None] <= jnp.arange(LANE)[None, :]).astype(F32)
    within = jnp.einsum('grk,kl->grl', oh, tri)
    row_tot = within[..., -1]
    row_off = jnp.cumsum(row_tot, axis=1) - row_tot
    rank = (jnp.sum((within + row_off[..., None]) * oh, axis=0).reshape(t) - 1.0).astype(jnp.int32)
    counts = jnp.sum(row_tot, axis=1).astype(jnp.int32)
    padded = ((counts + tm - 1) // tm) * tm
    start = jnp.cumsum(padded) - padded
    slot_of_token = start[grp] + rank
    slot_ids = jnp.arange(nslots, dtype=jnp.int32)
    spare = t + ((slot_ids // tm) % 2) * tm + slot_ids % tm
    dst = spare.at[slot_of_token].set(jnp.arange(t, dtype=jnp.int32))
    tile_start = jnp.arange(ntiles, dtype=jnp.int32) * tm
    ends = jnp.cumsum(padded)
    tile_group = jnp.minimum(jnp.sum((tile_start[:, None] >= ends[None, :]).astype(jnp.int32), axis=1),
                             N_GROUPS - 1).astype(jnp.int32)
    tile_nvalid = jnp.clip(counts[tile_group] - (tile_start - start[tile_group]), 0, tm).astype(jnp.int32)
    return dst, tile_group, tile_nvalid


def _moe(x, g, sh, sc, gate, w_router_t, b_router, w_gate, w_up, w_down):
    nb, r, d = x.shape
    t = nb * r
    tm = 512 if t >= 8192 else (256 if t >= 2048 else 128)
    hn_ext, info = _moe_route(x, g, sh, sc, w_router_t, b_router)
    grp = info[:, EXPERTS_PER_GROUP, :].reshape(t).astype(jnp.int32)
    dst, tile_group, tile_nvalid = _group_layout(grp, tm)
    y = _moe_experts(hn_ext.reshape(t, d + LANE), dst, tile_group, tile_nvalid, w_gate, w_up, w_down, tm)
    return _residual(x, y, gate)


def kernel(x, c, ctx, c_ctx, w_ada, b_ada, norm1_g, norm2_g, w_in, qn_a, kn_a, sink_a, qn_b, kn_b,
           lam_b, subln_b, conv_w, conv_b, w_rg, b_rg, w_ig, b_ig, lru_lambda, lb_d, onorm_d,
           w_branch, w_out, w_router, b_router, w_gate, w_up, w_down):
    bsz, seq, d = x.shape
    lc = ctx.shape[1]
    depth = w_in.shape[0]

    n_rows = -(-(bsz + 1) // SUBLANE) * SUBLANE
    c_pad = jnp.zeros((n_rows, d), F32).at[:bsz].set(c).at[bsz].set(c_ctx)
    mod = _ada(c_pad, w_ada, b_ada).reshape(depth, n_rows, 6, d)

    lb_w = jax.nn.softmax(lb_d.astype(F32), axis=0)
    lb_all = jnp.cumsum(lb_w, axis=0) - lb_w[0:1]
    rope_tabs = _rope_tables(seq)
    w_router_t = w_router.T

    xl = x
    xc = ctx.reshape(1, bsz * lc, d)
    for l in range(depth):
        ctx_out = l < depth - 1
        mod_l = [mod[l, :bsz, k][:, None, :] for k in range(6)]
        mod_c = [mod[l, bsz:bsz + 1, k][:, None, :] for k in range(6)]
        w_in_l = w_in[l].astype(BF16)
        g1 = norm1_g[l].reshape(1, d)
        g2 = norm2_g[l].reshape(1, d)

        lo_l, hi_l = _norm_proj(xl, g1, mod_l[0], mod_l[1], w_in_l)
        lo_c, hi_c = _norm_proj(xc, g1, mod_c[0], mod_c[1], w_in_l)
        lo_c = lo_c.reshape(bsz, lc, -1)
        hi_c = hi_c.reshape(bsz, lc, -1)

        qa = qn_a[l] * (HEAD_DIM ** -0.5 * LOG2E)
        qb = jnp.tile(qn_b[l], 2) * (B_DIM ** -0.5 * LOG2E)
        kb = jnp.tile(kn_b[l], 2)
        one = jnp.ones((LANE,), F32)
        gains = jnp.stack([qa] * 8 + [kn_a[l]] * 2 + [one] * 2 + [qb] * 8 + [kb] * 8 + [one] * 8)
        gains = gains.reshape(N_QKV_BLOCKS, 1, LANE)
        p_l = _prep(lo_l, gains, rope_tabs)
        p_c = _prep(lo_c, gains, None)

        oa_c, oa_l = _attn_a(sink_a[l], p_l, p_c, ctx_out)

        lq1, lk1, lq2, lk2 = lam_b[l].astype(F32)
        lam_init = 0.8 - 0.6 * math.exp(-0.3 * l)
        lam = (jnp.exp(jnp.sum(lq1 * lk1)) - jnp.exp(jnp.sum(lq2 * lk2)) + lam_init).reshape(1)
        ob_c, ob_l = _attn_b(lam, p_l, p_c, subln_b[l].reshape(1, LANE), 1.0 - lam_init, ctx_out)

        oc_c, oc_l = _lru(hi_l, hi_c, conv_w[l], conv_b[l], w_rg[l].astype(BF16), b_rg[l],
                          w_ig[l].astype(BF16), b_ig[l], lru_lambda[l], ctx_out)
        od_c, od_l = _gla(hi_l, hi_c, lb_all[l], onorm_d[l].reshape(1, LANE), ctx_out)

        w_branch_l = w_branch[l].astype(BF16)
        w_out_l = w_out[l].astype(BF16)
        moe_w = (w_router_t, b_router, w_gate[l].astype(BF16), w_up[l].astype(BF16),
                 w_down[l].astype(BF16))

        y_l = _merge((oa_l, ob_l, oc_l, od_l), lo_l, w_branch_l)
        xl = _out_proj(y_l, w_out_l, xl, mod_l[2])
        xl = _moe(xl, g2, mod_l[3], mod_l[4], mod_l[5], *moe_w)
        if ctx_out:
            flat = lambda t: t.reshape(1, bsz * lc, -1)
            y_c = _merge(tuple(flat(t) for t in (oa_c, ob_c, oc_c, od_c)), flat(lo_c), w_branch_l)
            xc = _out_proj(y_c, w_out_l, xc, mod_c[2])
            xc = _moe(xc, g2, mod_c[3], mod_c[4], mod_c[5], *moe_w)
    return xl
```

```python
import functools
import math

import jax
import jax.numpy as jnp
from jax import lax
from jax.experimental import pallas as pl
from jax.experimental.pallas import tpu as pltpu

F32 = jnp.float32
BF16 = jnp.bfloat16

LANE = 128
SUBLANE = 8
VMEM_LIMIT_BYTES = 56 * 1024 * 1024

EPS = 1e-6
ROPE_THETA = 10000.0
GRID_W = 64
HEAD_DIM = 128
WINDOW = 128
Q_BLOCK = 128
N_HEADS = 8
A_KV_HEADS = 2
A_GROUP = N_HEADS // A_KV_HEADS
A_Q_PER_STEP = 2
B_DIM = 64
B_KEY_CHUNK = 1024
LRU_C = 8.0
GLA_CHUNK = 64
GLA_SAFE_DECAY = 80.0
N_EXPERTS = 16
N_GROUPS = 4
EXPERTS_PER_GROUP = N_EXPERTS // N_GROUPS
NEG_BIG = -1e30
LOG2E = math.log2(math.e)
TINY = 1e-37

COL_AQ, COL_AK, COL_AV = 0, 8, 10
COL_BQ, COL_BK, COL_BV = 12, 20, 28
N_QKV_BLOCKS = 36
COL_GATE = N_QKV_BLOCKS
COL_CX, COL_CY = 0, 8
COL_DQ, COL_DFF, COL_DFB, COL_DI, COL_DG = 16, 24, 32, 40, 48
W_IN_QKV = (0, 4608)
W_IN_REC = (4608, 11776)
W_IN_GATE = (11776, 19968)
PROJ_TILE = 512

NT_DIMS = (((1,), (1,)), ((), ()))
TN_DIMS = (((0,), (0,)), ((), ()))


def _cparams(*sem):
    return pltpu.CompilerParams(dimension_semantics=sem, vmem_limit_bytes=VMEM_LIMIT_BYTES)


def _tile(n, pref):
    t = min(n, pref)
    while n % t:
        t //= 2
    return t


def _sigmoid(x):
    return jax.nn.sigmoid(x)


def _modulated_norm(x, g, sh, sc):
    ms = jnp.mean(x * x, axis=-1, keepdims=True)
    return (x * lax.rsqrt(ms + EPS) * g) * (1.0 + sc) + sh


def _ada_kernel(c_ref, w_ref, b_ref, o_ref):
    c = c_ref[...]
    s = (c * _sigmoid(c)).astype(BF16)
    o_ref[0] = jnp.dot(s, w_ref[0].astype(BF16), preferred_element_type=F32) + b_ref[0]


def _ada(c_pad, w_ada, b_ada):
    nl, d, n = w_ada.shape
    rows = c_pad.shape[0]
    tn = _tile(n, 1024)
    return pl.pallas_call(
        _ada_kernel,
        grid=(nl, n // tn),
        in_specs=[
            pl.BlockSpec((rows, d), lambda l, j: (0, 0)),
            pl.BlockSpec((1, d, tn), lambda l, j: (l, 0, j)),
            pl.BlockSpec((1, 1, tn), lambda l, j: (l, 0, j)),
        ],
        out_specs=pl.BlockSpec((1, rows, tn), lambda l, j: (l, 0, j)),
        out_shape=jax.ShapeDtypeStruct((nl, rows, n), F32),
        compiler_params=_cparams("parallel", "parallel"),
        name="ada_mod",
    )(c_pad, w_ada, b_ada.reshape(nl, 1, n))


def _norm_proj_kernel(x_ref, g_ref, sh_ref, sc_ref, w_ref, lo_ref, hi_ref, hn_ref, *, n_lo):
    j = pl.program_id(2)

    @pl.when(j == 0)
    def _():
        hn_ref[...] = _modulated_norm(x_ref[0], g_ref[...], sh_ref[0], sc_ref[0]).astype(BF16)

    @pl.when(j < n_lo)
    def _():
        lo_ref[0] = jnp.dot(hn_ref[...], w_ref[0], preferred_element_type=F32).astype(BF16)

    @pl.when(j >= n_lo)
    def _():
        hi_ref[0] = jnp.dot(hn_ref[...], w_ref[0], preferred_element_type=F32)


def _proj_weight_tile(j):
    n_qkv = (W_IN_QKV[1] - W_IN_QKV[0]) // PROJ_TILE
    n_gate = (W_IN_GATE[1] - W_IN_GATE[0]) // PROJ_TILE
    gate0 = W_IN_GATE[0] // PROJ_TILE
    rec0 = W_IN_REC[0] // PROJ_TILE
    return jnp.where(j < n_qkv, j, jnp.where(j < n_qkv + n_gate, j - n_qkv + gate0, j - n_qkv - n_gate + rec0))


def _norm_proj(x, g, sh, sc, w, layer):
    nb, r, d = x.shape
    tn = PROJ_TILE
    ntiles = w.shape[2] // tn
    n_lo = (W_IN_QKV[1] - W_IN_QKV[0] + W_IN_GATE[1] - W_IN_GATE[0]) // tn
    tm = _tile(r, 1024)
    n_hi = ntiles - n_lo
    return pl.pallas_call(
        functools.partial(_norm_proj_kernel, n_lo=n_lo),
        grid=(nb, r // tm, ntiles),
        in_specs=[
            pl.BlockSpec((1, tm, d), lambda b, t, j: (b, t, 0)),
            pl.BlockSpec((1, d), lambda b, t, j: (0, 0)),
            pl.BlockSpec((1, 1, d), lambda b, t, j: (b, 0, 0)),
            pl.BlockSpec((1, 1, d), lambda b, t, j: (b, 0, 0)),
            pl.BlockSpec((1, d, tn), lambda b, t, j: (layer, 0, _proj_weight_tile(j))),
        ],
        out_specs=[
            pl.BlockSpec((1, tm, tn), lambda b, t, j: (b, t, jnp.minimum(j, n_lo - 1))),
            pl.BlockSpec((1, tm, tn), lambda b, t, j: (b, t, jnp.maximum(j - n_lo, 0))),
        ],
        out_shape=[jax.ShapeDtypeStruct((nb, r, n_lo * tn), BF16),
                   jax.ShapeDtypeStruct((nb, r, n_hi * tn), F32)],
        scratch_shapes=[pltpu.VMEM((tm, d), BF16)],
        compiler_params=_cparams("parallel", "parallel", "arbitrary"),
        name="norm_proj",
    )(x, g, sh, sc, w)


def _group_mean_matrix(group):
    k = lax.broadcasted_iota(jnp.int32, (LANE, LANE), 0)
    l = lax.broadcasted_iota(jnp.int32, (LANE, LANE), 1)
    return jnp.where(k // group == l // group, 1.0 / group, 0.0).astype(BF16)


def _rope_partner_matrix(half):
    k = lax.broadcasted_iota(jnp.int32, (LANE, LANE), 0)
    l = lax.broadcasted_iota(jnp.int32, (LANE, LANE), 1)
    src = jnp.where((l % (2 * half)) < half, l + half, l - half)
    return jnp.where(k == src, 1.0, 0.0).astype(BF16)


def _group_mean(x2, mean_m):
    hi = x2.astype(BF16)
    lo = (x2 - hi.astype(F32)).astype(BF16)
    return (jnp.dot(hi, mean_m, preferred_element_type=F32)
            + jnp.dot(lo, mean_m, preferred_element_type=F32))


def _prep_kernel(*refs, rope):
    if rope:
        x_ref, g_ref, ca_ref, sa_ref, cb_ref, sb_ref, o_ref = refs
    else:
        x_ref, g_ref, o_ref = refs
    tm = x_ref.shape[1]

    def qk_blocks(col0, nblk, group, half, cos_ref, sin_ref):
        blocks = [slice((col0 + j) * LANE, (col0 + j + 1) * LANE) for j in range(nblk)]
        x = jnp.concatenate([x_ref[0, :, c].astype(F32) for c in blocks], axis=0)
        ms = _group_mean(x * x, _group_mean_matrix(group))
        y = (x * lax.rsqrt(ms + EPS)).reshape(nblk, tm, LANE) * g_ref[col0:col0 + nblk]
        if rope:
            part = jnp.dot(y.reshape(nblk * tm, LANE).astype(BF16), _rope_partner_matrix(half),
                           preferred_element_type=F32).reshape(nblk, tm, LANE)
            y = y * cos_ref[...][None] + part * sin_ref[...][None]
        for j, c in enumerate(blocks):
            o_ref[0, :, c] = y[j].astype(BF16)

    qk_blocks(COL_AQ, COL_AV - COL_AQ, HEAD_DIM, 32, ca_ref if rope else None, sa_ref if rope else None)
    qk_blocks(COL_BQ, COL_BV - COL_BQ, B_DIM, 16, cb_ref if rope else None, sb_ref if rope else None)
    for c0, c1 in ((COL_AV, COL_BQ), (COL_BV, N_QKV_BLOCKS)):
        o_ref[0, :, c0 * LANE:c1 * LANE] = x_ref[0, :, c0 * LANE:c1 * LANE]


def _prep(proj, gains, rope_tabs):
    nb, r, _ = proj.shape
    tm = _tile(r, 256)
    width = N_QKV_BLOCKS * LANE
    rope = rope_tabs is not None
    in_specs = [
        pl.BlockSpec((1, tm, width), lambda b, t: (b, t, 0)),
        pl.BlockSpec((N_QKV_BLOCKS, 1, LANE), lambda b, t: (0, 0, 0)),
    ]
    args = [proj, gains]
    if rope:
        in_specs += [pl.BlockSpec((tm, LANE), lambda b, t: (t, 0))] * 4
        args += list(rope_tabs)
    return pl.pallas_call(
        functools.partial(_prep_kernel, rope=rope),
        grid=(nb, r // tm),
        in_specs=in_specs,
        out_specs=pl.BlockSpec((1, tm, width), lambda b, t: (b, t, 0)),
        out_shape=jax.ShapeDtypeStruct((nb, r, width), BF16),
        compiler_params=_cparams("parallel", "parallel"),
        name="qkv_prep",
    )(*args)


def _rope_tables(seq):
    pos = jnp.arange(seq)
    rows = (pos // GRID_W).astype(F32)[:, None]
    cols = (pos % GRID_W).astype(F32)[:, None]
    lane = jnp.arange(LANE)

    def tables(half):
        inv = ROPE_THETA ** (-(lane % half).astype(F32) / half)
        use_rows = (lane % (4 * half)) < 2 * half
        ang = jnp.where(use_rows[None, :], rows, cols) * inv[None, :]
        sign = jnp.where((lane % (2 * half)) < half, -1.0, 1.0)
        return jnp.cos(ang), jnp.sin(ang) * sign[None, :]

    ca, sa = tables(32)
    cb, sb = tables(16)
    return ca, sa, cb, sb


def _stack_heads(q):
    return jnp.concatenate([q[:, g * HEAD_DIM:(g + 1) * HEAD_DIM] for g in range(A_GROUP)], axis=0)


def _unstack_heads(o, rows):
    return jnp.concatenate([o[g * rows:(g + 1) * rows] for g in range(A_GROUP)], axis=1)


def _sink_column(sink_ref, kvh, rows):
    return jnp.concatenate(
        [jnp.full((rows, 1), sink_ref[kvh * A_GROUP + g] * LOG2E, F32) for g in range(A_GROUP)], axis=0)


def _attn_a_lat_kernel(sink_ref, q_ref, bias_ref, kl_ref, vl_ref, kc_ref, vc_ref, o_ref, *, seq):
    band = 3 * Q_BLOCK
    nblk = seq // Q_BLOCK
    for qi in range(q_ref.shape[1] // Q_BLOCK):
        i = pl.program_id(1) * (q_ref.shape[1] // Q_BLOCK) + qi
        rows = slice(qi * Q_BLOCK, (qi + 1) * Q_BLOCK)
        first = jnp.clip(i - 1, 0, nblk - 3)
        start = pl.multiple_of(first * Q_BLOCK, Q_BLOCK)
        bias = bias_ref[i - first]
        outs = []
        for kvh in range(A_KV_HEADS):
            hs = slice(kvh * HEAD_DIM, (kvh + 1) * HEAD_DIM)
            qs = _stack_heads(q_ref[0, rows, kvh * A_GROUP * HEAD_DIM:(kvh + 1) * A_GROUP * HEAD_DIM])
            kb = kl_ref[0, pl.ds(start, band), hs]
            vb = vl_ref[0, pl.ds(start, band), hs]
            s_loc = lax.dot_general(qs, kb, NT_DIMS, preferred_element_type=F32)
            s_loc = (s_loc.reshape(A_GROUP, Q_BLOCK, band) + bias[None]).reshape(A_GROUP * Q_BLOCK, band)
            s_ctx = lax.dot_general(qs, kc_ref[0, :, hs], NT_DIMS, preferred_element_type=F32)
            sk = _sink_column(sink_ref, kvh, Q_BLOCK)
            m = jnp.maximum(jnp.maximum(jnp.max(s_loc, axis=-1, keepdims=True),
                                        jnp.max(s_ctx, axis=-1, keepdims=True)), sk)
            p_loc = jnp.exp2(s_loc - m)
            p_ctx = jnp.exp2(s_ctx - m)
            den = (jnp.sum(p_loc, axis=-1, keepdims=True) + jnp.sum(p_ctx, axis=-1, keepdims=True)
                   + jnp.exp2(sk - m))
            o = (jnp.dot(p_loc.astype(BF16), vb, preferred_element_type=F32)
                 + jnp.dot(p_ctx.astype(BF16), vc_ref[0, :, hs], preferred_element_type=F32)) / den
            outs.append(_unstack_heads(o, Q_BLOCK))
        o_ref[0, rows, :] = jnp.concatenate(outs, axis=1).astype(BF16)


def _attn_a_ctx_kernel(sink_ref, q_ref, kc_ref, vc_ref, o_ref):
    kvh = pl.program_id(1)
    rows = q_ref.shape[1]
    qs = _stack_heads(q_ref[0])
    s = lax.dot_general(qs, kc_ref[0], NT_DIMS, preferred_element_type=F32)
    sk = _sink_column(sink_ref, kvh, rows)
    m = jnp.maximum(jnp.max(s, axis=-1, keepdims=True), sk)
    p = jnp.exp2(s - m)
    den = jnp.sum(p, axis=-1, keepdims=True) + jnp.exp2(sk - m)
    o = jnp.dot(p.astype(BF16), vc_ref[0], preferred_element_type=F32) / den
    o_ref[0] = _unstack_heads(o, rows).astype(BF16)


def _window_bias():
    band = 3 * Q_BLOCK
    row = jnp.arange(Q_BLOCK)[None, :, None]
    col = jnp.arange(band)[None, None, :]
    delta = (jnp.arange(3) * Q_BLOCK)[:, None, None]
    return jnp.where(jnp.abs(col - delta - row) <= WINDOW, 0.0, NEG_BIG).astype(F32)


def _attn_a(sink, p_l, p_c, ctx_out):
    b, seq, _ = p_l.shape
    lc = p_c.shape[1]
    gw = A_GROUP * HEAD_DIM
    nblk = seq // Q_BLOCK
    kvw = A_KV_HEADS * HEAD_DIM
    smem = pl.BlockSpec(memory_space=pltpu.SMEM)

    qrows = A_Q_PER_STEP * Q_BLOCK if nblk % A_Q_PER_STEP == 0 else Q_BLOCK
    o_l = pl.pallas_call(
        functools.partial(_attn_a_lat_kernel, seq=seq),
        grid=(b, seq // qrows),
        in_specs=[
            smem,
            pl.BlockSpec((1, qrows, N_HEADS * HEAD_DIM), lambda n, i: (n, i, 0)),
            pl.BlockSpec((3, Q_BLOCK, 3 * Q_BLOCK), lambda n, i: (0, 0, 0)),
            pl.BlockSpec((1, seq, kvw), lambda n, i: (n, 0, COL_AK // A_KV_HEADS)),
            pl.BlockSpec((1, seq, kvw), lambda n, i: (n, 0, COL_AV // A_KV_HEADS)),
            pl.BlockSpec((1, lc, kvw), lambda n, i: (n, 0, COL_AK // A_KV_HEADS)),
            pl.BlockSpec((1, lc, kvw), lambda n, i: (n, 0, COL_AV // A_KV_HEADS)),
        ],
        out_specs=pl.BlockSpec((1, qrows, N_HEADS * HEAD_DIM), lambda n, i: (n, i, 0)),
        out_shape=jax.ShapeDtypeStruct((b, seq, N_HEADS * HEAD_DIM), BF16),
        compiler_params=_cparams("parallel", "arbitrary"),
        name="attn_a_latent",
    )(sink, p_l, _window_bias(), p_l, p_l, p_c, p_c)
    o_c = None
    if ctx_out:
        o_c = pl.pallas_call(
            _attn_a_ctx_kernel,
            grid=(b, A_KV_HEADS),
            in_specs=[
                smem,
                pl.BlockSpec((1, lc, gw), lambda n, h: (n, 0, h)),
                pl.BlockSpec((1, lc, LANE), lambda n, h: (n, 0, COL_AK + h)),
                pl.BlockSpec((1, lc, LANE), lambda n, h: (n, 0, COL_AV + h)),
            ],
            out_specs=pl.BlockSpec((1, lc, gw), lambda n, h: (n, 0, h)),
            out_shape=jax.ShapeDtypeStruct((b, lc, N_HEADS * HEAD_DIM), BF16),
            compiler_params=_cparams("parallel", "parallel"),
            name="attn_a_ctx",
        )(sink, p_c, p_c, p_c)
    return o_c, o_l


def _attn_b_kernel(lam_ref, q_ref, *refs, nseg, post_scale):
    k_refs = refs[:nseg]
    v_refs = refs[nseg:2 * nseg]
    g_ref, o_ref = refs[2 * nseg], refs[2 * nseg + 1]
    q = q_ref[0]
    tq = q.shape[0]
    lane = lax.broadcasted_iota(jnp.int32, q.shape, 1)
    zero = jnp.zeros_like(q)
    q2 = jnp.concatenate([jnp.where(lane < B_DIM, q, zero), jnp.where(lane < B_DIM, zero, q)], axis=0)
    m = jnp.full((2 * tq, 1), NEG_BIG, F32)
    den = jnp.zeros((2 * tq, 1), F32)
    acc = jnp.zeros((2 * tq, LANE), F32)
    for k_ref, v_ref in zip(k_refs, v_refs):
        nk = k_ref.shape[1]
        kc = _tile(nk, B_KEY_CHUNK)
        for c0 in range(0, nk, kc):
            s = lax.dot_general(q2, k_ref[0, c0:c0 + kc, :], NT_DIMS, preferred_element_type=F32)
            m_new = jnp.maximum(m, jnp.max(s, axis=-1, keepdims=True))
            alpha = jnp.exp2(m - m_new)
            p = jnp.exp2(s - m_new)
            den = alpha * den + jnp.sum(p, axis=-1, keepdims=True)
            acc = alpha * acc + jnp.dot(p.astype(BF16), v_ref[0, c0:c0 + kc, :],
                                        preferred_element_type=F32)
            m = m_new
    o2 = acc / den
    o = o2[:tq] - lam_ref[0] * o2[tq:]
    ms = jnp.mean(o * o, axis=-1, keepdims=True)
    o_ref[0] = ((o * lax.rsqrt(ms + EPS) * g_ref[...]) * post_scale).astype(BF16)


def _attn_b_call(lam, q_src, k_srcs, subln, post_scale, name):
    b, rq, _ = q_src.shape
    tq = _tile(rq, 512)
    smem = pl.BlockSpec(memory_space=pltpu.SMEM)
    in_specs = [smem, pl.BlockSpec((1, tq, LANE), lambda n, h, i: (n, i, COL_BQ + h))]
    in_specs += [pl.BlockSpec((1, s.shape[1], LANE), lambda n, h, i: (n, 0, COL_BK + h)) for s in k_srcs]
    in_specs += [pl.BlockSpec((1, s.shape[1], LANE), lambda n, h, i: (n, 0, COL_BV + h)) for s in k_srcs]
    in_specs += [pl.BlockSpec((1, LANE), lambda n, h, i: (0, 0))]
    return pl.pallas_call(
        functools.partial(_attn_b_kernel, nseg=len(k_srcs), post_scale=post_scale),
        grid=(b, N_HEADS, rq // tq),
        in_specs=in_specs,
        out_specs=pl.BlockSpec((1, tq, LANE), lambda n, h, i: (n, i, h)),
        out_shape=jax.ShapeDtypeStruct((b, rq, N_HEADS * HEAD_DIM), BF16),
        compiler_params=_cparams("parallel", "parallel", "arbitrary"),
        name=name,
    )(lam, q_src, *k_srcs, *k_srcs, subln)


def _attn_b(lam, p_l, p_c, subln, post_scale, ctx_out):
    o_l = _attn_b_call(lam, p_l, [p_c, p_l], subln, post_scale, "attn_b_latent")
    o_c = _attn_b_call(lam, p_c, [p_c], subln, post_scale, "attn_b_ctx") if ctx_out else None
    return o_c, o_l


def _centred_conv(x_ref, pad_ref, w_ref, b_ref):
    n = x_ref.shape[1]
    zeros = jnp.zeros((SUBLANE, LANE), F32)
    pad_ref[0:SUBLANE, :] = zeros
    pad_ref[SUBLANE:SUBLANE + n, :] = x_ref[0]
    pad_ref[SUBLANE + n:2 * SUBLANE + n, :] = zeros
    out = b_ref[...]
    for tap in range(4):
        out = out + pad_ref[pl.ds(SUBLANE - 2 + tap, n), :] * w_ref[tap:tap + 1, :]
    return out


def _every_8th(ref, d, j, n):
    return ref[d, pl.ds(j, n, stride=SUBLANE), :]


def _tile_order(reverse):
    return range(SUBLANE - 1, -1, -1) if reverse else range(SUBLANE)


def _scan_summaries(a_ref, v_ref, d, n, reverse):
    h = p = None
    for j in _tile_order(reverse):
        a, v = _every_8th(a_ref, d, j, n), _every_8th(v_ref, d, j, n)
        h, p = (v, a) if h is None else (a * h + v, a * p)
    return p, h


def _scan_entering(a_ref, v_ref, d, n, carry_in, reverse, store):
    e = carry_in
    for j in _tile_order(reverse):
        store(j, e)
        e = _every_8th(a_ref, d, j, n) * e + _every_8th(v_ref, d, j, n)


def _scan_apply(a_ref, v_ref, d, n, carry_in, reverse, store):
    h = carry_in
    for j in _tile_order(reverse):
        h = _every_8th(a_ref, d, j, n) * h + _every_8th(v_ref, d, j, n)
        store(j, h)


def _seg_tile(t, ntc, ntl, reverse):
    if not reverse:
        return t
    return jnp.where(t < ntc, ntc - 1 - t, 2 * ntc + ntl - 1 - t)


def _lru_kernel(xc_ref, yc_ref, xl_ref, yl_ref, cw_ref, cb_ref, wr_ref, br_ref, wi_ref, bi_ref,
                lam_ref, *refs, ctx_out):
    if ctx_out:
        oc_ref, ol_ref, ac_ref, hl_ref, hs_ref, pad_ref, tp_ref, th_ref, sp_ref, sh_ref, c3_ref, e1_ref = refs
    else:
        ol_ref, ac_ref, hl_ref, hs_ref, pad_ref, tp_ref, th_ref, sp_ref, sh_ref, c3_ref, e1_ref = refs
    lc, seq = xc_ref.shape[1], xl_ref.shape[1]
    for x_ref, off in ((xc_ref, 0), (xl_ref, lc)):
        n = x_ref.shape[1]
        u = _centred_conv(x_ref, pad_ref, cw_ref, cb_ref)
        ub = u.astype(BF16)
        for d in range(2):
            lam = lam_ref[d:d + 1, :]
            sp = jnp.maximum(-lam, 0.0) + jnp.log1p(jnp.exp(-jnp.abs(lam)))
            r = _sigmoid(jnp.dot(ub, wr_ref[d, 0], preferred_element_type=F32) + br_ref[d:d + 1, :])
            gi = _sigmoid(jnp.dot(ub, wi_ref[d, 0], preferred_element_type=F32) + bi_ref[d:d + 1, :])
            log_a = -LRU_C * r * sp
            a = jnp.exp(log_a)
            y = jnp.tanh(-log_a) * (1.0 + a * a)
            v = (y * lax.rsqrt(jnp.maximum(y, TINY))) * gi * u
            ac_ref[d, off:off + n, :] = a
            hl_ref[d, off:off + n, :] = v

    nt1 = (lc + seq) // SUBLANE
    nt2 = nt1 // SUBLANE
    group_rows = SUBLANE * SUBLANE
    for d in range(2):
        tp_ref[d], th_ref[d] = _scan_summaries(ac_ref, hl_ref, d, nt1, d == 1)
        sp_ref[d], sh_ref[d] = _scan_summaries(tp_ref, th_ref, d, nt2, d == 1)

    def step(s, carry):
        new = []
        for d in range(2):
            g = pl.ds(_seg_tile(s, lc // group_rows, seq // group_rows, d == 1), 1)
            c3_ref[d, g, :] = carry[d]
            new.append(sp_ref[d, g, :] * carry[d] + sh_ref[d, g, :])
        return tuple(new)

    zero = jnp.zeros((1, LANE), F32)
    lax.fori_loop(0, nt2, step, (zero, zero))

    for d in range(2):
        def store_entering(j, e, d=d):
            e1_ref[d, pl.ds(j, nt2, stride=SUBLANE), :] = e

        def store_h(j, h, d=d):
            if d == 0:
                hs_ref[pl.ds(j, nt1, stride=SUBLANE), :] = h
            else:
                hl_ref[d, pl.ds(j, nt1, stride=SUBLANE), :] = h

        _scan_entering(tp_ref, th_ref, d, nt2, c3_ref[d], d == 1, store_entering)
        _scan_apply(ac_ref, hl_ref, d, nt1, e1_ref[d], d == 1, store_h)
    if ctx_out:
        hc = hs_ref[0:lc, :] + hl_ref[1, 0:lc, :]
        oc_ref[0] = (hc * jax.nn.gelu(yc_ref[0])).astype(BF16)
    hl = hs_ref[lc:lc + seq, :] + hl_ref[1, lc:lc + seq, :]
    ol_ref[0] = (hl * jax.nn.gelu(yl_ref[0])).astype(BF16)


def _lru(proj_l, proj_c, conv_w, conv_b, w_r, b_r, w_i, b_i, lam, ctx_out):
    b, seq, _ = proj_l.shape
    lc = proj_c.shape[1]
    nblk = w_r.shape[1]
    width = nblk * LANE
    nt1 = (lc + seq) // SUBLANE
    nt2 = nt1 // SUBLANE

    def seg(n, col):
        return pl.BlockSpec((1, n, LANE), lambda i, j: (i, 0, col + j))

    vec2 = pl.BlockSpec((2, LANE), lambda i, j: (0, j))
    wspec = pl.BlockSpec((2, 1, LANE, LANE), lambda i, j: (0, j, 0, 0))
    out_specs = [pl.BlockSpec((1, seq, LANE), lambda i, j: (i, 0, j))]
    out_shape = [jax.ShapeDtypeStruct((b, seq, width), BF16)]
    if ctx_out:
        out_specs.insert(0, pl.BlockSpec((1, lc, LANE), lambda i, j: (i, 0, j)))
        out_shape.insert(0, jax.ShapeDtypeStruct((b, lc, width), BF16))
    outs = pl.pallas_call(
        functools.partial(_lru_kernel, ctx_out=ctx_out),
        grid=(b, nblk),
        in_specs=[
            seg(lc, COL_CX), seg(lc, COL_CY), seg(seq, COL_CX), seg(seq, COL_CY),
            pl.BlockSpec((4, LANE), lambda i, j: (0, j)),
            pl.BlockSpec((1, LANE), lambda i, j: (0, j)),
            wspec, vec2, wspec, vec2, vec2,
        ],
        out_specs=out_specs,
        out_shape=out_shape,
        scratch_shapes=[
            pltpu.VMEM((2, lc + seq, LANE), F32),
            pltpu.VMEM((2, lc + seq, LANE), F32),
            pltpu.VMEM((lc + seq, LANE), F32),
            pltpu.VMEM((max(lc, seq) + 2 * SUBLANE, LANE), F32),
            pltpu.VMEM((2, nt1, LANE), F32),
            pltpu.VMEM((2, nt1, LANE), F32),
            pltpu.VMEM((2, nt2, LANE), F32),
            pltpu.VMEM((2, nt2, LANE), F32),
            pltpu.VMEM((2, nt2, LANE), F32),
            pltpu.VMEM((2, nt1, LANE), F32),
        ],
        compiler_params=_cparams("parallel", "parallel"),
        name="rglru",
    )(proj_c, proj_c, proj_l, proj_l, conv_w, conv_b.reshape(1, width), w_r, b_r, w_i, b_i, lam)
    return (outs[0], outs[1]) if ctx_out else (None, outs[0])


def _gla_level_masks(reverse):
    c = GLA_CHUNK
    row = lax.broadcasted_iota(jnp.int32, (c, c), 0)
    col = lax.broadcasted_iota(jnp.int32, (c, c), 1)
    masks = {}
    for s in (32, 16, 8):
        same = (row // (2 * s)) == (col // (2 * s))
        if reverse:
            masks[s] = same & ((row % (2 * s)) < s) & ((col % (2 * s)) >= s)
        else:
            masks[s] = same & ((row % (2 * s)) >= s) & ((col % (2 * s)) < s)
    tri = (row <= col) if reverse else (row >= col)
    return masks, tri.astype(F32)


def _gla_chunk(q, k, v, g, st, masks, tri, reverse):
    c = GLA_CHUNK
    b = jnp.dot(tri, g, precision=lax.Precision.HIGHEST, preferred_element_type=F32)
    att = jnp.zeros((c, c), F32)
    for s in (32, 16, 8):
        b3 = b.reshape(c // (2 * s), 2 * s, LANE)
        rr = s if reverse else s - 1
        rho = jnp.broadcast_to(b3[:, rr:rr + 1, :], b3.shape).reshape(c, LANE)
        e = jnp.exp(-jnp.abs(b - rho))
        a = lax.dot_general((q * e).astype(BF16), (k * e).astype(BF16), NT_DIMS,
                            preferred_element_type=F32)
        att = att + jnp.where(masks[s], a, 0.0)
    vb = v.astype(BF16)
    o = jnp.dot(att.astype(BF16), vb, preferred_element_type=F32)
    nb = c // SUBLANE
    b3 = b.reshape(nb, SUBLANE, LANE)
    q3 = q.reshape(nb, SUBLANE, LANE)
    k3 = k.reshape(nb, SUBLANE, LANE)
    v3 = v.reshape(nb, SUBLANE, LANE)
    rowi = lax.broadcasted_iota(jnp.int32, b3.shape, 1)
    od = jnp.zeros(b3.shape, F32)
    for jj in range(SUBLANE):
        keep = (rowi <= jj) if reverse else (rowi >= jj)
        e = jnp.where(keep, jnp.exp(jnp.minimum(b3 - b3[:, jj:jj + 1, :], 0.0)), 0.0)
        sj = jnp.sum(q3 * e * k3[:, jj:jj + 1, :], axis=-1, keepdims=True)
        od = od + sj * v3[:, jj:jj + 1, :]
    o = o + od.reshape(c, LANE)
    o = o + lax.dot_general((q * jnp.exp(b)).astype(BF16), st.astype(BF16), NT_DIMS,
                            preferred_element_type=F32)
    b_end = b[0:1, :] if reverse else b[c - 1:c, :]
    khat = (k * jnp.exp(b_end - b)).astype(BF16)
    st_new = st * jnp.exp(b_end) + lax.dot_general(vb, khat, TN_DIMS, preferred_element_type=F32)
    return o, st_new


def _gla_fast_intra(q, k, v, g, tri, reverse):
    c = GLA_CHUNK
    g_hi = g.astype(BF16)
    g_lo = (g - g_hi.astype(F32)).astype(BF16)
    b2 = jnp.dot(tri.astype(BF16), jnp.concatenate([g_hi, g_lo], axis=1), preferred_element_type=F32)
    b = b2[:, :LANE] + b2[:, LANE:]
    rr = c // 2 if reverse else c // 2 - 1
    rho = b[rr:rr + 1, :]
    qt = q * jnp.exp(b - rho)
    kt = k * jnp.exp(rho - b)
    a = lax.dot_general(qt.astype(BF16), kt.astype(BF16), NT_DIMS, preferred_element_type=F32)
    att = jnp.where(tri > 0.0, a, 0.0)
    vb = v.astype(BF16)
    o_intra = jnp.dot(att.astype(BF16), vb, preferred_element_type=F32)
    b_end = b[0:1, :] if reverse else b[c - 1:c, :]
    qe = (qt * jnp.exp(rho)).astype(BF16)
    khat = (kt * jnp.exp(b_end - rho)).astype(BF16)
    return o_intra, qe, khat, vb, jnp.exp(b_end)


def _gla_fast_inter(intra, st):
    o_intra, qe, khat, vb, decay = intra
    o = o_intra + lax.dot_general(qe, st.astype(BF16), NT_DIMS, preferred_element_type=F32)
    st_new = st * decay + lax.dot_general(vb, khat, TN_DIMS, preferred_element_type=F32)
    return o, st_new


def _gla_kernel(lb_ref, on_ref, qc_ref, ffc_ref, fbc_ref, ic_ref, gc_ref,
                ql_ref, ffl_ref, fbl_ref, il_ref, gl_ref, *refs, ctx_out):
    if ctx_out:
        oc_ref, ol_ref, q_s, v_s, g_s, k_s, o_s, st_s = refs
    else:
        ol_ref, q_s, v_s, g_s, k_s, o_s, st_s = refs
    lc, seq = qc_ref.shape[1], ql_ref.shape[1]
    ncc, ncl = lc // GLA_CHUNK, seq // GLA_CHUNK
    for off, n, q_ref, i_ref, f_refs in ((0, lc, qc_ref, ic_ref, (ffc_ref, fbc_ref)),
                                         (lc, seq, ql_ref, il_ref, (ffl_ref, fbl_ref))):
        q_s[off:off + n, :] = q_ref[0]
        v_s[off:off + n, :] = i_ref[0]
        for d in range(2):
            z = f_refs[d][0]
            lbd = lb_ref[d:d + 1, :]
            ez = jnp.exp(-jnp.abs(z))
            r = 1.0 / (1.0 + ez)
            pos = z >= 0.0
            sig_p = jnp.where(pos, r, ez * r)
            sig_n = jnp.where(pos, ez * r, r)
            g_s[d, off:off + n, :] = jnp.log(lbd + (1.0 - lbd) * sig_p)
            k_s[d, off:off + n, :] = (1.0 - lbd) * sig_n
    st_s[...] = jnp.zeros(st_s.shape, F32)
    consts = [_gla_level_masks(False), _gla_level_masks(True)]

    half = GLA_CHUNK // 2
    worst = jnp.zeros((1, LANE), F32)
    for d in range(2):
        hs = jnp.sum(g_s[d].reshape((lc + seq) // half, half, LANE), axis=1)
        worst = jnp.maximum(worst, jnp.max(-hs, axis=0, keepdims=True))
    safe = jnp.max(worst) < GLA_SAFE_DECAY

    def chunk_rows(cidx, d):
        chunk = _seg_tile(cidx, ncc, ncl, d == 1)
        return pl.ds(pl.multiple_of(chunk * GLA_CHUNK, GLA_CHUNK), GLA_CHUNK)

    def robust_step(cidx, carry):
        for d in range(2):
            rows = chunk_rows(cidx, d)
            o, st_new = _gla_chunk(q_s[rows, :], k_s[d, rows, :], v_s[rows, :], g_s[d, rows, :],
                                   st_s[d], consts[d][0], consts[d][1], d == 1)
            st_s[d] = st_new
            o_s[d, rows, :] = o
        return carry

    nchunks = ncc + ncl
    group = next(u for u in (12, 6, 4, 3, 2, 1) if nchunks % u == 0)

    def fast_step(t, carry):
        work = [[] for _ in range(2)]
        for d in range(2):
            for u in range(group):
                rows = chunk_rows(t * group + u, d)
                work[d].append((rows, _gla_fast_intra(q_s[rows, :], k_s[d, rows, :], v_s[rows, :],
                                                      g_s[d, rows, :], consts[d][1], d == 1)))
        for d in range(2):
            st = st_s[d]
            outs = []
            for rows, intra in work[d]:
                o, st = _gla_fast_inter(intra, st)
                outs.append((rows, o))
            st_s[d] = st
            for rows, o in outs:
                o_s[d, rows, :] = o
        return carry

    @pl.when(safe)
    def _():
        lax.fori_loop(0, nchunks // group, fast_step, 0)

    @pl.when(jnp.logical_not(safe))
    def _():
        lax.fori_loop(0, nchunks, robust_step, 0)

    def finish(o, gate):
        ms = jnp.mean(o * o, axis=-1, keepdims=True)
        return ((o * lax.rsqrt(ms + EPS) * on_ref[...]) * (gate * _sigmoid(gate))).astype(BF16)

    if ctx_out:
        oc_ref[0] = finish(o_s[0, 0:lc, :] + o_s[1, 0:lc, :], gc_ref[0])
    ol_ref[0] = finish(o_s[0, lc:lc + seq, :] + o_s[1, lc:lc + seq, :], gl_ref[0])


def _gla(proj_l, proj_c, lb, onorm, ctx_out):
    b, seq, _ = proj_l.shape
    lc = proj_c.shape[1]
    width = N_HEADS * HEAD_DIM
    nt = lc + seq

    def seg(n, col):
        return pl.BlockSpec((1, n, LANE), lambda i, h: (i, 0, col + h))

    cols = (COL_DQ, COL_DFF, COL_DFB, COL_DI, COL_DG)
    out_specs = [pl.BlockSpec((1, seq, LANE), lambda i, h: (i, 0, h))]
    out_shape = [jax.ShapeDtypeStruct((b, seq, width), BF16)]
    if ctx_out:
        out_specs.insert(0, pl.BlockSpec((1, lc, LANE), lambda i, h: (i, 0, h)))
        out_shape.insert(0, jax.ShapeDtypeStruct((b, lc, width), BF16))
    outs = pl.pallas_call(
        functools.partial(_gla_kernel, ctx_out=ctx_out),
        grid=(b, N_HEADS),
        in_specs=[pl.BlockSpec((2, LANE), lambda i, h: (0, h)),
                  pl.BlockSpec((1, LANE), lambda i, h: (0, 0))]
        + [seg(lc, c) for c in cols] + [seg(seq, c) for c in cols],
        out_specs=out_specs,
        out_shape=out_shape,
        scratch_shapes=[
            pltpu.VMEM((nt, LANE), F32),
            pltpu.VMEM((nt, LANE), F32),
            pltpu.VMEM((2, nt, LANE), F32),
            pltpu.VMEM((2, nt, LANE), F32),
            pltpu.VMEM((2, nt, LANE), F32),
            pltpu.VMEM((2, LANE, LANE), F32),
        ],
        compiler_params=_cparams("parallel", "parallel"),
        name="hgrn2",
    )(lb, onorm, *([proj_c] * 5), *([proj_l] * 5))
    return (outs[0], outs[1]) if ctx_out else (None, outs[0])


def _merge_kernel(oa_ref, ob_ref, oc_ref, od_ref, g0_ref, g1_ref, g2_ref, g3_ref,
                  w0_ref, w1_ref, w2_ref, w3_ref, y_ref):
    acc = None
    for o_ref, g_ref, w_ref in ((oa_ref, g0_ref, w0_ref), (ob_ref, g1_ref, w1_ref),
                                (oc_ref, g2_ref, w2_ref), (od_ref, g3_ref, w3_ref)):
        t = _sigmoid(g_ref[0].astype(F32)) * jnp.dot(o_ref[0], w_ref[0], preferred_element_type=F32)
        acc = t if acc is None else acc + t
    y_ref[0] = acc.astype(BF16)


def _merge(outs, proj, w_branch, layer):
    nb, r, mw = outs[0].shape
    d = w_branch.shape[2]
    tm = _tile(r, 512)
    tn = 512
    nj = d // tn
    o_spec = pl.BlockSpec((1, tm, mw), lambda b, t, j: (b, t, 0))
    g_specs = [pl.BlockSpec((1, tm, tn), functools.partial(
        lambda b, t, j, n: (b, t, (COL_GATE * LANE) // tn + n * nj + j), n=n)) for n in range(4)]
    w_specs = [pl.BlockSpec((1, mw, tn), functools.partial(lambda b, t, j, n: (layer * 4 + n, 0, j), n=n))
               for n in range(4)]
    return pl.pallas_call(
        _merge_kernel,
        grid=(nb, r // tm, nj),
        in_specs=[o_spec] * 4 + g_specs + w_specs,
        out_specs=pl.BlockSpec((1, tm, tn), lambda b, t, j: (b, t, j)),
        out_shape=jax.ShapeDtypeStruct((nb, r, d), BF16),
        compiler_params=_cparams("parallel", "parallel", "arbitrary"),
        name="branch_merge",
    )(*outs, *([proj] * 4), *([w_branch] * 4))


def _out_proj_kernel(y_ref, w_ref, x_ref, g_ref, o_ref):
    o_ref[0] = x_ref[0] + g_ref[0] * jnp.dot(y_ref[0], w_ref[0], preferred_element_type=F32)


def _out_proj(y, w, layer, x, gate):
    nb, r, d = x.shape
    tm = _tile(r, 1024)
    tn = 512
    return pl.pallas_call(
        _out_proj_kernel,
        grid=(nb, r // tm, d // tn),
        in_specs=[
            pl.BlockSpec((1, tm, d), lambda b, t, j: (b, t, 0)),
            pl.BlockSpec((1, d, tn), lambda b, t, j: (layer, 0, j)),
            pl.BlockSpec((1, tm, tn), lambda b, t, j: (b, t, j)),
            pl.BlockSpec((1, 1, tn), lambda b, t, j: (b, 0, j)),
        ],
        out_specs=pl.BlockSpec((1, tm, tn), lambda b, t, j: (b, t, j)),
        out_shape=jax.ShapeDtypeStruct((nb, r, d), F32),
        compiler_params=_cparams("parallel", "parallel", "arbitrary"),
        name="out_proj",
    )(y, w, x, gate)


def _route(logits_t, bias_col):
    aff = _sigmoid(logits_t)
    sel = aff + bias_col
    aff_r = [aff[e:e + 1, :] for e in range(N_EXPERTS)]
    sel_r = [sel[e:e + 1, :] for e in range(N_EXPERTS)]
    scores = []
    for g in range(N_GROUPS):
        v = sel_r[g * EXPERTS_PER_GROUP:(g + 1) * EXPERTS_PER_GROUP]
        m1 = functools.reduce(jnp.maximum, v)
        taken = jnp.zeros(m1.shape, jnp.bool_)
        second = jnp.full(m1.shape, -jnp.inf, F32)
        for x in v:
            first = (x == m1) & jnp.logical_not(taken)
            taken = taken | first
            second = jnp.where(first, second, jnp.maximum(second, x))
        scores.append(m1 + second)
    best, gidx = scores[0], jnp.zeros(scores[0].shape, jnp.int32)
    for g in range(1, N_GROUPS):
        better = scores[g] > best
        gidx = jnp.where(better, g, gidx)
        best = jnp.where(better, scores[g], best)
    masked = [jnp.where(gidx == e // EXPERTS_PER_GROUP, sel_r[e], -jnp.inf) for e in range(N_EXPERTS)]

    def first_argmax(vals, exclude):
        bv = jnp.full(vals[0].shape, -jnp.inf, F32)
        bi = jnp.full(vals[0].shape, -1, jnp.int32)
        for e, x in enumerate(vals):
            better = x > bv
            if exclude is not None:
                better = better & (exclude != e)
            bi = jnp.where(better, e, bi)
            bv = jnp.where(better, x, bv)
        return bi

    i1 = first_argmax(masked, None)
    i2 = first_argmax(masked, i1)
    w1 = functools.reduce(jnp.add, [jnp.where(i1 == e, aff_r[e], 0.0) for e in range(N_EXPERTS)])
    w2 = functools.reduce(jnp.add, [jnp.where(i2 == e, aff_r[e], 0.0) for e in range(N_EXPERTS)])
    tot = w1 + w2
    g1, g2 = w1 / tot, w2 / tot
    width = logits_t.shape[1]
    rowi = lax.broadcasted_iota(jnp.int32, (SUBLANE, width), 0)
    local1 = jnp.broadcast_to(i1 - gidx * EXPERTS_PER_GROUP, (SUBLANE, width))
    local2 = jnp.broadcast_to(i2 - gidx * EXPERTS_PER_GROUP, (SUBLANE, width))
    info = (jnp.where(rowi == local1, jnp.broadcast_to(g1, (SUBLANE, width)), 0.0)
            + jnp.where(rowi == local2, jnp.broadcast_to(g2, (SUBLANE, width)), 0.0))
    return jnp.where(rowi == EXPERTS_PER_GROUP,
                     jnp.broadcast_to(gidx.astype(F32), (SUBLANE, width)), info)


def _moe_route_kernel(x_ref, g_ref, sh_ref, sc_ref, wrt_ref, br_ref, hn_ref, info_ref):
    d = x_ref.shape[2]
    h = _modulated_norm(x_ref[0], g_ref[...], sh_ref[0], sc_ref[0])
    logits_t = lax.dot_general(wrt_ref[...], h, NT_DIMS, precision=lax.Precision.HIGHEST,
                               preferred_element_type=F32)
    info = _route(logits_t, br_ref[...])
    info_ref[0] = info
    hn_ref[0, :, 0:d] = h
    pad = jnp.zeros((LANE - SUBLANE, info.shape[1]), F32)
    hn_ref[0, :, d:d + LANE] = jnp.concatenate([info, pad], axis=0).T


def _moe_route(x, g, sh, sc, w_router_t, b_router):
    nb, r, d = x.shape
    tm = _tile(r, 512)
    vec = pl.BlockSpec((1, 1, d), lambda b, t: (b, 0, 0))
    return pl.pallas_call(
        _moe_route_kernel,
        grid=(nb, r // tm),
        in_specs=[
            pl.BlockSpec((1, tm, d), lambda b, t: (b, t, 0)),
            pl.BlockSpec((1, d), lambda b, t: (0, 0)),
            vec, vec,
            pl.BlockSpec((N_EXPERTS, d), lambda b, t: (0, 0)),
            pl.BlockSpec((N_EXPERTS, 1), lambda b, t: (0, 0)),
        ],
        out_specs=[pl.BlockSpec((1, tm, d + LANE), lambda b, t: (b, t, 0)),
                   pl.BlockSpec((1, SUBLANE, tm), lambda b, t: (b, 0, t))],
        out_shape=[jax.ShapeDtypeStruct((nb, r, d + LANE), F32),
                   jax.ShapeDtypeStruct((nb, SUBLANE, r), F32)],
        compiler_params=_cparams("parallel", "parallel"),
        name="moe_route",
    )(x, g, sh, sc, w_router_t, b_router.reshape(N_EXPERTS, 1))


def _row_copy(src, src_row, dst, dst_row, sem):
    return pltpu.make_async_copy(src.at[pl.ds(src_row, 1), :], dst.at[pl.ds(dst_row, 1), :], sem)


def _moe_experts_kernel(dst_ref, tgrp_ref, nval_ref, hn_hbm, wg_ref, wu_ref, wd_ref,
                        y_hbm, xbuf, hb, gbuf, acc, gsem, ssem, *, tm, n_tok):
    i = pl.program_id(0)
    e = pl.program_id(1)
    nt = pl.num_programs(0)
    slot = i % 2
    per = tm // EXPERTS_PER_GROUP
    d = hb.shape[1]

    def gather_row(slot_idx, buf, row):
        s = dst_ref[slot_idx]
        return _row_copy(hn_hbm, jnp.where(s < n_tok, s, 0), xbuf.at[buf], row, gsem.at[buf])

    def wait_scatter(sl):
        pltpu.make_async_copy(acc.at[sl], y_hbm.at[pl.ds(0, tm), :], ssem.at[sl]).wait()

    @pl.when((i == 0) & (e == 0))
    def _():
        def body(r, c):
            gather_row(r, 0, r).start()
            return c
        lax.fori_loop(0, tm, body, 0)
        acc[1] = jnp.zeros(acc.shape[1:], F32)
        for half in range(2):
            fill = pltpu.make_async_copy(acc.at[1], y_hbm.at[pl.ds(n_tok + half * tm, tm), :], ssem.at[1])
            fill.start()
            fill.wait()

    @pl.when(e == 0)
    def _():
        pltpu.make_async_copy(hn_hbm.at[pl.ds(0, tm), :], xbuf.at[slot], gsem.at[slot]).wait()
        hb[...] = xbuf[slot, :, 0:d].astype(BF16)
        gbuf[...] = xbuf[slot, :, d:d + LANE]

        @pl.when(i >= 2)
        def _():
            wait_scatter(slot)
        acc[slot] = jnp.zeros(acc.shape[1:], F32)

    @pl.when(i + 1 < nt)
    def _():
        base = (i + 1) * tm + e * per
        for k in range(per):
            gather_row(base + k, 1 - slot, e * per + k).start()

    @pl.when(nval_ref[i] > 0)
    def _():
        hv = hb[...]
        a = jnp.dot(hv, wg_ref[0], preferred_element_type=F32)
        u = jnp.dot(hv, wu_ref[0], preferred_element_type=F32)
        hid = ((a * _sigmoid(a)) * u).astype(BF16)
        y = jnp.dot(hid, wd_ref[0], preferred_element_type=F32)
        lane = lax.broadcasted_iota(jnp.int32, gbuf.shape, 1)
        col = jnp.sum(jnp.where(lane == e, gbuf[...], 0.0), axis=-1, keepdims=True)
        acc[slot] = acc[slot] + col * y

    @pl.when(e == EXPERTS_PER_GROUP - 1)
    def _():
        def body(r, c):
            _row_copy(acc.at[slot], r, y_hbm, dst_ref[i * tm + r], ssem.at[slot]).start()
            return c
        lax.fori_loop(0, tm, body, 0, unroll=8)

        @pl.when(i == nt - 1)
        def _():
            wait_scatter(slot)

            @pl.when(i >= 1)
            def _():
                wait_scatter(1 - slot)


def _moe_experts(hn_ext, dst, tile_group, tile_nvalid, w_gate, w_up, w_down, layer, tm):
    t, dw = hn_ext.shape
    d = dw - LANE
    dff = w_gate.shape[2]
    ntiles = tile_group.shape[0]

    def wmap(i, e, dst_r, tgrp_r, nval_r):
        return (layer * N_EXPERTS + tgrp_r[i] * EXPERTS_PER_GROUP + e, 0, 0)

    grid_spec = pltpu.PrefetchScalarGridSpec(
        num_scalar_prefetch=3,
        grid=(ntiles, EXPERTS_PER_GROUP),
        in_specs=[
            pl.BlockSpec(memory_space=pl.ANY),
            pl.BlockSpec((1, d, dff), wmap),
            pl.BlockSpec((1, d, dff), wmap),
            pl.BlockSpec((1, dff, d), wmap),
        ],
        out_specs=pl.BlockSpec(memory_space=pl.ANY),
        scratch_shapes=[
            pltpu.VMEM((2, tm, dw), F32),
            pltpu.VMEM((tm, d), BF16),
            pltpu.VMEM((tm, LANE), F32),
            pltpu.VMEM((2, tm, d), F32),
            pltpu.SemaphoreType.DMA((2,)),
            pltpu.SemaphoreType.DMA((2,)),
        ],
    )
    return pl.pallas_call(
        functools.partial(_moe_experts_kernel, tm=tm, n_tok=t),
        grid_spec=grid_spec,
        out_shape=jax.ShapeDtypeStruct((t + 2 * tm, d), F32),
        compiler_params=_cparams("arbitrary", "arbitrary"),
        name="moe_experts",
    )(dst, tile_group, tile_nvalid, hn_ext, w_gate, w_up, w_down)


def _residual_kernel(x_ref, y_ref, g_ref, o_ref):
    o_ref[0] = x_ref[0] + g_ref[0] * y_ref[...]


def _residual(x, y, gate):
    nb, r, d = x.shape
    tm = _tile(r, 512)
    per_b = r // tm
    blk = pl.BlockSpec((1, tm, d), lambda b, t: (b, t, 0))
    return pl.pallas_call(
        _residual_kernel,
        grid=(nb, per_b),
        in_specs=[blk, pl.BlockSpec((tm, d), lambda b, t: (b * per_b + t, 0)),
                  pl.BlockSpec((1, 1, d), lambda b, t: (b, 0, 0))],
        out_specs=blk,
        out_shape=jax.ShapeDtypeStruct((nb, r, d), F32),
        compiler_params=_cparams("parallel", "parallel"),
        name="moe_residual",
    )(x, y, gate)


def _group_layout(grp, tm):
    t = grp.shape[0]
    ntiles = t // tm + N_GROUPS
    nslots = ntiles * tm
    oh = (grp[None, :] == jnp.arange(N_GROUPS)[:, None]).astype(F32).reshape(N_GROUPS, t // LANE, LANE)
    tri = (jnp.arange(LANE)[:, None] <= jnp.arange(LANE)[None, :]).astype(F32)
    within = jnp.einsum('grk,kl->grl', oh, tri)
    row_tot = within[..., -1]
    row_off = jnp.cumsum(row_tot, axis=1) - row_tot
    rank = (jnp.sum((within + row_off[..., None]) * oh, axis=0).reshape(t) - 1.0).astype(jnp.int32)
    counts = jnp.sum(row_tot, axis=1).astype(jnp.int32)
    padded = ((counts + tm - 1) // tm) * tm
    start = jnp.cumsum(padded) - padded
    slot_of_token = start[grp] + rank
    slot_ids = jnp.arange(nslots, dtype=jnp.int32)
    spare = t + ((slot_ids // tm) % 2) * tm + slot_ids % tm
    dst = spare.at[slot_of_token].set(jnp.arange(t, dtype=jnp.int32))
    tile_start = jnp.arange(ntiles, dtype=jnp.int32) * tm
    ends = jnp.cumsum(padded)
    tile_group = jnp.minimum(jnp.sum((tile_start[:, None] >= ends[None, :]).astype(jnp.int32), axis=1),
                             N_GROUPS - 1).astype(jnp.int32)
    tile_nvalid = jnp.clip(counts[tile_group] - (tile_start - start[tile_group]), 0, tm).astype(jnp.int32)
    return dst, tile_group, tile_nvalid


def _moe(x, g, sh, sc, gate, w_router_t, b_router, w_gate, w_up, w_down, layer):
    nb, r, d = x.shape
    t = nb * r
    tm = 512 if t >= 8192 else (256 if t >= 2048 else 128)
    hn_ext, info = _moe_route(x, g, sh, sc, w_router_t, b_router)
    grp = info[:, EXPERTS_PER_GROUP, :].reshape(t).astype(jnp.int32)
    dst, tile_group, tile_nvalid = _group_layout(grp, tm)
    y = _moe_experts(hn_ext.reshape(t, d + LANE), dst, tile_group, tile_nvalid, w_gate, w_up, w_down,
                     layer, tm)
    return _residual(x, y, gate)


def kernel(x, c, ctx, c_ctx, w_ada, b_ada, norm1_g, norm2_g, w_in, qn_a, kn_a, sink_a, qn_b, kn_b,
           lam_b, subln_b, conv_w, conv_b, w_rg, b_rg, w_ig, b_ig, lru_lambda, lb_d, onorm_d,
           w_branch, w_out, w_router, b_router, w_gate, w_up, w_down):
    bsz, seq, d = x.shape
    lc = ctx.shape[1]
    depth = w_in.shape[0]

    n_rows = -(-(bsz + 1) // SUBLANE) * SUBLANE
    c_pad = jnp.zeros((n_rows, d), F32).at[:bsz].set(c).at[bsz].set(c_ctx)
    mod = _ada(c_pad, w_ada, b_ada).reshape(depth, n_rows, 6, d)

    lb_w = jax.nn.softmax(lb_d.astype(F32), axis=0)
    lb_all = jnp.cumsum(lb_w, axis=0) - lb_w[0:1]
    rope_tabs = _rope_tables(seq)
    w_router_t = w_router.T
    w_in_b = w_in.astype(BF16)
    w_branch_b = w_branch.astype(BF16).reshape((depth * w_branch.shape[1],) + w_branch.shape[2:])
    w_out_b = w_out.astype(BF16)
    w_gate_b, w_up_b, w_down_b = (w.astype(BF16).reshape((depth * N_EXPERTS,) + w.shape[2:])
                                  for w in (w_gate, w_up, w_down))

    xl = x
    xc = ctx.reshape(1, bsz * lc, d)
    for l in range(depth):
        ctx_out = l < depth - 1
        mod_l = [mod[l, :bsz, k][:, None, :] for k in range(6)]
        mod_c = [mod[l, bsz:bsz + 1, k][:, None, :] for k in range(6)]
        g1 = norm1_g[l].reshape(1, d)
        g2 = norm2_g[l].reshape(1, d)

        lo_l, hi_l = _norm_proj(xl, g1, mod_l[0], mod_l[1], w_in_b, l)
        lo_c, hi_c = _norm_proj(xc, g1, mod_c[0], mod_c[1], w_in_b, l)
        lo_c = lo_c.reshape(bsz, lc, -1)
        hi_c = hi_c.reshape(bsz, lc, -1)

        qa = qn_a[l] * (HEAD_DIM ** -0.5 * LOG2E)
        qb = jnp.tile(qn_b[l], 2) * (B_DIM ** -0.5 * LOG2E)
        kb = jnp.tile(kn_b[l], 2)
        one = jnp.ones((LANE,), F32)
        gains = jnp.stack([qa] * 8 + [kn_a[l]] * 2 + [one] * 2 + [qb] * 8 + [kb] * 8 + [one] * 8)
        gains = gains.reshape(N_QKV_BLOCKS, 1, LANE)
        p_l = _prep(lo_l, gains, rope_tabs)
        p_c = _prep(lo_c, gains, None)

        oa_c, oa_l = _attn_a(sink_a[l], p_l, p_c, ctx_out)

        lq1, lk1, lq2, lk2 = lam_b[l].astype(F32)
        lam_init = 0.8 - 0.6 * math.exp(-0.3 * l)
        lam = (jnp.exp(jnp.sum(lq1 * lk1)) - jnp.exp(jnp.sum(lq2 * lk2)) + lam_init).reshape(1)
        ob_c, ob_l = _attn_b(lam, p_l, p_c, subln_b[l].reshape(1, LANE), 1.0 - lam_init, ctx_out)

        oc_c, oc_l = _lru(hi_l, hi_c, conv_w[l], conv_b[l], w_rg[l].astype(BF16), b_rg[l],
                          w_ig[l].astype(BF16), b_ig[l], lru_lambda[l], ctx_out)
        od_c, od_l = _gla(hi_l, hi_c, lb_all[l], onorm_d[l].reshape(1, LANE), ctx_out)

        moe_w = (w_router_t, b_router, w_gate_b, w_up_b, w_down_b, l)

        y_l = _merge((oa_l, ob_l, oc_l, od_l), lo_l, w_branch_b, l)
        xl = _out_proj(y_l, w_out_b, l, xl, mod_l[2])
        xl = _moe(xl, g2, mod_l[3], mod_l[4], mod_l[5], *moe_w)
        if ctx_out:
            flat = lambda t: t.reshape(1, bsz * lc, -1)
            y_c = _merge(tuple(flat(t) for t in (oa_c, ob_c, oc_c, od_c)), flat(lo_c), w_branch_b, l)
            xc = _out_proj(y_c, w_out_b, l, xc, mod_c[2])
            xc = _moe(xc, g2, mod_c[3], mod_c[4], mod_c[5], *moe_w)
    return xl
```

```python
import functools
import math

import jax
import jax.numpy as jnp
from jax import lax
from jax.experimental import pallas as pl
from jax.experimental.pallas import tpu as pltpu

F32 = jnp.float32
BF16 = jnp.bfloat16

LANE = 128
SUBLANE = 8
VMEM_LIMIT_BYTES = 56 * 1024 * 1024

EPS = 1e-6
ROPE_THETA = 10000.0
GRID_W = 64
HEAD_DIM = 128
WINDOW = 128
Q_BLOCK = 128
N_HEADS = 8
A_KV_HEADS = 2
A_GROUP = N_HEADS // A_KV_HEADS
A_Q_PER_STEP = 2
B_DIM = 64
B_KEY_CHUNK = 1024
LRU_C = 8.0
GLA_CHUNK = 64
GLA_SAFE_DECAY = 80.0
N_EXPERTS = 16
N_GROUPS = 4
EXPERTS_PER_GROUP = N_EXPERTS // N_GROUPS
NEG_BIG = -1e30
LOG2E = math.log2(math.e)
SOFTMAX_SAFE_LOGIT = 60.0
LOGIT_MARGIN = 1.02
TINY = 1e-37

COL_AQ, COL_AK, COL_AV = 0, 8, 10
COL_BQ, COL_BK, COL_BV = 12, 20, 28
N_QKV_BLOCKS = 36
COL_GATE = N_QKV_BLOCKS
COL_CX, COL_CY = 0, 8
COL_DQ, COL_DFF, COL_DFB, COL_DI, COL_DG = 16, 24, 32, 40, 48
W_IN_QKV = (0, 4608)
W_IN_REC = (4608, 11776)
W_IN_GATE = (11776, 19968)
PROJ_TILE = 512

NT_DIMS = (((1,), (1,)), ((), ()))
TN_DIMS = (((0,), (0,)), ((), ()))


def _cparams(*sem):
    return pltpu.CompilerParams(dimension_semantics=sem, vmem_limit_bytes=VMEM_LIMIT_BYTES)


def _tile(n, pref):
    t = min(n, pref)
    while n % t:
        t //= 2
    return t


def _sigmoid(x):
    return jax.nn.sigmoid(x)


def _modulated_norm(x, g, sh, sc):
    ms = jnp.mean(x * x, axis=-1, keepdims=True)
    return (x * lax.rsqrt(ms + EPS) * g) * (1.0 + sc) + sh


def _ada_kernel(c_ref, w_ref, b_ref, o_ref):
    c = c_ref[...]
    s = (c * _sigmoid(c)).astype(BF16)
    o_ref[0] = jnp.dot(s, w_ref[0].astype(BF16), preferred_element_type=F32) + b_ref[0]


def _ada(c_pad, w_ada, b_ada):
    nl, d, n = w_ada.shape
    rows = c_pad.shape[0]
    tn = _tile(n, 1024)
    return pl.pallas_call(
        _ada_kernel,
        grid=(nl, n // tn),
        in_specs=[
            pl.BlockSpec((rows, d), lambda l, j: (0, 0)),
            pl.BlockSpec((1, d, tn), lambda l, j: (l, 0, j)),
            pl.BlockSpec((1, 1, tn), lambda l, j: (l, 0, j)),
        ],
        out_specs=pl.BlockSpec((1, rows, tn), lambda l, j: (l, 0, j)),
        out_shape=jax.ShapeDtypeStruct((nl, rows, n), F32),
        compiler_params=_cparams("parallel", "parallel"),
        name="ada_mod",
    )(c_pad, w_ada, b_ada.reshape(nl, 1, n))


def _norm_proj_kernel(x_ref, g_ref, sh_ref, sc_ref, w_ref, lo_ref, hi_ref, hn_ref, *, n_lo):
    j = pl.program_id(2)

    @pl.when(j == 0)
    def _():
        hn_ref[...] = _modulated_norm(x_ref[0], g_ref[...], sh_ref[0], sc_ref[0]).astype(BF16)

    @pl.when(j < n_lo)
    def _():
        lo_ref[0] = jnp.dot(hn_ref[...], w_ref[0], preferred_element_type=F32).astype(BF16)

    @pl.when(j >= n_lo)
    def _():
        hi_ref[0] = jnp.dot(hn_ref[...], w_ref[0], preferred_element_type=F32)


def _proj_weight_tile(j):
    n_qkv = (W_IN_QKV[1] - W_IN_QKV[0]) // PROJ_TILE
    n_gate = (W_IN_GATE[1] - W_IN_GATE[0]) // PROJ_TILE
    gate0 = W_IN_GATE[0] // PROJ_TILE
    rec0 = W_IN_REC[0] // PROJ_TILE
    return jnp.where(j < n_qkv, j, jnp.where(j < n_qkv + n_gate, j - n_qkv + gate0, j - n_qkv - n_gate + rec0))


def _norm_proj(x, g, sh, sc, w, layer):
    nb, r, d = x.shape
    tn = PROJ_TILE
    ntiles = w.shape[2] // tn
    n_lo = (W_IN_QKV[1] - W_IN_QKV[0] + W_IN_GATE[1] - W_IN_GATE[0]) // tn
    tm = _tile(r, 1024)
    n_hi = ntiles - n_lo
    return pl.pallas_call(
        functools.partial(_norm_proj_kernel, n_lo=n_lo),
        grid=(nb, r // tm, ntiles),
        in_specs=[
            pl.BlockSpec((1, tm, d), lambda b, t, j: (b, t, 0)),
            pl.BlockSpec((1, d), lambda b, t, j: (0, 0)),
            pl.BlockSpec((1, 1, d), lambda b, t, j: (b, 0, 0)),
            pl.BlockSpec((1, 1, d), lambda b, t, j: (b, 0, 0)),
            pl.BlockSpec((1, d, tn), lambda b, t, j: (layer, 0, _proj_weight_tile(j))),
        ],
        out_specs=[
            pl.BlockSpec((1, tm, tn), lambda b, t, j: (b, t, jnp.minimum(j, n_lo - 1))),
            pl.BlockSpec((1, tm, tn), lambda b, t, j: (b, t, jnp.maximum(j - n_lo, 0))),
        ],
        out_shape=[jax.ShapeDtypeStruct((nb, r, n_lo * tn), BF16),
                   jax.ShapeDtypeStruct((nb, r, n_hi * tn), F32)],
        scratch_shapes=[pltpu.VMEM((tm, d), BF16)],
        compiler_params=_cparams("parallel", "parallel", "arbitrary"),
        name="norm_proj",
    )(x, g, sh, sc, w)


def _group_mean_matrix(group):
    k = lax.broadcasted_iota(jnp.int32, (LANE, LANE), 0)
    l = lax.broadcasted_iota(jnp.int32, (LANE, LANE), 1)
    return jnp.where(k // group == l // group, 1.0 / group, 0.0).astype(BF16)


def _rope_partner_matrix(half):
    k = lax.broadcasted_iota(jnp.int32, (LANE, LANE), 0)
    l = lax.broadcasted_iota(jnp.int32, (LANE, LANE), 1)
    src = jnp.where((l % (2 * half)) < half, l + half, l - half)
    return jnp.where(k == src, 1.0, 0.0).astype(BF16)


def _group_mean(x2, mean_m):
    hi = x2.astype(BF16)
    lo = (x2 - hi.astype(F32)).astype(BF16)
    return (jnp.dot(hi, mean_m, preferred_element_type=F32)
            + jnp.dot(lo, mean_m, preferred_element_type=F32))


def _prep_kernel(*refs, rope):
    if rope:
        x_ref, g_ref, ca_ref, sa_ref, cb_ref, sb_ref, o_ref = refs
    else:
        x_ref, g_ref, o_ref = refs
    tm = x_ref.shape[1]

    def qk_blocks(col0, nblk, group, half, cos_ref, sin_ref):
        blocks = [slice((col0 + j) * LANE, (col0 + j + 1) * LANE) for j in range(nblk)]
        x = jnp.concatenate([x_ref[0, :, c].astype(F32) for c in blocks], axis=0)
        ms = _group_mean(x * x, _group_mean_matrix(group))
        y = (x * lax.rsqrt(ms + EPS)).reshape(nblk, tm, LANE) * g_ref[col0:col0 + nblk]
        if rope:
            part = jnp.dot(y.reshape(nblk * tm, LANE).astype(BF16), _rope_partner_matrix(half),
                           preferred_element_type=F32).reshape(nblk, tm, LANE)
            y = y * cos_ref[...][None] + part * sin_ref[...][None]
        for j, c in enumerate(blocks):
            o_ref[0, :, c] = y[j].astype(BF16)

    qk_blocks(COL_AQ, COL_AV - COL_AQ, HEAD_DIM, 32, ca_ref if rope else None, sa_ref if rope else None)
    qk_blocks(COL_BQ, COL_BV - COL_BQ, B_DIM, 16, cb_ref if rope else None, sb_ref if rope else None)
    for c0, c1 in ((COL_AV, COL_BQ), (COL_BV, N_QKV_BLOCKS)):
        o_ref[0, :, c0 * LANE:c1 * LANE] = x_ref[0, :, c0 * LANE:c1 * LANE]


def _prep(proj, gains, rope_tabs):
    nb, r, _ = proj.shape
    tm = _tile(r, 256)
    width = N_QKV_BLOCKS * LANE
    rope = rope_tabs is not None
    in_specs = [
        pl.BlockSpec((1, tm, width), lambda b, t: (b, t, 0)),
        pl.BlockSpec((N_QKV_BLOCKS, 1, LANE), lambda b, t: (0, 0, 0)),
    ]
    args = [proj, gains]
    if rope:
        in_specs += [pl.BlockSpec((tm, LANE), lambda b, t: (t, 0))] * 4
        args += list(rope_tabs)
    return pl.pallas_call(
        functools.partial(_prep_kernel, rope=rope),
        grid=(nb, r // tm),
        in_specs=in_specs,
        out_specs=pl.BlockSpec((1, tm, width), lambda b, t: (b, t, 0)),
        out_shape=jax.ShapeDtypeStruct((nb, r, width), BF16),
        compiler_params=_cparams("parallel", "parallel"),
        name="qkv_prep",
    )(*args)


def _rope_tables(seq):
    pos = jnp.arange(seq)
    rows = (pos // GRID_W).astype(F32)[:, None]
    cols = (pos % GRID_W).astype(F32)[:, None]
    lane = jnp.arange(LANE)

    def tables(half):
        inv = ROPE_THETA ** (-(lane % half).astype(F32) / half)
        use_rows = (lane % (4 * half)) < 2 * half
        ang = jnp.where(use_rows[None, :], rows, cols) * inv[None, :]
        sign = jnp.where((lane % (2 * half)) < half, -1.0, 1.0)
        return jnp.cos(ang), jnp.sin(ang) * sign[None, :]

    ca, sa = tables(32)
    cb, sb = tables(16)
    return ca, sa, cb, sb


def _stack_heads(q):
    return jnp.concatenate([q[:, g * HEAD_DIM:(g + 1) * HEAD_DIM] for g in range(A_GROUP)], axis=0)


def _unstack_heads(o, rows):
    return jnp.concatenate([o[g * rows:(g + 1) * rows] for g in range(A_GROUP)], axis=1)


def _sink_column(sink_ref, kvh, rows):
    return jnp.concatenate(
        [jnp.full((rows, 1), sink_ref[kvh * A_GROUP + g] * LOG2E, F32) for g in range(A_GROUP)], axis=0)


def _attn_a_lat_kernel(sink_ref, bounded_ref, q_ref, bias_ref, kl_ref, vl_ref, kc_ref, vc_ref, o_ref, *, seq):
    band = 3 * Q_BLOCK
    nblk = seq // Q_BLOCK

    def run(bounded):
        for qi in range(q_ref.shape[1] // Q_BLOCK):
            i = pl.program_id(1) * (q_ref.shape[1] // Q_BLOCK) + qi
            rows = slice(qi * Q_BLOCK, (qi + 1) * Q_BLOCK)
            first = jnp.clip(i - 1, 0, nblk - 3)
            start = pl.multiple_of(first * Q_BLOCK, Q_BLOCK)
            bias = bias_ref[i - first]
            outs = []
            for kvh in range(A_KV_HEADS):
                hs = slice(kvh * HEAD_DIM, (kvh + 1) * HEAD_DIM)
                qs = _stack_heads(q_ref[0, rows, kvh * A_GROUP * HEAD_DIM:(kvh + 1) * A_GROUP * HEAD_DIM])
                kb = kl_ref[0, pl.ds(start, band), hs]
                vb = vl_ref[0, pl.ds(start, band), hs]
                s_loc = lax.dot_general(qs, kb, NT_DIMS, preferred_element_type=F32)
                s_loc = (s_loc.reshape(A_GROUP, Q_BLOCK, band) + bias[None]).reshape(A_GROUP * Q_BLOCK, band)
                s_ctx = lax.dot_general(qs, kc_ref[0, :, hs], NT_DIMS, preferred_element_type=F32)
                sk = _sink_column(sink_ref, kvh, Q_BLOCK)
                if not bounded:
                    m = jnp.maximum(jnp.maximum(jnp.max(s_loc, axis=-1, keepdims=True),
                                                jnp.max(s_ctx, axis=-1, keepdims=True)), sk)
                    s_loc, s_ctx, sk = s_loc - m, s_ctx - m, sk - m
                p_loc = jnp.exp2(s_loc)
                p_ctx = jnp.exp2(s_ctx)
                den = (jnp.sum(p_loc, axis=-1, keepdims=True) + jnp.sum(p_ctx, axis=-1, keepdims=True)
                       + jnp.exp2(sk))
                o = (jnp.dot(p_loc.astype(BF16), vb, preferred_element_type=F32)
                     + jnp.dot(p_ctx.astype(BF16), vc_ref[0, :, hs], preferred_element_type=F32)) / den
                outs.append(_unstack_heads(o, Q_BLOCK))
            o_ref[0, rows, :] = jnp.concatenate(outs, axis=1).astype(BF16)

    pl.when(bounded_ref[0] > 0)(lambda: run(True))
    pl.when(bounded_ref[0] <= 0)(lambda: run(False))


def _attn_a_ctx_kernel(sink_ref, q_ref, kc_ref, vc_ref, o_ref):
    kvh = pl.program_id(1)
    rows = q_ref.shape[1]
    qs = _stack_heads(q_ref[0])
    s = lax.dot_general(qs, kc_ref[0], NT_DIMS, preferred_element_type=F32)
    sk = _sink_column(sink_ref, kvh, rows)
    m = jnp.maximum(jnp.max(s, axis=-1, keepdims=True), sk)
    p = jnp.exp2(s - m)
    den = jnp.sum(p, axis=-1, keepdims=True) + jnp.exp2(sk - m)
    o = jnp.dot(p.astype(BF16), vc_ref[0], preferred_element_type=F32) / den
    o_ref[0] = _unstack_heads(o, rows).astype(BF16)


def _window_bias():
    band = 3 * Q_BLOCK
    row = jnp.arange(Q_BLOCK)[None, :, None]
    col = jnp.arange(band)[None, None, :]
    delta = (jnp.arange(3) * Q_BLOCK)[:, None, None]
    return jnp.where(jnp.abs(col - delta - row) <= WINDOW, 0.0, NEG_BIG).astype(F32)


def _attn_a(sink, bounded, p_l, p_c, ctx_out):
    b, seq, _ = p_l.shape
    lc = p_c.shape[1]
    gw = A_GROUP * HEAD_DIM
    nblk = seq // Q_BLOCK
    kvw = A_KV_HEADS * HEAD_DIM
    smem = pl.BlockSpec(memory_space=pltpu.SMEM)

    qrows = A_Q_PER_STEP * Q_BLOCK if nblk % A_Q_PER_STEP == 0 else Q_BLOCK
    o_l = pl.pallas_call(
        functools.partial(_attn_a_lat_kernel, seq=seq),
        grid=(b, seq // qrows),
        in_specs=[
            smem, smem,
            pl.BlockSpec((1, qrows, N_HEADS * HEAD_DIM), lambda n, i: (n, i, 0)),
            pl.BlockSpec((3, Q_BLOCK, 3 * Q_BLOCK), lambda n, i: (0, 0, 0)),
            pl.BlockSpec((1, seq, kvw), lambda n, i: (n, 0, COL_AK // A_KV_HEADS)),
            pl.BlockSpec((1, seq, kvw), lambda n, i: (n, 0, COL_AV // A_KV_HEADS)),
            pl.BlockSpec((1, lc, kvw), lambda n, i: (n, 0, COL_AK // A_KV_HEADS)),
            pl.BlockSpec((1, lc, kvw), lambda n, i: (n, 0, COL_AV // A_KV_HEADS)),
        ],
        out_specs=pl.BlockSpec((1, qrows, N_HEADS * HEAD_DIM), lambda n, i: (n, i, 0)),
        out_shape=jax.ShapeDtypeStruct((b, seq, N_HEADS * HEAD_DIM), BF16),
        compiler_params=_cparams("parallel", "arbitrary"),
        name="attn_a_latent",
    )(sink, bounded, p_l, _window_bias(), p_l, p_l, p_c, p_c)
    o_c = None
    if ctx_out:
        o_c = pl.pallas_call(
            _attn_a_ctx_kernel,
            grid=(b, A_KV_HEADS),
            in_specs=[
                smem,
                pl.BlockSpec((1, lc, gw), lambda n, h: (n, 0, h)),
                pl.BlockSpec((1, lc, LANE), lambda n, h: (n, 0, COL_AK + h)),
                pl.BlockSpec((1, lc, LANE), lambda n, h: (n, 0, COL_AV + h)),
            ],
            out_specs=pl.BlockSpec((1, lc, gw), lambda n, h: (n, 0, h)),
            out_shape=jax.ShapeDtypeStruct((b, lc, N_HEADS * HEAD_DIM), BF16),
            compiler_params=_cparams("parallel", "parallel"),
            name="attn_a_ctx",
        )(sink, p_c, p_c, p_c)
    return o_c, o_l


def _attn_b_kernel(lam_ref, bounded_ref, q_ref, *refs, nseg, post_scale):
    k_refs = refs[:nseg]
    v_refs = refs[nseg:2 * nseg]
    g_ref, o_ref = refs[2 * nseg], refs[2 * nseg + 1]
    q = q_ref[0]
    tq = q.shape[0]
    lane = lax.broadcasted_iota(jnp.int32, q.shape, 1)
    zero = jnp.zeros_like(q)
    q2 = jnp.concatenate([jnp.where(lane < B_DIM, q, zero), jnp.where(lane < B_DIM, zero, q)], axis=0)
    chunks = []
    for k_ref, v_ref in zip(k_refs, v_refs):
        kc = _tile(k_ref.shape[1], B_KEY_CHUNK)
        chunks += [(k_ref, v_ref, c0, kc) for c0 in range(0, k_ref.shape[1], kc)]

    def scores(k_ref, c0, kc):
        return lax.dot_general(q2, k_ref[0, c0:c0 + kc, :], NT_DIMS, preferred_element_type=F32)

    def finish(acc, den):
        o2 = acc / den
        o = o2[:tq] - lam_ref[0] * o2[tq:]
        ms = jnp.mean(o * o, axis=-1, keepdims=True)
        o_ref[0] = ((o * lax.rsqrt(ms + EPS) * g_ref[...]) * post_scale).astype(BF16)

    @pl.when(bounded_ref[0] > 0)
    def _():
        den = jnp.zeros((2 * tq, 1), F32)
        acc = jnp.zeros((2 * tq, LANE), F32)
        for k_ref, v_ref, c0, kc in chunks:
            p = jnp.exp2(scores(k_ref, c0, kc))
            den = den + jnp.sum(p, axis=-1, keepdims=True)
            acc = acc + jnp.dot(p.astype(BF16), v_ref[0, c0:c0 + kc, :], preferred_element_type=F32)
        finish(acc, den)

    @pl.when(bounded_ref[0] <= 0)
    def _():
        m = jnp.full((2 * tq, 1), NEG_BIG, F32)
        den = jnp.zeros((2 * tq, 1), F32)
        acc = jnp.zeros((2 * tq, LANE), F32)
        for k_ref, v_ref, c0, kc in chunks:
            s = scores(k_ref, c0, kc)
            m_new = jnp.maximum(m, jnp.max(s, axis=-1, keepdims=True))
            alpha = jnp.exp2(m - m_new)
            p = jnp.exp2(s - m_new)
            den = alpha * den + jnp.sum(p, axis=-1, keepdims=True)
            acc = alpha * acc + jnp.dot(p.astype(BF16), v_ref[0, c0:c0 + kc, :],
                                        preferred_element_type=F32)
            m = m_new
        finish(acc, den)


def _attn_b_call(lam, bounded, q_src, k_srcs, subln, post_scale, name):
    b, rq, _ = q_src.shape
    tq = _tile(rq, 512)
    smem = pl.BlockSpec(memory_space=pltpu.SMEM)
    in_specs = [smem, smem, pl.BlockSpec((1, tq, LANE), lambda n, h, i: (n, i, COL_BQ + h))]
    in_specs += [pl.BlockSpec((1, s.shape[1], LANE), lambda n, h, i: (n, 0, COL_BK + h)) for s in k_srcs]
    in_specs += [pl.BlockSpec((1, s.shape[1], LANE), lambda n, h, i: (n, 0, COL_BV + h)) for s in k_srcs]
    in_specs += [pl.BlockSpec((1, LANE), lambda n, h, i: (0, 0))]
    return pl.pallas_call(
        functools.partial(_attn_b_kernel, nseg=len(k_srcs), post_scale=post_scale),
        grid=(b, N_HEADS, rq // tq),
        in_specs=in_specs,
        out_specs=pl.BlockSpec((1, tq, LANE), lambda n, h, i: (n, i, h)),
        out_shape=jax.ShapeDtypeStruct((b, rq, N_HEADS * HEAD_DIM), BF16),
        compiler_params=_cparams("parallel", "parallel", "arbitrary"),
        name=name,
    )(lam, bounded, q_src, *k_srcs, *k_srcs, subln)


def _attn_b(lam, bounded, p_l, p_c, subln, post_scale, ctx_out):
    o_l = _attn_b_call(lam, bounded, p_l, [p_c, p_l], subln, post_scale, "attn_b_latent")
    o_c = _attn_b_call(lam, bounded, p_c, [p_c], subln, post_scale, "attn_b_ctx") if ctx_out else None
    return o_c, o_l


def _centred_conv(x_ref, pad_ref, w_ref, b_ref):
    n = x_ref.shape[1]
    zeros = jnp.zeros((SUBLANE, LANE), F32)
    pad_ref[0:SUBLANE, :] = zeros
    pad_ref[SUBLANE:SUBLANE + n, :] = x_ref[0]
    pad_ref[SUBLANE + n:2 * SUBLANE + n, :] = zeros
    out = b_ref[...]
    for tap in range(4):
        out = out + pad_ref[pl.ds(SUBLANE - 2 + tap, n), :] * w_ref[tap:tap + 1, :]
    return out


def _every_8th(ref, d, j, n):
    return ref[d, pl.ds(j, n, stride=SUBLANE), :]


def _tile_order(reverse):
    return range(SUBLANE - 1, -1, -1) if reverse else range(SUBLANE)


def _scan_summaries(a_ref, v_ref, d, n, reverse):
    h = p = None
    for j in _tile_order(reverse):
        a, v = _every_8th(a_ref, d, j, n), _every_8th(v_ref, d, j, n)
        h, p = (v, a) if h is None else (a * h + v, a * p)
    return p, h


def _scan_entering(a_ref, v_ref, d, n, carry_in, reverse, store):
    e = carry_in
    for j in _tile_order(reverse):
        store(j, e)
        e = _every_8th(a_ref, d, j, n) * e + _every_8th(v_ref, d, j, n)


def _scan_apply(a_ref, v_ref, d, n, carry_in, reverse, store):
    h = carry_in
    for j in _tile_order(reverse):
        h = _every_8th(a_ref, d, j, n) * h + _every_8th(v_ref, d, j, n)
        store(j, h)


def _seg_tile(t, ntc, ntl, reverse):
    if not reverse:
        return t
    return jnp.where(t < ntc, ntc - 1 - t, 2 * ntc + ntl - 1 - t)


def _lru_kernel(xc_ref, yc_ref, xl_ref, yl_ref, cw_ref, cb_ref, wr_ref, br_ref, wi_ref, bi_ref,
                lam_ref, *refs, ctx_out):
    if ctx_out:
        oc_ref, ol_ref, ac_ref, hl_ref, hs_ref, pad_ref, tp_ref, th_ref, sp_ref, sh_ref, c3_ref, e1_ref = refs
    else:
        ol_ref, ac_ref, hl_ref, hs_ref, pad_ref, tp_ref, th_ref, sp_ref, sh_ref, c3_ref, e1_ref = refs
    lc, seq = xc_ref.shape[1], xl_ref.shape[1]
    for x_ref, off in ((xc_ref, 0), (xl_ref, lc)):
        n = x_ref.shape[1]
        u = _centred_conv(x_ref, pad_ref, cw_ref, cb_ref)
        ub = u.astype(BF16)
        for d in range(2):
            lam = lam_ref[d:d + 1, :]
            sp = jnp.maximum(-lam, 0.0) + jnp.log1p(jnp.exp(-jnp.abs(lam)))
            r = _sigmoid(jnp.dot(ub, wr_ref[d, 0], preferred_element_type=F32) + br_ref[d:d + 1, :])
            gi = _sigmoid(jnp.dot(ub, wi_ref[d, 0], preferred_element_type=F32) + bi_ref[d:d + 1, :])
            log_a = -LRU_C * r * sp
            a = jnp.exp(log_a)
            y = jnp.tanh(-log_a) * (1.0 + a * a)
            v = (y * lax.rsqrt(jnp.maximum(y, TINY))) * gi * u
            ac_ref[d, off:off + n, :] = a
            hl_ref[d, off:off + n, :] = v

    nt1 = (lc + seq) // SUBLANE
    nt2 = nt1 // SUBLANE
    group_rows = SUBLANE * SUBLANE
    for d in range(2):
        tp_ref[d], th_ref[d] = _scan_summaries(ac_ref, hl_ref, d, nt1, d == 1)
        sp_ref[d], sh_ref[d] = _scan_summaries(tp_ref, th_ref, d, nt2, d == 1)

    def step(s, carry):
        new = []
        for d in range(2):
            g = pl.ds(_seg_tile(s, lc // group_rows, seq // group_rows, d == 1), 1)
            c3_ref[d, g, :] = carry[d]
            new.append(sp_ref[d, g, :] * carry[d] + sh_ref[d, g, :])
        return tuple(new)

    zero = jnp.zeros((1, LANE), F32)
    lax.fori_loop(0, nt2, step, (zero, zero))

    for d in range(2):
        def store_entering(j, e, d=d):
            e1_ref[d, pl.ds(j, nt2, stride=SUBLANE), :] = e

        def store_h(j, h, d=d):
            if d == 0:
                hs_ref[pl.ds(j, nt1, stride=SUBLANE), :] = h
            else:
                hl_ref[d, pl.ds(j, nt1, stride=SUBLANE), :] = h

        _scan_entering(tp_ref, th_ref, d, nt2, c3_ref[d], d == 1, store_entering)
        _scan_apply(ac_ref, hl_ref, d, nt1, e1_ref[d], d == 1, store_h)
    if ctx_out:
        hc = hs_ref[0:lc, :] + hl_ref[1, 0:lc, :]
        oc_ref[0] = (hc * jax.nn.gelu(yc_ref[0])).astype(BF16)
    hl = hs_ref[lc:lc + seq, :] + hl_ref[1, lc:lc + seq, :]
    ol_ref[0] = (hl * jax.nn.gelu(yl_ref[0])).astype(BF16)


def _lru(proj_l, proj_c, conv_w, conv_b, w_r, b_r, w_i, b_i, lam, ctx_out):
    b, seq, _ = proj_l.shape
    lc = proj_c.shape[1]
    nblk = w_r.shape[1]
    width = nblk * LANE
    nt1 = (lc + seq) // SUBLANE
    nt2 = nt1 // SUBLANE

    def seg(n, col):
        return pl.BlockSpec((1, n, LANE), lambda i, j: (i, 0, col + j))

    vec2 = pl.BlockSpec((2, LANE), lambda i, j: (0, j))
    wspec = pl.BlockSpec((2, 1, LANE, LANE), lambda i, j: (0, j, 0, 0))
    out_specs = [pl.BlockSpec((1, seq, LANE), lambda i, j: (i, 0, j))]
    out_shape = [jax.ShapeDtypeStruct((b, seq, width), BF16)]
    if ctx_out:
        out_specs.insert(0, pl.BlockSpec((1, lc, LANE), lambda i, j: (i, 0, j)))
        out_shape.insert(0, jax.ShapeDtypeStruct((b, lc, width), BF16))
    outs = pl.pallas_call(
        functools.partial(_lru_kernel, ctx_out=ctx_out),
        grid=(b, nblk),
        in_specs=[
            seg(lc, COL_CX), seg(lc, COL_CY), seg(seq, COL_CX), seg(seq, COL_CY),
            pl.BlockSpec((4, LANE), lambda i, j: (0, j)),
            pl.BlockSpec((1, LANE), lambda i, j: (0, j)),
            wspec, vec2, wspec, vec2, vec2,
        ],
        out_specs=out_specs,
        out_shape=out_shape,
        scratch_shapes=[
            pltpu.VMEM((2, lc + seq, LANE), F32),
            pltpu.VMEM((2, lc + seq, LANE), F32),
            pltpu.VMEM((lc + seq, LANE), F32),
            pltpu.VMEM((max(lc, seq) + 2 * SUBLANE, LANE), F32),
            pltpu.VMEM((2, nt1, LANE), F32),
            pltpu.VMEM((2, nt1, LANE), F32),
            pltpu.VMEM((2, nt2, LANE), F32),
            pltpu.VMEM((2, nt2, LANE), F32),
            pltpu.VMEM((2, nt2, LANE), F32),
            pltpu.VMEM((2, nt1, LANE), F32),
        ],
        compiler_params=_cparams("parallel", "parallel"),
        name="rglru",
    )(proj_c, proj_c, proj_l, proj_l, conv_w, conv_b.reshape(1, width), w_r, b_r, w_i, b_i, lam)
    return (outs[0], outs[1]) if ctx_out else (None, outs[0])


def _gla_level_masks(reverse):
    c = GLA_CHUNK
    row = lax.broadcasted_iota(jnp.int32, (c, c), 0)
    col = lax.broadcasted_iota(jnp.int32, (c, c), 1)
    masks = {}
    for s in (32, 16, 8):
        same = (row // (2 * s)) == (col // (2 * s))
        if reverse:
            masks[s] = same & ((row % (2 * s)) < s) & ((col % (2 * s)) >= s)
        else:
            masks[s] = same & ((row % (2 * s)) >= s) & ((col % (2 * s)) < s)
    tri = (row <= col) if reverse else (row >= col)
    return masks, tri.astype(F32)


def _gla_chunk(q, k, v, g, st, masks, tri, reverse):
    c = GLA_CHUNK
    b = jnp.dot(tri, g, precision=lax.Precision.HIGHEST, preferred_element_type=F32)
    att = jnp.zeros((c, c), F32)
    for s in (32, 16, 8):
        b3 = b.reshape(c // (2 * s), 2 * s, LANE)
        rr = s if reverse else s - 1
        rho = jnp.broadcast_to(b3[:, rr:rr + 1, :], b3.shape).reshape(c, LANE)
        e = jnp.exp(-jnp.abs(b - rho))
        a = lax.dot_general((q * e).astype(BF16), (k * e).astype(BF16), NT_DIMS,
                            preferred_element_type=F32)
        att = att + jnp.where(masks[s], a, 0.0)
    vb = v.astype(BF16)
    o = jnp.dot(att.astype(BF16), vb, preferred_element_type=F32)
    nb = c // SUBLANE
    b3 = b.reshape(nb, SUBLANE, LANE)
    q3 = q.reshape(nb, SUBLANE, LANE)
    k3 = k.reshape(nb, SUBLANE, LANE)
    v3 = v.reshape(nb, SUBLANE, LANE)
    rowi = lax.broadcasted_iota(jnp.int32, b3.shape, 1)
    od = jnp.zeros(b3.shape, F32)
    for jj in range(SUBLANE):
        keep = (rowi <= jj) if reverse else (rowi >= jj)
        e = jnp.where(keep, jnp.exp(jnp.minimum(b3 - b3[:, jj:jj + 1, :], 0.0)), 0.0)
        sj = jnp.sum(q3 * e * k3[:, jj:jj + 1, :], axis=-1, keepdims=True)
        od = od + sj * v3[:, jj:jj + 1, :]
    o = o + od.reshape(c, LANE)
    o = o + lax.dot_general((q * jnp.exp(b)).astype(BF16), st.astype(BF16), NT_DIMS,
                            preferred_element_type=F32)
    b_end = b[0:1, :] if reverse else b[c - 1:c, :]
    khat = (k * jnp.exp(b_end - b)).astype(BF16)
    st_new = st * jnp.exp(b_end) + lax.dot_general(vb, khat, TN_DIMS, preferred_element_type=F32)
    return o, st_new


def _gla_fast_intra(q, k, v, g, tri, reverse):
    c = GLA_CHUNK
    g_hi = g.astype(BF16)
    g_lo = (g - g_hi.astype(F32)).astype(BF16)
    b2 = jnp.dot(tri.astype(BF16), jnp.concatenate([g_hi, g_lo], axis=1), preferred_element_type=F32)
    b = b2[:, :LANE] + b2[:, LANE:]
    rr = c // 2 if reverse else c // 2 - 1
    rho = b[rr:rr + 1, :]
    qt = q * jnp.exp(b - rho)
    kt = k * jnp.exp(rho - b)
    a = lax.dot_general(qt.astype(BF16), kt.astype(BF16), NT_DIMS, preferred_element_type=F32)
    att = jnp.where(tri > 0.0, a, 0.0)
    vb = v.astype(BF16)
    o_intra = jnp.dot(att.astype(BF16), vb, preferred_element_type=F32)
    b_end = b[0:1, :] if reverse else b[c - 1:c, :]
    qe = (qt * jnp.exp(rho)).astype(BF16)
    khat = (kt * jnp.exp(b_end - rho)).astype(BF16)
    return o_intra, qe, khat, vb, jnp.exp(b_end)


def _gla_fast_inter(intra, st):
    o_intra, qe, khat, vb, decay = intra
    o = o_intra + lax.dot_general(qe, st.astype(BF16), NT_DIMS, preferred_element_type=F32)
    st_new = st * decay + lax.dot_general(vb, khat, TN_DIMS, preferred_element_type=F32)
    return o, st_new


def _gla_kernel(lb_ref, on_ref, qc_ref, ffc_ref, fbc_ref, ic_ref, gc_ref,
                ql_ref, ffl_ref, fbl_ref, il_ref, gl_ref, *refs, ctx_out):
    if ctx_out:
        oc_ref, ol_ref, q_s, v_s, g_s, k_s, o_s, st_s = refs
    else:
        ol_ref, q_s, v_s, g_s, k_s, o_s, st_s = refs
    lc, seq = qc_ref.shape[1], ql_ref.shape[1]
    ncc, ncl = lc // GLA_CHUNK, seq // GLA_CHUNK
    for off, n, q_ref, i_ref, f_refs in ((0, lc, qc_ref, ic_ref, (ffc_ref, fbc_ref)),
                                         (lc, seq, ql_ref, il_ref, (ffl_ref, fbl_ref))):
        q_s[off:off + n, :] = q_ref[0]
        v_s[off:off + n, :] = i_ref[0]
        for d in range(2):
            z = f_refs[d][0]
            lbd = lb_ref[d:d + 1, :]
            ez = jnp.exp(-jnp.abs(z))
            r = 1.0 / (1.0 + ez)
            pos = z >= 0.0
            sig_p = jnp.where(pos, r, ez * r)
            sig_n = jnp.where(pos, ez * r, r)
            g_s[d, off:off + n, :] = jnp.log(lbd + (1.0 - lbd) * sig_p)
            k_s[d, off:off + n, :] = (1.0 - lbd) * sig_n
    st_s[...] = jnp.zeros(st_s.shape, F32)
    consts = [_gla_level_masks(False), _gla_level_masks(True)]

    half = GLA_CHUNK // 2
    worst = jnp.zeros((1, LANE), F32)
    for d in range(2):
        hs = jnp.sum(g_s[d].reshape((lc + seq) // half, half, LANE), axis=1)
        worst = jnp.maximum(worst, jnp.max(-hs, axis=0, keepdims=True))
    safe = jnp.max(worst) < GLA_SAFE_DECAY

    def chunk_rows(cidx, d):
        chunk = _seg_tile(cidx, ncc, ncl, d == 1)
        return pl.ds(pl.multiple_of(chunk * GLA_CHUNK, GLA_CHUNK), GLA_CHUNK)

    def robust_step(cidx, carry):
        for d in range(2):
            rows = chunk_rows(cidx, d)
            o, st_new = _gla_chunk(q_s[rows, :], k_s[d, rows, :], v_s[rows, :], g_s[d, rows, :],
                                   st_s[d], consts[d][0], consts[d][1], d == 1)
            st_s[d] = st_new
            o_s[d, rows, :] = o
        return carry

    nchunks = ncc + ncl
    group = next(u for u in (12, 6, 4, 3, 2, 1) if nchunks % u == 0)

    def fast_step(t, carry):
        work = [[] for _ in range(2)]
        for d in range(2):
            for u in range(group):
                rows = chunk_rows(t * group + u, d)
                work[d].append((rows, _gla_fast_intra(q_s[rows, :], k_s[d, rows, :], v_s[rows, :],
                                                      g_s[d, rows, :], consts[d][1], d == 1)))
        for d in range(2):
            st = st_s[d]
            outs = []
            for rows, intra in work[d]:
                o, st = _gla_fast_inter(intra, st)
                outs.append((rows, o))
            st_s[d] = st
            for rows, o in outs:
                o_s[d, rows, :] = o
        return carry

    @pl.when(safe)
    def _():
        lax.fori_loop(0, nchunks // group, fast_step, 0)

    @pl.when(jnp.logical_not(safe))
    def _():
        lax.fori_loop(0, nchunks, robust_step, 0)

    def finish(o, gate):
        ms = jnp.mean(o * o, axis=-1, keepdims=True)
        return ((o * lax.rsqrt(ms + EPS) * on_ref[...]) * (gate * _sigmoid(gate))).astype(BF16)

    if ctx_out:
        oc_ref[0] = finish(o_s[0, 0:lc, :] + o_s[1, 0:lc, :], gc_ref[0])
    ol_ref[0] = finish(o_s[0, lc:lc + seq, :] + o_s[1, lc:lc + seq, :], gl_ref[0])


def _gla(proj_l, proj_c, lb, onorm, ctx_out):
    b, seq, _ = proj_l.shape
    lc = proj_c.shape[1]
    width = N_HEADS * HEAD_DIM
    nt = lc + seq

    def seg(n, col):
        return pl.BlockSpec((1, n, LANE), lambda i, h: (i, 0, col + h))

    cols = (COL_DQ, COL_DFF, COL_DFB, COL_DI, COL_DG)
    out_specs = [pl.BlockSpec((1, seq, LANE), lambda i, h: (i, 0, h))]
    out_shape = [jax.ShapeDtypeStruct((b, seq, width), BF16)]
    if ctx_out:
        out_specs.insert(0, pl.BlockSpec((1, lc, LANE), lambda i, h: (i, 0, h)))
        out_shape.insert(0, jax.ShapeDtypeStruct((b, lc, width), BF16))
    outs = pl.pallas_call(
        functools.partial(_gla_kernel, ctx_out=ctx_out),
        grid=(b, N_HEADS),
        in_specs=[pl.BlockSpec((2, LANE), lambda i, h: (0, h)),
                  pl.BlockSpec((1, LANE), lambda i, h: (0, 0))]
        + [seg(lc, c) for c in cols] + [seg(seq, c) for c in cols],
        out_specs=out_specs,
        out_shape=out_shape,
        scratch_shapes=[
            pltpu.VMEM((nt, LANE), F32),
            pltpu.VMEM((nt, LANE), F32),
            pltpu.VMEM((2, nt, LANE), F32),
            pltpu.VMEM((2, nt, LANE), F32),
            pltpu.VMEM((2, nt, LANE), F32),
            pltpu.VMEM((2, LANE, LANE), F32),
        ],
        compiler_params=_cparams("parallel", "parallel"),
        name="hgrn2",
    )(lb, onorm, *([proj_c] * 5), *([proj_l] * 5))
    return (outs[0], outs[1]) if ctx_out else (None, outs[0])


def _merge_kernel(oa_ref, ob_ref, oc_ref, od_ref, g0_ref, g1_ref, g2_ref, g3_ref,
                  w0_ref, w1_ref, w2_ref, w3_ref, y_ref):
    acc = None
    for o_ref, g_ref, w_ref in ((oa_ref, g0_ref, w0_ref), (ob_ref, g1_ref, w1_ref),
                                (oc_ref, g2_ref, w2_ref), (od_ref, g3_ref, w3_ref)):
        t = _sigmoid(g_ref[0].astype(F32)) * jnp.dot(o_ref[0], w_ref[0], preferred_element_type=F32)
        acc = t if acc is None else acc + t
    y_ref[0] = acc.astype(BF16)


def _merge(outs, proj, w_branch, layer):
    nb, r, mw = outs[0].shape
    d = w_branch.shape[2]
    tm = _tile(r, 512)
    tn = 512
    nj = d // tn
    o_spec = pl.BlockSpec((1, tm, mw), lambda b, t, j: (b, t, 0))
    g_specs = [pl.BlockSpec((1, tm, tn), functools.partial(
        lambda b, t, j, n: (b, t, (COL_GATE * LANE) // tn + n * nj + j), n=n)) for n in range(4)]
    w_specs = [pl.BlockSpec((1, mw, tn), functools.partial(lambda b, t, j, n: (layer * 4 + n, 0, j), n=n))
               for n in range(4)]
    return pl.pallas_call(
        _merge_kernel,
        grid=(nb, r // tm, nj),
        in_specs=[o_spec] * 4 + g_specs + w_specs,
        out_specs=pl.BlockSpec((1, tm, tn), lambda b, t, j: (b, t, j)),
        out_shape=jax.ShapeDtypeStruct((nb, r, d), BF16),
        compiler_params=_cparams("parallel", "parallel", "arbitrary"),
        name="branch_merge",
    )(*outs, *([proj] * 4), *([w_branch] * 4))


def _out_proj_kernel(y_ref, w_ref, x_ref, g_ref, o_ref):
    o_ref[0] = x_ref[0] + g_ref[0] * jnp.dot(y_ref[0], w_ref[0], preferred_element_type=F32)


def _out_proj(y, w, layer, x, gate):
    nb, r, d = x.shape
    tm = _tile(r, 1024)
    tn = 512
    return pl.pallas_call(
        _out_proj_kernel,
        grid=(nb, r // tm, d // tn),
        in_specs=[
            pl.BlockSpec((1, tm, d), lambda b, t, j: (b, t, 0)),
            pl.BlockSpec((1, d, tn), lambda b, t, j: (layer, 0, j)),
            pl.BlockSpec((1, tm, tn), lambda b, t, j: (b, t, j)),
            pl.BlockSpec((1, 1, tn), lambda b, t, j: (b, 0, j)),
        ],
        out_specs=pl.BlockSpec((1, tm, tn), lambda b, t, j: (b, t, j)),
        out_shape=jax.ShapeDtypeStruct((nb, r, d), F32),
        compiler_params=_cparams("parallel", "parallel", "arbitrary"),
        name="out_proj",
    )(y, w, x, gate)


def _route(logits_t, bias_col):
    aff = _sigmoid(logits_t)
    sel = aff + bias_col
    aff_r = [aff[e:e + 1, :] for e in range(N_EXPERTS)]
    sel_r = [sel[e:e + 1, :] for e in range(N_EXPERTS)]
    scores = []
    for g in range(N_GROUPS):
        v = sel_r[g * EXPERTS_PER_GROUP:(g + 1) * EXPERTS_PER_GROUP]
        m1 = functools.reduce(jnp.maximum, v)
        taken = jnp.zeros(m1.shape, jnp.bool_)
        second = jnp.full(m1.shape, -jnp.inf, F32)
        for x in v:
            first = (x == m1) & jnp.logical_not(taken)
            taken = taken | first
            second = jnp.where(first, second, jnp.maximum(second, x))
        scores.append(m1 + second)
    best, gidx = scores[0], jnp.zeros(scores[0].shape, jnp.int32)
    for g in range(1, N_GROUPS):
        better = scores[g] > best
        gidx = jnp.where(better, g, gidx)
        best = jnp.where(better, scores[g], best)
    masked = [jnp.where(gidx == e // EXPERTS_PER_GROUP, sel_r[e], -jnp.inf) for e in range(N_EXPERTS)]

    def first_argmax(vals, exclude):
        bv = jnp.full(vals[0].shape, -jnp.inf, F32)
        bi = jnp.full(vals[0].shape, -1, jnp.int32)
        for e, x in enumerate(vals):
            better = x > bv
            if exclude is not None:
                better = better & (exclude != e)
            bi = jnp.where(better, e, bi)
            bv = jnp.where(better, x, bv)
        return bi

    i1 = first_argmax(masked, None)
    i2 = first_argmax(masked, i1)
    w1 = functools.reduce(jnp.add, [jnp.where(i1 == e, aff_r[e], 0.0) for e in range(N_EXPERTS)])
    w2 = functools.reduce(jnp.add, [jnp.where(i2 == e, aff_r[e], 0.0) for e in range(N_EXPERTS)])
    tot = w1 + w2
    g1, g2 = w1 / tot, w2 / tot
    width = logits_t.shape[1]
    rowi = lax.broadcasted_iota(jnp.int32, (SUBLANE, width), 0)
    local1 = jnp.broadcast_to(i1 - gidx * EXPERTS_PER_GROUP, (SUBLANE, width))
    local2 = jnp.broadcast_to(i2 - gidx * EXPERTS_PER_GROUP, (SUBLANE, width))
    info = (jnp.where(rowi == local1, jnp.broadcast_to(g1, (SUBLANE, width)), 0.0)
            + jnp.where(rowi == local2, jnp.broadcast_to(g2, (SUBLANE, width)), 0.0))
    return jnp.where(rowi == EXPERTS_PER_GROUP,
                     jnp.broadcast_to(gidx.astype(F32), (SUBLANE, width)), info)


def _moe_route_kernel(x_ref, g_ref, sh_ref, sc_ref, wrt_ref, br_ref, hn_ref, info_ref):
    d = x_ref.shape[2]
    h = _modulated_norm(x_ref[0], g_ref[...], sh_ref[0], sc_ref[0])
    logits_t = lax.dot_general(wrt_ref[...], h, NT_DIMS, precision=lax.Precision.HIGHEST,
                               preferred_element_type=F32)
    info = _route(logits_t, br_ref[...])
    info_ref[0] = info
    hn_ref[0, :, 0:d] = h
    pad = jnp.zeros((LANE - SUBLANE, info.shape[1]), F32)
    hn_ref[0, :, d:d + LANE] = jnp.concatenate([info, pad], axis=0).T


def _moe_route(x, g, sh, sc, w_router_t, b_router):
    nb, r, d = x.shape
    tm = _tile(r, 512)
    vec = pl.BlockSpec((1, 1, d), lambda b, t: (b, 0, 0))
    return pl.pallas_call(
        _moe_route_kernel,
        grid=(nb, r // tm),
        in_specs=[
            pl.BlockSpec((1, tm, d), lambda b, t: (b, t, 0)),
            pl.BlockSpec((1, d), lambda b, t: (0, 0)),
            vec, vec,
            pl.BlockSpec((N_EXPERTS, d), lambda b, t: (0, 0)),
            pl.BlockSpec((N_EXPERTS, 1), lambda b, t: (0, 0)),
        ],
        out_specs=[pl.BlockSpec((1, tm, d + LANE), lambda b, t: (b, t, 0)),
                   pl.BlockSpec((1, SUBLANE, tm), lambda b, t: (b, 0, t))],
        out_shape=[jax.ShapeDtypeStruct((nb, r, d + LANE), F32),
                   jax.ShapeDtypeStruct((nb, SUBLANE, r), F32)],
        compiler_params=_cparams("parallel", "parallel"),
        name="moe_route",
    )(x, g, sh, sc, w_router_t, b_router.reshape(N_EXPERTS, 1))


def _row_copy(src, src_row, dst, dst_row, sem):
    return pltpu.make_async_copy(src.at[pl.ds(src_row, 1), :], dst.at[pl.ds(dst_row, 1), :], sem)


def _moe_experts_kernel(dst_ref, tgrp_ref, nval_ref, hn_hbm, wg_ref, wu_ref, wd_ref,
                        y_hbm, xbuf, hb, gbuf, acc, gsem, ssem, *, tm, n_tok):
    i = pl.program_id(0)
    e = pl.program_id(1)
    nt = pl.num_programs(0)
    slot = i % 2
    per = tm // EXPERTS_PER_GROUP
    d = hb.shape[1]

    def gather_row(slot_idx, buf, row):
        s = dst_ref[slot_idx]
        return _row_copy(hn_hbm, jnp.where(s < n_tok, s, 0), xbuf.at[buf], row, gsem.at[buf])

    def wait_scatter(sl):
        pltpu.make_async_copy(acc.at[sl], y_hbm.at[pl.ds(0, tm), :], ssem.at[sl]).wait()

    @pl.when((i == 0) & (e == 0))
    def _():
        def body(r, c):
            gather_row(r, 0, r).start()
            return c
        lax.fori_loop(0, tm, body, 0)
        acc[1] = jnp.zeros(acc.shape[1:], F32)
        for half in range(2):
            fill = pltpu.make_async_copy(acc.at[1], y_hbm.at[pl.ds(n_tok + half * tm, tm), :], ssem.at[1])
            fill.start()
            fill.wait()

    @pl.when(e == 0)
    def _():
        pltpu.make_async_copy(hn_hbm.at[pl.ds(0, tm), :], xbuf.at[slot], gsem.at[slot]).wait()
        hb[...] = xbuf[slot, :, 0:d].astype(BF16)
        gbuf[...] = xbuf[slot, :, d:d + LANE]

        @pl.when(i >= 2)
        def _():
            wait_scatter(slot)
        acc[slot] = jnp.zeros(acc.shape[1:], F32)

    @pl.when(i + 1 < nt)
    def _():
        base = (i + 1) * tm + e * per
        for k in range(per):
            gather_row(base + k, 1 - slot, e * per + k).start()

    @pl.when(nval_ref[i] > 0)
    def _():
        hv = hb[...]
        a = jnp.dot(hv, wg_ref[0], preferred_element_type=F32)
        u = jnp.dot(hv, wu_ref[0], preferred_element_type=F32)
        hid = ((a * _sigmoid(a)) * u).astype(BF16)
        y = jnp.dot(hid, wd_ref[0], preferred_element_type=F32)
        lane = lax.broadcasted_iota(jnp.int32, gbuf.shape, 1)
        col = jnp.sum(jnp.where(lane == e, gbuf[...], 0.0), axis=-1, keepdims=True)
        acc[slot] = acc[slot] + col * y

    @pl.when(e == EXPERTS_PER_GROUP - 1)
    def _():
        def body(r, c):
            _row_copy(acc.at[slot], r, y_hbm, dst_ref[i * tm + r], ssem.at[slot]).start()
            return c
        lax.fori_loop(0, tm, body, 0, unroll=8)

        @pl.when(i == nt - 1)
        def _():
            wait_scatter(slot)

            @pl.when(i >= 1)
            def _():
                wait_scatter(1 - slot)


def _moe_experts(hn_ext, dst, tile_group, tile_nvalid, w_gate, w_up, w_down, layer, tm):
    t, dw = hn_ext.shape
    d = dw - LANE
    dff = w_gate.shape[2]
    ntiles = tile_group.shape[0]

    def wmap(i, e, dst_r, tgrp_r, nval_r):
        return (layer * N_EXPERTS + tgrp_r[i] * EXPERTS_PER_GROUP + e, 0, 0)

    grid_spec = pltpu.PrefetchScalarGridSpec(
        num_scalar_prefetch=3,
        grid=(ntiles, EXPERTS_PER_GROUP),
        in_specs=[
            pl.BlockSpec(memory_space=pl.ANY),
            pl.BlockSpec((1, d, dff), wmap),
            pl.BlockSpec((1, d, dff), wmap),
            pl.BlockSpec((1, dff, d), wmap),
        ],
        out_specs=pl.BlockSpec(memory_space=pl.ANY),
        scratch_shapes=[
            pltpu.VMEM((2, tm, dw), F32),
            pltpu.VMEM((tm, d), BF16),
            pltpu.VMEM((tm, LANE), F32),
            pltpu.VMEM((2, tm, d), F32),
            pltpu.SemaphoreType.DMA((2,)),
            pltpu.SemaphoreType.DMA((2,)),
        ],
    )
    return pl.pallas_call(
        functools.partial(_moe_experts_kernel, tm=tm, n_tok=t),
        grid_spec=grid_spec,
        out_shape=jax.ShapeDtypeStruct((t + 2 * tm, d), F32),
        compiler_params=_cparams("arbitrary", "arbitrary"),
        name="moe_experts",
    )(dst, tile_group, tile_nvalid, hn_ext, w_gate, w_up, w_down)


def _residual_kernel(x_ref, y_ref, g_ref, o_ref):
    o_ref[0] = x_ref[0] + g_ref[0] * y_ref[...]


def _residual(x, y, gate):
    nb, r, d = x.shape
    tm = _tile(r, 512)
    per_b = r // tm
    blk = pl.BlockSpec((1, tm, d), lambda b, t: (b, t, 0))
    return pl.pallas_call(
        _residual_kernel,
        grid=(nb, per_b),
        in_specs=[blk, pl.BlockSpec((tm, d), lambda b, t: (b * per_b + t, 0)),
                  pl.BlockSpec((1, 1, d), lambda b, t: (b, 0, 0))],
        out_specs=blk,
        out_shape=jax.ShapeDtypeStruct((nb, r, d), F32),
        compiler_params=_cparams("parallel", "parallel"),
        name="moe_residual",
    )(x, y, gate)


def _group_layout(grp, tm):
    t = grp.shape[0]
    ntiles = t // tm + N_GROUPS
    nslots = ntiles * tm
    oh = (grp[None, :] == jnp.arange(N_GROUPS)[:, None]).astype(F32).reshape(N_GROUPS, t // LANE, LANE)
    tri = (jnp.arange(LANE)[:, None] <= jnp.arange(LANE)[None, :]).astype(F32)
    within = jnp.einsum('grk,kl->grl', oh, tri)
    row_tot = within[..., -1]
    row_off = jnp.cumsum(row_tot, axis=1) - row_tot
    rank = (jnp.sum((within + row_off[..., None]) * oh, axis=0).reshape(t) - 1.0).astype(jnp.int32)
    counts = jnp.sum(row_tot, axis=1).astype(jnp.int32)
    padded = ((counts + tm - 1) // tm) * tm
    start = jnp.cumsum(padded) - padded
    slot_of_token = start[grp] + rank
    slot_ids = jnp.arange(nslots, dtype=jnp.int32)
    spare = t + ((slot_ids // tm) % 2) * tm + slot_ids % tm
    dst = spare.at[slot_of_token].set(jnp.arange(t, dtype=jnp.int32))
    tile_start = jnp.arange(ntiles, dtype=jnp.int32) * tm
    ends = jnp.cumsum(padded)
    tile_group = jnp.minimum(jnp.sum((tile_start[:, None] >= ends[None, :]).astype(jnp.int32), axis=1),
                             N_GROUPS - 1).astype(jnp.int32)
    tile_nvalid = jnp.clip(counts[tile_group] - (tile_start - start[tile_group]), 0, tm).astype(jnp.int32)
    return dst, tile_group, tile_nvalid


def _moe(x, g, sh, sc, gate, w_router_t, b_router, w_gate, w_up, w_down, layer):
    nb, r, d = x.shape
    t = nb * r
    tm = 512 if t >= 8192 else (256 if t >= 2048 else 128)
    hn_ext, info = _moe_route(x, g, sh, sc, w_router_t, b_router)
    grp = info[:, EXPERTS_PER_GROUP, :].reshape(t).astype(jnp.int32)
    dst, tile_group, tile_nvalid = _group_layout(grp, tm)
    y = _moe_experts(hn_ext.reshape(t, d + LANE), dst, tile_group, tile_nvalid, w_gate, w_up, w_down,
                     layer, tm)
    return _residual(x, y, gate)


def kernel(x, c, ctx, c_ctx, w_ada, b_ada, norm1_g, norm2_g, w_in, qn_a, kn_a, sink_a, qn_b, kn_b,
           lam_b, subln_b, conv_w, conv_b, w_rg, b_rg, w_ig, b_ig, lru_lambda, lb_d, onorm_d,
           w_branch, w_out, w_router, b_router, w_gate, w_up, w_down):
    bsz, seq, d = x.shape
    lc = ctx.shape[1]
    depth = w_in.shape[0]

    n_rows = -(-(bsz + 1) // SUBLANE) * SUBLANE
    c_pad = jnp.zeros((n_rows, d), F32).at[:bsz].set(c).at[bsz].set(c_ctx)
    mod = _ada(c_pad, w_ada, b_ada).reshape(depth, n_rows, 6, d)

    lb_w = jax.nn.softmax(lb_d.astype(F32), axis=0)
    lb_all = jnp.cumsum(lb_w, axis=0) - lb_w[0:1]
    rope_tabs = _rope_tables(seq)
    w_router_t = w_router.T
    w_in_b = w_in.astype(BF16)
    w_branch_b = w_branch.astype(BF16).reshape((depth * w_branch.shape[1],) + w_branch.shape[2:])
    w_out_b = w_out.astype(BF16)
    w_gate_b, w_up_b, w_down_b = (w.astype(BF16).reshape((depth * N_EXPERTS,) + w.shape[2:])
                                  for w in (w_gate, w_up, w_down))

    xl = x
    xc = ctx.reshape(1, bsz * lc, d)
    for l in range(depth):
        ctx_out = l < depth - 1
        mod_l = [mod[l, :bsz, k][:, None, :] for k in range(6)]
        mod_c = [mod[l, bsz:bsz + 1, k][:, None, :] for k in range(6)]
        g1 = norm1_g[l].reshape(1, d)
        g2 = norm2_g[l].reshape(1, d)

        lo_l, hi_l = _norm_proj(xl, g1, mod_l[0], mod_l[1], w_in_b, l)
        lo_c, hi_c = _norm_proj(xc, g1, mod_c[0], mod_c[1], w_in_b, l)
        lo_c = lo_c.reshape(bsz, lc, -1)
        hi_c = hi_c.reshape(bsz, lc, -1)

        qa = qn_a[l] * (HEAD_DIM ** -0.5 * LOG2E)
        qb = jnp.tile(qn_b[l], 2) * (B_DIM ** -0.5 * LOG2E)
        kb = jnp.tile(kn_b[l], 2)
        one = jnp.ones((LANE,), F32)
        gains = jnp.stack([qa] * 8 + [kn_a[l]] * 2 + [one] * 2 + [qb] * 8 + [kb] * 8 + [one] * 8)
        gains = gains.reshape(N_QKV_BLOCKS, 1, LANE)
        p_l = _prep(lo_l, gains, rope_tabs)
        p_c = _prep(lo_c, gains, None)

        bound_a = jnp.maximum(
            LOG2E * HEAD_DIM ** 0.5 * jnp.max(jnp.abs(qn_a[l])) * jnp.max(jnp.abs(kn_a[l])) * LOGIT_MARGIN,
            LOG2E * jnp.max(jnp.abs(sink_a[l])))
        bounded_a = (bound_a < SOFTMAX_SAFE_LOGIT).astype(jnp.int32).reshape(1)
        oa_c, oa_l = _attn_a(sink_a[l], bounded_a, p_l, p_c, ctx_out)

        lq1, lk1, lq2, lk2 = lam_b[l].astype(F32)
        lam_init = 0.8 - 0.6 * math.exp(-0.3 * l)
        lam = (jnp.exp(jnp.sum(lq1 * lk1)) - jnp.exp(jnp.sum(lq2 * lk2)) + lam_init).reshape(1)
        bound_b = LOG2E * B_DIM ** 0.5 * jnp.max(jnp.abs(qn_b[l])) * jnp.max(jnp.abs(kn_b[l])) * LOGIT_MARGIN
        bounded_b = (bound_b < SOFTMAX_SAFE_LOGIT).astype(jnp.int32).reshape(1)
        ob_c, ob_l = _attn_b(lam, bounded_b, p_l, p_c, subln_b[l].reshape(1, LANE), 1.0 - lam_init, ctx_out)

        oc_c, oc_l = _lru(hi_l, hi_c, conv_w[l], conv_b[l], w_rg[l].astype(BF16), b_rg[l],
                          w_ig[l].astype(BF16), b_ig[l], lru_lambda[l], ctx_out)
        od_c, od_l = _gla(hi_l, hi_c, lb_all[l], onorm_d[l].reshape(1, LANE), ctx_out)

        moe_w = (w_router_t, b_router, w_gate_b, w_up_b, w_down_b, l)

        y_l = _merge((oa_l, ob_l, oc_l, od_l), lo_l, w_branch_b, l)
        xl = _out_proj(y_l, w_out_b, l, xl, mod_l[2])
        xl = _moe(xl, g2, mod_l[3], mod_l[4], mod_l[5], *moe_w)
        if ctx_out:
            flat = lambda t: t.reshape(1, bsz * lc, -1)
            y_c = _merge(tuple(flat(t) for t in (oa_c, ob_c, oc_c, od_c)), flat(lo_c), w_branch_b, l)
            xc = _out_proj(y_c, w_out_b, l, xc, mod_c[2])
            xc = _moe(xc, g2, mod_c[3], mod_c[4], mod_c[5], *moe_w)
    return xl
```

```python
import functools
import math

import jax
import jax.numpy as jnp
from jax import lax
from jax.experimental import pallas as pl
from jax.experimental.pallas import tpu as pltpu

F32 = jnp.float32
BF16 = jnp.bfloat16

LANE = 128
SUBLANE = 8
VMEM_LIMIT_BYTES = 56 * 1024 * 1024

EPS = 1e-6
ROPE_THETA = 10000.0
GRID_W = 64
HEAD_DIM = 128
WINDOW = 128
Q_BLOCK = 128
N_HEADS = 8
A_KV_HEADS = 2
A_GROUP = N_HEADS // A_KV_HEADS
A_Q_PER_STEP = 2
B_DIM = 64
B_KEY_CHUNK = 1024
LRU_C = 8.0
GLA_CHUNK = 64
GLA_SAFE_DECAY = 80.0
N_EXPERTS = 16
N_GROUPS = 4
EXPERTS_PER_GROUP = N_EXPERTS // N_GROUPS
NEG_BIG = -1e30
LOG2E = math.log2(math.e)
SOFTMAX_SAFE_LOGIT = 60.0
LOGIT_MARGIN = 1.02
TINY = 1e-37

COL_AQ, COL_AK, COL_AV = 0, 8, 10
COL_BQ, COL_BK, COL_BV = 12, 20, 28
N_QKV_BLOCKS = 36
COL_GATE = N_QKV_BLOCKS
COL_CX, COL_CY = 0, 8
COL_DQ, COL_DFF, COL_DFB, COL_DI, COL_DG = 16, 24, 32, 40, 48
W_IN_QKV = (0, 4608)
W_IN_REC = (4608, 11776)
W_IN_GATE = (11776, 19968)
PROJ_TILE = 512
PROJ_LO_SPARE = 1

NT_DIMS = (((1,), (1,)), ((), ()))
TN_DIMS = (((0,), (0,)), ((), ()))


def _cparams(*sem):
    return pltpu.CompilerParams(dimension_semantics=sem, vmem_limit_bytes=VMEM_LIMIT_BYTES)


def _tile(n, pref):
    t = min(n, pref)
    while n % t:
        t //= 2
    return t


def _sigmoid(x):
    return jax.nn.sigmoid(x)


def _modulated_norm(x, g, sh, sc):
    ms = jnp.mean(x * x, axis=-1, keepdims=True)
    return (x * lax.rsqrt(ms + EPS) * g) * (1.0 + sc) + sh


def _ada_kernel(c_ref, w_ref, b_ref, o_ref):
    c = c_ref[...]
    s = (c * _sigmoid(c)).astype(BF16)
    o_ref[0] = jnp.dot(s, w_ref[0].astype(BF16), preferred_element_type=F32) + b_ref[0]


def _ada(c_pad, w_ada, b_ada):
    nl, d, n = w_ada.shape
    rows = c_pad.shape[0]
    tn = _tile(n, 1024)
    return pl.pallas_call(
        _ada_kernel,
        grid=(nl, n // tn),
        in_specs=[
            pl.BlockSpec((rows, d), lambda l, j: (0, 0)),
            pl.BlockSpec((1, d, tn), lambda l, j: (l, 0, j)),
            pl.BlockSpec((1, 1, tn), lambda l, j: (l, 0, j)),
        ],
        out_specs=pl.BlockSpec((1, rows, tn), lambda l, j: (l, 0, j)),
        out_shape=jax.ShapeDtypeStruct((nl, rows, n), F32),
        compiler_params=_cparams("parallel", "parallel"),
        name="ada_mod",
    )(c_pad, w_ada, b_ada.reshape(nl, 1, n))


def _norm_proj_kernel(x_ref, g_ref, sh_ref, sc_ref, wa_ref, wb_ref, lo_ref, hi_ref, hn_ref, *, n_lo):
    j = pl.program_id(2)

    @pl.when(j == 0)
    def _():
        hn_ref[...] = _modulated_norm(x_ref[0], g_ref[...], sh_ref[0], sc_ref[0]).astype(BF16)

    def pair():
        hn = hn_ref[...]
        return jnp.concatenate([jnp.dot(hn, wa_ref[0], preferred_element_type=F32),
                                jnp.dot(hn, wb_ref[0], preferred_element_type=F32)], axis=1)

    @pl.when(j < n_lo)
    def _():
        lo_ref[0] = pair().astype(BF16)

    @pl.when(j >= n_lo)
    def _():
        hi_ref[0] = pair()


def _proj_weight_tile(t):
    n_qkv = (W_IN_QKV[1] - W_IN_QKV[0]) // PROJ_TILE
    n_gate = (W_IN_GATE[1] - W_IN_GATE[0]) // PROJ_TILE
    gate0 = W_IN_GATE[0] // PROJ_TILE
    rec0 = W_IN_REC[0] // PROJ_TILE
    n_lo = n_qkv + n_gate
    return jnp.where(t < n_qkv, t,
                     jnp.where(t < n_lo, t - n_qkv + gate0,
                               jnp.where(t < n_lo + PROJ_LO_SPARE, 0, t - n_lo - PROJ_LO_SPARE + rec0)))


def _norm_proj(x, g, sh, sc, w, layer):
    nb, r, d = x.shape
    tn = PROJ_TILE
    lo_tiles = (W_IN_QKV[1] - W_IN_QKV[0] + W_IN_GATE[1] - W_IN_GATE[0]) // tn + PROJ_LO_SPARE
    hi_tiles = (W_IN_REC[1] - W_IN_REC[0]) // tn
    n_lo, n_hi = lo_tiles // 2, hi_tiles // 2
    tm = _tile(r, 1024)
    return pl.pallas_call(
        functools.partial(_norm_proj_kernel, n_lo=n_lo),
        grid=(nb, r // tm, n_lo + n_hi),
        in_specs=[
            pl.BlockSpec((1, tm, d), lambda b, t, j: (b, t, 0)),
            pl.BlockSpec((1, d), lambda b, t, j: (0, 0)),
            pl.BlockSpec((1, 1, d), lambda b, t, j: (b, 0, 0)),
            pl.BlockSpec((1, 1, d), lambda b, t, j: (b, 0, 0)),
            pl.BlockSpec((1, d, tn), lambda b, t, j: (layer, 0, _proj_weight_tile(2 * j))),
            pl.BlockSpec((1, d, tn), lambda b, t, j: (layer, 0, _proj_weight_tile(2 * j + 1))),
        ],
        out_specs=[
            pl.BlockSpec((1, tm, 2 * tn), lambda b, t, j: (b, t, jnp.minimum(j, n_lo - 1))),
            pl.BlockSpec((1, tm, 2 * tn), lambda b, t, j: (b, t, jnp.maximum(j - n_lo, 0))),
        ],
        out_shape=[jax.ShapeDtypeStruct((nb, r, lo_tiles * tn), BF16),
                   jax.ShapeDtypeStruct((nb, r, hi_tiles * tn), F32)],
        scratch_shapes=[pltpu.VMEM((tm, d), BF16)],
        compiler_params=_cparams("parallel", "parallel", "arbitrary"),
        name="norm_proj",
    )(x, g, sh, sc, w, w)


def _group_mean_matrix(group):
    k = lax.broadcasted_iota(jnp.int32, (LANE, LANE), 0)
    l = lax.broadcasted_iota(jnp.int32, (LANE, LANE), 1)
    return jnp.where(k // group == l // group, 1.0 / group, 0.0).astype(BF16)


def _rope_partner_matrix(half):
    k = lax.broadcasted_iota(jnp.int32, (LANE, LANE), 0)
    l = lax.broadcasted_iota(jnp.int32, (LANE, LANE), 1)
    src = jnp.where((l % (2 * half)) < half, l + half, l - half)
    return jnp.where(k == src, 1.0, 0.0).astype(BF16)


def _group_mean(x2, mean_m):
    hi = x2.astype(BF16)
    lo = (x2 - hi.astype(F32)).astype(BF16)
    return (jnp.dot(hi, mean_m, preferred_element_type=F32)
            + jnp.dot(lo, mean_m, preferred_element_type=F32))


def _prep_kernel(*refs, rope):
    if rope:
        x_ref, g_ref, ca_ref, sa_ref, cb_ref, sb_ref, o_ref = refs
    else:
        x_ref, g_ref, o_ref = refs
    tm = x_ref.shape[1]

    def qk_blocks(col0, nblk, group, half, cos_ref, sin_ref):
        blocks = [slice((col0 + j) * LANE, (col0 + j + 1) * LANE) for j in range(nblk)]
        x = jnp.concatenate([x_ref[0, :, c].astype(F32) for c in blocks], axis=0)
        ms = _group_mean(x * x, _group_mean_matrix(group))
        y = (x * lax.rsqrt(ms + EPS)).reshape(nblk, tm, LANE) * g_ref[col0:col0 + nblk]
        if rope:
            part = jnp.dot(y.reshape(nblk * tm, LANE).astype(BF16), _rope_partner_matrix(half),
                           preferred_element_type=F32).reshape(nblk, tm, LANE)
            y = y * cos_ref[...][None] + part * sin_ref[...][None]
        for j, c in enumerate(blocks):
            o_ref[0, :, c] = y[j].astype(BF16)

    qk_blocks(COL_AQ, COL_AV - COL_AQ, HEAD_DIM, 32, ca_ref if rope else None, sa_ref if rope else None)
    qk_blocks(COL_BQ, COL_BV - COL_BQ, B_DIM, 16, cb_ref if rope else None, sb_ref if rope else None)
    for c0, c1 in ((COL_AV, COL_BQ), (COL_BV, N_QKV_BLOCKS)):
        o_ref[0, :, c0 * LANE:c1 * LANE] = x_ref[0, :, c0 * LANE:c1 * LANE]


def _prep(proj, gains, rope_tabs):
    nb, r, _ = proj.shape
    tm = _tile(r, 256)
    width = N_QKV_BLOCKS * LANE
    rope = rope_tabs is not None
    in_specs = [
        pl.BlockSpec((1, tm, width), lambda b, t: (b, t, 0)),
        pl.BlockSpec((N_QKV_BLOCKS, 1, LANE), lambda b, t: (0, 0, 0)),
    ]
    args = [proj, gains]
    if rope:
        in_specs += [pl.BlockSpec((tm, LANE), lambda b, t: (t, 0))] * 4
        args += list(rope_tabs)
    return pl.pallas_call(
        functools.partial(_prep_kernel, rope=rope),
        grid=(nb, r // tm),
        in_specs=in_specs,
        out_specs=pl.BlockSpec((1, tm, width), lambda b, t: (b, t, 0)),
        out_shape=jax.ShapeDtypeStruct((nb, r, width), BF16),
        compiler_params=_cparams("parallel", "parallel"),
        name="qkv_prep",
    )(*args)


def _rope_tables(seq):
    pos = jnp.arange(seq)
    rows = (pos // GRID_W).astype(F32)[:, None]
    cols = (pos % GRID_W).astype(F32)[:, None]
    lane = jnp.arange(LANE)

    def tables(half):
        inv = ROPE_THETA ** (-(lane % half).astype(F32) / half)
        use_rows = (lane % (4 * half)) < 2 * half
        ang = jnp.where(use_rows[None, :], rows, cols) * inv[None, :]
        sign = jnp.where((lane % (2 * half)) < half, -1.0, 1.0)
        return jnp.cos(ang), jnp.sin(ang) * sign[None, :]

    ca, sa = tables(32)
    cb, sb = tables(16)
    return ca, sa, cb, sb


def _stack_heads(q):
    return jnp.concatenate([q[:, g * HEAD_DIM:(g + 1) * HEAD_DIM] for g in range(A_GROUP)], axis=0)


def _unstack_heads(o, rows):
    return jnp.concatenate([o[g * rows:(g + 1) * rows] for g in range(A_GROUP)], axis=1)


def _sink_column(sink_ref, kvh, rows):
    return jnp.concatenate(
        [jnp.full((rows, 1), sink_ref[kvh * A_GROUP + g] * LOG2E, F32) for g in range(A_GROUP)], axis=0)


def _attn_a_lat_kernel(sink_ref, bounded_ref, q_ref, bias_ref, kl_ref, vl_ref, kc_ref, vc_ref, o_ref, *, seq):
    band = 3 * Q_BLOCK
    nblk = seq // Q_BLOCK

    def run(bounded):
        for qi in range(q_ref.shape[1] // Q_BLOCK):
            i = pl.program_id(1) * (q_ref.shape[1] // Q_BLOCK) + qi
            rows = slice(qi * Q_BLOCK, (qi + 1) * Q_BLOCK)
            first = jnp.clip(i - 1, 0, nblk - 3)
            start = pl.multiple_of(first * Q_BLOCK, Q_BLOCK)
            bias = bias_ref[i - first]
            outs = []
            for kvh in range(A_KV_HEADS):
                hs = slice(kvh * HEAD_DIM, (kvh + 1) * HEAD_DIM)
                qs = _stack_heads(q_ref[0, rows, kvh * A_GROUP * HEAD_DIM:(kvh + 1) * A_GROUP * HEAD_DIM])
                kb = kl_ref[0, pl.ds(start, band), hs]
                vb = vl_ref[0, pl.ds(start, band), hs]
                s_loc = lax.dot_general(qs, kb, NT_DIMS, preferred_element_type=F32)
                s_loc = (s_loc.reshape(A_GROUP, Q_BLOCK, band) + bias[None]).reshape(A_GROUP * Q_BLOCK, band)
                s_ctx = lax.dot_general(qs, kc_ref[0, :, hs], NT_DIMS, preferred_element_type=F32)
                sk = _sink_column(sink_ref, kvh, Q_BLOCK)
                if not bounded:
                    m = jnp.maximum(jnp.maximum(jnp.max(s_loc, axis=-1, keepdims=True),
                                                jnp.max(s_ctx, axis=-1, keepdims=True)), sk)
                    s_loc, s_ctx, sk = s_loc - m, s_ctx - m, sk - m
                p_loc = jnp.exp2(s_loc)
                p_ctx = jnp.exp2(s_ctx)
                den = (jnp.sum(p_loc, axis=-1, keepdims=True) + jnp.sum(p_ctx, axis=-1, keepdims=True)
                       + jnp.exp2(sk))
                o = (jnp.dot(p_loc.astype(BF16), vb, preferred_element_type=F32)
                     + jnp.dot(p_ctx.astype(BF16), vc_ref[0, :, hs], preferred_element_type=F32)) / den
                outs.append(_unstack_heads(o, Q_BLOCK))
            o_ref[0, rows, :] = jnp.concatenate(outs, axis=1).astype(BF16)

    pl.when(bounded_ref[0] > 0)(lambda: run(True))
    pl.when(bounded_ref[0] <= 0)(lambda: run(False))


def _attn_a_ctx_kernel(sink_ref, q_ref, kc_ref, vc_ref, o_ref):
    kvh = pl.program_id(1)
    rows = q_ref.shape[1]
    qs = _stack_heads(q_ref[0])
    s = lax.dot_general(qs, kc_ref[0], NT_DIMS, preferred_element_type=F32)
    sk = _sink_column(sink_ref, kvh, rows)
    m = jnp.maximum(jnp.max(s, axis=-1, keepdims=True), sk)
    p = jnp.exp2(s - m)
    den = jnp.sum(p, axis=-1, keepdims=True) + jnp.exp2(sk - m)
    o = jnp.dot(p.astype(BF16), vc_ref[0], preferred_element_type=F32) / den
    o_ref[0] = _unstack_heads(o, rows).astype(BF16)


def _window_bias():
    band = 3 * Q_BLOCK
    row = jnp.arange(Q_BLOCK)[None, :, None]
    col = jnp.arange(band)[None, None, :]
    delta = (jnp.arange(3) * Q_BLOCK)[:, None, None]
    return jnp.where(jnp.abs(col - delta - row) <= WINDOW, 0.0, NEG_BIG).astype(F32)


def _attn_a(sink, bounded, p_l, p_c, ctx_out):
    b, seq, _ = p_l.shape
    lc = p_c.shape[1]
    gw = A_GROUP * HEAD_DIM
    nblk = seq // Q_BLOCK
    kvw = A_KV_HEADS * HEAD_DIM
    smem = pl.BlockSpec(memory_space=pltpu.SMEM)

    qrows = A_Q_PER_STEP * Q_BLOCK if nblk % A_Q_PER_STEP == 0 else Q_BLOCK
    o_l = pl.pallas_call(
        functools.partial(_attn_a_lat_kernel, seq=seq),
        grid=(b, seq // qrows),
        in_specs=[
            smem, smem,
            pl.BlockSpec((1, qrows, N_HEADS * HEAD_DIM), lambda n, i: (n, i, 0)),
            pl.BlockSpec((3, Q_BLOCK, 3 * Q_BLOCK), lambda n, i: (0, 0, 0)),
            pl.BlockSpec((1, seq, kvw), lambda n, i: (n, 0, COL_AK // A_KV_HEADS)),
            pl.BlockSpec((1, seq, kvw), lambda n, i: (n, 0, COL_AV // A_KV_HEADS)),
            pl.BlockSpec((1, lc, kvw), lambda n, i: (n, 0, COL_AK // A_KV_HEADS)),
            pl.BlockSpec((1, lc, kvw), lambda n, i: (n, 0, COL_AV // A_KV_HEADS)),
        ],
        out_specs=pl.BlockSpec((1, qrows, N_HEADS * HEAD_DIM), lambda n, i: (n, i, 0)),
        out_shape=jax.ShapeDtypeStruct((b, seq, N_HEADS * HEAD_DIM), BF16),
        compiler_params=_cparams("parallel", "arbitrary"),
        name="attn_a_latent",
    )(sink, bounded, p_l, _window_bias(), p_l, p_l, p_c, p_c)
    o_c = None
    if ctx_out:
        o_c = pl.pallas_call(
            _attn_a_ctx_kernel,
            grid=(b, A_KV_HEADS),
            in_specs=[
                smem,
                pl.BlockSpec((1, lc, gw), lambda n, h: (n, 0, h)),
                pl.BlockSpec((1, lc, LANE), lambda n, h: (n, 0, COL_AK + h)),
                pl.BlockSpec((1, lc, LANE), lambda n, h: (n, 0, COL_AV + h)),
            ],
            out_specs=pl.BlockSpec((1, lc, gw), lambda n, h: (n, 0, h)),
            out_shape=jax.ShapeDtypeStruct((b, lc, N_HEADS * HEAD_DIM), BF16),
            compiler_params=_cparams("parallel", "parallel"),
            name="attn_a_ctx",
        )(sink, p_c, p_c, p_c)
    return o_c, o_l


def _attn_b_kernel(lam_ref, bounded_ref, q_ref, *refs, nseg, post_scale):
    k_refs = refs[:nseg]
    v_refs = refs[nseg:2 * nseg]
    g_ref, o_ref = refs[2 * nseg], refs[2 * nseg + 1]
    q = q_ref[0]
    tq = q.shape[0]
    lane = lax.broadcasted_iota(jnp.int32, q.shape, 1)
    zero = jnp.zeros_like(q)
    q2 = jnp.concatenate([jnp.where(lane < B_DIM, q, zero), jnp.where(lane < B_DIM, zero, q)], axis=0)
    chunks = []
    for k_ref, v_ref in zip(k_refs, v_refs):
        kc = _tile(k_ref.shape[1], B_KEY_CHUNK)
        chunks += [(k_ref, v_ref, c0, kc) for c0 in range(0, k_ref.shape[1], kc)]

    def scores(k_ref, c0, kc):
        return lax.dot_general(q2, k_ref[0, c0:c0 + kc, :], NT_DIMS, preferred_element_type=F32)

    def finish(acc, den):
        o2 = acc / den
        o = o2[:tq] - lam_ref[0] * o2[tq:]
        ms = jnp.mean(o * o, axis=-1, keepdims=True)
        o_ref[0] = ((o * lax.rsqrt(ms + EPS) * g_ref[...]) * post_scale).astype(BF16)

    @pl.when(bounded_ref[0] > 0)
    def _():
        den = jnp.zeros((2 * tq, 1), F32)
        acc = jnp.zeros((2 * tq, LANE), F32)
        for k_ref, v_ref, c0, kc in chunks:
            p = jnp.exp2(scores(k_ref, c0, kc))
            den = den + jnp.sum(p, axis=-1, keepdims=True)
            acc = acc + jnp.dot(p.astype(BF16), v_ref[0, c0:c0 + kc, :], preferred_element_type=F32)
        finish(acc, den)

    @pl.when(bounded_ref[0] <= 0)
    def _():
        m = jnp.full((2 * tq, 1), NEG_BIG, F32)
        den = jnp.zeros((2 * tq, 1), F32)
        acc = jnp.zeros((2 * tq, LANE), F32)
        for k_ref, v_ref, c0, kc in chunks:
            s = scores(k_ref, c0, kc)
            m_new = jnp.maximum(m, jnp.max(s, axis=-1, keepdims=True))
            alpha = jnp.exp2(m - m_new)
            p = jnp.exp2(s - m_new)
            den = alpha * den + jnp.sum(p, axis=-1, keepdims=True)
            acc = alpha * acc + jnp.dot(p.astype(BF16), v_ref[0, c0:c0 + kc, :],
                                        preferred_element_type=F32)
            m = m_new
        finish(acc, den)


def _attn_b_call(lam, bounded, q_src, k_srcs, subln, post_scale, name):
    b, rq, _ = q_src.shape
    tq = _tile(rq, 512)
    smem = pl.BlockSpec(memory_space=pltpu.SMEM)
    in_specs = [smem, smem, pl.BlockSpec((1, tq, LANE), lambda n, h, i: (n, i, COL_BQ + h))]
    in_specs += [pl.BlockSpec((1, s.shape[1], LANE), lambda n, h, i: (n, 0, COL_BK + h)) for s in k_srcs]
    in_specs += [pl.BlockSpec((1, s.shape[1], LANE), lambda n, h, i: (n, 0, COL_BV + h)) for s in k_srcs]
    in_specs += [pl.BlockSpec((1, LANE), lambda n, h, i: (0, 0))]
    return pl.pallas_call(
        functools.partial(_attn_b_kernel, nseg=len(k_srcs), post_scale=post_scale),
        grid=(b, N_HEADS, rq // tq),
        in_specs=in_specs,
        out_specs=pl.BlockSpec((1, tq, LANE), lambda n, h, i: (n, i, h)),
        out_shape=jax.ShapeDtypeStruct((b, rq, N_HEADS * HEAD_DIM), BF16),
        compiler_params=_cparams("parallel", "parallel", "arbitrary"),
        name=name,
    )(lam, bounded, q_src, *k_srcs, *k_srcs, subln)


def _attn_b(lam, bounded, p_l, p_c, subln, post_scale, ctx_out):
    o_l = _attn_b_call(lam, bounded, p_l, [p_c, p_l], subln, post_scale, "attn_b_latent")
    o_c = _attn_b_call(lam, bounded, p_c, [p_c], subln, post_scale, "attn_b_ctx") if ctx_out else None
    return o_c, o_l


def _centred_conv(x_ref, pad_ref, w_ref, b_ref):
    n = x_ref.shape[1]
    zeros = jnp.zeros((SUBLANE, LANE), F32)
    pad_ref[0:SUBLANE, :] = zeros
    pad_ref[SUBLANE:SUBLANE + n, :] = x_ref[0]
    pad_ref[SUBLANE + n:2 * SUBLANE + n, :] = zeros
    out = b_ref[...]
    for tap in range(4):
        out = out + pad_ref[pl.ds(SUBLANE - 2 + tap, n), :] * w_ref[tap:tap + 1, :]
    return out


def _every_8th(ref, d, j, n):
    return ref[d, pl.ds(j, n, stride=SUBLANE), :]


def _tile_order(reverse):
    return range(SUBLANE - 1, -1, -1) if reverse else range(SUBLANE)


def _scan_summaries(a_ref, v_ref, d, n, reverse):
    h = p = None
    for j in _tile_order(reverse):
        a, v = _every_8th(a_ref, d, j, n), _every_8th(v_ref, d, j, n)
        h, p = (v, a) if h is None else (a * h + v, a * p)
    return p, h


def _scan_entering(a_ref, v_ref, d, n, carry_in, reverse, store):
    e = carry_in
    for j in _tile_order(reverse):
        store(j, e)
        e = _every_8th(a_ref, d, j, n) * e + _every_8th(v_ref, d, j, n)


def _scan_apply(a_ref, v_ref, d, n, carry_in, reverse, store):
    h = carry_in
    for j in _tile_order(reverse):
        h = _every_8th(a_ref, d, j, n) * h + _every_8th(v_ref, d, j, n)
        store(j, h)


def _seg_tile(t, ntc, ntl, reverse):
    if not reverse:
        return t
    return jnp.where(t < ntc, ntc - 1 - t, 2 * ntc + ntl - 1 - t)


def _lru_kernel(xc_ref, yc_ref, xl_ref, yl_ref, cw_ref, cb_ref, wr_ref, br_ref, wi_ref, bi_ref,
                lam_ref, *refs, ctx_out):
    if ctx_out:
        oc_ref, ol_ref, ac_ref, hl_ref, hs_ref, pad_ref, tp_ref, th_ref, sp_ref, sh_ref, c3_ref, e1_ref = refs
    else:
        ol_ref, ac_ref, hl_ref, hs_ref, pad_ref, tp_ref, th_ref, sp_ref, sh_ref, c3_ref, e1_ref = refs
    lc, seq = xc_ref.shape[1], xl_ref.shape[1]
    for x_ref, off in ((xc_ref, 0), (xl_ref, lc)):
        n = x_ref.shape[1]
        u = _centred_conv(x_ref, pad_ref, cw_ref, cb_ref)
        ub = u.astype(BF16)
        for d in range(2):
            lam = lam_ref[d:d + 1, :]
            sp = jnp.maximum(-lam, 0.0) + jnp.log1p(jnp.exp(-jnp.abs(lam)))
            r = _sigmoid(jnp.dot(ub, wr_ref[d, 0], preferred_element_type=F32) + br_ref[d:d + 1, :])
            gi = _sigmoid(jnp.dot(ub, wi_ref[d, 0], preferred_element_type=F32) + bi_ref[d:d + 1, :])
            log_a = -LRU_C * r * sp
            a = jnp.exp(log_a)
            y = jnp.tanh(-log_a) * (1.0 + a * a)
            v = (y * lax.rsqrt(jnp.maximum(y, TINY))) * gi * u
            ac_ref[d, off:off + n, :] = a
            hl_ref[d, off:off + n, :] = v

    nt1 = (lc + seq) // SUBLANE
    nt2 = nt1 // SUBLANE
    group_rows = SUBLANE * SUBLANE
    for d in range(2):
        tp_ref[d], th_ref[d] = _scan_summaries(ac_ref, hl_ref, d, nt1, d == 1)
        sp_ref[d], sh_ref[d] = _scan_summaries(tp_ref, th_ref, d, nt2, d == 1)

    def step(s, carry):
        new = []
        for d in range(2):
            g = pl.ds(_seg_tile(s, lc // group_rows, seq // group_rows, d == 1), 1)
            c3_ref[d, g, :] = carry[d]
            new.append(sp_ref[d, g, :] * carry[d] + sh_ref[d, g, :])
        return tuple(new)

    zero = jnp.zeros((1, LANE), F32)
    lax.fori_loop(0, nt2, step, (zero, zero))

    for d in range(2):
        def store_entering(j, e, d=d):
            e1_ref[d, pl.ds(j, nt2, stride=SUBLANE), :] = e

        def store_h(j, h, d=d):
            if d == 0:
                hs_ref[pl.ds(j, nt1, stride=SUBLANE), :] = h
            else:
                hl_ref[d, pl.ds(j, nt1, stride=SUBLANE), :] = h

        _scan_entering(tp_ref, th_ref, d, nt2, c3_ref[d], d == 1, store_entering)
        _scan_apply(ac_ref, hl_ref, d, nt1, e1_ref[d], d == 1, store_h)
    if ctx_out:
        hc = hs_ref[0:lc, :] + hl_ref[1, 0:lc, :]
        oc_ref[0] = (hc * jax.nn.gelu(yc_ref[0])).astype(BF16)
    hl = hs_ref[lc:lc + seq, :] + hl_ref[1, lc:lc + seq, :]
    ol_ref[0] = (hl * jax.nn.gelu(yl_ref[0])).astype(BF16)


def _lru(proj_l, proj_c, conv_w, conv_b, w_r, b_r, w_i, b_i, lam, ctx_out):
    b, seq, _ = proj_l.shape
    lc = proj_c.shape[1]
    nblk = w_r.shape[1]
    width = nblk * LANE
    nt1 = (lc + seq) // SUBLANE
    nt2 = nt1 // SUBLANE

    def seg(n, col):
        return pl.BlockSpec((1, n, LANE), lambda i, j: (i, 0, col + j))

    vec2 = pl.BlockSpec((2, LANE), lambda i, j: (0, j))
    wspec = pl.BlockSpec((2, 1, LANE, LANE), lambda i, j: (0, j, 0, 0))
    out_specs = [pl.BlockSpec((1, seq, LANE), lambda i, j: (i, 0, j))]
    out_shape = [jax.ShapeDtypeStruct((b, seq, width), BF16)]
    if ctx_out:
        out_specs.insert(0, pl.BlockSpec((1, lc, LANE), lambda i, j: (i, 0, j)))
        out_shape.insert(0, jax.ShapeDtypeStruct((b, lc, width), BF16))
    outs = pl.pallas_call(
        functools.partial(_lru_kernel, ctx_out=ctx_out),
        grid=(b, nblk),
        in_specs=[
            seg(lc, COL_CX), seg(lc, COL_CY), seg(seq, COL_CX), seg(seq, COL_CY),
            pl.BlockSpec((4, LANE), lambda i, j: (0, j)),
            pl.BlockSpec((1, LANE), lambda i, j: (0, j)),
            wspec, vec2, wspec, vec2, vec2,
        ],
        out_specs=out_specs,
        out_shape=out_shape,
        scratch_shapes=[
            pltpu.VMEM((2, lc + seq, LANE), F32),
            pltpu.VMEM((2, lc + seq, LANE), F32),
            pltpu.VMEM((lc + seq, LANE), F32),
            pltpu.VMEM((max(lc, seq) + 2 * SUBLANE, LANE), F32),
            pltpu.VMEM((2, nt1, LANE), F32),
            pltpu.VMEM((2, nt1, LANE), F32),
            pltpu.VMEM((2, nt2, LANE), F32),
            pltpu.VMEM((2, nt2, LANE), F32),
            pltpu.VMEM((2, nt2, LANE), F32),
            pltpu.VMEM((2, nt1, LANE), F32),
        ],
        compiler_params=_cparams("parallel", "parallel"),
        name="rglru",
    )(proj_c, proj_c, proj_l, proj_l, conv_w, conv_b.reshape(1, width), w_r, b_r, w_i, b_i, lam)
    return (outs[0], outs[1]) if ctx_out else (None, outs[0])


def _gla_level_masks(reverse):
    c = GLA_CHUNK
    row = lax.broadcasted_iota(jnp.int32, (c, c), 0)
    col = lax.broadcasted_iota(jnp.int32, (c, c), 1)
    masks = {}
    for s in (32, 16, 8):
        same = (row // (2 * s)) == (col // (2 * s))
        if reverse:
            masks[s] = same & ((row % (2 * s)) < s) & ((col % (2 * s)) >= s)
        else:
            masks[s] = same & ((row % (2 * s)) >= s) & ((col % (2 * s)) < s)
    tri = (row <= col) if reverse else (row >= col)
    return masks, tri.astype(F32)


def _gla_chunk(q, k, v, g, st, masks, tri, reverse):
    c = GLA_CHUNK
    b = jnp.dot(tri, g, precision=lax.Precision.HIGHEST, preferred_element_type=F32)
    att = jnp.zeros((c, c), F32)
    for s in (32, 16, 8):
        b3 = b.reshape(c // (2 * s), 2 * s, LANE)
        rr = s if reverse else s - 1
        rho = jnp.broadcast_to(b3[:, rr:rr + 1, :], b3.shape).reshape(c, LANE)
        e = jnp.exp(-jnp.abs(b - rho))
        a = lax.dot_general((q * e).astype(BF16), (k * e).astype(BF16), NT_DIMS,
                            preferred_element_type=F32)
        att = att + jnp.where(masks[s], a, 0.0)
    vb = v.astype(BF16)
    o = jnp.dot(att.astype(BF16), vb, preferred_element_type=F32)
    nb = c // SUBLANE
    b3 = b.reshape(nb, SUBLANE, LANE)
    q3 = q.reshape(nb, SUBLANE, LANE)
    k3 = k.reshape(nb, SUBLANE, LANE)
    v3 = v.reshape(nb, SUBLANE, LANE)
    rowi = lax.broadcasted_iota(jnp.int32, b3.shape, 1)
    od = jnp.zeros(b3.shape, F32)
    for jj in range(SUBLANE):
        keep = (rowi <= jj) if reverse else (rowi >= jj)
        e = jnp.where(keep, jnp.exp(jnp.minimum(b3 - b3[:, jj:jj + 1, :], 0.0)), 0.0)
        sj = jnp.sum(q3 * e * k3[:, jj:jj + 1, :], axis=-1, keepdims=True)
        od = od + sj * v3[:, jj:jj + 1, :]
    o = o + od.reshape(c, LANE)
    o = o + lax.dot_general((q * jnp.exp(b)).astype(BF16), st.astype(BF16), NT_DIMS,
                            preferred_element_type=F32)
    b_end = b[0:1, :] if reverse else b[c - 1:c, :]
    khat = (k * jnp.exp(b_end - b)).astype(BF16)
    st_new = st * jnp.exp(b_end) + lax.dot_general(vb, khat, TN_DIMS, preferred_element_type=F32)
    return o, st_new


def _gla_fast_intra(q, k, v, g, tri, reverse):
    c = GLA_CHUNK
    g_hi = g.astype(BF16)
    g_lo = (g - g_hi.astype(F32)).astype(BF16)
    b2 = jnp.dot(tri.astype(BF16), jnp.concatenate([g_hi, g_lo], axis=1), preferred_element_type=F32)
    b = b2[:, :LANE] + b2[:, LANE:]
    rr = c // 2 if reverse else c // 2 - 1
    rho = b[rr:rr + 1, :]
    qt = q * jnp.exp(b - rho)
    kt = k * jnp.exp(rho - b)
    a = lax.dot_general(qt.astype(BF16), kt.astype(BF16), NT_DIMS, preferred_element_type=F32)
    att = jnp.where(tri > 0.0, a, 0.0)
    vb = v.astype(BF16)
    o_intra = jnp.dot(att.astype(BF16), vb, preferred_element_type=F32)
    b_end = b[0:1, :] if reverse else b[c - 1:c, :]
    qe = (qt * jnp.exp(rho)).astype(BF16)
    khat = (kt * jnp.exp(b_end - rho)).astype(BF16)
    return o_intra, qe, khat, vb, jnp.exp(b_end)


def _gla_fast_inter(intra, st):
    o_intra, qe, khat, vb, decay = intra
    o = o_intra + lax.dot_general(qe, st.astype(BF16), NT_DIMS, preferred_element_type=F32)
    st_new = st * decay + lax.dot_general(vb, khat, TN_DIMS, preferred_element_type=F32)
    return o, st_new


def _gla_kernel(lb_ref, on_ref, qc_ref, ffc_ref, fbc_ref, ic_ref, gc_ref,
                ql_ref, ffl_ref, fbl_ref, il_ref, gl_ref, *refs, ctx_out):
    if ctx_out:
        oc_ref, ol_ref, q_s, v_s, g_s, k_s, o_s, st_s = refs
    else:
        ol_ref, q_s, v_s, g_s, k_s, o_s, st_s = refs
    lc, seq = qc_ref.shape[1], ql_ref.shape[1]
    ncc, ncl = lc // GLA_CHUNK, seq // GLA_CHUNK
    for off, n, q_ref, i_ref, f_refs in ((0, lc, qc_ref, ic_ref, (ffc_ref, fbc_ref)),
                                         (lc, seq, ql_ref, il_ref, (ffl_ref, fbl_ref))):
        q_s[off:off + n, :] = q_ref[0]
        v_s[off:off + n, :] = i_ref[0]
        for d in range(2):
            z = f_refs[d][0]
            lbd = lb_ref[d:d + 1, :]
            ez = jnp.exp(-jnp.abs(z))
            r = 1.0 / (1.0 + ez)
            pos = z >= 0.0
            sig_p = jnp.where(pos, r, ez * r)
            sig_n = jnp.where(pos, ez * r, r)
            g_s[d, off:off + n, :] = jnp.log(lbd + (1.0 - lbd) * sig_p)
            k_s[d, off:off + n, :] = (1.0 - lbd) * sig_n
    st_s[...] = jnp.zeros(st_s.shape, F32)
    consts = [_gla_level_masks(False), _gla_level_masks(True)]

    half = GLA_CHUNK // 2
    worst = jnp.zeros((1, LANE), F32)
    for d in range(2):
        hs = jnp.sum(g_s[d].reshape((lc + seq) // half, half, LANE), axis=1)
        worst = jnp.maximum(worst, jnp.max(-hs, axis=0, keepdims=True))
    safe = jnp.max(worst) < GLA_SAFE_DECAY

    def chunk_rows(cidx, d):
        chunk = _seg_tile(cidx, ncc, ncl, d == 1)
        return pl.ds(pl.multiple_of(chunk * GLA_CHUNK, GLA_CHUNK), GLA_CHUNK)

    def robust_step(cidx, carry):
        for d in range(2):
            rows = chunk_rows(cidx, d)
            o, st_new = _gla_chunk(q_s[rows, :], k_s[d, rows, :], v_s[rows, :], g_s[d, rows, :],
                                   st_s[d], consts[d][0], consts[d][1], d == 1)
            st_s[d] = st_new
            o_s[d, rows, :] = o
        return carry

    nchunks = ncc + ncl
    group = next(u for u in (12, 6, 4, 3, 2, 1) if nchunks % u == 0)

    def fast_step(t, carry):
        work = [[] for _ in range(2)]
        for d in range(2):
            for u in range(group):
                rows = chunk_rows(t * group + u, d)
                work[d].append((rows, _gla_fast_intra(q_s[rows, :], k_s[d, rows, :], v_s[rows, :],
                                                      g_s[d, rows, :], consts[d][1], d == 1)))
        for d in range(2):
            st = st_s[d]
            outs = []
            for rows, intra in work[d]:
                o, st = _gla_fast_inter(intra, st)
                outs.append((rows, o))
            st_s[d] = st
            for rows, o in outs:
                o_s[d, rows, :] = o
        return carry

    @pl.when(safe)
    def _():
        lax.fori_loop(0, nchunks // group, fast_step, 0)

    @pl.when(jnp.logical_not(safe))
    def _():
        lax.fori_loop(0, nchunks, robust_step, 0)

    def finish(o, gate):
        ms = jnp.mean(o * o, axis=-1, keepdims=True)
        return ((o * lax.rsqrt(ms + EPS) * on_ref[...]) * (gate * _sigmoid(gate))).astype(BF16)

    if ctx_out:
        oc_ref[0] = finish(o_s[0, 0:lc, :] + o_s[1, 0:lc, :], gc_ref[0])
    ol_ref[0] = finish(o_s[0, lc:lc + seq, :] + o_s[1, lc:lc + seq, :], gl_ref[0])


def _gla(proj_l, proj_c, lb, onorm, ctx_out):
    b, seq, _ = proj_l.shape
    lc = proj_c.shape[1]
    width = N_HEADS * HEAD_DIM
    nt = lc + seq

    def seg(n, col):
        return pl.BlockSpec((1, n, LANE), lambda i, h: (i, 0, col + h))

    cols = (COL_DQ, COL_DFF, COL_DFB, COL_DI, COL_DG)
    out_specs = [pl.BlockSpec((1, seq, LANE), lambda i, h: (i, 0, h))]
    out_shape = [jax.ShapeDtypeStruct((b, seq, width), BF16)]
    if ctx_out:
        out_specs.insert(0, pl.BlockSpec((1, lc, LANE), lambda i, h: (i, 0, h)))
        out_shape.insert(0, jax.ShapeDtypeStruct((b, lc, width), BF16))
    outs = pl.pallas_call(
        functools.partial(_gla_kernel, ctx_out=ctx_out),
        grid=(b, N_HEADS),
        in_specs=[pl.BlockSpec((2, LANE), lambda i, h: (0, h)),
                  pl.BlockSpec((1, LANE), lambda i, h: (0, 0))]
        + [seg(lc, c) for c in cols] + [seg(seq, c) for c in cols],
        out_specs=out_specs,
        out_shape=out_shape,
        scratch_shapes=[
            pltpu.VMEM((nt, LANE), F32),
            pltpu.VMEM((nt, LANE), F32),
            pltpu.VMEM((2, nt, LANE), F32),
            pltpu.VMEM((2, nt, LANE), F32),
            pltpu.VMEM((2, nt, LANE), F32),
            pltpu.VMEM((2, LANE, LANE), F32),
        ],
        compiler_params=_cparams("parallel", "parallel"),
        name="hgrn2",
    )(lb, onorm, *([proj_c] * 5), *([proj_l] * 5))
    return (outs[0], outs[1]) if ctx_out else (None, outs[0])


def _merge_kernel(oa_ref, ob_ref, oc_ref, od_ref, g0_ref, g1_ref, g2_ref, g3_ref,
                  w0_ref, w1_ref, w2_ref, w3_ref, y_ref):
    acc = None
    for o_ref, g_ref, w_ref in ((oa_ref, g0_ref, w0_ref), (ob_ref, g1_ref, w1_ref),
                                (oc_ref, g2_ref, w2_ref), (od_ref, g3_ref, w3_ref)):
        t = _sigmoid(g_ref[0].astype(F32)) * jnp.dot(o_ref[0], w_ref[0], preferred_element_type=F32)
        acc = t if acc is None else acc + t
    y_ref[0] = acc.astype(BF16)


def _merge(outs, proj, w_branch, layer):
    nb, r, mw = outs[0].shape
    d = w_branch.shape[2]
    tm = _tile(r, 512)
    tn = 512
    nj = d // tn
    o_spec = pl.BlockSpec((1, tm, mw), lambda b, t, j: (b, t, 0))
    g_specs = [pl.BlockSpec((1, tm, tn), functools.partial(
        lambda b, t, j, n: (b, t, (COL_GATE * LANE) // tn + n * nj + j), n=n)) for n in range(4)]
    w_specs = [pl.BlockSpec((1, mw, tn), functools.partial(lambda b, t, j, n: (layer * 4 + n, 0, j), n=n))
               for n in range(4)]
    return pl.pallas_call(
        _merge_kernel,
        grid=(nb, r // tm, nj),
        in_specs=[o_spec] * 4 + g_specs + w_specs,
        out_specs=pl.BlockSpec((1, tm, tn), lambda b, t, j: (b, t, j)),
        out_shape=jax.ShapeDtypeStruct((nb, r, d), BF16),
        compiler_params=_cparams("parallel", "parallel", "arbitrary"),
        name="branch_merge",
    )(*outs, *([proj] * 4), *([w_branch] * 4))


def _out_proj_kernel(y_ref, w_ref, x_ref, g_ref, o_ref):
    o_ref[0] = x_ref[0] + g_ref[0] * jnp.dot(y_ref[0], w_ref[0], preferred_element_type=F32)


def _out_proj(y, w, layer, x, gate):
    nb, r, d = x.shape
    tm = _tile(r, 1024)
    tn = 512
    return pl.pallas_call(
        _out_proj_kernel,
        grid=(nb, r // tm, d // tn),
        in_specs=[
            pl.BlockSpec((1, tm, d), lambda b, t, j: (b, t, 0)),
            pl.BlockSpec((1, d, tn), lambda b, t, j: (layer, 0, j)),
            pl.BlockSpec((1, tm, tn), lambda b, t, j: (b, t, j)),
            pl.BlockSpec((1, 1, tn), lambda b, t, j: (b, 0, j)),
        ],
        out_specs=pl.BlockSpec((1, tm, tn), lambda b, t, j: (b, t, j)),
        out_shape=jax.ShapeDtypeStruct((nb, r, d), F32),
        compiler_params=_cparams("parallel", "parallel", "arbitrary"),
        name="out_proj",
    )(y, w, x, gate)


def _route(logits_t, bias_col):
    aff = _sigmoid(logits_t)
    sel = aff + bias_col
    aff_r = [aff[e:e + 1, :] for e in range(N_EXPERTS)]
    sel_r = [sel[e:e + 1, :] for e in range(N_EXPERTS)]
    scores = []
    for g in range(N_GROUPS):
        v = sel_r[g * EXPERTS_PER_GROUP:(g + 1) * EXPERTS_PER_GROUP]
        m1 = functools.reduce(jnp.maximum, v)
        taken = jnp.zeros(m1.shape, jnp.bool_)
        second = jnp.full(m1.shape, -jnp.inf, F32)
        for x in v:
            first = (x == m1) & jnp.logical_not(taken)
            taken = taken | first
            second = jnp.where(first, second, jnp.maximum(second, x))
        scores.append(m1 + second)
    best, gidx = scores[0], jnp.zeros(scores[0].shape, jnp.int32)
    for g in range(1, N_GROUPS):
        better = scores[g] > best
        gidx = jnp.where(better, g, gidx)
        best = jnp.where(better, scores[g], best)
    masked = [jnp.where(gidx == e // EXPERTS_PER_GROUP, sel_r[e], -jnp.inf) for e in range(N_EXPERTS)]

    def first_argmax(vals, exclude):
        bv = jnp.full(vals[0].shape, -jnp.inf, F32)
        bi = jnp.full(vals[0].shape, -1, jnp.int32)
        for e, x in enumerate(vals):
            better = x > bv
            if exclude is not None:
                better = better & (exclude != e)
            bi = jnp.where(better, e, bi)
            bv = jnp.where(better, x, bv)
        return bi

    i1 = first_argmax(masked, None)
    i2 = first_argmax(masked, i1)
    w1 = functools.reduce(jnp.add, [jnp.where(i1 == e, aff_r[e], 0.0) for e in range(N_EXPERTS)])
    w2 = functools.reduce(jnp.add, [jnp.where(i2 == e, aff_r[e], 0.0) for e in range(N_EXPERTS)])
    tot = w1 + w2
    g1, g2 = w1 / tot, w2 / tot
    width = logits_t.shape[1]
    rowi = lax.broadcasted_iota(jnp.int32, (SUBLANE, width), 0)
    local1 = jnp.broadcast_to(i1 - gidx * EXPERTS_PER_GROUP, (SUBLANE, width))
    local2 = jnp.broadcast_to(i2 - gidx * EXPERTS_PER_GROUP, (SUBLANE, width))
    info = (jnp.where(rowi == local1, jnp.broadcast_to(g1, (SUBLANE, width)), 0.0)
            + jnp.where(rowi == local2, jnp.broadcast_to(g2, (SUBLANE, width)), 0.0))
    return jnp.where(rowi == EXPERTS_PER_GROUP,
                     jnp.broadcast_to(gidx.astype(F32), (SUBLANE, width)), info)


def _moe_route_kernel(x_ref, g_ref, sh_ref, sc_ref, wrt_ref, br_ref, hn_ref, info_ref):
    d = x_ref.shape[2]
    h = _modulated_norm(x_ref[0], g_ref[...], sh_ref[0], sc_ref[0])
    logits_t = lax.dot_general(wrt_ref[...], h, NT_DIMS, precision=lax.Precision.HIGHEST,
                               preferred_element_type=F32)
    info = _route(logits_t, br_ref[...])
    info_ref[0] = info
    hn_ref[0, :, 0:d] = h
    pad = jnp.zeros((LANE - SUBLANE, info.shape[1]), F32)
    hn_ref[0, :, d:d + LANE] = jnp.concatenate([info, pad], axis=0).T


def _moe_route(x, g, sh, sc, w_router_t, b_router):
    nb, r, d = x.shape
    tm = _tile(r, 512)
    vec = pl.BlockSpec((1, 1, d), lambda b, t: (b, 0, 0))
    return pl.pallas_call(
        _moe_route_kernel,
        grid=(nb, r // tm),
        in_specs=[
            pl.BlockSpec((1, tm, d), lambda b, t: (b, t, 0)),
            pl.BlockSpec((1, d), lambda b, t: (0, 0)),
            vec, vec,
            pl.BlockSpec((N_EXPERTS, d), lambda b, t: (0, 0)),
            pl.BlockSpec((N_EXPERTS, 1), lambda b, t: (0, 0)),
        ],
        out_specs=[pl.BlockSpec((1, tm, d + LANE), lambda b, t: (b, t, 0)),
                   pl.BlockSpec((1, SUBLANE, tm), lambda b, t: (b, 0, t))],
        out_shape=[jax.ShapeDtypeStruct((nb, r, d + LANE), F32),
                   jax.ShapeDtypeStruct((nb, SUBLANE, r), F32)],
        compiler_params=_cparams("parallel", "parallel"),
        name="moe_route",
    )(x, g, sh, sc, w_router_t, b_router.reshape(N_EXPERTS, 1))


def _row_copy(src, src_row, dst, dst_row, sem):
    return pltpu.make_async_copy(src.at[pl.ds(src_row, 1), :], dst.at[pl.ds(dst_row, 1), :], sem)


def _moe_experts_kernel(dst_ref, tgrp_ref, nval_ref, hn_hbm, wg_ref, wu_ref, wd_ref,
                        y_hbm, xbuf, hb, gbuf, acc, gsem, ssem, *, tm, n_tok):
    i = pl.program_id(0)
    e = pl.program_id(1)
    nt = pl.num_programs(0)
    slot = i % 2
    per = tm // EXPERTS_PER_GROUP
    d = hb.shape[1]

    def gather_row(slot_idx, buf, row):
        s = dst_ref[slot_idx]
        return _row_copy(hn_hbm, jnp.where(s < n_tok, s, 0), xbuf.at[buf], row, gsem.at[buf])

    def wait_scatter(sl):
        pltpu.make_async_copy(acc.at[sl], y_hbm.at[pl.ds(0, tm), :], ssem.at[sl]).wait()

    @pl.when((i == 0) & (e == 0))
    def _():
        def body(r, c):
            gather_row(r, 0, r).start()
            return c
        lax.fori_loop(0, tm, body, 0)
        acc[1] = jnp.zeros(acc.shape[1:], F32)
        for half in range(2):
            fill = pltpu.make_async_copy(acc.at[1], y_hbm.at[pl.ds(n_tok + half * tm, tm), :], ssem.at[1])
            fill.start()
            fill.wait()

    @pl.when(e == 0)
    def _():
        pltpu.make_async_copy(hn_hbm.at[pl.ds(0, tm), :], xbuf.at[slot], gsem.at[slot]).wait()
        hb[...] = xbuf[slot, :, 0:d].astype(BF16)
        gbuf[...] = xbuf[slot, :, d:d + LANE]

        @pl.when(i >= 2)
        def _():
            wait_scatter(slot)
        acc[slot] = jnp.zeros(acc.shape[1:], F32)

    @pl.when(i + 1 < nt)
    def _():
        base = (i + 1) * tm + e * per
        for k in range(per):
            gather_row(base + k, 1 - slot, e * per + k).start()

    @pl.when(nval_ref[i] > 0)
    def _():
        hv = hb[...]
        a = jnp.dot(hv, wg_ref[0], preferred_element_type=F32)
        u = jnp.dot(hv, wu_ref[0], preferred_element_type=F32)
        hid = ((a * _sigmoid(a)) * u).astype(BF16)
        y = jnp.dot(hid, wd_ref[0], preferred_element_type=F32)
        lane = lax.broadcasted_iota(jnp.int32, gbuf.shape, 1)
        col = jnp.sum(jnp.where(lane == e, gbuf[...], 0.0), axis=-1, keepdims=True)
        acc[slot] = acc[slot] + col * y

    @pl.when(e == EXPERTS_PER_GROUP - 1)
    def _():
        def body(r, c):
            _row_copy(acc.at[slot], r, y_hbm, dst_ref[i * tm + r], ssem.at[slot]).start()
            return c
        lax.fori_loop(0, tm, body, 0, unroll=8)

        @pl.when(i == nt - 1)
        def _():
            wait_scatter(slot)

            @pl.when(i >= 1)
            def _():
                wait_scatter(1 - slot)


def _moe_experts(hn_ext, dst, tile_group, tile_nvalid, w_gate, w_up, w_down, layer, tm):
    t, dw = hn_ext.shape
    d = dw - LANE
    dff = w_gate.shape[2]
    ntiles = tile_group.shape[0]

    def wmap(i, e, dst_r, tgrp_r, nval_r):
        return (layer * N_EXPERTS + tgrp_r[i] * EXPERTS_PER_GROUP + e, 0, 0)

    grid_spec = pltpu.PrefetchScalarGridSpec(
        num_scalar_prefetch=3,
        grid=(ntiles, EXPERTS_PER_GROUP),
        in_specs=[
            pl.BlockSpec(memory_space=pl.ANY),
            pl.BlockSpec((1, d, dff), wmap),
            pl.BlockSpec((1, d, dff), wmap),
            pl.BlockSpec((1, dff, d), wmap),
        ],
        out_specs=pl.BlockSpec(memory_space=pl.ANY),
        scratch_shapes=[
            pltpu.VMEM((2, tm, dw), F32),
            pltpu.VMEM((tm, d), BF16),
            pltpu.VMEM((tm, LANE), F32),
            pltpu.VMEM((2, tm, d), F32),
            pltpu.SemaphoreType.DMA((2,)),
            pltpu.SemaphoreType.DMA((2,)),
        ],
    )
    return pl.pallas_call(
        functools.partial(_moe_experts_kernel, tm=tm, n_tok=t),
        grid_spec=grid_spec,
        out_shape=jax.ShapeDtypeStruct((t + 2 * tm, d), F32),
        compiler_params=_cparams("arbitrary", "arbitrary"),
        name="moe_experts",
    )(dst, tile_group, tile_nvalid, hn_ext, w_gate, w_up, w_down)


def _residual_kernel(x_ref, y_ref, g_ref, o_ref):
    o_ref[0] = x_ref[0] + g_ref[0] * y_ref[...]


def _residual(x, y, gate):
    nb, r, d = x.shape
    tm = _tile(r, 512)
    per_b = r // tm
    blk = pl.BlockSpec((1, tm, d), lambda b, t: (b, t, 0))
    return pl.pallas_call(
        _residual_kernel,
        grid=(nb, per_b),
        in_specs=[blk, pl.BlockSpec((tm, d), lambda b, t: (b * per_b + t, 0)),
                  pl.BlockSpec((1, 1, d), lambda b, t: (b, 0, 0))],
        out_specs=blk,
        out_shape=jax.ShapeDtypeStruct((nb, r, d), F32),
        compiler_params=_cparams("parallel", "parallel"),
        name="moe_residual",
    )(x, y, gate)


def _group_layout(grp, tm):
    t = grp.shape[0]
    ntiles = t // tm + N_GROUPS
    nslots = ntiles * tm
    oh = (grp[None, :] == jnp.arange(N_GROUPS)[:, None]).astype(F32).reshape(N_GROUPS, t // LANE, LANE)
    tri = (jnp.arange(LANE)[:, None] <= jnp.arange(LANE)[None, :]).astype(F32)
    within = jnp.einsum('grk,kl->grl', oh, tri)
    row_tot = within[..., -1]
    row_off = jnp.cumsum(row_tot, axis=1) - row_tot
    rank = (jnp.sum((within + row_off[..., None]) * oh, axis=0).reshape(t) - 1.0).astype(jnp.int32)
    counts = jnp.sum(row_tot, axis=1).astype(jnp.int32)
    padded = ((counts + tm - 1) // tm) * tm
    start = jnp.cumsum(padded) - padded
    slot_of_token = start[grp] + rank
    slot_ids = jnp.arange(nslots, dtype=jnp.int32)
    spare = t + ((slot_ids // tm) % 2) * tm + slot_ids % tm
    dst = spare.at[slot_of_token].set(jnp.arange(t, dtype=jnp.int32))
    tile_start = jnp.arange(ntiles, dtype=jnp.int32) * tm
    ends = jnp.cumsum(padded)
    tile_group = jnp.minimum(jnp.sum((tile_start[:, None] >= ends[None, :]).astype(jnp.int32), axis=1),
                             N_GROUPS - 1).astype(jnp.int32)
    tile_nvalid = jnp.clip(counts[tile_group] - (tile_start - start[tile_group]), 0, tm).astype(jnp.int32)
    return dst, tile_group, tile_nvalid


def _moe(x, g, sh, sc, gate, w_router_t, b_router, w_gate, w_up, w_down, layer):
    nb, r, d = x.shape
    t = nb * r
    tm = 512 if t >= 8192 else (256 if t >= 2048 else 128)
    hn_ext, info = _moe_route(x, g, sh, sc, w_router_t, b_router)
    grp = info[:, EXPERTS_PER_GROUP, :].reshape(t).astype(jnp.int32)
    dst, tile_group, tile_nvalid = _group_layout(grp, tm)
    y = _moe_experts(hn_ext.reshape(t, d + LANE), dst, tile_group, tile_nvalid, w_gate, w_up, w_down,
                     layer, tm)
    return _residual(x, y, gate)


def kernel(x, c, ctx, c_ctx, w_ada, b_ada, norm1_g, norm2_g, w_in, qn_a, kn_a, sink_a, qn_b, kn_b,
           lam_b, subln_b, conv_w, conv_b, w_rg, b_rg, w_ig, b_ig, lru_lambda, lb_d, onorm_d,
           w_branch, w_out, w_router, b_router, w_gate, w_up, w_down):
    bsz, seq, d = x.shape
    lc = ctx.shape[1]
    depth = w_in.shape[0]

    n_rows = -(-(bsz + 1) // SUBLANE) * SUBLANE
    c_pad = jnp.zeros((n_rows, d), F32).at[:bsz].set(c).at[bsz].set(c_ctx)
    mod = _ada(c_pad, w_ada, b_ada).reshape(depth, n_rows, 6, d)

    lb_w = jax.nn.softmax(lb_d.astype(F32), axis=0)
    lb_all = jnp.cumsum(lb_w, axis=0) - lb_w[0:1]
    rope_tabs = _rope_tables(seq)
    w_router_t = w_router.T
    w_in_b = w_in.astype(BF16)
    w_branch_b = w_branch.astype(BF16).reshape((depth * w_branch.shape[1],) + w_branch.shape[2:])
    w_out_b = w_out.astype(BF16)
    w_gate_b, w_up_b, w_down_b = (w.astype(BF16).reshape((depth * N_EXPERTS,) + w.shape[2:])
                                  for w in (w_gate, w_up, w_down))

    xl = x
    xc = ctx.reshape(1, bsz * lc, d)
    for l in range(depth):
        ctx_out = l < depth - 1
        mod_l = [mod[l, :bsz, k][:, None, :] for k in range(6)]
        mod_c = [mod[l, bsz:bsz + 1, k][:, None, :] for k in range(6)]
        g1 = norm1_g[l].reshape(1, d)
        g2 = norm2_g[l].reshape(1, d)

        lo_l, hi_l = _norm_proj(xl, g1, mod_l[0], mod_l[1], w_in_b, l)
        lo_c, hi_c = _norm_proj(xc, g1, mod_c[0], mod_c[1], w_in_b, l)
        lo_c = lo_c.reshape(bsz, lc, -1)
        hi_c = hi_c.reshape(bsz, lc, -1)

        qa = qn_a[l] * (HEAD_DIM ** -0.5 * LOG2E)
        qb = jnp.tile(qn_b[l], 2) * (B_DIM ** -0.5 * LOG2E)
        kb = jnp.tile(kn_b[l], 2)
        one = jnp.ones((LANE,), F32)
        gains = jnp.stack([qa] * 8 + [kn_a[l]] * 2 + [one] * 2 + [qb] * 8 + [kb] * 8 + [one] * 8)
        gains = gains.reshape(N_QKV_BLOCKS, 1, LANE)
        p_l = _prep(lo_l, gains, rope_tabs)
        p_c = _prep(lo_c, gains, None)

        bound_a = jnp.maximum(
            LOG2E * HEAD_DIM ** 0.5 * jnp.max(jnp.abs(qn_a[l])) * jnp.max(jnp.abs(kn_a[l])) * LOGIT_MARGIN,
            LOG2E * jnp.max(jnp.abs(sink_a[l])))
        bounded_a = (bound_a < SOFTMAX_SAFE_LOGIT).astype(jnp.int32).reshape(1)
        oa_c, oa_l = _attn_a(sink_a[l], bounded_a, p_l, p_c, ctx_out)

        lq1, lk1, lq2, lk2 = lam_b[l].astype(F32)
        lam_init = 0.8 - 0.6 * math.exp(-0.3 * l)
        lam = (jnp.exp(jnp.sum(lq1 * lk1)) - jnp.exp(jnp.sum(lq2 * lk2)) + lam_init).reshape(1)
        bound_b = LOG2E * B_DIM ** 0.5 * jnp.max(jnp.abs(qn_b[l])) * jnp.max(jnp.abs(kn_b[l])) * LOGIT_MARGIN
        bounded_b = (bound_b < SOFTMAX_SAFE_LOGIT).astype(jnp.int32).reshape(1)
        ob_c, ob_l = _attn_b(lam, bounded_b, p_l, p_c, subln_b[l].reshape(1, LANE), 1.0 - lam_init, ctx_out)

        oc_c, oc_l = _lru(hi_l, hi_c, conv_w[l], conv_b[l], w_rg[l].astype(BF16), b_rg[l],
                          w_ig[l].astype(BF16), b_ig[l], lru_lambda[l], ctx_out)
        od_c, od_l = _gla(hi_l, hi_c, lb_all[l], onorm_d[l].reshape(1, LANE), ctx_out)

        moe_w = (w_router_t, b_router, w_gate_b, w_up_b, w_down_b, l)

        y_l = _merge((oa_l, ob_l, oc_l, od_l), lo_l, w_branch_b, l)
        xl = _out_proj(y_l, w_out_b, l, xl, mod_l[2])
        xl = _moe(xl, g2, mod_l[3], mod_l[4], mod_l[5], *moe_w)
        if ctx_out:
            flat = lambda t: t.reshape(1, bsz * lc, -1)
            y_c = _merge(tuple(flat(t) for t in (oa_c, ob_c, oc_c, od_c)), flat(lo_c), w_branch_b, l)
            xc = _out_proj(y_c, w_out_b, l, xc, mod_c[2])
            xc = _moe(xc, g2, mod_c[3], mod_c[4], mod_c[5], *moe_w)
    return xl
```

```python
import functools
import math

import jax
import jax.numpy as jnp
from jax import lax
from jax.experimental import pallas as pl
from jax.experimental.pallas import tpu as pltpu

F32 = jnp.float32
BF16 = jnp.bfloat16

LANE = 128
SUBLANE = 8
VMEM_LIMIT_BYTES = 56 * 1024 * 1024

EPS = 1e-6
ROPE_THETA = 10000.0
GRID_W = 64
HEAD_DIM = 128
WINDOW = 128
Q_BLOCK = 128
N_HEADS = 8
A_KV_HEADS = 2
A_GROUP = N_HEADS // A_KV_HEADS
A_Q_PER_STEP = 2
B_DIM = 64
B_KEY_CHUNK = 1024
LRU_C = 8.0
GLA_CHUNK = 64
GLA_SAFE_DECAY = 60.0
N_EXPERTS = 16
N_GROUPS = 4
EXPERTS_PER_GROUP = N_EXPERTS // N_GROUPS
NEG_BIG = -1e30
LOG2E = math.log2(math.e)
SOFTMAX_SAFE_LOGIT = 60.0
LOGIT_MARGIN = 1.02
TINY = 1e-37

COL_AQ, COL_AK, COL_AV = 0, 8, 10
COL_BQ, COL_BK, COL_BV = 12, 20, 28
N_QKV_BLOCKS = 36
COL_GATE = N_QKV_BLOCKS
COL_CX, COL_CY = 0, 8
COL_DQ, COL_DFF, COL_DFB, COL_DI, COL_DG = 16, 24, 32, 40, 48
W_IN_QKV = (0, 4608)
W_IN_REC = (4608, 11776)
W_IN_GATE = (11776, 19968)
PROJ_TILE = 512
PROJ_LO_SPARE = 1

NT_DIMS = (((1,), (1,)), ((), ()))
TN_DIMS = (((0,), (0,)), ((), ()))


def _cparams(*sem):
    return pltpu.CompilerParams(dimension_semantics=sem, vmem_limit_bytes=VMEM_LIMIT_BYTES)


def _tile(n, pref):
    t = min(n, pref)
    while n % t:
        t //= 2
    return t


def _sigmoid(x):
    return jax.nn.sigmoid(x)


def _modulated_norm(x, g, sh, sc):
    ms = jnp.mean(x * x, axis=-1, keepdims=True)
    return (x * lax.rsqrt(ms + EPS) * g) * (1.0 + sc) + sh


def _ada_kernel(c_ref, w_ref, b_ref, o_ref):
    c = c_ref[...]
    s = (c * _sigmoid(c)).astype(BF16)
    o_ref[0] = jnp.dot(s, w_ref[0].astype(BF16), preferred_element_type=F32) + b_ref[0]


def _ada(c_pad, w_ada, b_ada):
    nl, d, n = w_ada.shape
    rows = c_pad.shape[0]
    tn = _tile(n, 1024)
    return pl.pallas_call(
        _ada_kernel,
        grid=(nl, n // tn),
        in_specs=[
            pl.BlockSpec((rows, d), lambda l, j: (0, 0)),
            pl.BlockSpec((1, d, tn), lambda l, j: (l, 0, j)),
            pl.BlockSpec((1, 1, tn), lambda l, j: (l, 0, j)),
        ],
        out_specs=pl.BlockSpec((1, rows, tn), lambda l, j: (l, 0, j)),
        out_shape=jax.ShapeDtypeStruct((nl, rows, n), F32),
        compiler_params=_cparams("parallel", "parallel"),
        name="ada_mod",
    )(c_pad, w_ada, b_ada.reshape(nl, 1, n))


def _norm_proj_kernel(x_ref, g_ref, sh_ref, sc_ref, wa_ref, wb_ref, lo_ref, hi_ref, hn_ref, *, n_lo):
    j = pl.program_id(2)

    @pl.when(j == 0)
    def _():
        hn_ref[...] = _modulated_norm(x_ref[0], g_ref[...], sh_ref[0], sc_ref[0]).astype(BF16)

    def pair():
        hn = hn_ref[...]
        return jnp.concatenate([jnp.dot(hn, wa_ref[0], preferred_element_type=F32),
                                jnp.dot(hn, wb_ref[0], preferred_element_type=F32)], axis=1)

    @pl.when(j < n_lo)
    def _():
        lo_ref[0] = pair().astype(BF16)

    @pl.when(j >= n_lo)
    def _():
        hi_ref[0] = pair()


def _proj_weight_tile(t):
    n_qkv = (W_IN_QKV[1] - W_IN_QKV[0]) // PROJ_TILE
    n_gate = (W_IN_GATE[1] - W_IN_GATE[0]) // PROJ_TILE
    gate0 = W_IN_GATE[0] // PROJ_TILE
    rec0 = W_IN_REC[0] // PROJ_TILE
    n_lo = n_qkv + n_gate
    return jnp.where(t < n_qkv, t,
                     jnp.where(t < n_lo, t - n_qkv + gate0,
                               jnp.where(t < n_lo + PROJ_LO_SPARE, 0, t - n_lo - PROJ_LO_SPARE + rec0)))


def _norm_proj(x, g, sh, sc, w, layer):
    nb, r, d = x.shape
    tn = PROJ_TILE
    lo_tiles = (W_IN_QKV[1] - W_IN_QKV[0] + W_IN_GATE[1] - W_IN_GATE[0]) // tn + PROJ_LO_SPARE
    hi_tiles = (W_IN_REC[1] - W_IN_REC[0]) // tn
    n_lo, n_hi = lo_tiles // 2, hi_tiles // 2
    tm = _tile(r, 1024)
    return pl.pallas_call(
        functools.partial(_norm_proj_kernel, n_lo=n_lo),
        grid=(nb, r // tm, n_lo + n_hi),
        in_specs=[
            pl.BlockSpec((1, tm, d), lambda b, t, j: (b, t, 0)),
            pl.BlockSpec((1, d), lambda b, t, j: (0, 0)),
            pl.BlockSpec((1, 1, d), lambda b, t, j: (b, 0, 0)),
            pl.BlockSpec((1, 1, d), lambda b, t, j: (b, 0, 0)),
            pl.BlockSpec((1, d, tn), lambda b, t, j: (layer, 0, _proj_weight_tile(2 * j))),
            pl.BlockSpec((1, d, tn), lambda b, t, j: (layer, 0, _proj_weight_tile(2 * j + 1))),
        ],
        out_specs=[
            pl.BlockSpec((1, tm, 2 * tn), lambda b, t, j: (b, t, jnp.minimum(j, n_lo - 1))),
            pl.BlockSpec((1, tm, 2 * tn), lambda b, t, j: (b, t, jnp.maximum(j - n_lo, 0))),
        ],
        out_shape=[jax.ShapeDtypeStruct((nb, r, lo_tiles * tn), BF16),
                   jax.ShapeDtypeStruct((nb, r, hi_tiles * tn), F32)],
        scratch_shapes=[pltpu.VMEM((tm, d), BF16)],
        compiler_params=_cparams("parallel", "parallel", "arbitrary"),
        name="norm_proj",
    )(x, g, sh, sc, w, w)


def _group_mean_matrix(group):
    k = lax.broadcasted_iota(jnp.int32, (LANE, LANE), 0)
    l = lax.broadcasted_iota(jnp.int32, (LANE, LANE), 1)
    return jnp.where(k // group == l // group, 1.0 / group, 0.0).astype(BF16)


def _rope_partner_matrix(half):
    k = lax.broadcasted_iota(jnp.int32, (LANE, LANE), 0)
    l = lax.broadcasted_iota(jnp.int32, (LANE, LANE), 1)
    src = jnp.where((l % (2 * half)) < half, l + half, l - half)
    return jnp.where(k == src, 1.0, 0.0).astype(BF16)


def _group_mean(x2, mean_m):
    hi = x2.astype(BF16)
    lo = (x2 - hi.astype(F32)).astype(BF16)
    return (jnp.dot(hi, mean_m, preferred_element_type=F32)
            + jnp.dot(lo, mean_m, preferred_element_type=F32))


def _prep_kernel(*refs, rope):
    if rope:
        x_ref, g_ref, ca_ref, sa_ref, cb_ref, sb_ref, o_ref = refs
    else:
        x_ref, g_ref, o_ref = refs
    tm = x_ref.shape[1]

    def qk_blocks(col0, nblk, group, half, cos_ref, sin_ref):
        blocks = [slice((col0 + j) * LANE, (col0 + j + 1) * LANE) for j in range(nblk)]
        x = jnp.concatenate([x_ref[0, :, c].astype(F32) for c in blocks], axis=0)
        ms = _group_mean(x * x, _group_mean_matrix(group))
        y = (x * lax.rsqrt(ms + EPS)).reshape(nblk, tm, LANE) * g_ref[col0:col0 + nblk]
        if rope:
            part = jnp.dot(y.reshape(nblk * tm, LANE).astype(BF16), _rope_partner_matrix(half),
                           preferred_element_type=F32).reshape(nblk, tm, LANE)
            y = y * cos_ref[...][None] + part * sin_ref[...][None]
        for j, c in enumerate(blocks):
            o_ref[0, :, c] = y[j].astype(BF16)

    qk_blocks(COL_AQ, COL_AV - COL_AQ, HEAD_DIM, 32, ca_ref if rope else None, sa_ref if rope else None)
    qk_blocks(COL_BQ, COL_BV - COL_BQ, B_DIM, 16, cb_ref if rope else None, sb_ref if rope else None)
    for c0, c1 in ((COL_AV, COL_BQ), (COL_BV, N_QKV_BLOCKS)):
        o_ref[0, :, c0 * LANE:c1 * LANE] = x_ref[0, :, c0 * LANE:c1 * LANE]


def _prep(proj, gains, rope_tabs):
    nb, r, _ = proj.shape
    tm = _tile(r, 256)
    width = N_QKV_BLOCKS * LANE
    rope = rope_tabs is not None
    in_specs = [
        pl.BlockSpec((1, tm, width), lambda b, t: (b, t, 0)),
        pl.BlockSpec((N_QKV_BLOCKS, 1, LANE), lambda b, t: (0, 0, 0)),
    ]
    args = [proj, gains]
    if rope:
        in_specs += [pl.BlockSpec((tm, LANE), lambda b, t: (t, 0))] * 4
        args += list(rope_tabs)
    return pl.pallas_call(
        functools.partial(_prep_kernel, rope=rope),
        grid=(nb, r // tm),
        in_specs=in_specs,
        out_specs=pl.BlockSpec((1, tm, width), lambda b, t: (b, t, 0)),
        out_shape=jax.ShapeDtypeStruct((nb, r, width), BF16),
        compiler_params=_cparams("parallel", "parallel"),
        name="qkv_prep",
    )(*args)


def _rope_tables(seq):
    pos = jnp.arange(seq)
    rows = (pos // GRID_W).astype(F32)[:, None]
    cols = (pos % GRID_W).astype(F32)[:, None]
    lane = jnp.arange(LANE)

    def tables(half):
        inv = ROPE_THETA ** (-(lane % half).astype(F32) / half)
        use_rows = (lane % (4 * half)) < 2 * half
        ang = jnp.where(use_rows[None, :], rows, cols) * inv[None, :]
        sign = jnp.where((lane % (2 * half)) < half, -1.0, 1.0)
        return jnp.cos(ang), jnp.sin(ang) * sign[None, :]

    ca, sa = tables(32)
    cb, sb = tables(16)
    return ca, sa, cb, sb


def _stack_heads(q):
    return jnp.concatenate([q[:, g * HEAD_DIM:(g + 1) * HEAD_DIM] for g in range(A_GROUP)], axis=0)


def _unstack_heads(o, rows):
    return jnp.concatenate([o[g * rows:(g + 1) * rows] for g in range(A_GROUP)], axis=1)


def _sink_column(sink_ref, kvh, rows):
    return jnp.concatenate(
        [jnp.full((rows, 1), sink_ref[kvh * A_GROUP + g] * LOG2E, F32) for g in range(A_GROUP)], axis=0)


def _attn_a_lat_kernel(sink_ref, bounded_ref, q_ref, bias_ref, kl_ref, vl_ref, kc_ref, vc_ref, o_ref, *, seq):
    band = 3 * Q_BLOCK
    nblk = seq // Q_BLOCK

    def run(bounded):
        for qi in range(q_ref.shape[1] // Q_BLOCK):
            i = pl.program_id(1) * (q_ref.shape[1] // Q_BLOCK) + qi
            rows = slice(qi * Q_BLOCK, (qi + 1) * Q_BLOCK)
            first = jnp.clip(i - 1, 0, nblk - 3)
            start = pl.multiple_of(first * Q_BLOCK, Q_BLOCK)
            bias = bias_ref[i - first]
            outs = []
            for kvh in range(A_KV_HEADS):
                hs = slice(kvh * HEAD_DIM, (kvh + 1) * HEAD_DIM)
                qs = _stack_heads(q_ref[0, rows, kvh * A_GROUP * HEAD_DIM:(kvh + 1) * A_GROUP * HEAD_DIM])
                kb = kl_ref[0, pl.ds(start, band), hs]
                vb = vl_ref[0, pl.ds(start, band), hs]
                s_loc = lax.dot_general(qs, kb, NT_DIMS, preferred_element_type=F32)
                s_loc = (s_loc.reshape(A_GROUP, Q_BLOCK, band) + bias[None]).reshape(A_GROUP * Q_BLOCK, band)
                s_ctx = lax.dot_general(qs, kc_ref[0, :, hs], NT_DIMS, preferred_element_type=F32)
                sk = _sink_column(sink_ref, kvh, Q_BLOCK)
                if not bounded:
                    m = jnp.maximum(jnp.maximum(jnp.max(s_loc, axis=-1, keepdims=True),
                                                jnp.max(s_ctx, axis=-1, keepdims=True)), sk)
                    s_loc, s_ctx, sk = s_loc - m, s_ctx - m, sk - m
                p_loc = jnp.exp2(s_loc)
                p_ctx = jnp.exp2(s_ctx)
                den = (jnp.sum(p_loc, axis=-1, keepdims=True) + jnp.sum(p_ctx, axis=-1, keepdims=True)
                       + jnp.exp2(sk))
                o = (jnp.dot(p_loc.astype(BF16), vb, preferred_element_type=F32)
                     + jnp.dot(p_ctx.astype(BF16), vc_ref[0, :, hs], preferred_element_type=F32)) / den
                outs.append(_unstack_heads(o, Q_BLOCK))
            o_ref[0, rows, :] = jnp.concatenate(outs, axis=1).astype(BF16)

    pl.when(bounded_ref[0] > 0)(lambda: run(True))
    pl.when(bounded_ref[0] <= 0)(lambda: run(False))


def _attn_a_ctx_kernel(sink_ref, q_ref, kc_ref, vc_ref, o_ref):
    kvh = pl.program_id(1)
    rows = q_ref.shape[1]
    qs = _stack_heads(q_ref[0])
    s = lax.dot_general(qs, kc_ref[0], NT_DIMS, preferred_element_type=F32)
    sk = _sink_column(sink_ref, kvh, rows)
    m = jnp.maximum(jnp.max(s, axis=-1, keepdims=True), sk)
    p = jnp.exp2(s - m)
    den = jnp.sum(p, axis=-1, keepdims=True) + jnp.exp2(sk - m)
    o = jnp.dot(p.astype(BF16), vc_ref[0], preferred_element_type=F32) / den
    o_ref[0] = _unstack_heads(o, rows).astype(BF16)


def _window_bias():
    band = 3 * Q_BLOCK
    row = jnp.arange(Q_BLOCK)[None, :, None]
    col = jnp.arange(band)[None, None, :]
    delta = (jnp.arange(3) * Q_BLOCK)[:, None, None]
    return jnp.where(jnp.abs(col - delta - row) <= WINDOW, 0.0, NEG_BIG).astype(F32)


def _attn_a(sink, bounded, p_l, p_c, ctx_out):
    b, seq, _ = p_l.shape
    lc = p_c.shape[1]
    gw = A_GROUP * HEAD_DIM
    nblk = seq // Q_BLOCK
    kvw = A_KV_HEADS * HEAD_DIM
    smem = pl.BlockSpec(memory_space=pltpu.SMEM)

    qrows = A_Q_PER_STEP * Q_BLOCK if nblk % A_Q_PER_STEP == 0 else Q_BLOCK
    o_l = pl.pallas_call(
        functools.partial(_attn_a_lat_kernel, seq=seq),
        grid=(b, seq // qrows),
        in_specs=[
            smem, smem,
            pl.BlockSpec((1, qrows, N_HEADS * HEAD_DIM), lambda n, i: (n, i, 0)),
            pl.BlockSpec((3, Q_BLOCK, 3 * Q_BLOCK), lambda n, i: (0, 0, 0)),
            pl.BlockSpec((1, seq, kvw), lambda n, i: (n, 0, COL_AK // A_KV_HEADS)),
            pl.BlockSpec((1, seq, kvw), lambda n, i: (n, 0, COL_AV // A_KV_HEADS)),
            pl.BlockSpec((1, lc, kvw), lambda n, i: (n, 0, COL_AK // A_KV_HEADS)),
            pl.BlockSpec((1, lc, kvw), lambda n, i: (n, 0, COL_AV // A_KV_HEADS)),
        ],
        out_specs=pl.BlockSpec((1, qrows, N_HEADS * HEAD_DIM), lambda n, i: (n, i, 0)),
        out_shape=jax.ShapeDtypeStruct((b, seq, N_HEADS * HEAD_DIM), BF16),
        compiler_params=_cparams("parallel", "arbitrary"),
        name="attn_a_latent",
    )(sink, bounded, p_l, _window_bias(), p_l, p_l, p_c, p_c)
    o_c = None
    if ctx_out:
        o_c = pl.pallas_call(
            _attn_a_ctx_kernel,
            grid=(b, A_KV_HEADS),
            in_specs=[
                smem,
                pl.BlockSpec((1, lc, gw), lambda n, h: (n, 0, h)),
                pl.BlockSpec((1, lc, LANE), lambda n, h: (n, 0, COL_AK + h)),
                pl.BlockSpec((1, lc, LANE), lambda n, h: (n, 0, COL_AV + h)),
            ],
            out_specs=pl.BlockSpec((1, lc, gw), lambda n, h: (n, 0, h)),
            out_shape=jax.ShapeDtypeStruct((b, lc, N_HEADS * HEAD_DIM), BF16),
            compiler_params=_cparams("parallel", "parallel"),
            name="attn_a_ctx",
        )(sink, p_c, p_c, p_c)
    return o_c, o_l


def _attn_b_kernel(lam_ref, bounded_ref, q_ref, *refs, nseg, post_scale):
    k_refs = refs[:nseg]
    v_refs = refs[nseg:2 * nseg]
    g_ref, o_ref = refs[2 * nseg], refs[2 * nseg + 1]
    q = q_ref[0]
    tq = q.shape[0]
    lane = lax.broadcasted_iota(jnp.int32, q.shape, 1)
    zero = jnp.zeros_like(q)
    q2 = jnp.concatenate([jnp.where(lane < B_DIM, q, zero), jnp.where(lane < B_DIM, zero, q)], axis=0)
    chunks = []
    for k_ref, v_ref in zip(k_refs, v_refs):
        kc = _tile(k_ref.shape[1], B_KEY_CHUNK)
        chunks += [(k_ref, v_ref, c0, kc) for c0 in range(0, k_ref.shape[1], kc)]

    def scores(k_ref, c0, kc):
        return lax.dot_general(q2, k_ref[0, c0:c0 + kc, :], NT_DIMS, preferred_element_type=F32)

    def finish(acc, den):
        o2 = acc / den
        o = o2[:tq] - lam_ref[0] * o2[tq:]
        ms = jnp.mean(o * o, axis=-1, keepdims=True)
        o_ref[0] = ((o * lax.rsqrt(ms + EPS) * g_ref[...]) * post_scale).astype(BF16)

    @pl.when(bounded_ref[0] > 0)
    def _():
        den = jnp.zeros((2 * tq, 1), F32)
        acc = jnp.zeros((2 * tq, LANE), F32)
        for k_ref, v_ref, c0, kc in chunks:
            p = jnp.exp2(scores(k_ref, c0, kc))
            den = den + jnp.sum(p, axis=-1, keepdims=True)
            acc = acc + jnp.dot(p.astype(BF16), v_ref[0, c0:c0 + kc, :], preferred_element_type=F32)
        finish(acc, den)

    @pl.when(bounded_ref[0] <= 0)
    def _():
        m = jnp.full((2 * tq, 1), NEG_BIG, F32)
        den = jnp.zeros((2 * tq, 1), F32)
        acc = jnp.zeros((2 * tq, LANE), F32)
        for k_ref, v_ref, c0, kc in chunks:
            s = scores(k_ref, c0, kc)
            m_new = jnp.maximum(m, jnp.max(s, axis=-1, keepdims=True))
            alpha = jnp.exp2(m - m_new)
            p = jnp.exp2(s - m_new)
            den = alpha * den + jnp.sum(p, axis=-1, keepdims=True)
            acc = alpha * acc + jnp.dot(p.astype(BF16), v_ref[0, c0:c0 + kc, :],
                                        preferred_element_type=F32)
            m = m_new
        finish(acc, den)


def _attn_b_call(lam, bounded, q_src, k_srcs, subln, post_scale, name):
    b, rq, _ = q_src.shape
    tq = _tile(rq, 1024)
    smem = pl.BlockSpec(memory_space=pltpu.SMEM)
    in_specs = [smem, smem, pl.BlockSpec((1, tq, LANE), lambda n, h, i: (n, i, COL_BQ + h))]
    in_specs += [pl.BlockSpec((1, s.shape[1], LANE), lambda n, h, i: (n, 0, COL_BK + h)) for s in k_srcs]
    in_specs += [pl.BlockSpec((1, s.shape[1], LANE), lambda n, h, i: (n, 0, COL_BV + h)) for s in k_srcs]
    in_specs += [pl.BlockSpec((1, LANE), lambda n, h, i: (0, 0))]
    return pl.pallas_call(
        functools.partial(_attn_b_kernel, nseg=len(k_srcs), post_scale=post_scale),
        grid=(b, N_HEADS, rq // tq),
        in_specs=in_specs,
        out_specs=pl.BlockSpec((1, tq, LANE), lambda n, h, i: (n, i, h)),
        out_shape=jax.ShapeDtypeStruct((b, rq, N_HEADS * HEAD_DIM), BF16),
        compiler_params=_cparams("parallel", "parallel", "arbitrary"),
        name=name,
    )(lam, bounded, q_src, *k_srcs, *k_srcs, subln)


def _attn_b(lam, bounded, p_l, p_c, subln, post_scale, ctx_out):
    o_l = _attn_b_call(lam, bounded, p_l, [p_c, p_l], subln, post_scale, "attn_b_latent")
    o_c = _attn_b_call(lam, bounded, p_c, [p_c], subln, post_scale, "attn_b_ctx") if ctx_out else None
    return o_c, o_l


def _centred_conv(x_ref, pad_ref, w_ref, b_ref):
    n = x_ref.shape[1]
    zeros = jnp.zeros((SUBLANE, LANE), F32)
    pad_ref[0:SUBLANE, :] = zeros
    pad_ref[SUBLANE:SUBLANE + n, :] = x_ref[0]
    pad_ref[SUBLANE + n:2 * SUBLANE + n, :] = zeros
    out = b_ref[...]
    for tap in range(4):
        out = out + pad_ref[pl.ds(SUBLANE - 2 + tap, n), :] * w_ref[tap:tap + 1, :]
    return out


def _every_8th(ref, d, j, n):
    return ref[d, pl.ds(j, n, stride=SUBLANE), :]


def _tile_order(reverse):
    return range(SUBLANE - 1, -1, -1) if reverse else range(SUBLANE)


def _scan_summaries(a_ref, v_ref, d, n, reverse):
    h = p = None
    for j in _tile_order(reverse):
        a, v = _every_8th(a_ref, d, j, n), _every_8th(v_ref, d, j, n)
        h, p = (v, a) if h is None else (a * h + v, a * p)
    return p, h


def _scan_entering(a_ref, v_ref, d, n, carry_in, reverse, store):
    e = carry_in
    for j in _tile_order(reverse):
        store(j, e)
        e = _every_8th(a_ref, d, j, n) * e + _every_8th(v_ref, d, j, n)


def _scan_apply(a_ref, v_ref, d, n, carry_in, reverse, store):
    h = carry_in
    for j in _tile_order(reverse):
        h = _every_8th(a_ref, d, j, n) * h + _every_8th(v_ref, d, j, n)
        store(j, h)


def _seg_tile(t, ntc, ntl, reverse):
    if not reverse:
        return t
    return jnp.where(t < ntc, ntc - 1 - t, 2 * ntc + ntl - 1 - t)


def _lru_kernel(xc_ref, yc_ref, xl_ref, yl_ref, cw_ref, cb_ref, wr_ref, br_ref, wi_ref, bi_ref,
                lam_ref, *refs, ctx_out):
    if ctx_out:
        oc_ref, ol_ref, ac_ref, hl_ref, hs_ref, pad_ref, tp_ref, th_ref, sp_ref, sh_ref, c3_ref, e1_ref = refs
    else:
        ol_ref, ac_ref, hl_ref, hs_ref, pad_ref, tp_ref, th_ref, sp_ref, sh_ref, c3_ref, e1_ref = refs
    lc, seq = xc_ref.shape[1], xl_ref.shape[1]
    for x_ref, off in ((xc_ref, 0), (xl_ref, lc)):
        n = x_ref.shape[1]
        u = _centred_conv(x_ref, pad_ref, cw_ref, cb_ref)
        ub = u.astype(BF16)
        for d in range(2):
            lam = lam_ref[d:d + 1, :]
            sp = jnp.maximum(-lam, 0.0) + jnp.log1p(jnp.exp(-jnp.abs(lam)))
            r = _sigmoid(jnp.dot(ub, wr_ref[d, 0], preferred_element_type=F32) + br_ref[d:d + 1, :])
            gi = _sigmoid(jnp.dot(ub, wi_ref[d, 0], preferred_element_type=F32) + bi_ref[d:d + 1, :])
            log_a = -LRU_C * r * sp
            a = jnp.exp(log_a)
            y = jnp.tanh(-log_a) * (1.0 + a * a)
            v = (y * lax.rsqrt(jnp.maximum(y, TINY))) * gi * u
            ac_ref[d, off:off + n, :] = a
            hl_ref[d, off:off + n, :] = v

    nt1 = (lc + seq) // SUBLANE
    nt2 = nt1 // SUBLANE
    group_rows = SUBLANE * SUBLANE
    for d in range(2):
        tp_ref[d], th_ref[d] = _scan_summaries(ac_ref, hl_ref, d, nt1, d == 1)
        sp_ref[d], sh_ref[d] = _scan_summaries(tp_ref, th_ref, d, nt2, d == 1)

    def step(s, carry):
        new = []
        for d in range(2):
            g = pl.ds(_seg_tile(s, lc // group_rows, seq // group_rows, d == 1), 1)
            c3_ref[d, g, :] = carry[d]
            new.append(sp_ref[d, g, :] * carry[d] + sh_ref[d, g, :])
        return tuple(new)

    zero = jnp.zeros((1, LANE), F32)
    lax.fori_loop(0, nt2, step, (zero, zero))

    for d in range(2):
        def store_entering(j, e, d=d):
            e1_ref[d, pl.ds(j, nt2, stride=SUBLANE), :] = e

        def store_h(j, h, d=d):
            if d == 0:
                hs_ref[pl.ds(j, nt1, stride=SUBLANE), :] = h
            else:
                hl_ref[d, pl.ds(j, nt1, stride=SUBLANE), :] = h

        _scan_entering(tp_ref, th_ref, d, nt2, c3_ref[d], d == 1, store_entering)
        _scan_apply(ac_ref, hl_ref, d, nt1, e1_ref[d], d == 1, store_h)
    if ctx_out:
        hc = hs_ref[0:lc, :] + hl_ref[1, 0:lc, :]
        oc_ref[0] = (hc * jax.nn.gelu(yc_ref[0])).astype(BF16)
    hl = hs_ref[lc:lc + seq, :] + hl_ref[1, lc:lc + seq, :]
    ol_ref[0] = (hl * jax.nn.gelu(yl_ref[0])).astype(BF16)


def _lru(proj_l, proj_c, conv_w, conv_b, w_r, b_r, w_i, b_i, lam, ctx_out):
    b, seq, _ = proj_l.shape
    lc = proj_c.shape[1]
    nblk = w_r.shape[1]
    width = nblk * LANE
    nt1 = (lc + seq) // SUBLANE
    nt2 = nt1 // SUBLANE

    def seg(n, col):
        return pl.BlockSpec((1, n, LANE), lambda i, j: (i, 0, col + j))

    vec2 = pl.BlockSpec((2, LANE), lambda i, j: (0, j))
    wspec = pl.BlockSpec((2, 1, LANE, LANE), lambda i, j: (0, j, 0, 0))
    out_specs = [pl.BlockSpec((1, seq, LANE), lambda i, j: (i, 0, j))]
    out_shape = [jax.ShapeDtypeStruct((b, seq, width), BF16)]
    if ctx_out:
        out_specs.insert(0, pl.BlockSpec((1, lc, LANE), lambda i, j: (i, 0, j)))
        out_shape.insert(0, jax.ShapeDtypeStruct((b, lc, width), BF16))
    outs = pl.pallas_call(
        functools.partial(_lru_kernel, ctx_out=ctx_out),
        grid=(b, nblk),
        in_specs=[
            seg(lc, COL_CX), seg(lc, COL_CY), seg(seq, COL_CX), seg(seq, COL_CY),
            pl.BlockSpec((4, LANE), lambda i, j: (0, j)),
            pl.BlockSpec((1, LANE), lambda i, j: (0, j)),
            wspec, vec2, wspec, vec2, vec2,
        ],
        out_specs=out_specs,
        out_shape=out_shape,
        scratch_shapes=[
            pltpu.VMEM((2, lc + seq, LANE), F32),
            pltpu.VMEM((2, lc + seq, LANE), F32),
            pltpu.VMEM((lc + seq, LANE), F32),
            pltpu.VMEM((max(lc, seq) + 2 * SUBLANE, LANE), F32),
            pltpu.VMEM((2, nt1, LANE), F32),
            pltpu.VMEM((2, nt1, LANE), F32),
            pltpu.VMEM((2, nt2, LANE), F32),
            pltpu.VMEM((2, nt2, LANE), F32),
            pltpu.VMEM((2, nt2, LANE), F32),
            pltpu.VMEM((2, nt1, LANE), F32),
        ],
        compiler_params=_cparams("parallel", "parallel"),
        name="rglru",
    )(proj_c, proj_c, proj_l, proj_l, conv_w, conv_b.reshape(1, width), w_r, b_r, w_i, b_i, lam)
    return (outs[0], outs[1]) if ctx_out else (None, outs[0])


def _gla_level_masks(reverse):
    c = GLA_CHUNK
    row = lax.broadcasted_iota(jnp.int32, (c, c), 0)
    col = lax.broadcasted_iota(jnp.int32, (c, c), 1)
    masks = {}
    for s in (32, 16, 8):
        same = (row // (2 * s)) == (col // (2 * s))
        if reverse:
            masks[s] = same & ((row % (2 * s)) < s) & ((col % (2 * s)) >= s)
        else:
            masks[s] = same & ((row % (2 * s)) >= s) & ((col % (2 * s)) < s)
    tri = (row <= col) if reverse else (row >= col)
    return masks, tri.astype(F32)


def _gla_chunk(q, k, v, g, st, masks, tri, reverse):
    c = GLA_CHUNK
    b = jnp.dot(tri, g, precision=lax.Precision.HIGHEST, preferred_element_type=F32)
    att = jnp.zeros((c, c), F32)
    for s in (32, 16, 8):
        b3 = b.reshape(c // (2 * s), 2 * s, LANE)
        rr = s if reverse else s - 1
        rho = jnp.broadcast_to(b3[:, rr:rr + 1, :], b3.shape).reshape(c, LANE)
        e = jnp.exp(-jnp.abs(b - rho))
        a = lax.dot_general((q * e).astype(BF16), (k * e).astype(BF16), NT_DIMS,
                            preferred_element_type=F32)
        att = att + jnp.where(masks[s], a, 0.0)
    vb = v.astype(BF16)
    o = jnp.dot(att.astype(BF16), vb, preferred_element_type=F32)
    nb = c // SUBLANE
    b3 = b.reshape(nb, SUBLANE, LANE)
    q3 = q.reshape(nb, SUBLANE, LANE)
    k3 = k.reshape(nb, SUBLANE, LANE)
    v3 = v.reshape(nb, SUBLANE, LANE)
    rowi = lax.broadcasted_iota(jnp.int32, b3.shape, 1)
    od = jnp.zeros(b3.shape, F32)
    for jj in range(SUBLANE):
        keep = (rowi <= jj) if reverse else (rowi >= jj)
        e = jnp.where(keep, jnp.exp(jnp.minimum(b3 - b3[:, jj:jj + 1, :], 0.0)), 0.0)
        sj = jnp.sum(q3 * e * k3[:, jj:jj + 1, :], axis=-1, keepdims=True)
        od = od + sj * v3[:, jj:jj + 1, :]
    o = o + od.reshape(c, LANE)
    o = o + lax.dot_general((q * jnp.exp(b)).astype(BF16), st.astype(BF16), NT_DIMS,
                            preferred_element_type=F32)
    b_end = b[0:1, :] if reverse else b[c - 1:c, :]
    khat = (k * jnp.exp(b_end - b)).astype(BF16)
    st_new = st * jnp.exp(b_end) + lax.dot_general(vb, khat, TN_DIMS, preferred_element_type=F32)
    return o, st_new


def _gla_fast_intra(q, k, v, g, tri, reverse):
    c = GLA_CHUNK
    g_hi = g.astype(BF16)
    g_lo = (g - g_hi.astype(F32)).astype(BF16)
    b2 = jnp.dot(tri.astype(BF16), jnp.concatenate([g_hi, g_lo], axis=1), preferred_element_type=F32)
    b = b2[:, :LANE] + b2[:, LANE:]
    rr = c // 2 if reverse else c // 2 - 1
    rho = b[rr:rr + 1, :]
    qt = q * jnp.exp(b - rho)
    kt = k * jnp.exp(rho - b)
    a = lax.dot_general(qt.astype(BF16), kt.astype(BF16), NT_DIMS, preferred_element_type=F32)
    att = jnp.where(tri > 0.0, a, 0.0)
    vb = v.astype(BF16)
    o_intra = jnp.dot(att.astype(BF16), vb, preferred_element_type=F32)
    b_end = b[0:1, :] if reverse else b[c - 1:c, :]
    qe = (qt * jnp.exp(rho)).astype(BF16)
    khat = (kt * jnp.exp(b_end - rho)).astype(BF16)
    return o_intra, qe, khat, vb, jnp.exp(b_end)


def _gla_fast_inter(intra, st):
    o_intra, qe, khat, vb, decay = intra
    o = o_intra + lax.dot_general(qe, st.astype(BF16), NT_DIMS, preferred_element_type=F32)
    st_new = st * decay + lax.dot_general(vb, khat, TN_DIMS, preferred_element_type=F32)
    return o, st_new


def _gla_kernel(lb_ref, on_ref, qc_ref, ffc_ref, fbc_ref, ic_ref, gc_ref,
                ql_ref, ffl_ref, fbl_ref, il_ref, gl_ref, *refs, ctx_out):
    if ctx_out:
        oc_ref, ol_ref, q_s, v_s, g_s, k_s, o_s, st_s = refs
    else:
        ol_ref, q_s, v_s, g_s, k_s, o_s, st_s = refs
    lc, seq = qc_ref.shape[1], ql_ref.shape[1]
    ncc, ncl = lc // GLA_CHUNK, seq // GLA_CHUNK
    for off, n, q_ref, i_ref, f_refs in ((0, lc, qc_ref, ic_ref, (ffc_ref, fbc_ref)),
                                         (lc, seq, ql_ref, il_ref, (ffl_ref, fbl_ref))):
        q_s[off:off + n, :] = q_ref[0]
        v_s[off:off + n, :] = i_ref[0]
        for d in range(2):
            z = f_refs[d][0]
            lbd = lb_ref[d:d + 1, :]
            ez = jnp.exp(-jnp.abs(z))
            r = 1.0 / (1.0 + ez)
            pos = z >= 0.0
            sig_p = jnp.where(pos, r, ez * r)
            sig_n = jnp.where(pos, ez * r, r)
            g_s[d, off:off + n, :] = jnp.log(lbd + (1.0 - lbd) * sig_p)
            k_s[d, off:off + n, :] = (1.0 - lbd) * sig_n
    st_s[...] = jnp.zeros(st_s.shape, F32)
    consts = [_gla_level_masks(False), _gla_level_masks(True)]

    half = GLA_CHUNK // 2
    worst = jnp.zeros((1, LANE), F32)
    for d in range(2):
        hs = jnp.sum(g_s[d].reshape((lc + seq) // half, half, LANE), axis=1)
        worst = jnp.maximum(worst, jnp.max(-hs, axis=0, keepdims=True))
    safe = jnp.max(worst) < GLA_SAFE_DECAY

    def chunk_rows(cidx, d):
        chunk = _seg_tile(cidx, ncc, ncl, d == 1)
        return pl.ds(pl.multiple_of(chunk * GLA_CHUNK, GLA_CHUNK), GLA_CHUNK)

    def robust_step(cidx, carry):
        for d in range(2):
            rows = chunk_rows(cidx, d)
            o, st_new = _gla_chunk(q_s[rows, :], k_s[d, rows, :], v_s[rows, :], g_s[d, rows, :],
                                   st_s[d], consts[d][0], consts[d][1], d == 1)
            st_s[d] = st_new
            o_s[d, rows, :] = o
        return carry

    nchunks = ncc + ncl
    group = next(u for u in (12, 6, 4, 3, 2, 1) if nchunks % u == 0)

    def fast_step(t, carry):
        work = [[] for _ in range(2)]
        for d in range(2):
            for u in range(group):
                rows = chunk_rows(t * group + u, d)
                work[d].append((rows, _gla_fast_intra(q_s[rows, :], k_s[d, rows, :], v_s[rows, :],
                                                      g_s[d, rows, :], consts[d][1], d == 1)))
        for d in range(2):
            st = st_s[d]
            outs = []
            for rows, intra in work[d]:
                o, st = _gla_fast_inter(intra, st)
                outs.append((rows, o))
            st_s[d] = st
            for rows, o in outs:
                o_s[d, rows, :] = o
        return carry

    @pl.when(safe)
    def _():
        lax.fori_loop(0, nchunks // group, fast_step, 0)

    @pl.when(jnp.logical_not(safe))
    def _():
        lax.fori_loop(0, nchunks, robust_step, 0)

    def finish(o, gate):
        ms = jnp.mean(o * o, axis=-1, keepdims=True)
        return ((o * lax.rsqrt(ms + EPS) * on_ref[...]) * (gate * _sigmoid(gate))).astype(BF16)

    if ctx_out:
        oc_ref[0] = finish(o_s[0, 0:lc, :] + o_s[1, 0:lc, :], gc_ref[0])
    ol_ref[0] = finish(o_s[0, lc:lc + seq, :] + o_s[1, lc:lc + seq, :], gl_ref[0])


def _gla(proj_l, proj_c, lb, onorm, ctx_out):
    b, seq, _ = proj_l.shape
    lc = proj_c.shape[1]
    width = N_HEADS * HEAD_DIM
    nt = lc + seq

    def seg(n, col):
        return pl.BlockSpec((1, n, LANE), lambda i, h: (i, 0, col + h))

    cols = (COL_DQ, COL_DFF, COL_DFB, COL_DI, COL_DG)
    out_specs = [pl.BlockSpec((1, seq, LANE), lambda i, h: (i, 0, h))]
    out_shape = [jax.ShapeDtypeStruct((b, seq, width), BF16)]
    if ctx_out:
        out_specs.insert(0, pl.BlockSpec((1, lc, LANE), lambda i, h: (i, 0, h)))
        out_shape.insert(0, jax.ShapeDtypeStruct((b, lc, width), BF16))
    outs = pl.pallas_call(
        functools.partial(_gla_kernel, ctx_out=ctx_out),
        grid=(b, N_HEADS),
        in_specs=[pl.BlockSpec((2, LANE), lambda i, h: (0, h)),
                  pl.BlockSpec((1, LANE), lambda i, h: (0, 0))]
        + [seg(lc, c) for c in cols] + [seg(seq, c) for c in cols],
        out_specs=out_specs,
        out_shape=out_shape,
        scratch_shapes=[
            pltpu.VMEM((nt, LANE), F32),
            pltpu.VMEM((nt, LANE), F32),
            pltpu.VMEM((2, nt, LANE), F32),
            pltpu.VMEM((2, nt, LANE), F32),
            pltpu.VMEM((2, nt, LANE), F32),
            pltpu.VMEM((2, LANE, LANE), F32),
        ],
        compiler_params=_cparams("parallel", "parallel"),
        name="hgrn2",
    )(lb, onorm, *([proj_c] * 5), *([proj_l] * 5))
    return (outs[0], outs[1]) if ctx_out else (None, outs[0])


def _merge_kernel(oa_ref, ob_ref, oc_ref, od_ref, g0_ref, g1_ref, g2_ref, g3_ref,
                  w0_ref, w1_ref, w2_ref, w3_ref, y_ref):
    acc = None
    for o_ref, g_ref, w_ref in ((oa_ref, g0_ref, w0_ref), (ob_ref, g1_ref, w1_ref),
                                (oc_ref, g2_ref, w2_ref), (od_ref, g3_ref, w3_ref)):
        t = _sigmoid(g_ref[0].astype(F32)) * jnp.dot(o_ref[0], w_ref[0], preferred_element_type=F32)
        acc = t if acc is None else acc + t
    y_ref[0] = acc.astype(BF16)


def _merge(outs, proj, w_branch, layer):
    nb, r, mw = outs[0].shape
    d = w_branch.shape[2]
    tm = _tile(r, 1024)
    tn = 512
    nj = d // tn
    o_spec = pl.BlockSpec((1, tm, mw), lambda b, t, j: (b, t, 0))
    g_specs = [pl.BlockSpec((1, tm, tn), functools.partial(
        lambda b, t, j, n: (b, t, (COL_GATE * LANE) // tn + n * nj + j), n=n)) for n in range(4)]
    w_specs = [pl.BlockSpec((1, mw, tn), functools.partial(lambda b, t, j, n: (layer * 4 + n, 0, j), n=n))
               for n in range(4)]
    return pl.pallas_call(
        _merge_kernel,
        grid=(nb, r // tm, nj),
        in_specs=[o_spec] * 4 + g_specs + w_specs,
        out_specs=pl.BlockSpec((1, tm, tn), lambda b, t, j: (b, t, j)),
        out_shape=jax.ShapeDtypeStruct((nb, r, d), BF16),
        compiler_params=_cparams("parallel", "parallel", "arbitrary"),
        name="branch_merge",
    )(*outs, *([proj] * 4), *([w_branch] * 4))


def _out_proj_kernel(y_ref, w_ref, x_ref, g_ref, o_ref):
    o_ref[0] = x_ref[0] + g_ref[0] * jnp.dot(y_ref[0], w_ref[0], preferred_element_type=F32)


def _out_proj(y, w, layer, x, gate):
    nb, r, d = x.shape
    tm = _tile(r, 1024)
    tn = 1024
    return pl.pallas_call(
        _out_proj_kernel,
        grid=(nb, r // tm, d // tn),
        in_specs=[
            pl.BlockSpec((1, tm, d), lambda b, t, j: (b, t, 0)),
            pl.BlockSpec((1, d, tn), lambda b, t, j: (layer, 0, j)),
            pl.BlockSpec((1, tm, tn), lambda b, t, j: (b, t, j)),
            pl.BlockSpec((1, 1, tn), lambda b, t, j: (b, 0, j)),
        ],
        out_specs=pl.BlockSpec((1, tm, tn), lambda b, t, j: (b, t, j)),
        out_shape=jax.ShapeDtypeStruct((nb, r, d), F32),
        compiler_params=_cparams("parallel", "parallel", "arbitrary"),
        name="out_proj",
    )(y, w, x, gate)


def _route(logits_t, bias_col):
    aff = _sigmoid(logits_t)
    sel = aff + bias_col
    aff_r = [aff[e:e + 1, :] for e in range(N_EXPERTS)]
    sel_r = [sel[e:e + 1, :] for e in range(N_EXPERTS)]
    scores = []
    for g in range(N_GROUPS):
        v = sel_r[g * EXPERTS_PER_GROUP:(g + 1) * EXPERTS_PER_GROUP]
        m1 = functools.reduce(jnp.maximum, v)
        taken = jnp.zeros(m1.shape, jnp.bool_)
        second = jnp.full(m1.shape, -jnp.inf, F32)
        for x in v:
            first = (x == m1) & jnp.logical_not(taken)
            taken = taken | first
            second = jnp.where(first, second, jnp.maximum(second, x))
        scores.append(m1 + second)
    best, gidx = scores[0], jnp.zeros(scores[0].shape, jnp.int32)
    for g in range(1, N_GROUPS):
        better = scores[g] > best
        gidx = jnp.where(better, g, gidx)
        best = jnp.where(better, scores[g], best)
    masked = [jnp.where(gidx == e // EXPERTS_PER_GROUP, sel_r[e], -jnp.inf) for e in range(N_EXPERTS)]

    def first_argmax(vals, exclude):
        bv = jnp.full(vals[0].shape, -jnp.inf, F32)
        bi = jnp.full(vals[0].shape, -1, jnp.int32)
        for e, x in enumerate(vals):
            better = x > bv
            if exclude is not None:
                better = better & (exclude != e)
            bi = jnp.where(better, e, bi)
            bv = jnp.where(better, x, bv)
        return bi

    i1 = first_argmax(masked, None)
    i2 = first_argmax(masked, i1)
    w1 = functools.reduce(jnp.add, [jnp.where(i1 == e, aff_r[e], 0.0) for e in range(N_EXPERTS)])
    w2 = functools.reduce(jnp.add, [jnp.where(i2 == e, aff_r[e], 0.0) for e in range(N_EXPERTS)])
    tot = w1 + w2
    g1, g2 = w1 / tot, w2 / tot
    width = logits_t.shape[1]
    rowi = lax.broadcasted_iota(jnp.int32, (SUBLANE, width), 0)
    local1 = jnp.broadcast_to(i1 - gidx * EXPERTS_PER_GROUP, (SUBLANE, width))
    local2 = jnp.broadcast_to(i2 - gidx * EXPERTS_PER_GROUP, (SUBLANE, width))
    info = (jnp.where(rowi == local1, jnp.broadcast_to(g1, (SUBLANE, width)), 0.0)
            + jnp.where(rowi == local2, jnp.broadcast_to(g2, (SUBLANE, width)), 0.0))
    return jnp.where(rowi == EXPERTS_PER_GROUP,
                     jnp.broadcast_to(gidx.astype(F32), (SUBLANE, width)), info)


def _moe_route_kernel(x_ref, g_ref, sh_ref, sc_ref, wrt_ref, br_ref, hn_ref, info_ref):
    d = x_ref.shape[2]
    h = _modulated_norm(x_ref[0], g_ref[...], sh_ref[0], sc_ref[0])
    logits_t = lax.dot_general(wrt_ref[...], h, NT_DIMS, precision=lax.Precision.HIGHEST,
                               preferred_element_type=F32)
    info = _route(logits_t, br_ref[...])
    info_ref[0] = info
    hn_ref[0, :, 0:d] = h
    pad = jnp.zeros((LANE - SUBLANE, info.shape[1]), F32)
    hn_ref[0, :, d:d + LANE] = jnp.concatenate([info, pad], axis=0).T


def _moe_route(x, g, sh, sc, w_router_t, b_router):
    nb, r, d = x.shape
    tm = _tile(r, 512)
    vec = pl.BlockSpec((1, 1, d), lambda b, t: (b, 0, 0))
    return pl.pallas_call(
        _moe_route_kernel,
        grid=(nb, r // tm),
        in_specs=[
            pl.BlockSpec((1, tm, d), lambda b, t: (b, t, 0)),
            pl.BlockSpec((1, d), lambda b, t: (0, 0)),
            vec, vec,
            pl.BlockSpec((N_EXPERTS, d), lambda b, t: (0, 0)),
            pl.BlockSpec((N_EXPERTS, 1), lambda b, t: (0, 0)),
        ],
        out_specs=[pl.BlockSpec((1, tm, d + LANE), lambda b, t: (b, t, 0)),
                   pl.BlockSpec((1, SUBLANE, tm), lambda b, t: (b, 0, t))],
        out_shape=[jax.ShapeDtypeStruct((nb, r, d + LANE), F32),
                   jax.ShapeDtypeStruct((nb, SUBLANE, r), F32)],
        compiler_params=_cparams("parallel", "parallel"),
        name="moe_route",
    )(x, g, sh, sc, w_router_t, b_router.reshape(N_EXPERTS, 1))


def _row_copy(src, src_row, dst, dst_row, sem):
    return pltpu.make_async_copy(src.at[pl.ds(src_row, 1), :], dst.at[pl.ds(dst_row, 1), :], sem)


def _moe_experts_kernel(dst_ref, tgrp_ref, nval_ref, hn_hbm, wg_ref, wu_ref, wd_ref,
                        y_hbm, xbuf, hb, gbuf, acc, gsem, ssem, *, tm, n_tok):
    i = pl.program_id(0)
    e = pl.program_id(1)
    nt = pl.num_programs(0)
    slot = i % 2
    per = tm // EXPERTS_PER_GROUP
    d = hb.shape[1]

    def gather_row(slot_idx, buf, row):
        s = dst_ref[slot_idx]
        return _row_copy(hn_hbm, jnp.where(s < n_tok, s, 0), xbuf.at[buf], row, gsem.at[buf])

    def wait_scatter(sl):
        pltpu.make_async_copy(acc.at[sl], y_hbm.at[pl.ds(0, tm), :], ssem.at[sl]).wait()

    @pl.when((i == 0) & (e == 0))
    def _():
        def body(r, c):
            gather_row(r, 0, r).start()
            return c
        lax.fori_loop(0, tm, body, 0)
        acc[1] = jnp.zeros(acc.shape[1:], F32)
        for half in range(2):
            fill = pltpu.make_async_copy(acc.at[1], y_hbm.at[pl.ds(n_tok + half * tm, tm), :], ssem.at[1])
            fill.start()
            fill.wait()

    @pl.when(e == 0)
    def _():
        pltpu.make_async_copy(hn_hbm.at[pl.ds(0, tm), :], xbuf.at[slot], gsem.at[slot]).wait()
        hb[...] = xbuf[slot, :, 0:d].astype(BF16)
        gbuf[...] = xbuf[slot, :, d:d + LANE]

        @pl.when(i >= 2)
        def _():
            wait_scatter(slot)
        acc[slot] = jnp.zeros(acc.shape[1:], F32)

    @pl.when(i + 1 < nt)
    def _():
        base = (i + 1) * tm + e * per
        for k in range(per):
            gather_row(base + k, 1 - slot, e * per + k).start()

    @pl.when(nval_ref[i] > 0)
    def _():
        hv = hb[...]
        a = jnp.dot(hv, wg_ref[0], preferred_element_type=F32)
        u = jnp.dot(hv, wu_ref[0], preferred_element_type=F32)
        hid = ((a * _sigmoid(a)) * u).astype(BF16)
        y = jnp.dot(hid, wd_ref[0], preferred_element_type=F32)
        lane = lax.broadcasted_iota(jnp.int32, gbuf.shape, 1)
        col = jnp.sum(jnp.where(lane == e, gbuf[...], 0.0), axis=-1, keepdims=True)
        acc[slot] = acc[slot] + col * y

    @pl.when(e == EXPERTS_PER_GROUP - 1)
    def _():
        def body(r, c):
            _row_copy(acc.at[slot], r, y_hbm, dst_ref[i * tm + r], ssem.at[slot]).start()
            return c
        lax.fori_loop(0, tm, body, 0, unroll=8)

        @pl.when(i == nt - 1)
        def _():
            wait_scatter(slot)

            @pl.when(i >= 1)
            def _():
                wait_scatter(1 - slot)


def _moe_experts(hn_ext, dst, tile_group, tile_nvalid, w_gate, w_up, w_down, layer, tm):
    t, dw = hn_ext.shape
    d = dw - LANE
    dff = w_gate.shape[2]
    ntiles = tile_group.shape[0]

    def wmap(i, e, dst_r, tgrp_r, nval_r):
        return (layer * N_EXPERTS + tgrp_r[i] * EXPERTS_PER_GROUP + e, 0, 0)

    grid_spec = pltpu.PrefetchScalarGridSpec(
        num_scalar_prefetch=3,
        grid=(ntiles, EXPERTS_PER_GROUP),
        in_specs=[
            pl.BlockSpec(memory_space=pl.ANY),
            pl.BlockSpec((1, d, dff), wmap),
            pl.BlockSpec((1, d, dff), wmap),
            pl.BlockSpec((1, dff, d), wmap),
        ],
        out_specs=pl.BlockSpec(memory_space=pl.ANY),
        scratch_shapes=[
            pltpu.VMEM((2, tm, dw), F32),
            pltpu.VMEM((tm, d), BF16),
            pltpu.VMEM((tm, LANE), F32),
            pltpu.VMEM((2, tm, d), F32),
            pltpu.SemaphoreType.DMA((2,)),
            pltpu.SemaphoreType.DMA((2,)),
        ],
    )
    return pl.pallas_call(
        functools.partial(_moe_experts_kernel, tm=tm, n_tok=t),
        grid_spec=grid_spec,
        out_shape=jax.ShapeDtypeStruct((t + 2 * tm, d), F32),
        compiler_params=_cparams("arbitrary", "arbitrary"),
        name="moe_experts",
    )(dst, tile_group, tile_nvalid, hn_ext, w_gate, w_up, w_down)


def _residual_kernel(x_ref, y_ref, g_ref, o_ref):
    o_ref[0] = x_ref[0] + g_ref[0] * y_ref[...]


def _residual(x, y, gate):
    nb, r, d = x.shape
    tm = _tile(r, 512)
    per_b = r // tm
    blk = pl.BlockSpec((1, tm, d), lambda b, t: (b, t, 0))
    return pl.pallas_call(
        _residual_kernel,
        grid=(nb, per_b),
        in_specs=[blk, pl.BlockSpec((tm, d), lambda b, t: (b * per_b + t, 0)),
                  pl.BlockSpec((1, 1, d), lambda b, t: (b, 0, 0))],
        out_specs=blk,
        out_shape=jax.ShapeDtypeStruct((nb, r, d), F32),
        compiler_params=_cparams("parallel", "parallel"),
        name="moe_residual",
    )(x, y, gate)


def _group_layout(grp, tm):
    t = grp.shape[0]
    ntiles = t // tm + N_GROUPS
    nslots = ntiles * tm
    oh = (grp[None, :] == jnp.arange(N_GROUPS)[:, None]).astype(F32).reshape(N_GROUPS, t // LANE, LANE)
    tri = (jnp.arange(LANE)[:, None] <= jnp.arange(LANE)[None, :]).astype(F32)
    within = jnp.einsum('grk,kl->grl', oh, tri)
    row_tot = within[..., -1]
    row_off = jnp.cumsum(row_tot, axis=1) - row_tot
    rank = (jnp.sum((within + row_off[..., None]) * oh, axis=0).reshape(t) - 1.0).astype(jnp.int32)
    counts = jnp.sum(row_tot, axis=1).astype(jnp.int32)
    padded = ((counts + tm - 1) // tm) * tm
    start = jnp.cumsum(padded) - padded
    slot_of_token = start[grp] + rank
    slot_ids = jnp.arange(nslots, dtype=jnp.int32)
    spare = t + ((slot_ids // tm) % 2) * tm + slot_ids % tm
    dst = spare.at[slot_of_token].set(jnp.arange(t, dtype=jnp.int32))
    tile_start = jnp.arange(ntiles, dtype=jnp.int32) * tm
    ends = jnp.cumsum(padded)
    tile_group = jnp.minimum(jnp.sum((tile_start[:, None] >= ends[None, :]).astype(jnp.int32), axis=1),
                             N_GROUPS - 1).astype(jnp.int32)
    tile_nvalid = jnp.clip(counts[tile_group] - (tile_start - start[tile_group]), 0, tm).astype(jnp.int32)
    return dst, tile_group, tile_nvalid


def _moe(x, g, sh, sc, gate, w_router_t, b_router, w_gate, w_up, w_down, layer):
    nb, r, d = x.shape
    t = nb * r
    tm = 512 if t >= 8192 else (256 if t >= 2048 else 128)
    hn_ext, info = _moe_route(x, g, sh, sc, w_router_t, b_router)
    grp = info[:, EXPERTS_PER_GROUP, :].reshape(t).astype(jnp.int32)
    dst, tile_group, tile_nvalid = _group_layout(grp, tm)
    y = _moe_experts(hn_ext.reshape(t, d + LANE), dst, tile_group, tile_nvalid, w_gate, w_up, w_down,
                     layer, tm)
    return _residual(x, y, gate)


def kernel(x, c, ctx, c_ctx, w_ada, b_ada, norm1_g, norm2_g, w_in, qn_a, kn_a, sink_a, qn_b, kn_b,
           lam_b, subln_b, conv_w, conv_b, w_rg, b_rg, w_ig, b_ig, lru_lambda, lb_d, onorm_d,
           w_branch, w_out, w_router, b_router, w_gate, w_up, w_down):
    bsz, seq, d = x.shape
    lc = ctx.shape[1]
    depth = w_in.shape[0]

    n_rows = -(-(bsz + 1) // SUBLANE) * SUBLANE
    c_pad = jnp.zeros((n_rows, d), F32).at[:bsz].set(c).at[bsz].set(c_ctx)
    mod = _ada(c_pad, w_ada, b_ada).reshape(depth, n_rows, 6, d)

    lb_w = jax.nn.softmax(lb_d.astype(F32), axis=0)
    lb_all = jnp.cumsum(lb_w, axis=0) - lb_w[0:1]
    rope_tabs = _rope_tables(seq)
    w_router_t = w_router.T
    w_in_b = w_in.astype(BF16)
    w_branch_b = w_branch.astype(BF16).reshape((depth * w_branch.shape[1],) + w_branch.shape[2:])
    w_out_b = w_out.astype(BF16)
    w_gate_b, w_up_b, w_down_b = (w.astype(BF16).reshape((depth * N_EXPERTS,) + w.shape[2:])
                                  for w in (w_gate, w_up, w_down))

    xl = x
    xc = ctx.reshape(1, bsz * lc, d)
    for l in range(depth):
        ctx_out = l < depth - 1
        mod_l = [mod[l, :bsz, k][:, None, :] for k in range(6)]
        mod_c = [mod[l, bsz:bsz + 1, k][:, None, :] for k in range(6)]
        g1 = norm1_g[l].reshape(1, d)
        g2 = norm2_g[l].reshape(1, d)

        lo_l, hi_l = _norm_proj(xl, g1, mod_l[0], mod_l[1], w_in_b, l)
        lo_c, hi_c = _norm_proj(xc, g1, mod_c[0], mod_c[1], w_in_b, l)
        lo_c = lo_c.reshape(bsz, lc, -1)
        hi_c = hi_c.reshape(bsz, lc, -1)

        qa = qn_a[l] * (HEAD_DIM ** -0.5 * LOG2E)
        qb = jnp.tile(qn_b[l], 2) * (B_DIM ** -0.5 * LOG2E)
        kb = jnp.tile(kn_b[l], 2)
        one = jnp.ones((LANE,), F32)
        gains = jnp.stack([qa] * 8 + [kn_a[l]] * 2 + [one] * 2 + [qb] * 8 + [kb] * 8 + [one] * 8)
        gains = gains.reshape(N_QKV_BLOCKS, 1, LANE)
        p_l = _prep(lo_l, gains, rope_tabs)
        p_c = _prep(lo_c, gains, None)

        bound_a = jnp.maximum(
            LOG2E * HEAD_DIM ** 0.5 * jnp.max(jnp.abs(qn_a[l])) * jnp.max(jnp.abs(kn_a[l])) * LOGIT_MARGIN,
            LOG2E * jnp.max(jnp.abs(sink_a[l])))
        bounded_a = (bound_a < SOFTMAX_SAFE_LOGIT).astype(jnp.int32).reshape(1)
        oa_c, oa_l = _attn_a(sink_a[l], bounded_a, p_l, p_c, ctx_out)

        lq1, lk1, lq2, lk2 = lam_b[l].astype(F32)
        lam_init = 0.8 - 0.6 * math.exp(-0.3 * l)
        lam = (jnp.exp(jnp.sum(lq1 * lk1)) - jnp.exp(jnp.sum(lq2 * lk2)) + lam_init).reshape(1)
        bound_b = LOG2E * B_DIM ** 0.5 * jnp.max(jnp.abs(qn_b[l])) * jnp.max(jnp.abs(kn_b[l])) * LOGIT_MARGIN
        bounded_b = (bound_b < SOFTMAX_SAFE_LOGIT).astype(jnp.int32).reshape(1)
        ob_c, ob_l = _attn_b(lam, bounded_b, p_l, p_c, subln_b[l].reshape(1, LANE), 1.0 - lam_init, ctx_out)

        oc_c, oc_l = _lru(hi_l, hi_c, conv_w[l], conv_b[l], w_rg[l].astype(BF16), b_rg[l],
                          w_ig[l].astype(BF16), b_ig[l], lru_lambda[l], ctx_out)
        od_c, od_l = _gla(hi_l, hi_c, lb_all[l], onorm_d[l].reshape(1, LANE), ctx_out)

        moe_w = (w_router_t, b_router, w_gate_b, w_up_b, w_down_b, l)

        y_l = _merge((oa_l, ob_l, oc_l, od_l), lo_l, w_branch_b, l)
        xl = _out_proj(y_l, w_out_b, l, xl, mod_l[2])
        xl = _moe(xl, g2, mod_l[3], mod_l[4], mod_l[5], *moe_w)
        if ctx_out:
            flat = lambda t: t.reshape(1, bsz * lc, -1)
            y_c = _merge(tuple(flat(t) for t in (oa_c, ob_c, oc_c, od_c)), flat(lo_c), w_branch_b, l)
            xc = _out_proj(y_c, w_out_b, l, xc, mod_c[2])
            xc = _moe(xc, g2, mod_c[3], mod_c[4], mod_c[5], *moe_w)
    return xl
```

```python
import functools
import math

import jax
import jax.numpy as jnp
from jax import lax
from jax.experimental import pallas as pl
from jax.experimental.pallas import tpu as pltpu

F32 = jnp.float32
BF16 = jnp.bfloat16

LANE = 128
SUBLANE = 8
VMEM_LIMIT_BYTES = 56 * 1024 * 1024

EPS = 1e-6
ROPE_THETA = 10000.0
GRID_W = 64
HEAD_DIM = 128
WINDOW = 128
Q_BLOCK = 128
N_HEADS = 8
A_KV_HEADS = 2
A_GROUP = N_HEADS // A_KV_HEADS
A_Q_PER_STEP = 2
B_DIM = 64
B_KEY_CHUNK = 1024
LRU_C = 8.0
GLA_CHUNK = 64
GLA_SAFE_DECAY = 80.0
N_EXPERTS = 16
N_GROUPS = 4
EXPERTS_PER_GROUP = N_EXPERTS // N_GROUPS
NEG_BIG = -1e30
LOG2E = math.log2(math.e)
SOFTMAX_SAFE_LOGIT = 60.0
LOGIT_MARGIN = 1.02
TINY = 1e-37

COL_AQ, COL_AK, COL_AV = 0, 8, 10
COL_BQ, COL_BK, COL_BV = 12, 20, 28
N_QKV_BLOCKS = 36
COL_GATE = N_QKV_BLOCKS
COL_CX, COL_CY = 0, 8
COL_DQ, COL_DFF, COL_DFB, COL_DI, COL_DG = 16, 24, 32, 40, 48
W_IN_QKV = (0, 4608)
W_IN_REC = (4608, 11776)
W_IN_GATE = (11776, 19968)
PROJ_TILE = 512
PROJ_LO_SPARE = 1

NT_DIMS = (((1,), (1,)), ((), ()))
TN_DIMS = (((0,), (0,)), ((), ()))


def _cparams(*sem):
    return pltpu.CompilerParams(dimension_semantics=sem, vmem_limit_bytes=VMEM_LIMIT_BYTES)


def _tile(n, pref):
    t = min(n, pref)
    while n % t:
        t //= 2
    return t


def _sigmoid(x):
    return jax.nn.sigmoid(x)


def _modulated_norm(x, g, sh, sc):
    ms = jnp.mean(x * x, axis=-1, keepdims=True)
    return (x * lax.rsqrt(ms + EPS) * g) * (1.0 + sc) + sh


def _ada_kernel(c_ref, w_ref, b_ref, o_ref):
    c = c_ref[...]
    s = (c * _sigmoid(c)).astype(BF16)
    o_ref[0] = jnp.dot(s, w_ref[0].astype(BF16), preferred_element_type=F32) + b_ref[0]


def _ada(c_pad, w_ada, b_ada):
    nl, d, n = w_ada.shape
    rows = c_pad.shape[0]
    tn = _tile(n, 1024)
    return pl.pallas_call(
        _ada_kernel,
        grid=(nl, n // tn),
        in_specs=[
            pl.BlockSpec((rows, d), lambda l, j: (0, 0)),
            pl.BlockSpec((1, d, tn), lambda l, j: (l, 0, j)),
            pl.BlockSpec((1, 1, tn), lambda l, j: (l, 0, j)),
        ],
        out_specs=pl.BlockSpec((1, rows, tn), lambda l, j: (l, 0, j)),
        out_shape=jax.ShapeDtypeStruct((nl, rows, n), F32),
        compiler_params=_cparams("parallel", "parallel"),
        name="ada_mod",
    )(c_pad, w_ada, b_ada.reshape(nl, 1, n))


def _norm_proj_kernel(x_ref, g_ref, sh_ref, sc_ref, wa_ref, wb_ref, lo_ref, hi_ref, hn_ref, *, n_lo):
    j = pl.program_id(2)

    @pl.when(j == 0)
    def _():
        hn_ref[...] = _modulated_norm(x_ref[0], g_ref[...], sh_ref[0], sc_ref[0]).astype(BF16)

    def pair():
        hn = hn_ref[...]
        return jnp.concatenate([jnp.dot(hn, wa_ref[0], preferred_element_type=F32),
                                jnp.dot(hn, wb_ref[0], preferred_element_type=F32)], axis=1)

    @pl.when(j < n_lo)
    def _():
        lo_ref[0] = pair().astype(BF16)

    @pl.when(j >= n_lo)
    def _():
        hi_ref[0] = pair()


def _proj_weight_tile(t):
    n_qkv = (W_IN_QKV[1] - W_IN_QKV[0]) // PROJ_TILE
    n_gate = (W_IN_GATE[1] - W_IN_GATE[0]) // PROJ_TILE
    gate0 = W_IN_GATE[0] // PROJ_TILE
    rec0 = W_IN_REC[0] // PROJ_TILE
    n_lo = n_qkv + n_gate
    return jnp.where(t < n_qkv, t,
                     jnp.where(t < n_lo, t - n_qkv + gate0,
                               jnp.where(t < n_lo + PROJ_LO_SPARE, 0, t - n_lo - PROJ_LO_SPARE + rec0)))


def _norm_proj(x, g, sh, sc, w, layer):
    nb, r, d = x.shape
    tn = PROJ_TILE
    lo_tiles = (W_IN_QKV[1] - W_IN_QKV[0] + W_IN_GATE[1] - W_IN_GATE[0]) // tn + PROJ_LO_SPARE
    hi_tiles = (W_IN_REC[1] - W_IN_REC[0]) // tn
    n_lo, n_hi = lo_tiles // 2, hi_tiles // 2
    tm = _tile(r, 1024)
    return pl.pallas_call(
        functools.partial(_norm_proj_kernel, n_lo=n_lo),
        grid=(nb, r // tm, n_lo + n_hi),
        in_specs=[
            pl.BlockSpec((1, tm, d), lambda b, t, j: (b, t, 0)),
            pl.BlockSpec((1, d), lambda b, t, j: (0, 0)),
            pl.BlockSpec((1, 1, d), lambda b, t, j: (b, 0, 0)),
            pl.BlockSpec((1, 1, d), lambda b, t, j: (b, 0, 0)),
            pl.BlockSpec((1, d, tn), lambda b, t, j: (layer, 0, _proj_weight_tile(2 * j))),
            pl.BlockSpec((1, d, tn), lambda b, t, j: (layer, 0, _proj_weight_tile(2 * j + 1))),
        ],
        out_specs=[
            pl.BlockSpec((1, tm, 2 * tn), lambda b, t, j: (b, t, jnp.minimum(j, n_lo - 1))),
            pl.BlockSpec((1, tm, 2 * tn), lambda b, t, j: (b, t, jnp.maximum(j - n_lo, 0))),
        ],
        out_shape=[jax.ShapeDtypeStruct((nb, r, lo_tiles * tn), BF16),
                   jax.ShapeDtypeStruct((nb, r, hi_tiles * tn), F32)],
        scratch_shapes=[pltpu.VMEM((tm, d), BF16)],
        compiler_params=_cparams("parallel", "parallel", "arbitrary"),
        name="norm_proj",
    )(x, g, sh, sc, w, w)


def _group_mean_matrix(group):
    k = lax.broadcasted_iota(jnp.int32, (LANE, LANE), 0)
    l = lax.broadcasted_iota(jnp.int32, (LANE, LANE), 1)
    return jnp.where(k // group == l // group, 1.0 / group, 0.0).astype(BF16)


def _rope_partner_matrix(half):
    k = lax.broadcasted_iota(jnp.int32, (LANE, LANE), 0)
    l = lax.broadcasted_iota(jnp.int32, (LANE, LANE), 1)
    src = jnp.where((l % (2 * half)) < half, l + half, l - half)
    return jnp.where(k == src, 1.0, 0.0).astype(BF16)


def _group_mean(x2, mean_m):
    hi = x2.astype(BF16)
    lo = (x2 - hi.astype(F32)).astype(BF16)
    return (jnp.dot(hi, mean_m, preferred_element_type=F32)
            + jnp.dot(lo, mean_m, preferred_element_type=F32))


def _prep_kernel(*refs, rope):
    if rope:
        x_ref, g_ref, ca_ref, sa_ref, cb_ref, sb_ref, o_ref = refs
    else:
        x_ref, g_ref, o_ref = refs
    tm = x_ref.shape[1]

    def qk_blocks(col0, nblk, group, half, cos_ref, sin_ref):
        blocks = [slice((col0 + j) * LANE, (col0 + j + 1) * LANE) for j in range(nblk)]
        x = jnp.concatenate([x_ref[0, :, c].astype(F32) for c in blocks], axis=0)
        ms = _group_mean(x * x, _group_mean_matrix(group))
        y = (x * lax.rsqrt(ms + EPS)).reshape(nblk, tm, LANE) * g_ref[col0:col0 + nblk]
        if rope:
            part = jnp.dot(y.reshape(nblk * tm, LANE).astype(BF16), _rope_partner_matrix(half),
                           preferred_element_type=F32).reshape(nblk, tm, LANE)
            y = y * cos_ref[...][None] + part * sin_ref[...][None]
        for j, c in enumerate(blocks):
            o_ref[0, :, c] = y[j].astype(BF16)

    qk_blocks(COL_AQ, COL_AV - COL_AQ, HEAD_DIM, 32, ca_ref if rope else None, sa_ref if rope else None)
    qk_blocks(COL_BQ, COL_BV - COL_BQ, B_DIM, 16, cb_ref if rope else None, sb_ref if rope else None)
    for c0, c1 in ((COL_AV, COL_BQ), (COL_BV, N_QKV_BLOCKS)):
        o_ref[0, :, c0 * LANE:c1 * LANE] = x_ref[0, :, c0 * LANE:c1 * LANE]


def _prep(proj, gains, rope_tabs):
    nb, r, _ = proj.shape
    tm = _tile(r, 256)
    width = N_QKV_BLOCKS * LANE
    rope = rope_tabs is not None
    in_specs = [
        pl.BlockSpec((1, tm, width), lambda b, t: (b, t, 0)),
        pl.BlockSpec((N_QKV_BLOCKS, 1, LANE), lambda b, t: (0, 0, 0)),
    ]
    args = [proj, gains]
    if rope:
        in_specs += [pl.BlockSpec((tm, LANE), lambda b, t: (t, 0))] * 4
        args += list(rope_tabs)
    return pl.pallas_call(
        functools.partial(_prep_kernel, rope=rope),
        grid=(nb, r // tm),
        in_specs=in_specs,
        out_specs=pl.BlockSpec((1, tm, width), lambda b, t: (b, t, 0)),
        out_shape=jax.ShapeDtypeStruct((nb, r, width), BF16),
        compiler_params=_cparams("parallel", "parallel"),
        name="qkv_prep",
    )(*args)


def _rope_tables(seq):
    pos = jnp.arange(seq)
    rows = (pos // GRID_W).astype(F32)[:, None]
    cols = (pos % GRID_W).astype(F32)[:, None]
    lane = jnp.arange(LANE)

    def tables(half):
        inv = ROPE_THETA ** (-(lane % half).astype(F32) / half)
        use_rows = (lane % (4 * half)) < 2 * half
        ang = jnp.where(use_rows[None, :], rows, cols) * inv[None, :]
        sign = jnp.where((lane % (2 * half)) < half, -1.0, 1.0)
        return jnp.cos(ang), jnp.sin(ang) * sign[None, :]

    ca, sa = tables(32)
    cb, sb = tables(16)
    return ca, sa, cb, sb


def _stack_heads(q):
    return jnp.concatenate([q[:, g * HEAD_DIM:(g + 1) * HEAD_DIM] for g in range(A_GROUP)], axis=0)


def _unstack_heads(o, rows):
    return jnp.concatenate([o[g * rows:(g + 1) * rows] for g in range(A_GROUP)], axis=1)


def _sink_column(sink_ref, kvh, rows):
    return jnp.concatenate(
        [jnp.full((rows, 1), sink_ref[kvh * A_GROUP + g] * LOG2E, F32) for g in range(A_GROUP)], axis=0)


def _attn_a_lat_kernel(sink_ref, bounded_ref, q_ref, bias_ref, kl_ref, vl_ref, kc_ref, vc_ref, o_ref, *, seq):
    band = 3 * Q_BLOCK
    nblk = seq // Q_BLOCK

    def run(bounded):
        for qi in range(q_ref.shape[1] // Q_BLOCK):
            i = pl.program_id(1) * (q_ref.shape[1] // Q_BLOCK) + qi
            rows = slice(qi * Q_BLOCK, (qi + 1) * Q_BLOCK)
            first = jnp.clip(i - 1, 0, nblk - 3)
            start = pl.multiple_of(first * Q_BLOCK, Q_BLOCK)
            bias = bias_ref[i - first]
            outs = []
            for kvh in range(A_KV_HEADS):
                hs = slice(kvh * HEAD_DIM, (kvh + 1) * HEAD_DIM)
                qs = _stack_heads(q_ref[0, rows, kvh * A_GROUP * HEAD_DIM:(kvh + 1) * A_GROUP * HEAD_DIM])
                kb = kl_ref[0, pl.ds(start, band), hs]
                vb = vl_ref[0, pl.ds(start, band), hs]
                s_loc = lax.dot_general(qs, kb, NT_DIMS, preferred_element_type=F32)
                s_loc = (s_loc.reshape(A_GROUP, Q_BLOCK, band) + bias[None]).reshape(A_GROUP * Q_BLOCK, band)
                s_ctx = lax.dot_general(qs, kc_ref[0, :, hs], NT_DIMS, preferred_element_type=F32)
                sk = _sink_column(sink_ref, kvh, Q_BLOCK)
                if not bounded:
                    m = jnp.maximum(jnp.maximum(jnp.max(s_loc, axis=-1, keepdims=True),
                                                jnp.max(s_ctx, axis=-1, keepdims=True)), sk)
                    s_loc, s_ctx, sk = s_loc - m, s_ctx - m, sk - m
                p_loc = jnp.exp2(s_loc)
                p_ctx = jnp.exp2(s_ctx)
                den = (jnp.sum(p_loc, axis=-1, keepdims=True) + jnp.sum(p_ctx, axis=-1, keepdims=True)
                       + jnp.exp2(sk))
                o = (jnp.dot(p_loc.astype(BF16), vb, preferred_element_type=F32)
                     + jnp.dot(p_ctx.astype(BF16), vc_ref[0, :, hs], preferred_element_type=F32)) / den
                outs.append(_unstack_heads(o, Q_BLOCK))
            o_ref[0, rows, :] = jnp.concatenate(outs, axis=1).astype(BF16)

    pl.when(bounded_ref[0] > 0)(lambda: run(True))
    pl.when(bounded_ref[0] <= 0)(lambda: run(False))


def _attn_a_ctx_kernel(sink_ref, q_ref, kc_ref, vc_ref, o_ref):
    kvh = pl.program_id(1)
    rows = q_ref.shape[1]
    qs = _stack_heads(q_ref[0])
    s = lax.dot_general(qs, kc_ref[0], NT_DIMS, preferred_element_type=F32)
    sk = _sink_column(sink_ref, kvh, rows)
    m = jnp.maximum(jnp.max(s, axis=-1, keepdims=True), sk)
    p = jnp.exp2(s - m)
    den = jnp.sum(p, axis=-1, keepdims=True) + jnp.exp2(sk - m)
    o = jnp.dot(p.astype(BF16), vc_ref[0], preferred_element_type=F32) / den
    o_ref[0] = _unstack_heads(o, rows).astype(BF16)


def _window_bias():
    band = 3 * Q_BLOCK
    row = jnp.arange(Q_BLOCK)[None, :, None]
    col = jnp.arange(band)[None, None, :]
    delta = (jnp.arange(3) * Q_BLOCK)[:, None, None]
    return jnp.where(jnp.abs(col - delta - row) <= WINDOW, 0.0, NEG_BIG).astype(F32)


def _attn_a(sink, bounded, p_l, p_c, ctx_out):
    b, seq, _ = p_l.shape
    lc = p_c.shape[1]
    gw = A_GROUP * HEAD_DIM
    nblk = seq // Q_BLOCK
    kvw = A_KV_HEADS * HEAD_DIM
    smem = pl.BlockSpec(memory_space=pltpu.SMEM)

    qrows = A_Q_PER_STEP * Q_BLOCK if nblk % A_Q_PER_STEP == 0 else Q_BLOCK
    o_l = pl.pallas_call(
        functools.partial(_attn_a_lat_kernel, seq=seq),
        grid=(b, seq // qrows),
        in_specs=[
            smem, smem,
            pl.BlockSpec((1, qrows, N_HEADS * HEAD_DIM), lambda n, i: (n, i, 0)),
            pl.BlockSpec((3, Q_BLOCK, 3 * Q_BLOCK), lambda n, i: (0, 0, 0)),
            pl.BlockSpec((1, seq, kvw), lambda n, i: (n, 0, COL_AK // A_KV_HEADS)),
            pl.BlockSpec((1, seq, kvw), lambda n, i: (n, 0, COL_AV // A_KV_HEADS)),
            pl.BlockSpec((1, lc, kvw), lambda n, i: (n, 0, COL_AK // A_KV_HEADS)),
            pl.BlockSpec((1, lc, kvw), lambda n, i: (n, 0, COL_AV // A_KV_HEADS)),
        ],
        out_specs=pl.BlockSpec((1, qrows, N_HEADS * HEAD_DIM), lambda n, i: (n, i, 0)),
        out_shape=jax.ShapeDtypeStruct((b, seq, N_HEADS * HEAD_DIM), BF16),
        compiler_params=_cparams("parallel", "arbitrary"),
        name="attn_a_latent",
    )(sink, bounded, p_l, _window_bias(), p_l, p_l, p_c, p_c)
    o_c = None
    if ctx_out:
        o_c = pl.pallas_call(
            _attn_a_ctx_kernel,
            grid=(b, A_KV_HEADS),
            in_specs=[
                smem,
                pl.BlockSpec((1, lc, gw), lambda n, h: (n, 0, h)),
                pl.BlockSpec((1, lc, LANE), lambda n, h: (n, 0, COL_AK + h)),
                pl.BlockSpec((1, lc, LANE), lambda n, h: (n, 0, COL_AV + h)),
            ],
            out_specs=pl.BlockSpec((1, lc, gw), lambda n, h: (n, 0, h)),
            out_shape=jax.ShapeDtypeStruct((b, lc, N_HEADS * HEAD_DIM), BF16),
            compiler_params=_cparams("parallel", "parallel"),
            name="attn_a_ctx",
        )(sink, p_c, p_c, p_c)
    return o_c, o_l


def _attn_b_kernel(lam_ref, bounded_ref, q_ref, *refs, nseg, post_scale):
    k_refs = refs[:nseg]
    v_refs = refs[nseg:2 * nseg]
    g_ref, o_ref = refs[2 * nseg], refs[2 * nseg + 1]
    q = q_ref[0]
    tq = q.shape[0]
    lane = lax.broadcasted_iota(jnp.int32, q.shape, 1)
    zero = jnp.zeros_like(q)
    q2 = jnp.concatenate([jnp.where(lane < B_DIM, q, zero), jnp.where(lane < B_DIM, zero, q)], axis=0)
    chunks = []
    for k_ref, v_ref in zip(k_refs, v_refs):
        kc = _tile(k_ref.shape[1], B_KEY_CHUNK)
        chunks += [(k_ref, v_ref, c0, kc) for c0 in range(0, k_ref.shape[1], kc)]

    def scores(k_ref, c0, kc):
        return lax.dot_general(q2, k_ref[0, c0:c0 + kc, :], NT_DIMS, preferred_element_type=F32)

    def finish(acc, den):
        o2 = acc / den
        o = o2[:tq] - lam_ref[0] * o2[tq:]
        ms = jnp.mean(o * o, axis=-1, keepdims=True)
        o_ref[0] = ((o * lax.rsqrt(ms + EPS) * g_ref[...]) * post_scale).astype(BF16)

    @pl.when(bounded_ref[0] > 0)
    def _():
        den = jnp.zeros((2 * tq, 1), F32)
        acc = jnp.zeros((2 * tq, LANE), F32)
        for k_ref, v_ref, c0, kc in chunks:
            p = jnp.exp2(scores(k_ref, c0, kc))
            den = den + jnp.sum(p, axis=-1, keepdims=True)
            acc = acc + jnp.dot(p.astype(BF16), v_ref[0, c0:c0 + kc, :], preferred_element_type=F32)
        finish(acc, den)

    @pl.when(bounded_ref[0] <= 0)
    def _():
        m = jnp.full((2 * tq, 1), NEG_BIG, F32)
        den = jnp.zeros((2 * tq, 1), F32)
        acc = jnp.zeros((2 * tq, LANE), F32)
        for k_ref, v_ref, c0, kc in chunks:
            s = scores(k_ref, c0, kc)
            m_new = jnp.maximum(m, jnp.max(s, axis=-1, keepdims=True))
            alpha = jnp.exp2(m - m_new)
            p = jnp.exp2(s - m_new)
            den = alpha * den + jnp.sum(p, axis=-1, keepdims=True)
            acc = alpha * acc + jnp.dot(p.astype(BF16), v_ref[0, c0:c0 + kc, :],
                                        preferred_element_type=F32)
            m = m_new
        finish(acc, den)


def _attn_b_call(lam, bounded, q_src, k_srcs, subln, post_scale, name):
    b, rq, _ = q_src.shape
    tq = _tile(rq, 1024)
    smem = pl.BlockSpec(memory_space=pltpu.SMEM)
    in_specs = [smem, smem, pl.BlockSpec((1, tq, LANE), lambda n, h, i: (n, i, COL_BQ + h))]
    in_specs += [pl.BlockSpec((1, s.shape[1], LANE), lambda n, h, i: (n, 0, COL_BK + h)) for s in k_srcs]
    in_specs += [pl.BlockSpec((1, s.shape[1], LANE), lambda n, h, i: (n, 0, COL_BV + h)) for s in k_srcs]
    in_specs += [pl.BlockSpec((1, LANE), lambda n, h, i: (0, 0))]
    return pl.pallas_call(
        functools.partial(_attn_b_kernel, nseg=len(k_srcs), post_scale=post_scale),
        grid=(b, N_HEADS, rq // tq),
        in_specs=in_specs,
        out_specs=pl.BlockSpec((1, tq, LANE), lambda n, h, i: (n, i, h)),
        out_shape=jax.ShapeDtypeStruct((b, rq, N_HEADS * HEAD_DIM), BF16),
        compiler_params=_cparams("parallel", "parallel", "arbitrary"),
        name=name,
    )(lam, bounded, q_src, *k_srcs, *k_srcs, subln)


def _attn_b(lam, bounded, p_l, p_c, subln, post_scale, ctx_out):
    o_l = _attn_b_call(lam, bounded, p_l, [p_c, p_l], subln, post_scale, "attn_b_latent")
    o_c = _attn_b_call(lam, bounded, p_c, [p_c], subln, post_scale, "attn_b_ctx") if ctx_out else None
    return o_c, o_l


def _centred_conv(x_ref, pad_ref, w_ref, b_ref):
    n = x_ref.shape[1]
    zeros = jnp.zeros((SUBLANE, LANE), F32)
    pad_ref[0:SUBLANE, :] = zeros
    pad_ref[SUBLANE:SUBLANE + n, :] = x_ref[0]
    pad_ref[SUBLANE + n:2 * SUBLANE + n, :] = zeros
    out = b_ref[...]
    for tap in range(4):
        out = out + pad_ref[pl.ds(SUBLANE - 2 + tap, n), :] * w_ref[tap:tap + 1, :]
    return out


def _every_8th(ref, d, j, n):
    return ref[d, pl.ds(j, n, stride=SUBLANE), :]


def _tile_order(reverse):
    return range(SUBLANE - 1, -1, -1) if reverse else range(SUBLANE)


def _scan_summaries(a_ref, v_ref, d, n, reverse):
    h = p = None
    for j in _tile_order(reverse):
        a, v = _every_8th(a_ref, d, j, n), _every_8th(v_ref, d, j, n)
        h, p = (v, a) if h is None else (a * h + v, a * p)
    return p, h


def _scan_entering(a_ref, v_ref, d, n, carry_in, reverse, store):
    e = carry_in
    for j in _tile_order(reverse):
        store(j, e)
        e = _every_8th(a_ref, d, j, n) * e + _every_8th(v_ref, d, j, n)


def _scan_apply(a_ref, v_ref, d, n, carry_in, reverse, store):
    h = carry_in
    for j in _tile_order(reverse):
        h = _every_8th(a_ref, d, j, n) * h + _every_8th(v_ref, d, j, n)
        store(j, h)


def _seg_tile(t, ntc, ntl, reverse):
    if not reverse:
        return t
    return jnp.where(t < ntc, ntc - 1 - t, 2 * ntc + ntl - 1 - t)


def _lru_kernel(xc_ref, yc_ref, xl_ref, yl_ref, cw_ref, cb_ref, wr_ref, br_ref, wi_ref, bi_ref,
                lam_ref, *refs, ctx_out):
    if ctx_out:
        oc_ref, ol_ref, ac_ref, hl_ref, hs_ref, pad_ref, tp_ref, th_ref, sp_ref, sh_ref, c3_ref, e1_ref = refs
    else:
        ol_ref, ac_ref, hl_ref, hs_ref, pad_ref, tp_ref, th_ref, sp_ref, sh_ref, c3_ref, e1_ref = refs
    lc, seq = xc_ref.shape[1], xl_ref.shape[1]
    for x_ref, off in ((xc_ref, 0), (xl_ref, lc)):
        n = x_ref.shape[1]
        u = _centred_conv(x_ref, pad_ref, cw_ref, cb_ref)
        ub = u.astype(BF16)
        for d in range(2):
            lam = lam_ref[d:d + 1, :]
            sp = jnp.maximum(-lam, 0.0) + jnp.log1p(jnp.exp(-jnp.abs(lam)))
            r = _sigmoid(jnp.dot(ub, wr_ref[d, 0], preferred_element_type=F32) + br_ref[d:d + 1, :])
            gi = _sigmoid(jnp.dot(ub, wi_ref[d, 0], preferred_element_type=F32) + bi_ref[d:d + 1, :])
            log_a = -LRU_C * r * sp
            a = jnp.exp(log_a)
            y = jnp.tanh(-log_a) * (1.0 + a * a)
            v = (y * lax.rsqrt(jnp.maximum(y, TINY))) * gi * u
            ac_ref[d, off:off + n, :] = a
            hl_ref[d, off:off + n, :] = v

    nt1 = (lc + seq) // SUBLANE
    nt2 = nt1 // SUBLANE
    group_rows = SUBLANE * SUBLANE
    for d in range(2):
        tp_ref[d], th_ref[d] = _scan_summaries(ac_ref, hl_ref, d, nt1, d == 1)
        sp_ref[d], sh_ref[d] = _scan_summaries(tp_ref, th_ref, d, nt2, d == 1)

    def step(s, carry):
        new = []
        for d in range(2):
            g = pl.ds(_seg_tile(s, lc // group_rows, seq // group_rows, d == 1), 1)
            c3_ref[d, g, :] = carry[d]
            new.append(sp_ref[d, g, :] * carry[d] + sh_ref[d, g, :])
        return tuple(new)

    zero = jnp.zeros((1, LANE), F32)
    lax.fori_loop(0, nt2, step, (zero, zero))

    for d in range(2):
        def store_entering(j, e, d=d):
            e1_ref[d, pl.ds(j, nt2, stride=SUBLANE), :] = e

        def store_h(j, h, d=d):
            if d == 0:
                hs_ref[pl.ds(j, nt1, stride=SUBLANE), :] = h
            else:
                hl_ref[d, pl.ds(j, nt1, stride=SUBLANE), :] = h

        _scan_entering(tp_ref, th_ref, d, nt2, c3_ref[d], d == 1, store_entering)
        _scan_apply(ac_ref, hl_ref, d, nt1, e1_ref[d], d == 1, store_h)
    if ctx_out:
        hc = hs_ref[0:lc, :] + hl_ref[1, 0:lc, :]
        oc_ref[0] = (hc * jax.nn.gelu(yc_ref[0])).astype(BF16)
    hl = hs_ref[lc:lc + seq, :] + hl_ref[1, lc:lc + seq, :]
    ol_ref[0] = (hl * jax.nn.gelu(yl_ref[0])).astype(BF16)


def _lru(proj_l, proj_c, conv_w, conv_b, w_r, b_r, w_i, b_i, lam, ctx_out):
    b, seq, _ = proj_l.shape
    lc = proj_c.shape[1]
    nblk = w_r.shape[1]
    width = nblk * LANE
    nt1 = (lc + seq) // SUBLANE
    nt2 = nt1 // SUBLANE

    def seg(n, col):
        return pl.BlockSpec((1, n, LANE), lambda i, j: (i, 0, col + j))

    vec2 = pl.BlockSpec((2, LANE), lambda i, j: (0, j))
    wspec = pl.BlockSpec((2, 1, LANE, LANE), lambda i, j: (0, j, 0, 0))
    out_specs = [pl.BlockSpec((1, seq, LANE), lambda i, j: (i, 0, j))]
    out_shape = [jax.ShapeDtypeStruct((b, seq, width), BF16)]
    if ctx_out:
        out_specs.insert(0, pl.BlockSpec((1, lc, LANE), lambda i, j: (i, 0, j)))
        out_shape.insert(0, jax.ShapeDtypeStruct((b, lc, width), BF16))
    outs = pl.pallas_call(
        functools.partial(_lru_kernel, ctx_out=ctx_out),
        grid=(b, nblk),
        in_specs=[
            seg(lc, COL_CX), seg(lc, COL_CY), seg(seq, COL_CX), seg(seq, COL_CY),
            pl.BlockSpec((4, LANE), lambda i, j: (0, j)),
            pl.BlockSpec((1, LANE), lambda i, j: (0, j)),
            wspec, vec2, wspec, vec2, vec2,
        ],
        out_specs=out_specs,
        out_shape=out_shape,
        scratch_shapes=[
            pltpu.VMEM((2, lc + seq, LANE), F32),
            pltpu.VMEM((2, lc + seq, LANE), F32),
            pltpu.VMEM((lc + seq, LANE), F32),
            pltpu.VMEM((max(lc, seq) + 2 * SUBLANE, LANE), F32),
            pltpu.VMEM((2, nt1, LANE), F32),
            pltpu.VMEM((2, nt1, LANE), F32),
            pltpu.VMEM((2, nt2, LANE), F32),
            pltpu.VMEM((2, nt2, LANE), F32),
            pltpu.VMEM((2, nt2, LANE), F32),
            pltpu.VMEM((2, nt1, LANE), F32),
        ],
        compiler_params=_cparams("parallel", "parallel"),
        name="rglru",
    )(proj_c, proj_c, proj_l, proj_l, conv_w, conv_b.reshape(1, width), w_r, b_r, w_i, b_i, lam)
    return (outs[0], outs[1]) if ctx_out else (None, outs[0])


def _gla_level_masks(reverse):
    c = GLA_CHUNK
    row = lax.broadcasted_iota(jnp.int32, (c, c), 0)
    col = lax.broadcasted_iota(jnp.int32, (c, c), 1)
    masks = {}
    for s in (32, 16, 8):
        same = (row // (2 * s)) == (col // (2 * s))
        if reverse:
            masks[s] = same & ((row % (2 * s)) < s) & ((col % (2 * s)) >= s)
        else:
            masks[s] = same & ((row % (2 * s)) >= s) & ((col % (2 * s)) < s)
    tri = (row <= col) if reverse else (row >= col)
    return masks, tri.astype(F32)


def _gla_chunk(q, k, v, g, st, masks, tri, reverse):
    c = GLA_CHUNK
    b = jnp.dot(tri, g, precision=lax.Precision.HIGHEST, preferred_element_type=F32)
    att = jnp.zeros((c, c), F32)
    for s in (32, 16, 8):
        b3 = b.reshape(c // (2 * s), 2 * s, LANE)
        rr = s if reverse else s - 1
        rho = jnp.broadcast_to(b3[:, rr:rr + 1, :], b3.shape).reshape(c, LANE)
        e = jnp.exp(-jnp.abs(b - rho))
        a = lax.dot_general((q * e).astype(BF16), (k * e).astype(BF16), NT_DIMS,
                            preferred_element_type=F32)
        att = att + jnp.where(masks[s], a, 0.0)
    vb = v.astype(BF16)
    o = jnp.dot(att.astype(BF16), vb, preferred_element_type=F32)
    nb = c // SUBLANE
    b3 = b.reshape(nb, SUBLANE, LANE)
    q3 = q.reshape(nb, SUBLANE, LANE)
    k3 = k.reshape(nb, SUBLANE, LANE)
    v3 = v.reshape(nb, SUBLANE, LANE)
    rowi = lax.broadcasted_iota(jnp.int32, b3.shape, 1)
    od = jnp.zeros(b3.shape, F32)
    for jj in range(SUBLANE):
        keep = (rowi <= jj) if reverse else (rowi >= jj)
        e = jnp.where(keep, jnp.exp(jnp.minimum(b3 - b3[:, jj:jj + 1, :], 0.0)), 0.0)
        sj = jnp.sum(q3 * e * k3[:, jj:jj + 1, :], axis=-1, keepdims=True)
        od = od + sj * v3[:, jj:jj + 1, :]
    o = o + od.reshape(c, LANE)
    o = o + lax.dot_general((q * jnp.exp(b)).astype(BF16), st.astype(BF16), NT_DIMS,
                            preferred_element_type=F32)
    b_end = b[0:1, :] if reverse else b[c - 1:c, :]
    khat = (k * jnp.exp(b_end - b)).astype(BF16)
    st_new = st * jnp.exp(b_end) + lax.dot_general(vb, khat, TN_DIMS, preferred_element_type=F32)
    return o, st_new


def _gla_fast_intra(q, k, v, g, tri, reverse):
    c = GLA_CHUNK
    g_hi = g.astype(BF16)
    g_lo = (g - g_hi.astype(F32)).astype(BF16)
    b2 = jnp.dot(tri.astype(BF16), jnp.concatenate([g_hi, g_lo], axis=1), preferred_element_type=F32)
    b = b2[:, :LANE] + b2[:, LANE:]
    rr = c // 2 if reverse else c // 2 - 1
    rho = b[rr:rr + 1, :]
    qt = q * jnp.exp(b - rho)
    kt = k * jnp.exp(rho - b)
    a = lax.dot_general(qt.astype(BF16), kt.astype(BF16), NT_DIMS, preferred_element_type=F32)
    att = jnp.where(tri > 0.0, a, 0.0)
    vb = v.astype(BF16)
    o_intra = jnp.dot(att.astype(BF16), vb, preferred_element_type=F32)
    b_end = b[0:1, :] if reverse else b[c - 1:c, :]
    qe = (qt * jnp.exp(rho)).astype(BF16)
    khat = (kt * jnp.exp(b_end - rho)).astype(BF16)
    return o_intra, qe, khat, vb, jnp.exp(b_end)


def _gla_fast_inter(intra, st):
    o_intra, qe, khat, vb, decay = intra
    o = o_intra + lax.dot_general(qe, st.astype(BF16), NT_DIMS, preferred_element_type=F32)
    st_new = st * decay + lax.dot_general(vb, khat, TN_DIMS, preferred_element_type=F32)
    return o, st_new


def _gla_kernel(lb_ref, on_ref, qc_ref, ffc_ref, fbc_ref, ic_ref, gc_ref,
                ql_ref, ffl_ref, fbl_ref, il_ref, gl_ref, *refs, ctx_out):
    if ctx_out:
        oc_ref, ol_ref, q_s, v_s, g_s, k_s, o_s, st_s = refs
    else:
        ol_ref, q_s, v_s, g_s, k_s, o_s, st_s = refs
    lc, seq = qc_ref.shape[1], ql_ref.shape[1]
    ncc, ncl = lc // GLA_CHUNK, seq // GLA_CHUNK
    for off, n, q_ref, i_ref, f_refs in ((0, lc, qc_ref, ic_ref, (ffc_ref, fbc_ref)),
                                         (lc, seq, ql_ref, il_ref, (ffl_ref, fbl_ref))):
        q_s[off:off + n, :] = q_ref[0]
        v_s[off:off + n, :] = i_ref[0]
        for d in range(2):
            z = f_refs[d][0]
            lbd = lb_ref[d:d + 1, :]
            ez = jnp.exp(-jnp.abs(z))
            r = 1.0 / (1.0 + ez)
            pos = z >= 0.0
            sig_p = jnp.where(pos, r, ez * r)
            sig_n = jnp.where(pos, ez * r, r)
            g_s[d, off:off + n, :] = jnp.log(lbd + (1.0 - lbd) * sig_p)
            k_s[d, off:off + n, :] = (1.0 - lbd) * sig_n
    st_s[...] = jnp.zeros(st_s.shape, F32)
    consts = [_gla_level_masks(False), _gla_level_masks(True)]

    half = GLA_CHUNK // 2
    worst = jnp.zeros((1, LANE), F32)
    for d in range(2):
        hs = jnp.sum(g_s[d].reshape((lc + seq) // half, half, LANE), axis=1)
        worst = jnp.maximum(worst, jnp.max(-hs, axis=0, keepdims=True))
    q_mag = jnp.max(jnp.abs(q_s[...]), axis=0, keepdims=True)
    safe = jnp.max(worst + jnp.log(jnp.maximum(q_mag, 1.0))) < GLA_SAFE_DECAY

    def chunk_rows(cidx, d):
        chunk = _seg_tile(cidx, ncc, ncl, d == 1)
        return pl.ds(pl.multiple_of(chunk * GLA_CHUNK, GLA_CHUNK), GLA_CHUNK)

    def robust_step(cidx, carry):
        for d in range(2):
            rows = chunk_rows(cidx, d)
            o, st_new = _gla_chunk(q_s[rows, :], k_s[d, rows, :], v_s[rows, :], g_s[d, rows, :],
                                   st_s[d], consts[d][0], consts[d][1], d == 1)
            st_s[d] = st_new
            o_s[d, rows, :] = o
        return carry

    nchunks = ncc + ncl
    group = next(u for u in (12, 6, 4, 3, 2, 1) if nchunks % u == 0)

    def fast_step(t, carry):
        work = [[] for _ in range(2)]
        for d in range(2):
            for u in range(group):
                rows = chunk_rows(t * group + u, d)
                work[d].append((rows, _gla_fast_intra(q_s[rows, :], k_s[d, rows, :], v_s[rows, :],
                                                      g_s[d, rows, :], consts[d][1], d == 1)))
        for d in range(2):
            st = st_s[d]
            outs = []
            for rows, intra in work[d]:
                o, st = _gla_fast_inter(intra, st)
                outs.append((rows, o))
            st_s[d] = st
            for rows, o in outs:
                o_s[d, rows, :] = o
        return carry

    @pl.when(safe)
    def _():
        lax.fori_loop(0, nchunks // group, fast_step, 0)

    @pl.when(jnp.logical_not(safe))
    def _():
        lax.fori_loop(0, nchunks, robust_step, 0)

    def finish(o, gate):
        ms = jnp.mean(o * o, axis=-1, keepdims=True)
        return ((o * lax.rsqrt(ms + EPS) * on_ref[...]) * (gate * _sigmoid(gate))).astype(BF16)

    if ctx_out:
        oc_ref[0] = finish(o_s[0, 0:lc, :] + o_s[1, 0:lc, :], gc_ref[0])
    ol_ref[0] = finish(o_s[0, lc:lc + seq, :] + o_s[1, lc:lc + seq, :], gl_ref[0])


def _gla(proj_l, proj_c, lb, onorm, ctx_out):
    b, seq, _ = proj_l.shape
    lc = proj_c.shape[1]
    width = N_HEADS * HEAD_DIM
    nt = lc + seq

    def seg(n, col):
        return pl.BlockSpec((1, n, LANE), lambda i, h: (i, 0, col + h))

    cols = (COL_DQ, COL_DFF, COL_DFB, COL_DI, COL_DG)
    out_specs = [pl.BlockSpec((1, seq, LANE), lambda i, h: (i, 0, h))]
    out_shape = [jax.ShapeDtypeStruct((b, seq, width), BF16)]
    if ctx_out:
        out_specs.insert(0, pl.BlockSpec((1, lc, LANE), lambda i, h: (i, 0, h)))
        out_shape.insert(0, jax.ShapeDtypeStruct((b, lc, width), BF16))
    outs = pl.pallas_call(
        functools.partial(_gla_kernel, ctx_out=ctx_out),
        grid=(b, N_HEADS),
        in_specs=[pl.BlockSpec((2, LANE), lambda i, h: (0, h)),
                  pl.BlockSpec((1, LANE), lambda i, h: (0, 0))]
        + [seg(lc, c) for c in cols] + [seg(seq, c) for c in cols],
        out_specs=out_specs,
        out_shape=out_shape,
        scratch_shapes=[
            pltpu.VMEM((nt, LANE), F32),
            pltpu.VMEM((nt, LANE), F32),
            pltpu.VMEM((2, nt, LANE), F32),
            pltpu.VMEM((2, nt, LANE), F32),
            pltpu.VMEM((2, nt, LANE), F32),
            pltpu.VMEM((2, LANE, LANE), F32),
        ],
        compiler_params=_cparams("parallel", "parallel"),
        name="hgrn2",
    )(lb, onorm, *([proj_c] * 5), *([proj_l] * 5))
    return (outs[0], outs[1]) if ctx_out else (None, outs[0])


def _merge_kernel(oa_ref, ob_ref, oc_ref, od_ref, g0_ref, g1_ref, g2_ref, g3_ref,
                  w0_ref, w1_ref, w2_ref, w3_ref, y_ref):
    acc = None
    for o_ref, g_ref, w_ref in ((oa_ref, g0_ref, w0_ref), (ob_ref, g1_ref, w1_ref),
                                (oc_ref, g2_ref, w2_ref), (od_ref, g3_ref, w3_ref)):
        t = _sigmoid(g_ref[0].astype(F32)) * jnp.dot(o_ref[0], w_ref[0], preferred_element_type=F32)
        acc = t if acc is None else acc + t
    y_ref[0] = acc.astype(BF16)


def _merge(outs, proj, w_branch, layer):
    nb, r, mw = outs[0].shape
    d = w_branch.shape[2]
    tm = _tile(r, 1024)
    tn = 512
    nj = d // tn
    o_spec = pl.BlockSpec((1, tm, mw), lambda b, t, j: (b, t, 0))
    g_specs = [pl.BlockSpec((1, tm, tn), functools.partial(
        lambda b, t, j, n: (b, t, (COL_GATE * LANE) // tn + n * nj + j), n=n)) for n in range(4)]
    w_specs = [pl.BlockSpec((1, mw, tn), functools.partial(lambda b, t, j, n: (layer * 4 + n, 0, j), n=n))
               for n in range(4)]
    return pl.pallas_call(
        _merge_kernel,
        grid=(nb, r // tm, nj),
        in_specs=[o_spec] * 4 + g_specs + w_specs,
        out_specs=pl.BlockSpec((1, tm, tn), lambda b, t, j: (b, t, j)),
        out_shape=jax.ShapeDtypeStruct((nb, r, d), BF16),
        compiler_params=_cparams("parallel", "parallel", "arbitrary"),
        name="branch_merge",
    )(*outs, *([proj] * 4), *([w_branch] * 4))


def _out_proj_kernel(y_ref, w_ref, x_ref, g_ref, o_ref):
    o_ref[0] = x_ref[0] + g_ref[0] * jnp.dot(y_ref[0], w_ref[0], preferred_element_type=F32)


def _out_proj(y, w, layer, x, gate):
    nb, r, d = x.shape
    tm = _tile(r, 1024)
    tn = 1024
    return pl.pallas_call(
        _out_proj_kernel,
        grid=(nb, r // tm, d // tn),
        in_specs=[
            pl.BlockSpec((1, tm, d), lambda b, t, j: (b, t, 0)),
            pl.BlockSpec((1, d, tn), lambda b, t, j: (layer, 0, j)),
            pl.BlockSpec((1, tm, tn), lambda b, t, j: (b, t, j)),
            pl.BlockSpec((1, 1, tn), lambda b, t, j: (b, 0, j)),
        ],
        out_specs=pl.BlockSpec((1, tm, tn), lambda b, t, j: (b, t, j)),
        out_shape=jax.ShapeDtypeStruct((nb, r, d), F32),
        compiler_params=_cparams("parallel", "parallel", "arbitrary"),
        name="out_proj",
    )(y, w, x, gate)


def _route(logits_t, bias_col):
    aff = _sigmoid(logits_t)
    sel = aff + bias_col
    aff_r = [aff[e:e + 1, :] for e in range(N_EXPERTS)]
    sel_r = [sel[e:e + 1, :] for e in range(N_EXPERTS)]
    scores = []
    for g in range(N_GROUPS):
        v = sel_r[g * EXPERTS_PER_GROUP:(g + 1) * EXPERTS_PER_GROUP]
        m1 = functools.reduce(jnp.maximum, v)
        taken = jnp.zeros(m1.shape, jnp.bool_)
        second = jnp.full(m1.shape, -jnp.inf, F32)
        for x in v:
            first = (x == m1) & jnp.logical_not(taken)
            taken = taken | first
            second = jnp.where(first, second, jnp.maximum(second, x))
        scores.append(m1 + second)
    best, gidx = scores[0], jnp.zeros(scores[0].shape, jnp.int32)
    for g in range(1, N_GROUPS):
        better = scores[g] > best
        gidx = jnp.where(better, g, gidx)
        best = jnp.where(better, scores[g], best)
    masked = [jnp.where(gidx == e // EXPERTS_PER_GROUP, sel_r[e], -jnp.inf) for e in range(N_EXPERTS)]

    def first_argmax(vals, exclude):
        bv = jnp.full(vals[0].shape, -jnp.inf, F32)
        bi = jnp.full(vals[0].shape, -1, jnp.int32)
        for e, x in enumerate(vals):
            better = x > bv
            if exclude is not None:
                better = better & (exclude != e)
            bi = jnp.where(better, e, bi)
            bv = jnp.where(better, x, bv)
        return bi

    i1 = first_argmax(masked, None)
    i2 = first_argmax(masked, i1)
    w1 = functools.reduce(jnp.add, [jnp.where(i1 == e, aff_r[e], 0.0) for e in range(N_EXPERTS)])
    w2 = functools.reduce(jnp.add, [jnp.where(i2 == e, aff_r[e], 0.0) for e in range(N_EXPERTS)])
    tot = w1 + w2
    g1, g2 = w1 / tot, w2 / tot
    width = logits_t.shape[1]
    rowi = lax.broadcasted_iota(jnp.int32, (SUBLANE, width), 0)
    local1 = jnp.broadcast_to(i1 - gidx * EXPERTS_PER_GROUP, (SUBLANE, width))
    local2 = jnp.broadcast_to(i2 - gidx * EXPERTS_PER_GROUP, (SUBLANE, width))
    info = (jnp.where(rowi == local1, jnp.broadcast_to(g1, (SUBLANE, width)), 0.0)
            + jnp.where(rowi == local2, jnp.broadcast_to(g2, (SUBLANE, width)), 0.0))
    return jnp.where(rowi == EXPERTS_PER_GROUP,
                     jnp.broadcast_to(gidx.astype(F32), (SUBLANE, width)), info)


def _moe_route_kernel(x_ref, g_ref, sh_ref, sc_ref, wrt_ref, br_ref, hn_ref, info_ref):
    d = x_ref.shape[2]
    h = _modulated_norm(x_ref[0], g_ref[...], sh_ref[0], sc_ref[0])
    logits_t = lax.dot_general(wrt_ref[...], h, NT_DIMS, precision=lax.Precision.HIGHEST,
                               preferred_element_type=F32)
    info = _route(logits_t, br_ref[...])
    info_ref[0] = info
    hn_ref[0, :, 0:d] = h
    pad = jnp.zeros((LANE - SUBLANE, info.shape[1]), F32)
    hn_ref[0, :, d:d + LANE] = jnp.concatenate([info, pad], axis=0).T


def _moe_route(x, g, sh, sc, w_router_t, b_router):
    nb, r, d = x.shape
    tm = _tile(r, 512)
    vec = pl.BlockSpec((1, 1, d), lambda b, t: (b, 0, 0))
    return pl.pallas_call(
        _moe_route_kernel,
        grid=(nb, r // tm),
        in_specs=[
            pl.BlockSpec((1, tm, d), lambda b, t: (b, t, 0)),
            pl.BlockSpec((1, d), lambda b, t: (0, 0)),
            vec, vec,
            pl.BlockSpec((N_EXPERTS, d), lambda b, t: (0, 0)),
            pl.BlockSpec((N_EXPERTS, 1), lambda b, t: (0, 0)),
        ],
        out_specs=[pl.BlockSpec((1, tm, d + LANE), lambda b, t: (b, t, 0)),
                   pl.BlockSpec((1, SUBLANE, tm), lambda b, t: (b, 0, t))],
        out_shape=[jax.ShapeDtypeStruct((nb, r, d + LANE), F32),
                   jax.ShapeDtypeStruct((nb, SUBLANE, r), F32)],
        compiler_params=_cparams("parallel", "parallel"),
        name="moe_route",
    )(x, g, sh, sc, w_router_t, b_router.reshape(N_EXPERTS, 1))


def _row_copy(src, src_row, dst, dst_row, sem):
    return pltpu.make_async_copy(src.at[pl.ds(src_row, 1), :], dst.at[pl.ds(dst_row, 1), :], sem)


def _moe_experts_kernel(dst_ref, tgrp_ref, nval_ref, hn_hbm, wg_ref, wu_ref, wd_ref,
                        y_hbm, xbuf, hb, gbuf, acc, gsem, ssem, *, tm, n_tok):
    i = pl.program_id(0)
    e = pl.program_id(1)
    nt = pl.num_programs(0)
    slot = i % 2
    per = tm // EXPERTS_PER_GROUP
    d = hb.shape[1]

    def gather_row(slot_idx, buf, row):
        s = dst_ref[slot_idx]
        return _row_copy(hn_hbm, jnp.where(s < n_tok, s, 0), xbuf.at[buf], row, gsem.at[buf])

    def wait_scatter(sl):
        pltpu.make_async_copy(acc.at[sl], y_hbm.at[pl.ds(0, tm), :], ssem.at[sl]).wait()

    @pl.when((i == 0) & (e == 0))
    def _():
        def body(r, c):
            gather_row(r, 0, r).start()
            return c
        lax.fori_loop(0, tm, body, 0)
        acc[1] = jnp.zeros(acc.shape[1:], F32)
        for half in range(2):
            fill = pltpu.make_async_copy(acc.at[1], y_hbm.at[pl.ds(n_tok + half * tm, tm), :], ssem.at[1])
            fill.start()
            fill.wait()

    @pl.when(e == 0)
    def _():
        pltpu.make_async_copy(hn_hbm.at[pl.ds(0, tm), :], xbuf.at[slot], gsem.at[slot]).wait()
        hb[...] = xbuf[slot, :, 0:d].astype(BF16)
        gbuf[...] = xbuf[slot, :, d:d + LANE]

        @pl.when(i >= 2)
        def _():
            wait_scatter(slot)
        acc[slot] = jnp.zeros(acc.shape[1:], F32)

    @pl.when(i + 1 < nt)
    def _():
        base = (i + 1) * tm + e * per
        for k in range(per):
            gather_row(base + k, 1 - slot, e * per + k).start()

    @pl.when(nval_ref[i] > 0)
    def _():
        hv = hb[...]
        a = jnp.dot(hv, wg_ref[0], preferred_element_type=F32)
        u = jnp.dot(hv, wu_ref[0], preferred_element_type=F32)
        hid = ((a * _sigmoid(a)) * u).astype(BF16)
        y = jnp.dot(hid, wd_ref[0], preferred_element_type=F32)
        lane = lax.broadcasted_iota(jnp.int32, gbuf.shape, 1)
        col = jnp.sum(jnp.where(lane == e, gbuf[...], 0.0), axis=-1, keepdims=True)
        acc[slot] = acc[slot] + col * y

    @pl.when(e == EXPERTS_PER_GROUP - 1)
    def _():
        def body(r, c):
            _row_copy(acc.at[slot], r, y_hbm, dst_ref[i * tm + r], ssem.at[slot]).start()
            return c
        lax.fori_loop(0, tm, body, 0, unroll=8)

        @pl.when(i == nt - 1)
        def _():
            wait_scatter(slot)

            @pl.when(i >= 1)
            def _():
                wait_scatter(1 - slot)


def _moe_experts(hn_ext, dst, tile_group, tile_nvalid, w_gate, w_up, w_down, layer, tm):
    t, dw = hn_ext.shape
    d = dw - LANE
    dff = w_gate.shape[2]
    ntiles = tile_group.shape[0]

    def wmap(i, e, dst_r, tgrp_r, nval_r):
        return (layer * N_EXPERTS + tgrp_r[i] * EXPERTS_PER_GROUP + e, 0, 0)

    grid_spec = pltpu.PrefetchScalarGridSpec(
        num_scalar_prefetch=3,
        grid=(ntiles, EXPERTS_PER_GROUP),
        in_specs=[
            pl.BlockSpec(memory_space=pl.ANY),
            pl.BlockSpec((1, d, dff), wmap),
            pl.BlockSpec((1, d, dff), wmap),
            pl.BlockSpec((1, dff, d), wmap),
        ],
        out_specs=pl.BlockSpec(memory_space=pl.ANY),
        scratch_shapes=[
            pltpu.VMEM((2, tm, dw), F32),
            pltpu.VMEM((tm, d), BF16),
            pltpu.VMEM((tm, LANE), F32),
            pltpu.VMEM((2, tm, d), F32),
            pltpu.SemaphoreType.DMA((2,)),
            pltpu.SemaphoreType.DMA((2,)),
        ],
    )
    return pl.pallas_call(
        functools.partial(_moe_experts_kernel, tm=tm, n_tok=t),
        grid_spec=grid_spec,
        out_shape=jax.ShapeDtypeStruct((t + 2 * tm, d), F32),
        compiler_params=_cparams("arbitrary", "arbitrary"),
        name="moe_experts",
    )(dst, tile_group, tile_nvalid, hn_ext, w_gate, w_up, w_down)


def _residual_kernel(x_ref, y_ref, g_ref, o_ref):
    o_ref[0] = x_ref[0] + g_ref[0] * y_ref[...]


def _residual(x, y, gate):
    nb, r, d = x.shape
    tm = _tile(r, 512)
    per_b = r // tm
    blk = pl.BlockSpec((1, tm, d), lambda b, t: (b, t, 0))
    return pl.pallas_call(
        _residual_kernel,
        grid=(nb, per_b),
        in_specs=[blk, pl.BlockSpec((tm, d), lambda b, t: (b * per_b + t, 0)),
                  pl.BlockSpec((1, 1, d), lambda b, t: (b, 0, 0))],
        out_specs=blk,
        out_shape=jax.ShapeDtypeStruct((nb, r, d), F32),
        compiler_params=_cparams("parallel", "parallel"),
        name="moe_residual",
    )(x, y, gate)


def _group_layout(grp, tm):
    t = grp.shape[0]
    ntiles = t // tm + N_GROUPS
    nslots = ntiles * tm
    oh = (grp[None, :] == jnp.arange(N_GROUPS)[:, None]).astype(F32).reshape(N_GROUPS, t // LANE, LANE)
    tri = (jnp.arange(LANE)[:, None] <= jnp.arange(LANE)[None, :]).astype(F32)
    within = jnp.einsum('grk,kl->grl', oh, tri)
    row_tot = within[..., -1]
    row_off = jnp.cumsum(row_tot, axis=1) - row_tot
    rank = (jnp.sum((within + row_off[..., None]) * oh, axis=0).reshape(t) - 1.0).astype(jnp.int32)
    counts = jnp.sum(row_tot, axis=1).astype(jnp.int32)
    padded = ((counts + tm - 1) // tm) * tm
    start = jnp.cumsum(padded) - padded
    slot_of_token = start[grp] + rank
    slot_ids = jnp.arange(nslots, dtype=jnp.int32)
    spare = t + ((slot_ids // tm) % 2) * tm + slot_ids % tm
    dst = spare.at[slot_of_token].set(jnp.arange(t, dtype=jnp.int32))
    tile_start = jnp.arange(ntiles, dtype=jnp.int32) * tm
    ends = jnp.cumsum(padded)
    tile_group = jnp.minimum(jnp.sum((tile_start[:, None] >= ends[None, :]).astype(jnp.int32), axis=1),
                             N_GROUPS - 1).astype(jnp.int32)
    tile_nvalid = jnp.clip(counts[tile_group] - (tile_start - start[tile_group]), 0, tm).astype(jnp.int32)
    return dst, tile_group, tile_nvalid


def _moe(x, g, sh, sc, gate, w_router_t, b_router, w_gate, w_up, w_down, layer):
    nb, r, d = x.shape
    t = nb * r
    tm = 512 if t >= 8192 else (256 if t >= 2048 else 128)
    hn_ext, info = _moe_route(x, g, sh, sc, w_router_t, b_router)
    grp = info[:, EXPERTS_PER_GROUP, :].reshape(t).astype(jnp.int32)
    dst, tile_group, tile_nvalid = _group_layout(grp, tm)
    y = _moe_experts(hn_ext.reshape(t, d + LANE), dst, tile_group, tile_nvalid, w_gate, w_up, w_down,
                     layer, tm)
    return _residual(x, y, gate)


def kernel(x, c, ctx, c_ctx, w_ada, b_ada, norm1_g, norm2_g, w_in, qn_a, kn_a, sink_a, qn_b, kn_b,
           lam_b, subln_b, conv_w, conv_b, w_rg, b_rg, w_ig, b_ig, lru_lambda, lb_d, onorm_d,
           w_branch, w_out, w_router, b_router, w_gate, w_up, w_down):
    bsz, seq, d = x.shape
    lc = ctx.shape[1]
    depth = w_in.shape[0]

    n_rows = -(-(bsz + 1) // SUBLANE) * SUBLANE
    c_pad = jnp.zeros((n_rows, d), F32).at[:bsz].set(c).at[bsz].set(c_ctx)
    mod = _ada(c_pad, w_ada, b_ada).reshape(depth, n_rows, 6, d)

    lb_w = jax.nn.softmax(lb_d.astype(F32), axis=0)
    lb_all = jnp.cumsum(lb_w, axis=0) - lb_w[0:1]
    rope_tabs = _rope_tables(seq)
    w_router_t = w_router.T
    w_in_b = w_in.astype(BF16)
    w_branch_b = w_branch.astype(BF16).reshape((depth * w_branch.shape[1],) + w_branch.shape[2:])
    w_out_b = w_out.astype(BF16)
    w_gate_b, w_up_b, w_down_b = (w.astype(BF16).reshape((depth * N_EXPERTS,) + w.shape[2:])
                                  for w in (w_gate, w_up, w_down))

    xl = x
    xc = ctx.reshape(1, bsz * lc, d)
    for l in range(depth):
        ctx_out = l < depth - 1
        mod_l = [mod[l, :bsz, k][:, None, :] for k in range(6)]
        mod_c = [mod[l, bsz:bsz + 1, k][:, None, :] for k in range(6)]
        g1 = norm1_g[l].reshape(1, d)
        g2 = norm2_g[l].reshape(1, d)

        lo_l, hi_l = _norm_proj(xl, g1, mod_l[0], mod_l[1], w_in_b, l)
        lo_c, hi_c = _norm_proj(xc, g1, mod_c[0], mod_c[1], w_in_b, l)
        lo_c = lo_c.reshape(bsz, lc, -1)
        hi_c = hi_c.reshape(bsz, lc, -1)

        qa = qn_a[l] * (HEAD_DIM ** -0.5 * LOG2E)
        qb = jnp.tile(qn_b[l], 2) * (B_DIM ** -0.5 * LOG2E)
        kb = jnp.tile(kn_b[l], 2)
        one = jnp.ones((LANE,), F32)
        gains = jnp.stack([qa] * 8 + [kn_a[l]] * 2 + [one] * 2 + [qb] * 8 + [kb] * 8 + [one] * 8)
        gains = gains.reshape(N_QKV_BLOCKS, 1, LANE)
        p_l = _prep(lo_l, gains, rope_tabs)
        p_c = _prep(lo_c, gains, None)

        bound_a = jnp.maximum(
            LOG2E * HEAD_DIM ** 0.5 * jnp.max(jnp.abs(qn_a[l])) * jnp.max(jnp.abs(kn_a[l])) * LOGIT_MARGIN,
            LOG2E * jnp.max(jnp.abs(sink_a[l])))
        bounded_a = (bound_a < SOFTMAX_SAFE_LOGIT).astype(jnp.int32).reshape(1)
        oa_c, oa_l = _attn_a(sink_a[l], bounded_a, p_l, p_c, ctx_out)

        lq1, lk1, lq2, lk2 = lam_b[l].astype(F32)
        lam_init = 0.8 - 0.6 * math.exp(-0.3 * l)
        lam = (jnp.exp(jnp.sum(lq1 * lk1)) - jnp.exp(jnp.sum(lq2 * lk2)) + lam_init).reshape(1)
        bound_b = LOG2E * B_DIM ** 0.5 * jnp.max(jnp.abs(qn_b[l])) * jnp.max(jnp.abs(kn_b[l])) * LOGIT_MARGIN
        bounded_b = (bound_b < SOFTMAX_SAFE_LOGIT).astype(jnp.int32).reshape(1)
        ob_c, ob_l = _attn_b(lam, bounded_b, p_l, p_c, subln_b[l].reshape(1, LANE), 1.0 - lam_init, ctx_out)

        oc_c, oc_l = _lru(hi_l, hi_c, conv_w[l], conv_b[l], w_rg[l].astype(BF16), b_rg[l],
                          w_ig[l].astype(BF16), b_ig[l], lru_lambda[l], ctx_out)
        od_c, od_l = _gla(hi_l, hi_c, lb_all[l], onorm_d[l].reshape(1, LANE), ctx_out)

        moe_w = (w_router_t, b_router, w_gate_b, w_up_b, w_down_b, l)

        y_l = _merge((oa_l, ob_l, oc_l, od_l), lo_l, w_branch_b, l)
        xl = _out_proj(y_l, w_out_b, l, xl, mod_l[2])
        xl = _moe(xl, g2, mod_l[3], mod_l[4], mod_l[5], *moe_w)
        if ctx_out:
            flat = lambda t: t.reshape(1, bsz * lc, -1)
            y_c = _merge(tuple(flat(t) for t in (oa_c, ob_c, oc_c, od_c)), flat(lo_c), w_branch_b, l)
            xc = _out_proj(y_c, w_out_b, l, xc, mod_c[2])
            xc = _moe(xc, g2, mod_c[3], mod_c[4], mod_c[5], *moe_w)
    return xl
```

```python
import functools
import math

import jax
import jax.numpy as jnp
from jax import lax
from jax.experimental import pallas as pl
from jax.experimental.pallas import tpu as pltpu

F32 = jnp.float32
BF16 = jnp.bfloat16

LANE = 128
SUBLANE = 8
VMEM_LIMIT_BYTES = 56 * 1024 * 1024

EPS = 1e-6
ROPE_THETA = 10000.0
GRID_W = 64
HEAD_DIM = 128
WINDOW = 128
Q_BLOCK = 128
N_HEADS = 8
A_KV_HEADS = 2
A_GROUP = N_HEADS // A_KV_HEADS
A_Q_PER_STEP = 4
B_DIM = 64
B_KEY_CHUNK = 1024
LRU_C = 8.0
GLA_CHUNK = 64
GLA_SAFE_DECAY = 80.0
N_EXPERTS = 16
N_GROUPS = 4
EXPERTS_PER_GROUP = N_EXPERTS // N_GROUPS
NEG_BIG = -1e30
LOG2E = math.log2(math.e)
SOFTMAX_SAFE_LOGIT = 60.0
LOGIT_MARGIN = 1.02
TINY = 1e-37

COL_AQ, COL_AK, COL_AV = 0, 8, 10
COL_BQ, COL_BK, COL_BV = 12, 20, 28
N_QKV_BLOCKS = 36
COL_GATE = N_QKV_BLOCKS
COL_CX, COL_CY = 0, 8
COL_DQ, COL_DFF, COL_DFB, COL_DI, COL_DG = 16, 24, 32, 40, 48
W_IN_QKV = (0, 4608)
W_IN_REC = (4608, 11776)
W_IN_GATE = (11776, 19968)
PROJ_TILE = 512
PROJ_LO_SPARE = 1

NT_DIMS = (((1,), (1,)), ((), ()))
TN_DIMS = (((0,), (0,)), ((), ()))


def _cparams(*sem):
    return pltpu.CompilerParams(dimension_semantics=sem, vmem_limit_bytes=VMEM_LIMIT_BYTES)


def _tile(n, pref):
    t = min(n, pref)
    while n % t:
        t //= 2
    return t


def _sigmoid(x):
    return jax.nn.sigmoid(x)


def _modulated_norm(x, g, sh, sc):
    ms = jnp.mean(x * x, axis=-1, keepdims=True)
    return (x * lax.rsqrt(ms + EPS) * g) * (1.0 + sc) + sh


def _ada_kernel(c_ref, w_ref, b_ref, o_ref):
    c = c_ref[...]
    s = (c * _sigmoid(c)).astype(BF16)
    o_ref[0] = jnp.dot(s, w_ref[0].astype(BF16), preferred_element_type=F32) + b_ref[0]


def _ada(c_pad, w_ada, b_ada):
    nl, d, n = w_ada.shape
    rows = c_pad.shape[0]
    tn = _tile(n, 1024)
    return pl.pallas_call(
        _ada_kernel,
        grid=(nl, n // tn),
        in_specs=[
            pl.BlockSpec((rows, d), lambda l, j: (0, 0)),
            pl.BlockSpec((1, d, tn), lambda l, j: (l, 0, j)),
            pl.BlockSpec((1, 1, tn), lambda l, j: (l, 0, j)),
        ],
        out_specs=pl.BlockSpec((1, rows, tn), lambda l, j: (l, 0, j)),
        out_shape=jax.ShapeDtypeStruct((nl, rows, n), F32),
        compiler_params=_cparams("parallel", "parallel"),
        name="ada_mod",
    )(c_pad, w_ada, b_ada.reshape(nl, 1, n))


def _norm_proj_kernel(x_ref, g_ref, sh_ref, sc_ref, wa_ref, wb_ref, lo_ref, hi_ref, hn_ref, *, n_lo):
    j = pl.program_id(2)

    @pl.when(j == 0)
    def _():
        hn_ref[...] = _modulated_norm(x_ref[0], g_ref[...], sh_ref[0], sc_ref[0]).astype(BF16)

    def pair():
        hn = hn_ref[...]
        return jnp.concatenate([jnp.dot(hn, wa_ref[0], preferred_element_type=F32),
                                jnp.dot(hn, wb_ref[0], preferred_element_type=F32)], axis=1)

    @pl.when(j < n_lo)
    def _():
        lo_ref[0] = pair().astype(BF16)

    @pl.when(j >= n_lo)
    def _():
        hi_ref[0] = pair()


def _proj_weight_tile(t):
    n_qkv = (W_IN_QKV[1] - W_IN_QKV[0]) // PROJ_TILE
    n_gate = (W_IN_GATE[1] - W_IN_GATE[0]) // PROJ_TILE
    gate0 = W_IN_GATE[0] // PROJ_TILE
    rec0 = W_IN_REC[0] // PROJ_TILE
    n_lo = n_qkv + n_gate
    return jnp.where(t < n_qkv, t,
                     jnp.where(t < n_lo, t - n_qkv + gate0,
                               jnp.where(t < n_lo + PROJ_LO_SPARE, 0, t - n_lo - PROJ_LO_SPARE + rec0)))


def _norm_proj(x, g, sh, sc, w, layer):
    nb, r, d = x.shape
    tn = PROJ_TILE
    lo_tiles = (W_IN_QKV[1] - W_IN_QKV[0] + W_IN_GATE[1] - W_IN_GATE[0]) // tn + PROJ_LO_SPARE
    hi_tiles = (W_IN_REC[1] - W_IN_REC[0]) // tn
    n_lo, n_hi = lo_tiles // 2, hi_tiles // 2
    tm = _tile(r, 1024)
    return pl.pallas_call(
        functools.partial(_norm_proj_kernel, n_lo=n_lo),
        grid=(nb, r // tm, n_lo + n_hi),
        in_specs=[
            pl.BlockSpec((1, tm, d), lambda b, t, j: (b, t, 0)),
            pl.BlockSpec((1, d), lambda b, t, j: (0, 0)),
            pl.BlockSpec((1, 1, d), lambda b, t, j: (b, 0, 0)),
            pl.BlockSpec((1, 1, d), lambda b, t, j: (b, 0, 0)),
            pl.BlockSpec((1, d, tn), lambda b, t, j: (layer, 0, _proj_weight_tile(2 * j))),
            pl.BlockSpec((1, d, tn), lambda b, t, j: (layer, 0, _proj_weight_tile(2 * j + 1))),
        ],
        out_specs=[
            pl.BlockSpec((1, tm, 2 * tn), lambda b, t, j: (b, t, jnp.minimum(j, n_lo - 1))),
            pl.BlockSpec((1, tm, 2 * tn), lambda b, t, j: (b, t, jnp.maximum(j - n_lo, 0))),
        ],
        out_shape=[jax.ShapeDtypeStruct((nb, r, lo_tiles * tn), BF16),
                   jax.ShapeDtypeStruct((nb, r, hi_tiles * tn), F32)],
        scratch_shapes=[pltpu.VMEM((tm, d), BF16)],
        compiler_params=_cparams("parallel", "parallel", "arbitrary"),
        name="norm_proj",
    )(x, g, sh, sc, w, w)


def _group_mean_matrix(group):
    k = lax.broadcasted_iota(jnp.int32, (LANE, LANE), 0)
    l = lax.broadcasted_iota(jnp.int32, (LANE, LANE), 1)
    return jnp.where(k // group == l // group, 1.0 / group, 0.0).astype(BF16)


def _rope_partner_matrix(half):
    k = lax.broadcasted_iota(jnp.int32, (LANE, LANE), 0)
    l = lax.broadcasted_iota(jnp.int32, (LANE, LANE), 1)
    src = jnp.where((l % (2 * half)) < half, l + half, l - half)
    return jnp.where(k == src, 1.0, 0.0).astype(BF16)


def _group_mean(x2, mean_m):
    hi = x2.astype(BF16)
    lo = (x2 - hi.astype(F32)).astype(BF16)
    return (jnp.dot(hi, mean_m, preferred_element_type=F32)
            + jnp.dot(lo, mean_m, preferred_element_type=F32))


def _prep_kernel(*refs, rope):
    if rope:
        x_ref, g_ref, ca_ref, sa_ref, cb_ref, sb_ref, o_ref = refs
    else:
        x_ref, g_ref, o_ref = refs
    tm = x_ref.shape[1]

    def qk_blocks(col0, nblk, group, half, cos_ref, sin_ref):
        blocks = [slice((col0 + j) * LANE, (col0 + j + 1) * LANE) for j in range(nblk)]
        x = jnp.concatenate([x_ref[0, :, c].astype(F32) for c in blocks], axis=0)
        ms = _group_mean(x * x, _group_mean_matrix(group))
        y = (x * lax.rsqrt(ms + EPS)).reshape(nblk, tm, LANE) * g_ref[col0:col0 + nblk]
        if rope:
            part = jnp.dot(y.reshape(nblk * tm, LANE).astype(BF16), _rope_partner_matrix(half),
                           preferred_element_type=F32).reshape(nblk, tm, LANE)
            y = y * cos_ref[...][None] + part * sin_ref[...][None]
        for j, c in enumerate(blocks):
            o_ref[0, :, c] = y[j].astype(BF16)

    qk_blocks(COL_AQ, COL_AV - COL_AQ, HEAD_DIM, 32, ca_ref if rope else None, sa_ref if rope else None)
    qk_blocks(COL_BQ, COL_BV - COL_BQ, B_DIM, 16, cb_ref if rope else None, sb_ref if rope else None)
    for c0, c1 in ((COL_AV, COL_BQ), (COL_BV, N_QKV_BLOCKS)):
        o_ref[0, :, c0 * LANE:c1 * LANE] = x_ref[0, :, c0 * LANE:c1 * LANE]


def _prep(proj, gains, rope_tabs):
    nb, r, _ = proj.shape
    tm = _tile(r, 256)
    width = N_QKV_BLOCKS * LANE
    rope = rope_tabs is not None
    in_specs = [
        pl.BlockSpec((1, tm, width), lambda b, t: (b, t, 0)),
        pl.BlockSpec((N_QKV_BLOCKS, 1, LANE), lambda b, t: (0, 0, 0)),
    ]
    args = [proj, gains]
    if rope:
        in_specs += [pl.BlockSpec((tm, LANE), lambda b, t: (t, 0))] * 4
        args += list(rope_tabs)
    return pl.pallas_call(
        functools.partial(_prep_kernel, rope=rope),
        grid=(nb, r // tm),
        in_specs=in_specs,
        out_specs=pl.BlockSpec((1, tm, width), lambda b, t: (b, t, 0)),
        out_shape=jax.ShapeDtypeStruct((nb, r, width), BF16),
        compiler_params=_cparams("parallel", "parallel"),
        name="qkv_prep",
    )(*args)


def _rope_tables(seq):
    pos = jnp.arange(seq)
    rows = (pos // GRID_W).astype(F32)[:, None]
    cols = (pos % GRID_W).astype(F32)[:, None]
    lane = jnp.arange(LANE)

    def tables(half):
        inv = ROPE_THETA ** (-(lane % half).astype(F32) / half)
        use_rows = (lane % (4 * half)) < 2 * half
        ang = jnp.where(use_rows[None, :], rows, cols) * inv[None, :]
        sign = jnp.where((lane % (2 * half)) < half, -1.0, 1.0)
        return jnp.cos(ang), jnp.sin(ang) * sign[None, :]

    ca, sa = tables(32)
    cb, sb = tables(16)
    return ca, sa, cb, sb


def _stack_heads(q):
    return jnp.concatenate([q[:, g * HEAD_DIM:(g + 1) * HEAD_DIM] for g in range(A_GROUP)], axis=0)


def _unstack_heads(o, rows):
    return jnp.concatenate([o[g * rows:(g + 1) * rows] for g in range(A_GROUP)], axis=1)


def _sink_column(sink_ref, kvh, rows):
    return jnp.concatenate(
        [jnp.full((rows, 1), sink_ref[kvh * A_GROUP + g] * LOG2E, F32) for g in range(A_GROUP)], axis=0)


def _attn_a_lat_kernel(sink_ref, bounded_ref, q_ref, bias_ref, kl_ref, vl_ref, kc_ref, vc_ref, o_ref, *, seq):
    band = 3 * Q_BLOCK
    nblk = seq // Q_BLOCK

    def run(bounded):
        for qi in range(q_ref.shape[1] // Q_BLOCK):
            i = pl.program_id(1) * (q_ref.shape[1] // Q_BLOCK) + qi
            rows = slice(qi * Q_BLOCK, (qi + 1) * Q_BLOCK)
            first = jnp.clip(i - 1, 0, nblk - 3)
            start = pl.multiple_of(first * Q_BLOCK, Q_BLOCK)
            bias = bias_ref[i - first]
            outs = []
            for kvh in range(A_KV_HEADS):
                hs = slice(kvh * HEAD_DIM, (kvh + 1) * HEAD_DIM)
                qs = _stack_heads(q_ref[0, rows, kvh * A_GROUP * HEAD_DIM:(kvh + 1) * A_GROUP * HEAD_DIM])
                kb = kl_ref[0, pl.ds(start, band), hs]
                vb = vl_ref[0, pl.ds(start, band), hs]
                s_loc = lax.dot_general(qs, kb, NT_DIMS, preferred_element_type=F32)
                s_loc = (s_loc.reshape(A_GROUP, Q_BLOCK, band) + bias[None]).reshape(A_GROUP * Q_BLOCK, band)
                s_ctx = lax.dot_general(qs, kc_ref[0, :, hs], NT_DIMS, preferred_element_type=F32)
                sk = _sink_column(sink_ref, kvh, Q_BLOCK)
                if not bounded:
                    m = jnp.maximum(jnp.maximum(jnp.max(s_loc, axis=-1, keepdims=True),
                                                jnp.max(s_ctx, axis=-1, keepdims=True)), sk)
                    s_loc, s_ctx, sk = s_loc - m, s_ctx - m, sk - m
                p_loc = jnp.exp2(s_loc)
                p_ctx = jnp.exp2(s_ctx)
                den = (jnp.sum(p_loc, axis=-1, keepdims=True) + jnp.sum(p_ctx, axis=-1, keepdims=True)
                       + jnp.exp2(sk))
                o = (jnp.dot(p_loc.astype(BF16), vb, preferred_element_type=F32)
                     + jnp.dot(p_ctx.astype(BF16), vc_ref[0, :, hs], preferred_element_type=F32)) / den
                outs.append(_unstack_heads(o, Q_BLOCK))
            o_ref[0, rows, :] = jnp.concatenate(outs, axis=1).astype(BF16)

    pl.when(bounded_ref[0] > 0)(lambda: run(True))
    pl.when(bounded_ref[0] <= 0)(lambda: run(False))


def _attn_a_ctx_kernel(sink_ref, q_ref, kc_ref, vc_ref, o_ref):
    kvh = pl.program_id(1)
    rows = q_ref.shape[1]
    qs = _stack_heads(q_ref[0])
    s = lax.dot_general(qs, kc_ref[0], NT_DIMS, preferred_element_type=F32)
    sk = _sink_column(sink_ref, kvh, rows)
    m = jnp.maximum(jnp.max(s, axis=-1, keepdims=True), sk)
    p = jnp.exp2(s - m)
    den = jnp.sum(p, axis=-1, keepdims=True) + jnp.exp2(sk - m)
    o = jnp.dot(p.astype(BF16), vc_ref[0], preferred_element_type=F32) / den
    o_ref[0] = _unstack_heads(o, rows).astype(BF16)


def _window_bias():
    band = 3 * Q_BLOCK
    row = jnp.arange(Q_BLOCK)[None, :, None]
    col = jnp.arange(band)[None, None, :]
    delta = (jnp.arange(3) * Q_BLOCK)[:, None, None]
    return jnp.where(jnp.abs(col - delta - row) <= WINDOW, 0.0, NEG_BIG).astype(F32)


def _attn_a(sink, bounded, p_l, p_c, ctx_out):
    b, seq, _ = p_l.shape
    lc = p_c.shape[1]
    gw = A_GROUP * HEAD_DIM
    nblk = seq // Q_BLOCK
    kvw = A_KV_HEADS * HEAD_DIM
    smem = pl.BlockSpec(memory_space=pltpu.SMEM)

    qrows = A_Q_PER_STEP * Q_BLOCK if nblk % A_Q_PER_STEP == 0 else Q_BLOCK
    o_l = pl.pallas_call(
        functools.partial(_attn_a_lat_kernel, seq=seq),
        grid=(b, seq // qrows),
        in_specs=[
            smem, smem,
            pl.BlockSpec((1, qrows, N_HEADS * HEAD_DIM), lambda n, i: (n, i, 0)),
            pl.BlockSpec((3, Q_BLOCK, 3 * Q_BLOCK), lambda n, i: (0, 0, 0)),
            pl.BlockSpec((1, seq, kvw), lambda n, i: (n, 0, COL_AK // A_KV_HEADS)),
            pl.BlockSpec((1, seq, kvw), lambda n, i: (n, 0, COL_AV // A_KV_HEADS)),
            pl.BlockSpec((1, lc, kvw), lambda n, i: (n, 0, COL_AK // A_KV_HEADS)),
            pl.BlockSpec((1, lc, kvw), lambda n, i: (n, 0, COL_AV // A_KV_HEADS)),
        ],
        out_specs=pl.BlockSpec((1, qrows, N_HEADS * HEAD_DIM), lambda n, i: (n, i, 0)),
        out_shape=jax.ShapeDtypeStruct((b, seq, N_HEADS * HEAD_DIM), BF16),
        compiler_params=_cparams("parallel", "arbitrary"),
        name="attn_a_latent",
    )(sink, bounded, p_l, _window_bias(), p_l, p_l, p_c, p_c)
    o_c = None
    if ctx_out:
        o_c = pl.pallas_call(
            _attn_a_ctx_kernel,
            grid=(b, A_KV_HEADS),
            in_specs=[
                smem,
                pl.BlockSpec((1, lc, gw), lambda n, h: (n, 0, h)),
                pl.BlockSpec((1, lc, LANE), lambda n, h: (n, 0, COL_AK + h)),
                pl.BlockSpec((1, lc, LANE), lambda n, h: (n, 0, COL_AV + h)),
            ],
            out_specs=pl.BlockSpec((1, lc, gw), lambda n, h: (n, 0, h)),
            out_shape=jax.ShapeDtypeStruct((b, lc, N_HEADS * HEAD_DIM), BF16),
            compiler_params=_cparams("parallel", "parallel"),
            name="attn_a_ctx",
        )(sink, p_c, p_c, p_c)
    return o_c, o_l


def _attn_b_kernel(lam_ref, bounded_ref, q_ref, *refs, nseg, post_scale):
    k_refs = refs[:nseg]
    v_refs = refs[nseg:2 * nseg]
    g_ref, o_ref = refs[2 * nseg], refs[2 * nseg + 1]
    q = q_ref[0]
    tq = q.shape[0]
    lane = lax.broadcasted_iota(jnp.int32, q.shape, 1)
    zero = jnp.zeros_like(q)
    q2 = jnp.concatenate([jnp.where(lane < B_DIM, q, zero), jnp.where(lane < B_DIM, zero, q)], axis=0)
    chunks = []
    for k_ref, v_ref in zip(k_refs, v_refs):
        kc = _tile(k_ref.shape[1], B_KEY_CHUNK)
        chunks += [(k_ref, v_ref, c0, kc) for c0 in range(0, k_ref.shape[1], kc)]

    def scores(k_ref, c0, kc):
        return lax.dot_general(q2, k_ref[0, c0:c0 + kc, :], NT_DIMS, preferred_element_type=F32)

    def finish(acc, den):
        o2 = acc / den
        o = o2[:tq] - lam_ref[0] * o2[tq:]
        ms = jnp.mean(o * o, axis=-1, keepdims=True)
        o_ref[0] = ((o * lax.rsqrt(ms + EPS) * g_ref[...]) * post_scale).astype(BF16)

    @pl.when(bounded_ref[0] > 0)
    def _():
        den = jnp.zeros((2 * tq, 1), F32)
        acc = jnp.zeros((2 * tq, LANE), F32)
        for k_ref, v_ref, c0, kc in chunks:
            p = jnp.exp2(scores(k_ref, c0, kc))
            den = den + jnp.sum(p, axis=-1, keepdims=True)
            acc = acc + jnp.dot(p.astype(BF16), v_ref[0, c0:c0 + kc, :], preferred_element_type=F32)
        finish(acc, den)

    @pl.when(bounded_ref[0] <= 0)
    def _():
        m = jnp.full((2 * tq, 1), NEG_BIG, F32)
        den = jnp.zeros((2 * tq, 1), F32)
        acc = jnp.zeros((2 * tq, LANE), F32)
        for k_ref, v_ref, c0, kc in chunks:
            s = scores(k_ref, c0, kc)
            m_new = jnp.maximum(m, jnp.max(s, axis=-1, keepdims=True))
            alpha = jnp.exp2(m - m_new)
            p = jnp.exp2(s - m_new)
            den = alpha * den + jnp.sum(p, axis=-1, keepdims=True)
            acc = alpha * acc + jnp.dot(p.astype(BF16), v_ref[0, c0:c0 + kc, :],
                                        preferred_element_type=F32)
            m = m_new
        finish(acc, den)


def _attn_b_call(lam, bounded, q_src, k_srcs, subln, post_scale, name):
    b, rq, _ = q_src.shape
    tq = _tile(rq, 1024)
    smem = pl.BlockSpec(memory_space=pltpu.SMEM)
    in_specs = [smem, smem, pl.BlockSpec((1, tq, LANE), lambda n, h, i: (n, i, COL_BQ + h))]
    in_specs += [pl.BlockSpec((1, s.shape[1], LANE), lambda n, h, i: (n, 0, COL_BK + h)) for s in k_srcs]
    in_specs += [pl.BlockSpec((1, s.shape[1], LANE), lambda n, h, i: (n, 0, COL_BV + h)) for s in k_srcs]
    in_specs += [pl.BlockSpec((1, LANE), lambda n, h, i: (0, 0))]
    return pl.pallas_call(
        functools.partial(_attn_b_kernel, nseg=len(k_srcs), post_scale=post_scale),
        grid=(b, N_HEADS, rq // tq),
        in_specs=in_specs,
        out_specs=pl.BlockSpec((1, tq, LANE), lambda n, h, i: (n, i, h)),
        out_shape=jax.ShapeDtypeStruct((b, rq, N_HEADS * HEAD_DIM), BF16),
        compiler_params=_cparams("parallel", "parallel", "arbitrary"),
        name=name,
    )(lam, bounded, q_src, *k_srcs, *k_srcs, subln)


def _attn_b(lam, bounded, p_l, p_c, subln, post_scale, ctx_out):
    o_l = _attn_b_call(lam, bounded, p_l, [p_c, p_l], subln, post_scale, "attn_b_latent")
    o_c = _attn_b_call(lam, bounded, p_c, [p_c], subln, post_scale, "attn_b_ctx") if ctx_out else None
    return o_c, o_l


def _centred_conv(x_ref, pad_ref, w_ref, b_ref):
    n = x_ref.shape[1]
    zeros = jnp.zeros((SUBLANE, LANE), F32)
    pad_ref[0:SUBLANE, :] = zeros
    pad_ref[SUBLANE:SUBLANE + n, :] = x_ref[0]
    pad_ref[SUBLANE + n:2 * SUBLANE + n, :] = zeros
    out = b_ref[...]
    for tap in range(4):
        out = out + pad_ref[pl.ds(SUBLANE - 2 + tap, n), :] * w_ref[tap:tap + 1, :]
    return out


def _every_8th(ref, d, j, n):
    return ref[d, pl.ds(j, n, stride=SUBLANE), :]


def _tile_order(reverse):
    return range(SUBLANE - 1, -1, -1) if reverse else range(SUBLANE)


def _scan_summaries(a_ref, v_ref, d, n, reverse):
    h = p = None
    for j in _tile_order(reverse):
        a, v = _every_8th(a_ref, d, j, n), _every_8th(v_ref, d, j, n)
        h, p = (v, a) if h is None else (a * h + v, a * p)
    return p, h


def _scan_entering(a_ref, v_ref, d, n, carry_in, reverse, store):
    e = carry_in
    for j in _tile_order(reverse):
        store(j, e)
        e = _every_8th(a_ref, d, j, n) * e + _every_8th(v_ref, d, j, n)


def _scan_apply(a_ref, v_ref, d, n, carry_in, reverse, store):
    h = carry_in
    for j in _tile_order(reverse):
        h = _every_8th(a_ref, d, j, n) * h + _every_8th(v_ref, d, j, n)
        store(j, h)


def _seg_tile(t, ntc, ntl, reverse):
    if not reverse:
        return t
    return jnp.where(t < ntc, ntc - 1 - t, 2 * ntc + ntl - 1 - t)


def _lru_kernel(xc_ref, yc_ref, xl_ref, yl_ref, cw_ref, cb_ref, wr_ref, br_ref, wi_ref, bi_ref,
                lam_ref, *refs, ctx_out):
    if ctx_out:
        oc_ref, ol_ref, ac_ref, hl_ref, hs_ref, pad_ref, tp_ref, th_ref, sp_ref, sh_ref, c3_ref, e1_ref = refs
    else:
        ol_ref, ac_ref, hl_ref, hs_ref, pad_ref, tp_ref, th_ref, sp_ref, sh_ref, c3_ref, e1_ref = refs
    lc, seq = xc_ref.shape[1], xl_ref.shape[1]
    for x_ref, off in ((xc_ref, 0), (xl_ref, lc)):
        n = x_ref.shape[1]
        u = _centred_conv(x_ref, pad_ref, cw_ref, cb_ref)
        ub = u.astype(BF16)
        for d in range(2):
            lam = lam_ref[d:d + 1, :]
            sp = jnp.maximum(-lam, 0.0) + jnp.log1p(jnp.exp(-jnp.abs(lam)))
            r = _sigmoid(jnp.dot(ub, wr_ref[d, 0], preferred_element_type=F32) + br_ref[d:d + 1, :])
            gi = _sigmoid(jnp.dot(ub, wi_ref[d, 0], preferred_element_type=F32) + bi_ref[d:d + 1, :])
            log_a = -LRU_C * r * sp
            a = jnp.exp(log_a)
            y = jnp.tanh(-log_a) * (1.0 + a * a)
            v = (y * lax.rsqrt(jnp.maximum(y, TINY))) * gi * u
            ac_ref[d, off:off + n, :] = a
            hl_ref[d, off:off + n, :] = v

    nt1 = (lc + seq) // SUBLANE
    nt2 = nt1 // SUBLANE
    group_rows = SUBLANE * SUBLANE
    for d in range(2):
        tp_ref[d], th_ref[d] = _scan_summaries(ac_ref, hl_ref, d, nt1, d == 1)
        sp_ref[d], sh_ref[d] = _scan_summaries(tp_ref, th_ref, d, nt2, d == 1)

    def step(s, carry):
        new = []
        for d in range(2):
            g = pl.ds(_seg_tile(s, lc // group_rows, seq // group_rows, d == 1), 1)
            c3_ref[d, g, :] = carry[d]
            new.append(sp_ref[d, g, :] * carry[d] + sh_ref[d, g, :])
        return tuple(new)

    zero = jnp.zeros((1, LANE), F32)
    lax.fori_loop(0, nt2, step, (zero, zero))

    for d in range(2):
        def store_entering(j, e, d=d):
            e1_ref[d, pl.ds(j, nt2, stride=SUBLANE), :] = e

        def store_h(j, h, d=d):
            if d == 0:
                hs_ref[pl.ds(j, nt1, stride=SUBLANE), :] = h
            else:
                hl_ref[d, pl.ds(j, nt1, stride=SUBLANE), :] = h

        _scan_entering(tp_ref, th_ref, d, nt2, c3_ref[d], d == 1, store_entering)
        _scan_apply(ac_ref, hl_ref, d, nt1, e1_ref[d], d == 1, store_h)
    if ctx_out:
        hc = hs_ref[0:lc, :] + hl_ref[1, 0:lc, :]
        oc_ref[0] = (hc * jax.nn.gelu(yc_ref[0])).astype(BF16)
    hl = hs_ref[lc:lc + seq, :] + hl_ref[1, lc:lc + seq, :]
    ol_ref[0] = (hl * jax.nn.gelu(yl_ref[0])).astype(BF16)


def _lru(proj_l, proj_c, conv_w, conv_b, w_r, b_r, w_i, b_i, lam, ctx_out):
    b, seq, _ = proj_l.shape
    lc = proj_c.shape[1]
    nblk = w_r.shape[1]
    width = nblk * LANE
    nt1 = (lc + seq) // SUBLANE
    nt2 = nt1 // SUBLANE

    def seg(n, col):
        return pl.BlockSpec((1, n, LANE), lambda i, j: (i, 0, col + j))

    vec2 = pl.BlockSpec((2, LANE), lambda i, j: (0, j))
    wspec = pl.BlockSpec((2, 1, LANE, LANE), lambda i, j: (0, j, 0, 0))
    out_specs = [pl.BlockSpec((1, seq, LANE), lambda i, j: (i, 0, j))]
    out_shape = [jax.ShapeDtypeStruct((b, seq, width), BF16)]
    if ctx_out:
        out_specs.insert(0, pl.BlockSpec((1, lc, LANE), lambda i, j: (i, 0, j)))
        out_shape.insert(0, jax.ShapeDtypeStruct((b, lc, width), BF16))
    outs = pl.pallas_call(
        functools.partial(_lru_kernel, ctx_out=ctx_out),
        grid=(b, nblk),
        in_specs=[
            seg(lc, COL_CX), seg(lc, COL_CY), seg(seq, COL_CX), seg(seq, COL_CY),
            pl.BlockSpec((4, LANE), lambda i, j: (0, j)),
            pl.BlockSpec((1, LANE), lambda i, j: (0, j)),
            wspec, vec2, wspec, vec2, vec2,
        ],
        out_specs=out_specs,
        out_shape=out_shape,
        scratch_shapes=[
            pltpu.VMEM((2, lc + seq, LANE), F32),
            pltpu.VMEM((2, lc + seq, LANE), F32),
            pltpu.VMEM((lc + seq, LANE), F32),
            pltpu.VMEM((max(lc, seq) + 2 * SUBLANE, LANE), F32),
            pltpu.VMEM((2, nt1, LANE), F32),
            pltpu.VMEM((2, nt1, LANE), F32),
            pltpu.VMEM((2, nt2, LANE), F32),
            pltpu.VMEM((2, nt2, LANE), F32),
            pltpu.VMEM((2, nt2, LANE), F32),
            pltpu.VMEM((2, nt1, LANE), F32),
        ],
        compiler_params=_cparams("parallel", "parallel"),
        name="rglru",
    )(proj_c, proj_c, proj_l, proj_l, conv_w, conv_b.reshape(1, width), w_r, b_r, w_i, b_i, lam)
    return (outs[0], outs[1]) if ctx_out else (None, outs[0])


def _gla_level_masks(reverse):
    c = GLA_CHUNK
    row = lax.broadcasted_iota(jnp.int32, (c, c), 0)
    col = lax.broadcasted_iota(jnp.int32, (c, c), 1)
    masks = {}
    for s in (32, 16, 8):
        same = (row // (2 * s)) == (col // (2 * s))
        if reverse:
            masks[s] = same & ((row % (2 * s)) < s) & ((col % (2 * s)) >= s)
        else:
            masks[s] = same & ((row % (2 * s)) >= s) & ((col % (2 * s)) < s)
    tri = (row <= col) if reverse else (row >= col)
    return masks, tri.astype(F32)


def _gla_chunk(q, k, v, g, st, masks, tri, reverse):
    c = GLA_CHUNK
    b = jnp.dot(tri, g, precision=lax.Precision.HIGHEST, preferred_element_type=F32)
    att = jnp.zeros((c, c), F32)
    for s in (32, 16, 8):
        b3 = b.reshape(c // (2 * s), 2 * s, LANE)
        rr = s if reverse else s - 1
        rho = jnp.broadcast_to(b3[:, rr:rr + 1, :], b3.shape).reshape(c, LANE)
        e = jnp.exp(-jnp.abs(b - rho))
        a = lax.dot_general((q * e).astype(BF16), (k * e).astype(BF16), NT_DIMS,
                            preferred_element_type=F32)
        att = att + jnp.where(masks[s], a, 0.0)
    vb = v.astype(BF16)
    o = jnp.dot(att.astype(BF16), vb, preferred_element_type=F32)
    nb = c // SUBLANE
    b3 = b.reshape(nb, SUBLANE, LANE)
    q3 = q.reshape(nb, SUBLANE, LANE)
    k3 = k.reshape(nb, SUBLANE, LANE)
    v3 = v.reshape(nb, SUBLANE, LANE)
    rowi = lax.broadcasted_iota(jnp.int32, b3.shape, 1)
    od = jnp.zeros(b3.shape, F32)
    for jj in range(SUBLANE):
        keep = (rowi <= jj) if reverse else (rowi >= jj)
        e = jnp.where(keep, jnp.exp(jnp.minimum(b3 - b3[:, jj:jj + 1, :], 0.0)), 0.0)
        sj = jnp.sum(q3 * e * k3[:, jj:jj + 1, :], axis=-1, keepdims=True)
        od = od + sj * v3[:, jj:jj + 1, :]
    o = o + od.reshape(c, LANE)
    o = o + lax.dot_general((q * jnp.exp(b)).astype(BF16), st.astype(BF16), NT_DIMS,
                            preferred_element_type=F32)
    b_end = b[0:1, :] if reverse else b[c - 1:c, :]
    khat = (k * jnp.exp(b_end - b)).astype(BF16)
    st_new = st * jnp.exp(b_end) + lax.dot_general(vb, khat, TN_DIMS, preferred_element_type=F32)
    return o, st_new


def _gla_fast_intra(q, k, v, g, tri, reverse):
    c = GLA_CHUNK
    g_hi = g.astype(BF16)
    g_lo = (g - g_hi.astype(F32)).astype(BF16)
    b2 = jnp.dot(tri.astype(BF16), jnp.concatenate([g_hi, g_lo], axis=1), preferred_element_type=F32)
    b = b2[:, :LANE] + b2[:, LANE:]
    rr = c // 2 if reverse else c // 2 - 1
    rho = b[rr:rr + 1, :]
    qt = q * jnp.exp(b - rho)
    kt = k * jnp.exp(rho - b)
    a = lax.dot_general(qt.astype(BF16), kt.astype(BF16), NT_DIMS, preferred_element_type=F32)
    att = jnp.where(tri > 0.0, a, 0.0)
    vb = v.astype(BF16)
    o_intra = jnp.dot(att.astype(BF16), vb, preferred_element_type=F32)
    b_end = b[0:1, :] if reverse else b[c - 1:c, :]
    qe = (qt * jnp.exp(rho)).astype(BF16)
    khat = (kt * jnp.exp(b_end - rho)).astype(BF16)
    return o_intra, qe, khat, vb, jnp.exp(b_end)


def _gla_fast_inter(intra, st):
    o_intra, qe, khat, vb, decay = intra
    o = o_intra + lax.dot_general(qe, st.astype(BF16), NT_DIMS, preferred_element_type=F32)
    st_new = st * decay + lax.dot_general(vb, khat, TN_DIMS, preferred_element_type=F32)
    return o, st_new


def _gla_kernel(lb_ref, on_ref, qc_ref, ffc_ref, fbc_ref, ic_ref, gc_ref,
                ql_ref, ffl_ref, fbl_ref, il_ref, gl_ref, *refs, ctx_out):
    if ctx_out:
        oc_ref, ol_ref, q_s, v_s, g_s, k_s, o_s, st_s = refs
    else:
        ol_ref, q_s, v_s, g_s, k_s, o_s, st_s = refs
    lc, seq = qc_ref.shape[1], ql_ref.shape[1]
    ncc, ncl = lc // GLA_CHUNK, seq // GLA_CHUNK
    for off, n, q_ref, i_ref, f_refs in ((0, lc, qc_ref, ic_ref, (ffc_ref, fbc_ref)),
                                         (lc, seq, ql_ref, il_ref, (ffl_ref, fbl_ref))):
        q_s[off:off + n, :] = q_ref[0]
        v_s[off:off + n, :] = i_ref[0]
        for d in range(2):
            z = f_refs[d][0]
            lbd = lb_ref[d:d + 1, :]
            ez = jnp.exp(-jnp.abs(z))
            r = 1.0 / (1.0 + ez)
            pos = z >= 0.0
            sig_p = jnp.where(pos, r, ez * r)
            sig_n = jnp.where(pos, ez * r, r)
            g_s[d, off:off + n, :] = jnp.log(lbd + (1.0 - lbd) * sig_p)
            k_s[d, off:off + n, :] = (1.0 - lbd) * sig_n
    st_s[...] = jnp.zeros(st_s.shape, F32)
    consts = [_gla_level_masks(False), _gla_level_masks(True)]

    half = GLA_CHUNK // 2
    worst = jnp.zeros((1, LANE), F32)
    for d in range(2):
        hs = jnp.sum(g_s[d].reshape((lc + seq) // half, half, LANE), axis=1)
        worst = jnp.maximum(worst, jnp.max(-hs, axis=0, keepdims=True))
    q_mag = jnp.max(jnp.abs(q_s[...]), axis=0, keepdims=True)
    safe = jnp.max(worst + jnp.log(jnp.maximum(q_mag, 1.0))) < GLA_SAFE_DECAY

    def chunk_rows(cidx, d):
        chunk = _seg_tile(cidx, ncc, ncl, d == 1)
        return pl.ds(pl.multiple_of(chunk * GLA_CHUNK, GLA_CHUNK), GLA_CHUNK)

    def robust_step(cidx, carry):
        for d in range(2):
            rows = chunk_rows(cidx, d)
            o, st_new = _gla_chunk(q_s[rows, :], k_s[d, rows, :], v_s[rows, :], g_s[d, rows, :],
                                   st_s[d], consts[d][0], consts[d][1], d == 1)
            st_s[d] = st_new
            o_s[d, rows, :] = o
        return carry

    nchunks = ncc + ncl
    group = next(u for u in (12, 6, 4, 3, 2, 1) if nchunks % u == 0)

    def fast_step(t, carry):
        work = [[] for _ in range(2)]
        for d in range(2):
            for u in range(group):
                rows = chunk_rows(t * group + u, d)
                work[d].append((rows, _gla_fast_intra(q_s[rows, :], k_s[d, rows, :], v_s[rows, :],
                                                      g_s[d, rows, :], consts[d][1], d == 1)))
        for d in range(2):
            st = st_s[d]
            outs = []
            for rows, intra in work[d]:
                o, st = _gla_fast_inter(intra, st)
                outs.append((rows, o))
            st_s[d] = st
            for rows, o in outs:
                o_s[d, rows, :] = o
        return carry

    @pl.when(safe)
    def _():
        lax.fori_loop(0, nchunks // group, fast_step, 0)

    @pl.when(jnp.logical_not(safe))
    def _():
        lax.fori_loop(0, nchunks, robust_step, 0)

    def finish(o, gate):
        ms = jnp.mean(o * o, axis=-1, keepdims=True)
        return ((o * lax.rsqrt(ms + EPS) * on_ref[...]) * (gate * _sigmoid(gate))).astype(BF16)

    if ctx_out:
        oc_ref[0] = finish(o_s[0, 0:lc, :] + o_s[1, 0:lc, :], gc_ref[0])
    ol_ref[0] = finish(o_s[0, lc:lc + seq, :] + o_s[1, lc:lc + seq, :], gl_ref[0])


def _gla(proj_l, proj_c, lb, onorm, ctx_out):
    b, seq, _ = proj_l.shape
    lc = proj_c.shape[1]
    width = N_HEADS * HEAD_DIM
    nt = lc + seq

    def seg(n, col):
        return pl.BlockSpec((1, n, LANE), lambda i, h: (i, 0, col + h))

    cols = (COL_DQ, COL_DFF, COL_DFB, COL_DI, COL_DG)
    out_specs = [pl.BlockSpec((1, seq, LANE), lambda i, h: (i, 0, h))]
    out_shape = [jax.ShapeDtypeStruct((b, seq, width), BF16)]
    if ctx_out:
        out_specs.insert(0, pl.BlockSpec((1, lc, LANE), lambda i, h: (i, 0, h)))
        out_shape.insert(0, jax.ShapeDtypeStruct((b, lc, width), BF16))
    outs = pl.pallas_call(
        functools.partial(_gla_kernel, ctx_out=ctx_out),
        grid=(b, N_HEADS),
        in_specs=[pl.BlockSpec((2, LANE), lambda i, h: (0, h)),
                  pl.BlockSpec((1, LANE), lambda i, h: (0, 0))]
        + [seg(lc, c) for c in cols] + [seg(seq, c) for c in cols],
        out_specs=out_specs,
        out_shape=out_shape,
        scratch_shapes=[
            pltpu.VMEM((nt, LANE), F32),
            pltpu.VMEM((nt, LANE), F32),
            pltpu.VMEM((2, nt, LANE), F32),
            pltpu.VMEM((2, nt, LANE), F32),
            pltpu.VMEM((2, nt, LANE), F32),
            pltpu.VMEM((2, LANE, LANE), F32),
        ],
        compiler_params=_cparams("parallel", "parallel"),
        name="hgrn2",
    )(lb, onorm, *([proj_c] * 5), *([proj_l] * 5))
    return (outs[0], outs[1]) if ctx_out else (None, outs[0])


def _merge_kernel(oa_ref, ob_ref, oc_ref, od_ref, g0_ref, g1_ref, g2_ref, g3_ref,
                  w0_ref, w1_ref, w2_ref, w3_ref, y_ref):
    acc = None
    for o_ref, g_ref, w_ref in ((oa_ref, g0_ref, w0_ref), (ob_ref, g1_ref, w1_ref),
                                (oc_ref, g2_ref, w2_ref), (od_ref, g3_ref, w3_ref)):
        t = _sigmoid(g_ref[0].astype(F32)) * jnp.dot(o_ref[0], w_ref[0], preferred_element_type=F32)
        acc = t if acc is None else acc + t
    y_ref[0] = acc.astype(BF16)


def _merge(outs, proj, w_branch, layer):
    nb, r, mw = outs[0].shape
    d = w_branch.shape[2]
    tm = _tile(r, 1024)
    tn = 512
    nj = d // tn
    o_spec = pl.BlockSpec((1, tm, mw), lambda b, t, j: (b, t, 0))
    g_specs = [pl.BlockSpec((1, tm, tn), functools.partial(
        lambda b, t, j, n: (b, t, (COL_GATE * LANE) // tn + n * nj + j), n=n)) for n in range(4)]
    w_specs = [pl.BlockSpec((1, mw, tn), functools.partial(lambda b, t, j, n: (layer * 4 + n, 0, j), n=n))
               for n in range(4)]
    return pl.pallas_call(
        _merge_kernel,
        grid=(nb, r // tm, nj),
        in_specs=[o_spec] * 4 + g_specs + w_specs,
        out_specs=pl.BlockSpec((1, tm, tn), lambda b, t, j: (b, t, j)),
        out_shape=jax.ShapeDtypeStruct((nb, r, d), BF16),
        compiler_params=_cparams("parallel", "parallel", "arbitrary"),
        name="branch_merge",
    )(*outs, *([proj] * 4), *([w_branch] * 4))


def _out_proj_kernel(y_ref, w_ref, x_ref, g_ref, o_ref):
    o_ref[0] = x_ref[0] + g_ref[0] * jnp.dot(y_ref[0], w_ref[0], preferred_element_type=F32)


def _out_proj(y, w, layer, x, gate):
    nb, r, d = x.shape
    tm = _tile(r, 1024)
    tn = 1024
    return pl.pallas_call(
        _out_proj_kernel,
        grid=(nb, r // tm, d // tn),
        in_specs=[
            pl.BlockSpec((1, tm, d), lambda b, t, j: (b, t, 0)),
            pl.BlockSpec((1, d, tn), lambda b, t, j: (layer, 0, j)),
            pl.BlockSpec((1, tm, tn), lambda b, t, j: (b, t, j)),
            pl.BlockSpec((1, 1, tn), lambda b, t, j: (b, 0, j)),
        ],
        out_specs=pl.BlockSpec((1, tm, tn), lambda b, t, j: (b, t, j)),
        out_shape=jax.ShapeDtypeStruct((nb, r, d), F32),
        compiler_params=_cparams("parallel", "parallel", "arbitrary"),
        name="out_proj",
    )(y, w, x, gate)


def _route(logits_t, bias_col):
    aff = _sigmoid(logits_t)
    sel = aff + bias_col
    aff_r = [aff[e:e + 1, :] for e in range(N_EXPERTS)]
    sel_r = [sel[e:e + 1, :] for e in range(N_EXPERTS)]
    scores = []
    for g in range(N_GROUPS):
        v = sel_r[g * EXPERTS_PER_GROUP:(g + 1) * EXPERTS_PER_GROUP]
        m1 = functools.reduce(jnp.maximum, v)
        taken = jnp.zeros(m1.shape, jnp.bool_)
        second = jnp.full(m1.shape, -jnp.inf, F32)
        for x in v:
            first = (x == m1) & jnp.logical_not(taken)
            taken = taken | first
            second = jnp.where(first, second, jnp.maximum(second, x))
        scores.append(m1 + second)
    best, gidx = scores[0], jnp.zeros(scores[0].shape, jnp.int32)
    for g in range(1, N_GROUPS):
        better = scores[g] > best
        gidx = jnp.where(better, g, gidx)
        best = jnp.where(better, scores[g], best)
    masked = [jnp.where(gidx == e // EXPERTS_PER_GROUP, sel_r[e], -jnp.inf) for e in range(N_EXPERTS)]

    def first_argmax(vals, exclude):
        bv = jnp.full(vals[0].shape, -jnp.inf, F32)
        bi = jnp.full(vals[0].shape, -1, jnp.int32)
        for e, x in enumerate(vals):
            better = x > bv
            if exclude is not None:
                better = better & (exclude != e)
            bi = jnp.where(better, e, bi)
            bv = jnp.where(better, x, bv)
        return bi

    i1 = first_argmax(masked, None)
    i2 = first_argmax(masked, i1)
    w1 = functools.reduce(jnp.add, [jnp.where(i1 == e, aff_r[e], 0.0) for e in range(N_EXPERTS)])
    w2 = functools.reduce(jnp.add, [jnp.where(i2 == e, aff_r[e], 0.0) for e in range(N_EXPERTS)])
    tot = w1 + w2
    g1, g2 = w1 / tot, w2 / tot
    width = logits_t.shape[1]
    rowi = lax.broadcasted_iota(jnp.int32, (SUBLANE, width), 0)
    local1 = jnp.broadcast_to(i1 - gidx * EXPERTS_PER_GROUP, (SUBLANE, width))
    local2 = jnp.broadcast_to(i2 - gidx * EXPERTS_PER_GROUP, (SUBLANE, width))
    info = (jnp.where(rowi == local1, jnp.broadcast_to(g1, (SUBLANE, width)), 0.0)
            + jnp.where(rowi == local2, jnp.broadcast_to(g2, (SUBLANE, width)), 0.0))
    return jnp.where(rowi == EXPERTS_PER_GROUP,
                     jnp.broadcast_to(gidx.astype(F32), (SUBLANE, width)), info)


def _moe_route_kernel(x_ref, g_ref, sh_ref, sc_ref, wrt_ref, br_ref, hn_ref, info_ref):
    d = x_ref.shape[2]
    h = _modulated_norm(x_ref[0], g_ref[...], sh_ref[0], sc_ref[0])
    logits_t = lax.dot_general(wrt_ref[...], h, NT_DIMS, precision=lax.Precision.HIGHEST,
                               preferred_element_type=F32)
    info = _route(logits_t, br_ref[...])
    info_ref[0] = info
    hn_ref[0, :, 0:d] = h
    pad = jnp.zeros((LANE - SUBLANE, info.shape[1]), F32)
    hn_ref[0, :, d:d + LANE] = jnp.concatenate([info, pad], axis=0).T


def _moe_route(x, g, sh, sc, w_router_t, b_router):
    nb, r, d = x.shape
    tm = _tile(r, 512)
    vec = pl.BlockSpec((1, 1, d), lambda b, t: (b, 0, 0))
    return pl.pallas_call(
        _moe_route_kernel,
        grid=(nb, r // tm),
        in_specs=[
            pl.BlockSpec((1, tm, d), lambda b, t: (b, t, 0)),
            pl.BlockSpec((1, d), lambda b, t: (0, 0)),
            vec, vec,
            pl.BlockSpec((N_EXPERTS, d), lambda b, t: (0, 0)),
            pl.BlockSpec((N_EXPERTS, 1), lambda b, t: (0, 0)),
        ],
        out_specs=[pl.BlockSpec((1, tm, d + LANE), lambda b, t: (b, t, 0)),
                   pl.BlockSpec((1, SUBLANE, tm), lambda b, t: (b, 0, t))],
        out_shape=[jax.ShapeDtypeStruct((nb, r, d + LANE), F32),
                   jax.ShapeDtypeStruct((nb, SUBLANE, r), F32)],
        compiler_params=_cparams("parallel", "parallel"),
        name="moe_route",
    )(x, g, sh, sc, w_router_t, b_router.reshape(N_EXPERTS, 1))


def _row_copy(src, src_row, dst, dst_row, sem):
    return pltpu.make_async_copy(src.at[pl.ds(src_row, 1), :], dst.at[pl.ds(dst_row, 1), :], sem)


def _moe_experts_kernel(dst_ref, tgrp_ref, nval_ref, hn_hbm, wg_ref, wu_ref, wd_ref,
                        y_hbm, xbuf, hb, gbuf, acc, gsem, ssem, *, tm, n_tok):
    i = pl.program_id(0)
    e = pl.program_id(1)
    nt = pl.num_programs(0)
    slot = i % 2
    per = tm // EXPERTS_PER_GROUP
    d = hb.shape[1]

    def gather_row(slot_idx, buf, row):
        s = dst_ref[slot_idx]
        return _row_copy(hn_hbm, jnp.where(s < n_tok, s, 0), xbuf.at[buf], row, gsem.at[buf])

    def wait_scatter(sl):
        pltpu.make_async_copy(acc.at[sl], y_hbm.at[pl.ds(0, tm), :], ssem.at[sl]).wait()

    @pl.when((i == 0) & (e == 0))
    def _():
        def body(r, c):
            gather_row(r, 0, r).start()
            return c
        lax.fori_loop(0, tm, body, 0)
        acc[1] = jnp.zeros(acc.shape[1:], F32)
        for half in range(2):
            fill = pltpu.make_async_copy(acc.at[1], y_hbm.at[pl.ds(n_tok + half * tm, tm), :], ssem.at[1])
            fill.start()
            fill.wait()

    @pl.when(e == 0)
    def _():
        pltpu.make_async_copy(hn_hbm.at[pl.ds(0, tm), :], xbuf.at[slot], gsem.at[slot]).wait()
        hb[...] = xbuf[slot, :, 0:d].astype(BF16)
        gbuf[...] = xbuf[slot, :, d:d + LANE]

        @pl.when(i >= 2)
        def _():
            wait_scatter(slot)
        acc[slot] = jnp.zeros(acc.shape[1:], F32)

    @pl.when(i + 1 < nt)
    def _():
        base = (i + 1) * tm + e * per
        for k in range(per):
            gather_row(base + k, 1 - slot, e * per + k).start()

    @pl.when(nval_ref[i] > 0)
    def _():
        hv = hb[...]
        a = jnp.dot(hv, wg_ref[0], preferred_element_type=F32)
        u = jnp.dot(hv, wu_ref[0], preferred_element_type=F32)
        hid = ((a * _sigmoid(a)) * u).astype(BF16)
        y = jnp.dot(hid, wd_ref[0], preferred_element_type=F32)
        lane = lax.broadcasted_iota(jnp.int32, gbuf.shape, 1)
        col = jnp.sum(jnp.where(lane == e, gbuf[...], 0.0), axis=-1, keepdims=True)
        acc[slot] = acc[slot] + col * y

    @pl.when(e == EXPERTS_PER_GROUP - 1)
    def _():
        def body(r, c):
            _row_copy(acc.at[slot], r, y_hbm, dst_ref[i * tm + r], ssem.at[slot]).start()
            return c
        lax.fori_loop(0, tm, body, 0, unroll=8)

        @pl.when(i == nt - 1)
        def _():
            wait_scatter(slot)

            @pl.when(i >= 1)
            def _():
                wait_scatter(1 - slot)


def _moe_experts(hn_ext, dst, tile_group, tile_nvalid, w_gate, w_up, w_down, layer, tm):
    t, dw = hn_ext.shape
    d = dw - LANE
    dff = w_gate.shape[2]
    ntiles = tile_group.shape[0]

    def wmap(i, e, dst_r, tgrp_r, nval_r):
        return (layer * N_EXPERTS + tgrp_r[i] * EXPERTS_PER_GROUP + e, 0, 0)

    grid_spec = pltpu.PrefetchScalarGridSpec(
        num_scalar_prefetch=3,
        grid=(ntiles, EXPERTS_PER_GROUP),
        in_specs=[
            pl.BlockSpec(memory_space=pl.ANY),
            pl.BlockSpec((1, d, dff), wmap),
            pl.BlockSpec((1, d, dff), wmap),
            pl.BlockSpec((1, dff, d), wmap),
        ],
        out_specs=pl.BlockSpec(memory_space=pl.ANY),
        scratch_shapes=[
            pltpu.VMEM((2, tm, dw), F32),
            pltpu.VMEM((tm, d), BF16),
            pltpu.VMEM((tm, LANE), F32),
            pltpu.VMEM((2, tm, d), F32),
            pltpu.SemaphoreType.DMA((2,)),
            pltpu.SemaphoreType.DMA((2,)),
        ],
    )
    return pl.pallas_call(
        functools.partial(_moe_experts_kernel, tm=tm, n_tok=t),
        grid_spec=grid_spec,
        out_shape=jax.ShapeDtypeStruct((t + 2 * tm, d), F32),
        compiler_params=_cparams("arbitrary", "arbitrary"),
        name="moe_experts",
    )(dst, tile_group, tile_nvalid, hn_ext, w_gate, w_up, w_down)


def _residual_kernel(x_ref, y_ref, g_ref, o_ref):
    o_ref[0] = x_ref[0] + g_ref[0] * y_ref[...]


def _residual(x, y, gate):
    nb, r, d = x.shape
    tm = _tile(r, 512)
    per_b = r // tm
    blk = pl.BlockSpec((1, tm, d), lambda b, t: (b, t, 0))
    return pl.pallas_call(
        _residual_kernel,
        grid=(nb, per_b),
        in_specs=[blk, pl.BlockSpec((tm, d), lambda b, t: (b * per_b + t, 0)),
                  pl.BlockSpec((1, 1, d), lambda b, t: (b, 0, 0))],
        out_specs=blk,
        out_shape=jax.ShapeDtypeStruct((nb, r, d), F32),
        compiler_params=_cparams("parallel", "parallel"),
        name="moe_residual",
    )(x, y, gate)


def _group_layout(grp, tm):
    t = grp.shape[0]
    ntiles = t // tm + N_GROUPS
    nslots = ntiles * tm
    oh = (grp[None, :] == jnp.arange(N_GROUPS)[:, None]).astype(F32).reshape(N_GROUPS, t // LANE, LANE)
    tri = (jnp.arange(LANE)[:, None] <= jnp.arange(LANE)[None, :]).astype(F32)
    within = jnp.einsum('grk,kl->grl', oh, tri)
    row_tot = within[..., -1]
    row_off = jnp.cumsum(row_tot, axis=1) - row_tot
    rank = (jnp.sum((within + row_off[..., None]) * oh, axis=0).reshape(t) - 1.0).astype(jnp.int32)
    counts = jnp.sum(row_tot, axis=1).astype(jnp.int32)
    padded = ((counts + tm - 1) // tm) * tm
    start = jnp.cumsum(padded) - padded
    slot_of_token = start[grp] + rank
    slot_ids = jnp.arange(nslots, dtype=jnp.int32)
    spare = t + ((slot_ids // tm) % 2) * tm + slot_ids % tm
    dst = spare.at[slot_of_token].set(jnp.arange(t, dtype=jnp.int32))
    tile_start = jnp.arange(ntiles, dtype=jnp.int32) * tm
    ends = jnp.cumsum(padded)
    tile_group = jnp.minimum(jnp.sum((tile_start[:, None] >= ends[None, :]).astype(jnp.int32), axis=1),
                             N_GROUPS - 1).astype(jnp.int32)
    tile_nvalid = jnp.clip(counts[tile_group] - (tile_start - start[tile_group]), 0, tm).astype(jnp.int32)
    return dst, tile_group, tile_nvalid


def _moe(x, g, sh, sc, gate, w_router_t, b_router, w_gate, w_up, w_down, layer):
    nb, r, d = x.shape
    t = nb * r
    tm = 512 if t >= 8192 else (256 if t >= 2048 else 128)
    hn_ext, info = _moe_route(x, g, sh, sc, w_router_t, b_router)
    grp = info[:, EXPERTS_PER_GROUP, :].reshape(t).astype(jnp.int32)
    dst, tile_group, tile_nvalid = _group_layout(grp, tm)
    y = _moe_experts(hn_ext.reshape(t, d + LANE), dst, tile_group, tile_nvalid, w_gate, w_up, w_down,
                     layer, tm)
    return _residual(x, y, gate)


def kernel(x, c, ctx, c_ctx, w_ada, b_ada, norm1_g, norm2_g, w_in, qn_a, kn_a, sink_a, qn_b, kn_b,
           lam_b, subln_b, conv_w, conv_b, w_rg, b_rg, w_ig, b_ig, lru_lambda, lb_d, onorm_d,
           w_branch, w_out, w_router, b_router, w_gate, w_up, w_down):
    bsz, seq, d = x.shape
    lc = ctx.shape[1]
    depth = w_in.shape[0]

    n_rows = -(-(bsz + 1) // SUBLANE) * SUBLANE
    c_pad = jnp.zeros((n_rows, d), F32).at[:bsz].set(c).at[bsz].set(c_ctx)
    mod = _ada(c_pad, w_ada, b_ada).reshape(depth, n_rows, 6, d)

    lb_w = jax.nn.softmax(lb_d.astype(F32), axis=0)
    lb_all = jnp.cumsum(lb_w, axis=0) - lb_w[0:1]
    rope_tabs = _rope_tables(seq)
    w_router_t = w_router.T
    w_in_b = w_in.astype(BF16)
    w_branch_b = w_branch.astype(BF16).reshape((depth * w_branch.shape[1],) + w_branch.shape[2:])
    w_out_b = w_out.astype(BF16)
    w_gate_b, w_up_b, w_down_b = (w.astype(BF16).reshape((depth * N_EXPERTS,) + w.shape[2:])
                                  for w in (w_gate, w_up, w_down))

    xl = x
    xc = ctx.reshape(1, bsz * lc, d)
    for l in range(depth):
        ctx_out = l < depth - 1
        mod_l = [mod[l, :bsz, k][:, None, :] for k in range(6)]
        mod_c = [mod[l, bsz:bsz + 1, k][:, None, :] for k in range(6)]
        g1 = norm1_g[l].reshape(1, d)
        g2 = norm2_g[l].reshape(1, d)

        lo_l, hi_l = _norm_proj(xl, g1, mod_l[0], mod_l[1], w_in_b, l)
        lo_c, hi_c = _norm_proj(xc, g1, mod_c[0], mod_c[1], w_in_b, l)
        lo_c = lo_c.reshape(bsz, lc, -1)
        hi_c = hi_c.reshape(bsz, lc, -1)

        qa = qn_a[l] * (HEAD_DIM ** -0.5 * LOG2E)
        qb = jnp.tile(qn_b[l], 2) * (B_DIM ** -0.5 * LOG2E)
        kb = jnp.tile(kn_b[l], 2)
        one = jnp.ones((LANE,), F32)
        gains = jnp.stack([qa] * 8 + [kn_a[l]] * 2 + [one] * 2 + [qb] * 8 + [kb] * 8 + [one] * 8)
        gains = gains.reshape(N_QKV_BLOCKS, 1, LANE)
        p_l = _prep(lo_l, gains, rope_tabs)
        p_c = _prep(lo_c, gains, None)

        bound_a = jnp.maximum(
            LOG2E * HEAD_DIM ** 0.5 * jnp.max(jnp.abs(qn_a[l])) * jnp.max(jnp.abs(kn_a[l])) * LOGIT_MARGIN,
            LOG2E * jnp.max(jnp.abs(sink_a[l])))
        bounded_a = (bound_a < SOFTMAX_SAFE_LOGIT).astype(jnp.int32).reshape(1)
        oa_c, oa_l = _attn_a(sink_a[l], bounded_a, p_l, p_c, ctx_out)

        lq1, lk1, lq2, lk2 = lam_b[l].astype(F32)
        lam_init = 0.8 - 0.6 * math.exp(-0.3 * l)
        lam = (jnp.exp(jnp.sum(lq1 * lk1)) - jnp.exp(jnp.sum(lq2 * lk2)) + lam_init).reshape(1)
        bound_b = LOG2E * B_DIM ** 0.5 * jnp.max(jnp.abs(qn_b[l])) * jnp.max(jnp.abs(kn_b[l])) * LOGIT_MARGIN
        bounded_b = (bound_b < SOFTMAX_SAFE_LOGIT).astype(jnp.int32).reshape(1)
        ob_c, ob_l = _attn_b(lam, bounded_b, p_l, p_c, subln_b[l].reshape(1, LANE), 1.0 - lam_init, ctx_out)

        oc_c, oc_l = _lru(hi_l, hi_c, conv_w[l], conv_b[l], w_rg[l].astype(BF16), b_rg[l],
                          w_ig[l].astype(BF16), b_ig[l], lru_lambda[l], ctx_out)
        od_c, od_l = _gla(hi_l, hi_c, lb_all[l], onorm_d[l].reshape(1, LANE), ctx_out)

        moe_w = (w_router_t, b_router, w_gate_b, w_up_b, w_down_b, l)

        y_l = _merge((oa_l, ob_l, oc_l, od_l), lo_l, w_branch_b, l)
        xl = _out_proj(y_l, w_out_b, l, xl, mod_l[2])
        xl = _moe(xl, g2, mod_l[3], mod_l[4], mod_l[5], *moe_w)
        if ctx_out:
            flat = lambda t: t.reshape(1, bsz * lc, -1)
            y_c = _merge(tuple(flat(t) for t in (oa_c, ob_c, oc_c, od_c)), flat(lo_c), w_branch_b, l)
            xc = _out_proj(y_c, w_out_b, l, xc, mod_c[2])
            xc = _moe(xc, g2, mod_c[3], mod_c[4], mod_c[5], *moe_w)
    return xl
```
